```python
import math
import jax, jax.numpy as jnp
from jax import lax
import numpy as np

D_MODEL = 1024
BATCH = 4
SEQ = 4096
DEPTH = 1

GLA_HEADS = 4
GLA_DK = 64
GLA_DV = 128
GLA_RANK = 16
GLA_TAU = 16.0
GLA_CHUNK = 16
DIFF_HEADS = 4
DIFF_DH = 64
DIFF_DV = 2 * DIFF_DH
Q_BLOCK = 128
ALIBI_MAX_BIAS = 8.0
D_FF = 2816
EPS = 1e-6
N_BRANCH = 2
N_MOD = 9

GLA_KW = GLA_HEADS * GLA_DK
GLA_VW = GLA_HEADS * GLA_DV
DIFF_QW = DIFF_HEADS * 2 * DIFF_DH
DIFF_VW = DIFF_HEADS * DIFF_DV
IN_SIZES = (GLA_KW, GLA_KW, GLA_VW, GLA_VW, GLA_RANK, DIFF_QW, DIFF_QW, DIFF_VW, N_BRANCH * D_MODEL)
IN_COLS = GLA_KW * 2 + GLA_VW * 2 + GLA_RANK + DIFF_QW * 2 + DIFF_VW + N_BRANCH * D_MODEL

kernel_name = "hybrid_gla_diffattn_macaron_adaln"


def split_points(sizes):
    pts, acc = [], 0
    for s in sizes[:-1]:
        acc += s
        pts.append(acc)
    return pts


def rmsnorm(x, g):
    xf = x.astype(jnp.float32)
    y = xf * lax.rsqrt(jnp.mean(xf * xf, axis=-1, keepdims=True) + EPS)
    return (y * g.astype(jnp.float32)).astype(x.dtype)


def modulate(xn, shift, scale):
    return xn * (1.0 + scale[:, None, :]) + shift[:, None, :]


def swiglu(x, w_in, w_out):
    g, u = jnp.split(x @ w_in, 2, axis=-1)
    return (jax.nn.silu(g) * u) @ w_out


def gla(q, k, v, a_low, w2, b2):
    B, S, _ = q.shape
    C = GLA_CHUNK
    N = S // C
    f32 = jnp.float32
    log_a = jax.nn.log_sigmoid((a_low @ w2 + b2).astype(f32)) / GLA_TAU

    def chunks(t, d):
        return t.reshape(B, N, C, GLA_HEADS, d).transpose(0, 3, 1, 2, 4).astype(f32)

    qc = chunks(q, GLA_DK) * (GLA_DK ** -0.5)
    kc = chunks(k, GLA_DK)
    vc = chunks(v, GLA_DV)
    b = jnp.cumsum(chunks(log_a, GLA_DK), axis=3)
    b_last = b[:, :, :, -1:, :]

    causal = jnp.tril(jnp.ones((C, C), dtype=bool))
    diff = b[:, :, :, :, None, :] - b[:, :, :, None, :, :]
    decay = jnp.exp(jnp.where(causal[:, :, None], diff, -jnp.inf))
    scores = jnp.einsum('bhntk,bhnsk,bhntsk->bhnts', qc, kc, decay)
    o_intra = jnp.einsum('bhnts,bhnsv->bhntv', scores, vc)

    kv = jnp.einsum('bhnck,bhncv->nbhkv', kc * jnp.exp(b_last - b), vc)
    chunk_decay = jnp.exp(b_last[:, :, :, 0, :]).transpose(2, 0, 1, 3)

    def step(state, inp):
        dec, kvn = inp
        return dec[..., None] * state + kvn, state

    s0 = jnp.zeros((B, GLA_HEADS, GLA_DK, GLA_DV), f32)
    _, s_prev = lax.scan(step, s0, (chunk_decay, kv))
    o_inter = jnp.einsum('bhnck,nbhkv->bhncv', qc * jnp.exp(b), s_prev)
    o = o_intra + o_inter
    return o.transpose(0, 2, 3, 1, 4).reshape(B, S, GLA_HEADS, GLA_DV).astype(v.dtype)


def diff_attention(q, k, v, lq1, lk1, lq2, lk2, lam_init):
    B, S, _ = q.shape
    H = DIFF_HEADS
    f32 = jnp.float32
    q = q.reshape(B, S, H, 2, DIFF_DH).transpose(0, 2, 3, 1, 4) * (DIFF_DH ** -0.5)
    k = k.reshape(B, S, H, 2, DIFF_DH).transpose(0, 2, 3, 1, 4)
    v = v.reshape(B, S, H, DIFF_DV).transpose(0, 2, 1, 3)
    lam = (jnp.exp(jnp.sum(lq1.astype(f32) * lk1.astype(f32)))
           - jnp.exp(jnp.sum(lq2.astype(f32) * lk2.astype(f32))) + lam_init)
    slopes = jnp.exp2(-ALIBI_MAX_BIAS * jnp.arange(1, H + 1, dtype=f32) / H)
    NB = S // Q_BLOCK
    qb = q.reshape(B, H, 2, NB, Q_BLOCK, DIFF_DH).transpose(3, 0, 1, 2, 4, 5)
    kpos = jnp.arange(S)

    def block(args):
        qi, i = args
        qpos = i * Q_BLOCK + jnp.arange(Q_BLOCK)
        rel = (qpos[:, None] - kpos[None, :]).astype(f32)
        s = jnp.einsum('bhiqd,bhikd->bhiqk', qi, k).astype(f32)
        s = s - slopes[None, :, None, None, None] * rel
        s = jnp.where(rel >= 0, s, -jnp.inf)
        p = jax.nn.softmax(s, axis=-1)
        a = p[:, :, 0] - lam * p[:, :, 1]
        return jnp.einsum('bhqk,bhkv->bhqv', a.astype(v.dtype), v)

    o = lax.map(block, (qb, jnp.arange(NB)))
    return o.transpose(1, 0, 3, 2, 4).reshape(B, S, H, DIFF_DV)


def token_mixer(u, w_in, gla_alpha_w2, gla_alpha_b, gla_head_norm,
                diff_lq1, diff_lk1, diff_lq2, diff_lk2, diff_head_norm,
                w_branch_a, w_branch_b, w_out, lam_init):
    B, S, _ = u.shape
    proj = u @ w_in
    gq, gk, gv, gr, ga, dq, dk, dv, gates = jnp.split(proj, split_points(IN_SIZES), axis=-1)
    ya = gla(gq, gk, gv, ga, gla_alpha_w2, gla_alpha_b)
    ya = rmsnorm(ya, gla_head_norm) * jax.nn.silu(gr).reshape(B, S, GLA_HEADS, GLA_DV)
    ya = ya.reshape(B, S, GLA_VW) @ w_branch_a
    yb = diff_attention(dq, dk, dv, diff_lq1, diff_lk1, diff_lq2, diff_lk2, lam_init)
    yb = rmsnorm(yb, diff_head_norm) * (1.0 - lam_init)
    yb = yb.reshape(B, S, DIFF_VW) @ w_branch_b
    g_a, g_b = jnp.split(jax.nn.sigmoid(gates), N_BRANCH, axis=-1)
    return (g_a * ya + g_b * yb) @ w_out


def setup_inputs(seed: int = 0) -> dict:
    key = jax.random.key(seed)
    ks = jax.random.split(key, 24)
    L, D = DEPTH, D_MODEL
    nrm = jax.random.normal

    def gain(k, n):
        return 1.0 + 0.02 * nrm(k, (L, n), jnp.float32)

    return {
        "x": nrm(ks[0], (BATCH, SEQ, D), jnp.float32),
        "c": nrm(ks[1], (BATCH, D), jnp.float32),
        "w_ada": nrm(ks[2], (L, D, N_MOD * D), jnp.float32) * (0.5 * D ** -0.5),
        "b_ada": 0.02 * nrm(ks[3], (L, N_MOD * D), jnp.float32),
        "ffn1_norm": gain(ks[4], D),
        "ffn1_w_in": nrm(ks[5], (L, D, 2 * D_FF), jnp.float32) * D ** -0.5,
        "ffn1_w_out": nrm(ks[6], (L, D_FF, D), jnp.float32) * D_FF ** -0.5,
        "mix_norm": gain(ks[7], D),
        "w_in": nrm(ks[8], (L, D, IN_COLS), jnp.float32) * D ** -0.5,
        "gla_alpha_w2": nrm(ks[9], (L, GLA_RANK, GLA_KW), jnp.float32) * GLA_RANK ** -0.5,
        "gla_alpha_b": 0.1 * nrm(ks[10], (L, GLA_KW), jnp.float32),
        "gla_head_norm": gain(ks[11], GLA_DV),
        "diff_lq1": 0.1 * nrm(ks[12], (L, DIFF_DH), jnp.float32),
        "diff_lk1": 0.1 * nrm(ks[13], (L, DIFF_DH), jnp.float32),
        "diff_lq2": 0.1 * nrm(ks[14], (L, DIFF_DH), jnp.float32),
        "diff_lk2": 0.1 * nrm(ks[15], (L, DIFF_DH), jnp.float32),
        "diff_head_norm": gain(ks[16], DIFF_DV),
        "w_branch_a": nrm(ks[17], (L, GLA_VW, D), jnp.float32) * GLA_VW ** -0.5,
        "w_branch_b": nrm(ks[18], (L, DIFF_VW, D), jnp.float32) * DIFF_VW ** -0.5,
        "w_out": nrm(ks[19], (L, D, D), jnp.float32) * D ** -0.5,
        "ffn2_norm": gain(ks[20], D),
        "ffn2_w_in": nrm(ks[21], (L, D, 2 * D_FF), jnp.float32) * D ** -0.5,
        "ffn2_w_out": nrm(ks[22], (L, D_FF, D), jnp.float32) * D_FF ** -0.5,
        "final_norm": 1.0 + 0.02 * nrm(ks[23], (D,), jnp.float32),
    }


def reference(x, c, w_ada, b_ada, ffn1_norm, ffn1_w_in, ffn1_w_out, mix_norm, w_in,
              gla_alpha_w2, gla_alpha_b, gla_head_norm, diff_lq1, diff_lk1, diff_lq2,
              diff_lk2, diff_head_norm, w_branch_a, w_branch_b, w_out, ffn2_norm,
              ffn2_w_in, ffn2_w_out, final_norm):
    h = x
    c_act = jax.nn.silu(c)
    for l in range(DEPTH):
        lam_init = 0.8 - 0.6 * math.exp(-0.3 * l)
        mod = c_act @ w_ada[l] + b_ada[l]
        (sh1, sc1, gt1, sh2, sc2, gt2, sh3, sc3, gt3) = jnp.split(mod, N_MOD, axis=-1)
        u = modulate(rmsnorm(h, ffn1_norm[l]), sh1, sc1)
        h = h + 0.5 * gt1[:, None, :] * swiglu(u, ffn1_w_in[l], ffn1_w_out[l])
        u = modulate(rmsnorm(h, mix_norm[l]), sh2, sc2)
        m = token_mixer(u, w_in[l], gla_alpha_w2[l], gla_alpha_b[l], gla_head_norm[l],
                        diff_lq1[l], diff_lk1[l], diff_lq2[l], diff_lk2[l], diff_head_norm[l],
                        w_branch_a[l], w_branch_b[l], w_out[l], lam_init)
        h = h + gt2[:, None, :] * m
        u = modulate(rmsnorm(h, ffn2_norm[l]), sh3, sc3)
        h = h + 0.5 * gt3[:, None, :] * swiglu(u, ffn2_w_in[l], ffn2_w_out[l])
    return rmsnorm(h, final_norm)
```

```python
import functools
import math

import jax
import jax.numpy as jnp
from jax import lax
from jax.experimental import pallas as pl
from jax.experimental.pallas import tpu as pltpu

F32 = jnp.float32
BF16 = jnp.bfloat16

EPS = 1e-6
GLA_HEADS = 4
GLA_DK = 64
GLA_DV = 128
GLA_RANK = 16
GLA_TAU = 16.0
DIFF_HEADS = 4
DIFF_DH = 64
DIFF_DV = 128
ALIBI_MAX_BIAS = 8.0
N_MOD = 9

LANES = 128
GLA_CHUNK = 128
GLA_SUB = 16
EXP_CLAMP = 80.0
VMEM_LIMIT = 56 * 1024 * 1024


def _params(sem):
    return pltpu.CompilerParams(dimension_semantics=sem, vmem_limit_bytes=VMEM_LIMIT)


def _resident(shape):
    nd = len(shape)
    return pl.BlockSpec(shape, lambda *_: (0,) * nd, pipeline_mode=pl.Buffered(1))


def _rmsnorm(x, g):
    return x * lax.rsqrt(jnp.mean(x * x, axis=-1, keepdims=True) + EPS) * g


def _adaln_kernel(c_ref, w_ref, b_ref, o_ref):
    c = c_ref[...]
    ca = (c * jax.nn.sigmoid(c)).astype(BF16)
    o_ref[...] = jnp.dot(ca, w_ref[...].astype(BF16), preferred_element_type=F32) + b_ref[...]


def _adaln(c, w_ada, b_ada):
    bsz, d = c.shape
    n = w_ada.shape[1]
    tn = n // 4
    return pl.pallas_call(
        _adaln_kernel,
        out_shape=jax.ShapeDtypeStruct((bsz, n), F32),
        grid=(n // tn,),
        in_specs=[
            pl.BlockSpec((bsz, d), lambda j: (0, 0)),
            pl.BlockSpec((d, tn), lambda j: (0, j)),
            pl.BlockSpec((1, tn), lambda j: (0, j)),
        ],
        out_specs=pl.BlockSpec((bsz, tn), lambda j: (0, j)),
        compiler_params=_params(("arbitrary",)),
        name="adaln",
    )(c, w_ada, b_ada.reshape(1, n))


def _ffn_kernel(x_ref, sh_ref, sc_ref, gt_ref, nw_ref, win_ref, wout_ref, fn_ref, o_ref,
                acc_ref, *, d_ff, tf, final_norm):
    x = x_ref[0]
    u = (_rmsnorm(x, nw_ref[...]) * (1.0 + sc_ref[0]) + sh_ref[0]).astype(BF16)
    for i in range(d_ff // tf):
        hg = jnp.dot(u, win_ref[:, i * tf:(i + 1) * tf], preferred_element_type=F32)
        hu = jnp.dot(u, win_ref[:, d_ff + i * tf:d_ff + (i + 1) * tf], preferred_element_type=F32)
        act = (hg * jax.nn.sigmoid(hg) * hu).astype(BF16)
        part = jnp.dot(act, wout_ref[i * tf:(i + 1) * tf, :], preferred_element_type=F32)
        if i == 0:
            acc_ref[...] = part
        else:
            acc_ref[...] += part
    h = x + (0.5 * gt_ref[0]) * acc_ref[...]
    if final_norm:
        h = _rmsnorm(h, fn_ref[...])
    o_ref[0] = h


def _ffn(x, sh, sc, gt, nw, w_in, w_out, fn, *, final_norm, tm=512, tf=256):
    bsz, s, d = x.shape
    d_ff = w_out.shape[0]
    vec = pl.BlockSpec((1, 1, d), lambda b, i: (b, 0, 0))
    row = pl.BlockSpec((1, tm, d), lambda b, i: (b, i, 0))
    return pl.pallas_call(
        functools.partial(_ffn_kernel, d_ff=d_ff, tf=tf, final_norm=final_norm),
        out_shape=jax.ShapeDtypeStruct((bsz, s, d), F32),
        grid=(bsz, s // tm),
        in_specs=[row, vec, vec, vec, _resident((1, d)), _resident((d, 2 * d_ff)),
                  _resident((d_ff, d)), _resident((1, d))],
        out_specs=row,
        scratch_shapes=[pltpu.VMEM((tm, d), F32)],
        compiler_params=_params(("parallel", "parallel")),
        name="ffn_final" if final_norm else "ffn",
    )(x, sh, sc, gt, nw.reshape(1, d), w_in, w_out, fn.reshape(1, d))


def _proj_kernel(x_ref, sh_ref, sc_ref, nw_ref, w_ref, wga_ref, w2_ref, b2_ref,
                 gq_ref, gk_ref, gv_ref, gr_ref, g_ref, dq_ref, dk_ref, dv_ref, sg_ref):
    x = x_ref[0]
    u = (_rmsnorm(x, nw_ref[...]) * (1.0 + sc_ref[0]) + sh_ref[0]).astype(BF16)

    def seg(start, size):
        return jnp.dot(u, w_ref[:, start:start + size], preferred_element_type=F32)

    kw = GLA_HEADS * GLA_DK
    vw = GLA_HEADS * GLA_DV
    qw = DIFF_HEADS * 2 * DIFF_DH
    dvw = DIFF_HEADS * DIFF_DV
    off = 0
    gq_ref[0] = (seg(off, kw) * (GLA_DK ** -0.5)).astype(BF16)
    off += kw
    gk_ref[0] = seg(off, kw).astype(BF16)
    off += kw
    gv_ref[0] = seg(off, vw).astype(BF16)
    off += vw
    r = seg(off, vw)
    gr_ref[0] = (r * jax.nn.sigmoid(r)).astype(BF16)
    off += vw
    dq_ref[0] = (seg(off, qw) * (DIFF_DH ** -0.5)).astype(BF16)
    off += qw
    dk_ref[0] = seg(off, qw).astype(BF16)
    off += qw
    dv_ref[0] = seg(off, dvw).astype(BF16)
    off += dvw
    n_gate = sg_ref.shape[-1]
    step = 512
    for j in range(n_gate // step):
        sg_ref[0, :, j * step:(j + 1) * step] = jax.nn.sigmoid(seg(off + j * step, step)).astype(BF16)
    a_low = jnp.dot(u, wga_ref[...], preferred_element_type=F32).astype(BF16)
    z = jnp.dot(a_low, w2_ref[...], preferred_element_type=F32) + b2_ref[...]
    log_sig = jnp.minimum(z, 0.0) - jnp.log1p(jnp.exp(-jnp.abs(z)))
    g_ref[0] = log_sig * (1.0 / GLA_TAU)


def _mixer_proj(h, sh, sc, nw, w_main, w_ga, w2, b2, *, tm=512):
    bsz, s, d = h.shape
    kw = GLA_HEADS * GLA_DK
    vw = GLA_HEADS * GLA_DV
    qw = DIFF_HEADS * 2 * DIFF_DH
    dvw = DIFF_HEADS * DIFF_DV
    n_gate = 2 * d
    vec = pl.BlockSpec((1, 1, d), lambda b, i: (b, 0, 0))

    def rows(w):
        return pl.BlockSpec((1, tm, w), lambda b, i: (b, i, 0))

    widths = [(kw, BF16), (kw, BF16), (vw, BF16), (vw, BF16), (kw, F32),
              (qw, BF16), (qw, BF16), (dvw, BF16), (n_gate, BF16)]
    return pl.pallas_call(
        _proj_kernel,
        out_shape=[jax.ShapeDtypeStruct((bsz, s, w), dt) for w, dt in widths],
        grid=(bsz, s // tm),
        in_specs=[rows(d), vec, vec, _resident((1, d)), _resident(w_main.shape),
                  _resident(w_ga.shape), _resident(w2.shape), _resident((1, kw))],
        out_specs=[rows(w) for w, _ in widths],
        compiler_params=_params(("parallel", "parallel")),
        name="mixer_proj",
    )(h, sh, sc, nw.reshape(1, d), w_main, w_ga, w2, b2.reshape(1, kw))


def _gla_kernel(q_ref, k_ref, v_ref, r_ref, g_ref, hn_ref, o_ref, st_ref, *, chunk, sub):
    @pl.when(pl.program_id(1) == 0)
    def _():
        st_ref[...] = jnp.zeros_like(st_ref)

    nsub = chunk // sub
    row = lax.broadcasted_iota(jnp.int32, (chunk, chunk), 0)
    col = lax.broadcasted_iota(jnp.int32, (chunk, chunk), 1)
    causal = col <= row
    cum_mat = jnp.concatenate(
        [jnp.where(causal, 1.0, 0.0), jnp.where(causal & (col >= (row // sub) * sub), 1.0, 0.0)],
        axis=0).astype(BF16)
    row_blk = lax.broadcasted_iota(jnp.int32, (chunk, LANES), 0) // sub
    lane_head = lax.broadcasted_iota(jnp.int32, (chunk, LANES), 1) // GLA_DK

    for pair in range(GLA_HEADS // 2):
        lanes = slice(pair * LANES, (pair + 1) * LANES)
        g = g_ref[0, :, lanes]
        g1 = g.astype(BF16)
        e1 = g - g1.astype(F32)
        g2 = e1.astype(BF16)
        g3 = (e1 - g2.astype(F32)).astype(BF16)
        cs = jnp.dot(cum_mat, jnp.concatenate([g1, g2, g3], axis=1), preferred_element_type=F32)
        cs = cs[:, :LANES] + cs[:, LANES:2 * LANES] + cs[:, 2 * LANES:]
        b = cs[:chunk]
        w = cs[chunk:]
        b_last = b[chunk - 1:chunk]
        q = q_ref[0, :, lanes].astype(F32)
        k = k_ref[0, :, lanes].astype(F32)
        q_state = (q * jnp.exp(b)).astype(BF16)
        k_state = k * jnp.exp(b_last - b)
        k_hat = k * jnp.exp(jnp.minimum(-w, EXP_CLAMP))
        q_cat = []
        for j in range(nsub):
            ref_b = b[j * sub - 1:j * sub] if j else jnp.zeros_like(b_last)
            q_cat.append((q * jnp.exp(jnp.minimum(b - ref_b, 0.0))).astype(BF16))
        q_cat = jnp.concatenate(q_cat, axis=1)
        for hh in range(2):
            head = pair * 2 + hh
            in_head = lane_head == hh
            kh = jnp.where(in_head, k_hat, 0.0).astype(BF16)
            k_cat = jnp.concatenate(
                [jnp.where(row_blk == j, kh, jnp.zeros_like(kh)) for j in range(nsub)], axis=1)
            scores = lax.dot_general(q_cat, k_cat, (((1,), (1,)), ((), ())),
                                     preferred_element_type=F32)
            scores = jnp.where(causal, scores, 0.0).astype(BF16)
            vh = v_ref[0, :, head * GLA_DV:(head + 1) * GLA_DV]
            state_t = st_ref[head]
            o = jnp.dot(scores, vh, preferred_element_type=F32)
            o += lax.dot_general(q_state, state_t.astype(BF16), (((1,), (1,)), ((), ())),
                                 preferred_element_type=F32)
            ks = jnp.where(in_head, k_state, 0.0).astype(BF16)
            kv_t = lax.dot_general(vh, ks, (((0,), (0,)), ((), ())), preferred_element_type=F32)
            st_ref[head] = state_t * jnp.exp(b_last) + kv_t
            y = _rmsnorm(o, hn_ref[...]) * r_ref[0, :, head * GLA_DV:(head + 1) * GLA_DV].astype(F32)
            o_ref[0, :, head * GLA_DV:(head + 1) * GLA_DV] = y.astype(BF16)


def _gla(gq, gk, gv, gr, g, head_norm):
    bsz, s, kw = gq.shape
    vw = gv.shape[-1]
    chunk = GLA_CHUNK

    def rows(w):
        return pl.BlockSpec((1, chunk, w), lambda b, c: (b, c, 0))

    return pl.pallas_call(
        functools.partial(_gla_kernel, chunk=chunk, sub=GLA_SUB),
        out_shape=jax.ShapeDtypeStruct((bsz, s, vw), BF16),
        grid=(bsz, s // chunk),
        in_specs=[rows(kw), rows(kw), rows(vw), rows(vw), rows(kw), _resident((1, GLA_DV))],
        out_specs=rows(vw),
        scratch_shapes=[pltpu.VMEM((GLA_HEADS, GLA_DV, LANES), F32)],
        compiler_params=_params(("parallel", "arbitrary")),
        name="gla",
    )(gq, gk, gv, gr, g, head_norm.reshape(1, GLA_DV))


def _attn_kernel(q_ref, k_ref, v_ref, slope_ref, lam_ref, hn_ref, o_ref, qq_ref, m_ref, l_ref,
                 acc_ref, *, tq, lam_init):
    qi = pl.program_id(2)
    slope = slope_ref[0]

    q = q_ref[0]
    lane = lax.broadcasted_iota(jnp.int32, (tq, LANES), 1)
    zero = jnp.zeros_like(q)
    qq_ref[:tq] = jnp.where(lane < DIFF_DH, q, zero)
    qq_ref[tq:] = jnp.where(lane >= DIFF_DH, q, zero)
    m_ref[...] = jnp.full_like(m_ref, -jnp.inf)
    l_ref[...] = jnp.zeros_like(l_ref)
    acc_ref[...] = jnp.zeros_like(acc_ref)

    r2 = lax.broadcasted_iota(jnp.int32, (2 * tq, tq), 0)
    c2 = lax.broadcasted_iota(jnp.int32, (2 * tq, tq), 1)
    rel = jnp.where(r2 >= tq, r2 - tq, r2) - c2
    bias = -slope * rel.astype(F32)

    def step(j, masked):
        start = pl.multiple_of(j * tq, tq)
        kb = k_ref[0, pl.ds(start, tq), :]
        vb = v_ref[0, pl.ds(start, tq), :]
        s = lax.dot_general(qq_ref[...], kb, (((1,), (1,)), ((), ())), preferred_element_type=F32)
        s = s + bias
        if masked:
            s = jnp.where(rel >= 0, s, -jnp.inf)
        shift = slope * ((qi - j) * tq).astype(F32)
        m_old = m_ref[...]
        m_new = jnp.maximum(m_old, jnp.max(s, axis=-1, keepdims=True) - shift)
        alpha = jnp.exp(m_old - m_new)
        p = jnp.exp(s - (m_new + shift))
        l_ref[...] = alpha * l_ref[...] + jnp.sum(p, axis=-1, keepdims=True)
        acc_ref[...] = alpha * acc_ref[...] + jnp.dot(p.astype(BF16), vb, preferred_element_type=F32)
        m_ref[...] = m_new

    def body(j, carry):
        step(j, False)
        return carry

    lax.fori_loop(0, qi, body, 0)
    step(qi, True)

    lam = lam_ref[...]
    lam = (jnp.exp(jnp.sum(lam[0:1] * lam[1:2], axis=-1, keepdims=True))
           - jnp.exp(jnp.sum(lam[2:3] * lam[3:4], axis=-1, keepdims=True)) + lam_init)
    out = acc_ref[...] / l_ref[...]
    y = out[:tq] - lam * out[tq:]
    o_ref[0] = (_rmsnorm(y, hn_ref[...]) * (1.0 - lam_init)).astype(BF16)


def _diff_attn(dq, dk, dv, lam_vecs, head_norm, lam_init, *, tq=512):
    bsz, s, _ = dq.shape
    slopes = jnp.asarray(
        [2.0 ** (-ALIBI_MAX_BIAS * (i + 1) / DIFF_HEADS) for i in range(DIFF_HEADS)], F32
    ).reshape(DIFF_HEADS, 1, 1)
    return pl.pallas_call(
        functools.partial(_attn_kernel, tq=tq, lam_init=lam_init),
        out_shape=jax.ShapeDtypeStruct((bsz, s, DIFF_HEADS * DIFF_DV), BF16),
        grid=(bsz, DIFF_HEADS, s // tq),
        in_specs=[
            pl.BlockSpec((1, tq, LANES), lambda b, h, i: (b, i, h)),
            pl.BlockSpec((1, s, LANES), lambda b, h, i: (b, 0, h)),
            pl.BlockSpec((1, s, DIFF_DV), lambda b, h, i: (b, 0, h)),
            pl.BlockSpec((1, 1, 1), lambda b, h, i: (h, 0, 0)),
            _resident(lam_vecs.shape),
            _resident((1, DIFF_DV)),
        ],
        out_specs=pl.BlockSpec((1, tq, DIFF_DV), lambda b, h, i: (b, i, h)),
        scratch_shapes=[
            pltpu.VMEM((2 * tq, LANES), BF16),
            pltpu.VMEM((2 * tq, 1), F32),
            pltpu.VMEM((2 * tq, 1), F32),
            pltpu.VMEM((2 * tq, DIFF_DV), F32),
        ],
        compiler_params=_params(("parallel", "parallel", "arbitrary")),
        name="diff_attn",
    )(dq, dk, dv, slopes, lam_vecs, head_norm.reshape(1, DIFF_DV))


def _merge_kernel(h_ref, za_ref, zb_ref, sg_ref, gt_ref, wa_ref, wb_ref, wo_ref, o_ref):
    d = h_ref.shape[-1]
    ya = jnp.dot(za_ref[0], wa_ref[...], preferred_element_type=F32)
    yb = jnp.dot(zb_ref[0], wb_ref[...], preferred_element_type=F32)
    mix = sg_ref[0, :, :d].astype(F32) * ya + sg_ref[0, :, d:].astype(F32) * yb
    m = jnp.dot(mix.astype(BF16), wo_ref[...], preferred_element_type=F32)
    o_ref[0] = h_ref[0] + gt_ref[0] * m


def _merge(h, za, zb, sg, gt, w_a, w_b, w_o, *, tm=512):
    bsz, s, d = h.shape
    vec = pl.BlockSpec((1, 1, d), lambda b, i: (b, 0, 0))

    def rows(w):
        return pl.BlockSpec((1, tm, w), lambda b, i: (b, i, 0))

    return pl.pallas_call(
        _merge_kernel,
        out_shape=jax.ShapeDtypeStruct((bsz, s, d), F32),
        grid=(bsz, s // tm),
        in_specs=[rows(d), rows(za.shape[-1]), rows(zb.shape[-1]), rows(2 * d), vec,
                  _resident(w_a.shape), _resident(w_b.shape), _resident(w_o.shape)],
        out_specs=rows(d),
        compiler_params=_params(("parallel", "parallel")),
        name="merge",
    )(h, za, zb, sg, gt, w_a, w_b, w_o)


def kernel(x, c, w_ada, b_ada, ffn1_norm, ffn1_w_in, ffn1_w_out, mix_norm, w_in, gla_alpha_w2, gla_alpha_b, gla_head_norm, diff_lq1, diff_lk1, diff_lq2, diff_lk2, diff_head_norm, w_branch_a, w_branch_b, w_out, ffn2_norm, ffn2_w_in, ffn2_w_out, final_norm):
    depth = w_ada.shape[0]
    bsz, _, d = x.shape
    kw = GLA_HEADS * GLA_DK
    vw = GLA_HEADS * GLA_DV
    ga_start = 2 * kw + 2 * vw
    h = x
    for l in range(depth):
        lam_init = 0.8 - 0.6 * math.exp(-0.3 * l)
        mod = _adaln(c, w_ada[l], b_ada[l])
        sh1, sc1, gt1, sh2, sc2, gt2, sh3, sc3, gt3 = [
            mod[:, i * d:(i + 1) * d].reshape(bsz, 1, d) for i in range(N_MOD)]
        last = l == depth - 1
        h = _ffn(h, sh1, sc1, gt1, ffn1_norm[l], ffn1_w_in[l].astype(BF16),
                 ffn1_w_out[l].astype(BF16), final_norm, final_norm=False)
        w = w_in[l]
        w_main = jnp.concatenate([w[:, :ga_start], w[:, ga_start + GLA_RANK:]], axis=1).astype(BF16)
        w_ga = w[:, ga_start:ga_start + GLA_RANK].astype(BF16)
        gq, gk, gv, gr, g, dq, dk, dv, sg = _mixer_proj(
            h, sh2, sc2, mix_norm[l], w_main, w_ga, gla_alpha_w2[l].astype(BF16), gla_alpha_b[l])
        za = _gla(gq, gk, gv, gr, g, gla_head_norm[l])
        lam_vecs = jnp.stack([diff_lq1[l], diff_lk1[l], diff_lq2[l], diff_lk2[l]])
        zb = _diff_attn(dq, dk, dv, lam_vecs, diff_head_norm[l], lam_init)
        h = _merge(h, za, zb, sg, gt2, w_branch_a[l].astype(BF16), w_branch_b[l].astype(BF16),
                   w_out[l].astype(BF16))
        h = _ffn(h, sh3, sc3, gt3, ffn2_norm[l], ffn2_w_in[l].astype(BF16),
                 ffn2_w_out[l].astype(BF16), final_norm, final_norm=last)
    return h
```

```python
import functools
import math

import jax
import jax.numpy as jnp
from jax import lax
from jax.experimental import pallas as pl
from jax.experimental.pallas import tpu as pltpu

F32 = jnp.float32
BF16 = jnp.bfloat16

EPS = 1e-6
GLA_HEADS = 4
GLA_DK = 64
GLA_DV = 128
GLA_RANK = 16
GLA_TAU = 16.0
DIFF_HEADS = 4
DIFF_DH = 64
DIFF_DV = 128
ALIBI_MAX_BIAS = 8.0
N_MOD = 9

LOG2E = math.log2(math.e)
LANES = 128
GLA_CHUNK = 128
GLA_SUB = 16
EXP_CLAMP = 80.0
VMEM_LIMIT = 56 * 1024 * 1024


def _params(sem):
    return pltpu.CompilerParams(dimension_semantics=sem, vmem_limit_bytes=VMEM_LIMIT)


def _resident(shape):
    nd = len(shape)
    return pl.BlockSpec(shape, lambda *_: (0,) * nd, pipeline_mode=pl.Buffered(1))


def _rmsnorm(x, g):
    return x * lax.rsqrt(jnp.mean(x * x, axis=-1, keepdims=True) + EPS) * g


def _adaln_kernel(c_ref, w_ref, b_ref, o_ref):
    c = c_ref[...]
    ca = (c * jax.nn.sigmoid(c)).astype(BF16)
    o_ref[...] = jnp.dot(ca, w_ref[...].astype(BF16), preferred_element_type=F32) + b_ref[...]


def _adaln(c, w_ada, b_ada):
    bsz, d = c.shape
    n = w_ada.shape[1]
    tn = n // 4
    return pl.pallas_call(
        _adaln_kernel,
        out_shape=jax.ShapeDtypeStruct((bsz, n), F32),
        grid=(n // tn,),
        in_specs=[
            pl.BlockSpec((bsz, d), lambda j: (0, 0)),
            pl.BlockSpec((d, tn), lambda j: (0, j)),
            pl.BlockSpec((1, tn), lambda j: (0, j)),
        ],
        out_specs=pl.BlockSpec((bsz, tn), lambda j: (0, j)),
        compiler_params=_params(("arbitrary",)),
        name="adaln",
    )(c, w_ada, b_ada.reshape(1, n))


def _ffn_kernel(x_ref, sh_ref, sc_ref, gt_ref, nw_ref, win_ref, wout_ref, fn_ref, o_ref,
                acc_ref, *, d_ff, tf, final_norm):
    x = x_ref[0]
    u = (_rmsnorm(x, nw_ref[...]) * (1.0 + sc_ref[0]) + sh_ref[0]).astype(BF16)
    for i in range(d_ff // tf):
        hg = jnp.dot(u, win_ref[:, i * tf:(i + 1) * tf], preferred_element_type=F32)
        hu = jnp.dot(u, win_ref[:, d_ff + i * tf:d_ff + (i + 1) * tf], preferred_element_type=F32)
        act = (hg * jax.nn.sigmoid(hg) * hu).astype(BF16)
        part = jnp.dot(act, wout_ref[i * tf:(i + 1) * tf, :], preferred_element_type=F32)
        if i == 0:
            acc_ref[...] = part
        else:
            acc_ref[...] += part
    h = x + (0.5 * gt_ref[0]) * acc_ref[...]
    if final_norm:
        h = _rmsnorm(h, fn_ref[...])
    o_ref[0] = h


def _ffn(x, sh, sc, gt, nw, w_in, w_out, fn, *, final_norm, tm=512, tf=256):
    bsz, s, d = x.shape
    d_ff = w_out.shape[0]
    vec = pl.BlockSpec((1, 1, d), lambda b, i: (b, 0, 0))
    row = pl.BlockSpec((1, tm, d), lambda b, i: (b, i, 0))
    return pl.pallas_call(
        functools.partial(_ffn_kernel, d_ff=d_ff, tf=tf, final_norm=final_norm),
        out_shape=jax.ShapeDtypeStruct((bsz, s, d), F32),
        grid=(bsz, s // tm),
        in_specs=[row, vec, vec, vec, _resident((1, d)), _resident((d, 2 * d_ff)),
                  _resident((d_ff, d)), _resident((1, d))],
        out_specs=row,
        scratch_shapes=[pltpu.VMEM((tm, d), F32)],
        compiler_params=_params(("parallel", "parallel")),
        name="ffn_final" if final_norm else "ffn",
    )(x, sh, sc, gt, nw.reshape(1, d), w_in, w_out, fn.reshape(1, d))


def _proj_kernel(x_ref, sh_ref, sc_ref, nw_ref, w_ref, wga_ref, w2_ref, b2_ref,
                 gq_ref, gk_ref, gv_ref, gr_ref, g_ref, dq_ref, dk_ref, dv_ref, sg_ref):
    x = x_ref[0]
    u = (_rmsnorm(x, nw_ref[...]) * (1.0 + sc_ref[0]) + sh_ref[0]).astype(BF16)

    def seg(start, size):
        return jnp.dot(u, w_ref[:, start:start + size], preferred_element_type=F32)

    kw = GLA_HEADS * GLA_DK
    vw = GLA_HEADS * GLA_DV
    qw = DIFF_HEADS * 2 * DIFF_DH
    dvw = DIFF_HEADS * DIFF_DV
    off = 0
    gq_ref[0] = (seg(off, kw) * (GLA_DK ** -0.5)).astype(BF16)
    off += kw
    gk_ref[0] = seg(off, kw).astype(BF16)
    off += kw
    gv_ref[0] = seg(off, vw).astype(BF16)
    off += vw
    r = seg(off, vw)
    gr_ref[0] = (r * jax.nn.sigmoid(r)).astype(BF16)
    off += vw
    dq_ref[0] = (seg(off, qw) * (DIFF_DH ** -0.5 * LOG2E)).astype(BF16)
    off += qw
    dk_ref[0] = seg(off, qw).astype(BF16)
    off += qw
    dv_ref[0] = seg(off, dvw).astype(BF16)
    off += dvw
    n_gate = sg_ref.shape[-1]
    step = 512
    for j in range(n_gate // step):
        sg_ref[0, :, j * step:(j + 1) * step] = jax.nn.sigmoid(seg(off + j * step, step)).astype(BF16)
    a_low = jnp.dot(u, wga_ref[...], preferred_element_type=F32).astype(BF16)
    z = jnp.dot(a_low, w2_ref[...], preferred_element_type=F32) + b2_ref[...]
    log_sig = jnp.minimum(z, 0.0) - jnp.log1p(jnp.exp(-jnp.abs(z)))
    g_ref[0] = log_sig * (1.0 / GLA_TAU)


def _mixer_proj(h, sh, sc, nw, w_main, w_ga, w2, b2, *, tm=512):
    bsz, s, d = h.shape
    kw = GLA_HEADS * GLA_DK
    vw = GLA_HEADS * GLA_DV
    qw = DIFF_HEADS * 2 * DIFF_DH
    dvw = DIFF_HEADS * DIFF_DV
    n_gate = 2 * d
    vec = pl.BlockSpec((1, 1, d), lambda b, i: (b, 0, 0))

    def rows(w):
        return pl.BlockSpec((1, tm, w), lambda b, i: (b, i, 0))

    widths = [(kw, BF16), (kw, BF16), (vw, BF16), (vw, BF16), (kw, F32),
              (qw, BF16), (qw, BF16), (dvw, BF16), (n_gate, BF16)]
    return pl.pallas_call(
        _proj_kernel,
        out_shape=[jax.ShapeDtypeStruct((bsz, s, w), dt) for w, dt in widths],
        grid=(bsz, s // tm),
        in_specs=[rows(d), vec, vec, _resident((1, d)), _resident(w_main.shape),
                  _resident(w_ga.shape), _resident(w2.shape), _resident((1, kw))],
        out_specs=[rows(w) for w, _ in widths],
        compiler_params=_params(("parallel", "parallel")),
        name="mixer_proj",
    )(h, sh, sc, nw.reshape(1, d), w_main, w_ga, w2, b2.reshape(1, kw))


def _gla_kernel(q_ref, k_ref, v_ref, r_ref, g_ref, hn_ref, o_ref, st_ref, *, chunk, sub):
    @pl.when(pl.program_id(1) == 0)
    def _():
        st_ref[...] = jnp.zeros_like(st_ref)

    nsub = chunk // sub
    row = lax.broadcasted_iota(jnp.int32, (chunk, chunk), 0)
    col = lax.broadcasted_iota(jnp.int32, (chunk, chunk), 1)
    causal = col <= row
    cum_mat = jnp.concatenate(
        [jnp.where(causal, 1.0, 0.0), jnp.where(causal & (col >= (row // sub) * sub), 1.0, 0.0)],
        axis=0).astype(BF16)
    row_blk = lax.broadcasted_iota(jnp.int32, (chunk, LANES), 0) // sub
    lane_head = lax.broadcasted_iota(jnp.int32, (chunk, LANES), 1) // GLA_DK

    for pair in range(GLA_HEADS // 2):
        lanes = slice(pair * LANES, (pair + 1) * LANES)
        g = g_ref[0, :, lanes]
        g1 = g.astype(BF16)
        e1 = g - g1.astype(F32)
        g2 = e1.astype(BF16)
        g3 = (e1 - g2.astype(F32)).astype(BF16)
        cs = jnp.dot(cum_mat, jnp.concatenate([g1, g2, g3], axis=1), preferred_element_type=F32)
        cs = cs[:, :LANES] + cs[:, LANES:2 * LANES] + cs[:, 2 * LANES:]
        b = cs[:chunk]
        w = cs[chunk:]
        b_last = b[chunk - 1:chunk]
        q = q_ref[0, :, lanes].astype(F32)
        k = k_ref[0, :, lanes].astype(F32)
        q_state = (q * jnp.exp(b)).astype(BF16)
        k_state = k * jnp.exp(b_last - b)
        k_hat = k * jnp.exp(jnp.minimum(-w, EXP_CLAMP))
        q_cat = []
        for j in range(nsub):
            ref_b = b[j * sub - 1:j * sub] if j else jnp.zeros_like(b_last)
            q_cat.append((q * jnp.exp(jnp.minimum(b - ref_b, 0.0))).astype(BF16))
        q_cat = jnp.concatenate(q_cat, axis=1)
        for hh in range(2):
            head = pair * 2 + hh
            in_head = lane_head == hh
            kh = jnp.where(in_head, k_hat, 0.0).astype(BF16)
            k_cat = jnp.concatenate(
                [jnp.where(row_blk == j, kh, jnp.zeros_like(kh)) for j in range(nsub)], axis=1)
            scores = lax.dot_general(q_cat, k_cat, (((1,), (1,)), ((), ())),
                                     preferred_element_type=F32)
            scores = jnp.where(causal, scores, 0.0).astype(BF16)
            vh = v_ref[0, :, head * GLA_DV:(head + 1) * GLA_DV]
            state_t = st_ref[head]
            o = jnp.dot(scores, vh, preferred_element_type=F32)
            o += lax.dot_general(q_state, state_t.astype(BF16), (((1,), (1,)), ((), ())),
                                 preferred_element_type=F32)
            ks = jnp.where(in_head, k_state, 0.0).astype(BF16)
            kv_t = lax.dot_general(vh, ks, (((0,), (0,)), ((), ())), preferred_element_type=F32)
            st_ref[head] = state_t * jnp.exp(b_last) + kv_t
            y = _rmsnorm(o, hn_ref[...]) * r_ref[0, :, head * GLA_DV:(head + 1) * GLA_DV].astype(F32)
            o_ref[0, :, head * GLA_DV:(head + 1) * GLA_DV] = y.astype(BF16)


def _gla(gq, gk, gv, gr, g, head_norm):
    bsz, s, kw = gq.shape
    vw = gv.shape[-1]
    chunk = GLA_CHUNK

    def rows(w):
        return pl.BlockSpec((1, chunk, w), lambda b, c: (b, c, 0))

    return pl.pallas_call(
        functools.partial(_gla_kernel, chunk=chunk, sub=GLA_SUB),
        out_shape=jax.ShapeDtypeStruct((bsz, s, vw), BF16),
        grid=(bsz, s // chunk),
        in_specs=[rows(kw), rows(kw), rows(vw), rows(vw), rows(kw), _resident((1, GLA_DV))],
        out_specs=rows(vw),
        scratch_shapes=[pltpu.VMEM((GLA_HEADS, GLA_DV, LANES), F32)],
        compiler_params=_params(("parallel", "arbitrary")),
        name="gla",
    )(gq, gk, gv, gr, g, head_norm.reshape(1, GLA_DV))


def _attn_kernel(q_ref, k_ref, v_ref, slope_ref, lam_ref, hn_ref, o_ref, kk_ref, qq_ref, s_ref,
                 p_ref, acc_ref, *, tq, rc, lam_init):
    qi = pl.program_id(2)
    seq = k_ref.shape[1]
    c = slope_ref[0] * LOG2E
    c1 = c.astype(BF16).astype(F32)
    c2 = (c - c1).astype(BF16).astype(F32)
    c3 = c - c1 - c2

    @pl.when(qi == 0)
    def _():
        kk_ref[:, :LANES] = k_ref[0]
        pos = lax.broadcasted_iota(jnp.int32, (seq, LANES), 0) & (tq - 1)
        lane = lax.broadcasted_iota(jnp.int32, (seq, LANES), 1)
        feat = jnp.where(lane < 3, pos & ~15, jnp.where(lane < 6, pos & 15, 0))
        kk_ref[:, LANES:] = feat.astype(F32).astype(BF16)
        sub = lax.broadcasted_iota(jnp.int32, (LANES, 2 * tq), 0)
        cf = jnp.where((sub == 0) | (sub == 3), c1, jnp.where((sub == 1) | (sub == 4), c2, c3))
        qq_ref[LANES:, :] = jnp.where(sub < 6, cf, 0.0).astype(BF16)

    q_t = q_ref[0].astype(F32).T
    sub = lax.broadcasted_iota(jnp.int32, (LANES, tq), 0)
    qq_ref[:LANES, :tq] = jnp.where(sub < DIFF_DH, q_t, 0.0).astype(BF16)
    qq_ref[:LANES, tq:] = jnp.where(sub >= DIFF_DH, q_t, 0.0).astype(BF16)
    acc_ref[...] = jnp.zeros_like(acc_ref)
    nchunk = tq // rc

    def step(j, m_old, l_old, masked):
        start = pl.multiple_of(j * tq, tq)
        s_ref[...] = jnp.dot(kk_ref[pl.ds(start, tq), :], qq_ref[...], preferred_element_type=F32)
        shift = c * ((qi - j) * tq).astype(F32)

        def load(r):
            rows = pl.ds(pl.multiple_of(r * rc, rc), rc)
            blk = s_ref[rows, :]
            if masked:
                key = r * rc + lax.broadcasted_iota(jnp.int32, (rc, 2 * tq), 0)
                qry = lax.broadcasted_iota(jnp.int32, (rc, 2 * tq), 1) & (tq - 1)
                blk = jnp.where(key <= qry, blk, -jnp.inf)
            return rows, blk

        def col_max(r, mx):
            _, blk = load(r)
            return jnp.maximum(mx, jnp.max(blk.reshape(rc // 8, 8, 2 * tq), axis=0))

        mx = lax.fori_loop(0, nchunk, col_max, jnp.full((8, 2 * tq), -jnp.inf, F32))
        m_new = jnp.maximum(m_old, jnp.max(mx, axis=0, keepdims=True) - shift)
        alpha = jnp.exp2(m_old - m_new)
        m_shift = m_new + shift

        def probs(r, ls):
            rows, blk = load(r)
            p = jnp.exp2(blk - m_shift)
            p_ref[rows, :] = p.astype(BF16)
            return ls + jnp.sum(p.reshape(rc // 8, 8, 2 * tq), axis=0)

        ls = lax.fori_loop(0, nchunk, probs, jnp.zeros((8, 2 * tq), F32))
        l_new = alpha * l_old + jnp.sum(ls, axis=0, keepdims=True)
        pv = lax.dot_general(v_ref[0, pl.ds(start, tq), :], p_ref[...], (((0,), (0,)), ((), ())),
                             preferred_element_type=F32)
        acc_ref[...] = alpha * acc_ref[...] + pv
        return m_new, l_new

    m, l = lax.fori_loop(
        0, qi, lambda j, ml: step(j, ml[0], ml[1], False),
        (jnp.full((1, 2 * tq), -jnp.inf, F32), jnp.zeros((1, 2 * tq), F32)))
    m, l = step(qi, m, l, True)

    lam = lam_ref[...]
    lam = (jnp.exp(jnp.sum(lam[0:1] * lam[1:2], axis=-1, keepdims=True))
           - jnp.exp(jnp.sum(lam[2:3] * lam[3:4], axis=-1, keepdims=True)) + lam_init)
    out = acc_ref[...] * (1.0 / l)
    y_t = out[:, :tq] - lam * out[:, tq:]
    y_t = y_t * lax.rsqrt(jnp.mean(y_t * y_t, axis=0, keepdims=True) + EPS)
    o_ref[0] = (y_t.T * hn_ref[...] * (1.0 - lam_init)).astype(BF16)


def _diff_attn(dq, dk, dv, lam_vecs, head_norm, lam_init, *, tq=512, rc=32):
    bsz, s, _ = dq.shape
    slopes = jnp.asarray(
        [2.0 ** (-ALIBI_MAX_BIAS * (i + 1) / DIFF_HEADS) for i in range(DIFF_HEADS)], F32
    ).reshape(DIFF_HEADS, 1, 1)
    return pl.pallas_call(
        functools.partial(_attn_kernel, tq=tq, rc=rc, lam_init=lam_init),
        out_shape=jax.ShapeDtypeStruct((bsz, s, DIFF_HEADS * DIFF_DV), BF16),
        grid=(bsz, DIFF_HEADS, s // tq),
        in_specs=[
            pl.BlockSpec((1, tq, LANES), lambda b, h, i: (b, i, h)),
            pl.BlockSpec((1, s, LANES), lambda b, h, i: (b, 0, h)),
            pl.BlockSpec((1, s, DIFF_DV), lambda b, h, i: (b, 0, h)),
            pl.BlockSpec((1, 1, 1), lambda b, h, i: (h, 0, 0)),
            _resident(lam_vecs.shape),
            _resident((1, DIFF_DV)),
        ],
        out_specs=pl.BlockSpec((1, tq, DIFF_DV), lambda b, h, i: (b, i, h)),
        scratch_shapes=[
            pltpu.VMEM((s, 2 * LANES), BF16),
            pltpu.VMEM((2 * LANES, 2 * tq), BF16),
            pltpu.VMEM((tq, 2 * tq), F32),
            pltpu.VMEM((tq, 2 * tq), BF16),
            pltpu.VMEM((DIFF_DV, 2 * tq), F32),
        ],
        compiler_params=_params(("parallel", "parallel", "arbitrary")),
        name="diff_attn",
    )(dq, dk, dv, slopes, lam_vecs, head_norm.reshape(1, DIFF_DV))


def _merge_kernel(h_ref, za_ref, zb_ref, sg_ref, gt_ref, wa_ref, wb_ref, wo_ref, o_ref):
    d = h_ref.shape[-1]
    ya = jnp.dot(za_ref[0], wa_ref[...], preferred_element_type=F32)
    yb = jnp.dot(zb_ref[0], wb_ref[...], preferred_element_type=F32)
    mix = sg_ref[0, :, :d].astype(F32) * ya + sg_ref[0, :, d:].astype(F32) * yb
    m = jnp.dot(mix.astype(BF16), wo_ref[...], preferred_element_type=F32)
    o_ref[0] = h_ref[0] + gt_ref[0] * m


def _merge(h, za, zb, sg, gt, w_a, w_b, w_o, *, tm=512):
    bsz, s, d = h.shape
    vec = pl.BlockSpec((1, 1, d), lambda b, i: (b, 0, 0))

    def rows(w):
        return pl.BlockSpec((1, tm, w), lambda b, i: (b, i, 0))

    return pl.pallas_call(
        _merge_kernel,
        out_shape=jax.ShapeDtypeStruct((bsz, s, d), F32),
        grid=(bsz, s // tm),
        in_specs=[rows(d), rows(za.shape[-1]), rows(zb.shape[-1]), rows(2 * d), vec,
                  _resident(w_a.shape), _resident(w_b.shape), _resident(w_o.shape)],
        out_specs=rows(d),
        compiler_params=_params(("parallel", "parallel")),
        name="merge",
    )(h, za, zb, sg, gt, w_a, w_b, w_o)


def kernel(x, c, w_ada, b_ada, ffn1_norm, ffn1_w_in, ffn1_w_out, mix_norm, w_in, gla_alpha_w2, gla_alpha_b, gla_head_norm, diff_lq1, diff_lk1, diff_lq2, diff_lk2, diff_head_norm, w_branch_a, w_branch_b, w_out, ffn2_norm, ffn2_w_in, ffn2_w_out, final_norm):
    depth = w_ada.shape[0]
    bsz, _, d = x.shape
    kw = GLA_HEADS * GLA_DK
    vw = GLA_HEADS * GLA_DV
    ga_start = 2 * kw + 2 * vw
    h = x
    for l in range(depth):
        lam_init = 0.8 - 0.6 * math.exp(-0.3 * l)
        mod = _adaln(c, w_ada[l], b_ada[l])
        sh1, sc1, gt1, sh2, sc2, gt2, sh3, sc3, gt3 = [
            mod[:, i * d:(i + 1) * d].reshape(bsz, 1, d) for i in range(N_MOD)]
        last = l == depth - 1
        h = _ffn(h, sh1, sc1, gt1, ffn1_norm[l], ffn1_w_in[l].astype(BF16),
                 ffn1_w_out[l].astype(BF16), final_norm, final_norm=False)
        w = w_in[l]
        w_main = jnp.concatenate([w[:, :ga_start], w[:, ga_start + GLA_RANK:]], axis=1).astype(BF16)
        w_ga = w[:, ga_start:ga_start + GLA_RANK].astype(BF16)
        gq, gk, gv, gr, g, dq, dk, dv, sg = _mixer_proj(
            h, sh2, sc2, mix_norm[l], w_main, w_ga, gla_alpha_w2[l].astype(BF16), gla_alpha_b[l])
        za = _gla(gq, gk, gv, gr, g, gla_head_norm[l])
        lam_vecs = jnp.stack([diff_lq1[l], diff_lk1[l], diff_lq2[l], diff_lk2[l]])
        zb = _diff_attn(dq, dk, dv, lam_vecs, diff_head_norm[l], lam_init)
        h = _merge(h, za, zb, sg, gt2, w_branch_a[l].astype(BF16), w_branch_b[l].astype(BF16),
                   w_out[l].astype(BF16))
        h = _ffn(h, sh3, sc3, gt3, ffn2_norm[l], ffn2_w_in[l].astype(BF16),
                 ffn2_w_out[l].astype(BF16), final_norm, final_norm=last)
    return h
```

```python
import functools
import math

import jax
import jax.numpy as jnp
from jax import lax
from jax.experimental import pallas as pl
from jax.experimental.pallas import tpu as pltpu

F32 = jnp.float32
BF16 = jnp.bfloat16

EPS = 1e-6
GLA_HEADS = 4
GLA_DK = 64
GLA_DV = 128
GLA_RANK = 16
GLA_TAU = 16.0
DIFF_HEADS = 4
DIFF_DH = 64
DIFF_DV = 128
ALIBI_MAX_BIAS = 8.0
N_MOD = 9

LOG2E = math.log2(math.e)
LANES = 128
GLA_CHUNK = 128
GLA_SUB = 16
EXP_CLAMP = 80.0
VMEM_LIMIT = 56 * 1024 * 1024


def _params(sem):
    return pltpu.CompilerParams(dimension_semantics=sem, vmem_limit_bytes=VMEM_LIMIT)


def _resident(shape):
    nd = len(shape)
    return pl.BlockSpec(shape, lambda *_: (0,) * nd, pipeline_mode=pl.Buffered(1))


def _rmsnorm(x, g):
    return x * lax.rsqrt(jnp.mean(x * x, axis=-1, keepdims=True) + EPS) * g


def _adaln_kernel(c_ref, w_ref, b_ref, o_ref):
    c = c_ref[...]
    ca = (c * jax.nn.sigmoid(c)).astype(BF16)
    o_ref[...] = jnp.dot(ca, w_ref[...].astype(BF16), preferred_element_type=F32) + b_ref[...]


def _adaln(c, w_ada, b_ada):
    bsz, d = c.shape
    n = w_ada.shape[1]
    tn = n // 4
    return pl.pallas_call(
        _adaln_kernel,
        out_shape=jax.ShapeDtypeStruct((bsz, n), F32),
        grid=(n // tn,),
        in_specs=[
            pl.BlockSpec((bsz, d), lambda j: (0, 0)),
            pl.BlockSpec((d, tn), lambda j: (0, j)),
            pl.BlockSpec((1, tn), lambda j: (0, j)),
        ],
        out_specs=pl.BlockSpec((bsz, tn), lambda j: (0, j)),
        compiler_params=_params(("arbitrary",)),
        name="adaln",
    )(c, w_ada, b_ada.reshape(1, n))


def _ffn_kernel(x_ref, sh_ref, sc_ref, gt_ref, nw_ref, win_ref, wout_ref, fn_ref, o_ref,
                acc_ref, *, d_ff, tf, final_norm):
    x = x_ref[0]
    u = (_rmsnorm(x, nw_ref[...]) * (1.0 + sc_ref[0]) + sh_ref[0]).astype(BF16)
    for i in range(d_ff // tf):
        hg = jnp.dot(u, win_ref[:, i * tf:(i + 1) * tf], preferred_element_type=F32)
        hu = jnp.dot(u, win_ref[:, d_ff + i * tf:d_ff + (i + 1) * tf], preferred_element_type=F32)
        act = (hg * jax.nn.sigmoid(hg) * hu).astype(BF16)
        part = jnp.dot(act, wout_ref[i * tf:(i + 1) * tf, :], preferred_element_type=F32)
        if i == 0:
            acc_ref[...] = part
        else:
            acc_ref[...] += part
    h = x + (0.5 * gt_ref[0]) * acc_ref[...]
    if final_norm:
        h = _rmsnorm(h, fn_ref[...])
    o_ref[0] = h


def _ffn(x, sh, sc, gt, nw, w_in, w_out, fn, *, final_norm, tm=512, tf=256):
    bsz, s, d = x.shape
    d_ff = w_out.shape[0]
    vec = pl.BlockSpec((1, 1, d), lambda b, i: (b, 0, 0))
    row = pl.BlockSpec((1, tm, d), lambda b, i: (b, i, 0))
    return pl.pallas_call(
        functools.partial(_ffn_kernel, d_ff=d_ff, tf=tf, final_norm=final_norm),
        out_shape=jax.ShapeDtypeStruct((bsz, s, d), F32),
        grid=(bsz, s // tm),
        in_specs=[row, vec, vec, vec, _resident((1, d)), _resident((d, 2 * d_ff)),
                  _resident((d_ff, d)), _resident((1, d))],
        out_specs=row,
        scratch_shapes=[pltpu.VMEM((tm, d), F32)],
        compiler_params=_params(("parallel", "parallel")),
        name="ffn_final" if final_norm else "ffn",
    )(x, sh, sc, gt, nw.reshape(1, d), w_in, w_out, fn.reshape(1, d))


def _proj_kernel(x_ref, sh_ref, sc_ref, nw_ref, w_ref, wga_ref, w2_ref, b2_ref,
                 gq_ref, gk_ref, gv_ref, gr_ref, g_ref, dq_ref, dk_ref, dv_ref, sg_ref):
    x = x_ref[0]
    u = (_rmsnorm(x, nw_ref[...]) * (1.0 + sc_ref[0]) + sh_ref[0]).astype(BF16)

    def seg(start, size):
        return jnp.dot(u, w_ref[:, start:start + size], preferred_element_type=F32)

    kw = GLA_HEADS * GLA_DK
    vw = GLA_HEADS * GLA_DV
    qw = DIFF_HEADS * 2 * DIFF_DH
    dvw = DIFF_HEADS * DIFF_DV
    off = 0
    gq_ref[0] = (seg(off, kw) * (GLA_DK ** -0.5)).astype(BF16)
    off += kw
    gk_ref[0] = seg(off, kw).astype(BF16)
    off += kw
    gv_ref[0] = seg(off, vw).astype(BF16)
    off += vw
    r = seg(off, vw)
    gr_ref[0] = (r * jax.nn.sigmoid(r)).astype(BF16)
    off += vw
    dq_ref[0] = (seg(off, qw) * (DIFF_DH ** -0.5 * LOG2E)).astype(BF16)
    off += qw
    dk_ref[0] = seg(off, qw).astype(BF16)
    off += qw
    dv_ref[0] = seg(off, dvw).astype(BF16)
    off += dvw
    n_gate = sg_ref.shape[-1]
    step = 512
    for j in range(n_gate // step):
        sg_ref[0, :, j * step:(j + 1) * step] = jax.nn.sigmoid(seg(off + j * step, step)).astype(BF16)
    a_low = jnp.dot(u, wga_ref[...], preferred_element_type=F32).astype(BF16)
    z = jnp.dot(a_low, w2_ref[...], preferred_element_type=F32) + b2_ref[...]
    log_sig = jnp.minimum(z, 0.0) - jnp.log1p(jnp.exp(-jnp.abs(z)))
    g_ref[0] = log_sig * (1.0 / GLA_TAU)


def _mixer_proj(h, sh, sc, nw, w_main, w_ga, w2, b2, *, tm=512):
    bsz, s, d = h.shape
    kw = GLA_HEADS * GLA_DK
    vw = GLA_HEADS * GLA_DV
    qw = DIFF_HEADS * 2 * DIFF_DH
    dvw = DIFF_HEADS * DIFF_DV
    n_gate = 2 * d
    vec = pl.BlockSpec((1, 1, d), lambda b, i: (b, 0, 0))

    def rows(w):
        return pl.BlockSpec((1, tm, w), lambda b, i: (b, i, 0))

    widths = [(kw, BF16), (kw, BF16), (vw, BF16), (vw, BF16), (kw, F32),
              (qw, BF16), (qw, BF16), (dvw, BF16), (n_gate, BF16)]
    return pl.pallas_call(
        _proj_kernel,
        out_shape=[jax.ShapeDtypeStruct((bsz, s, w), dt) for w, dt in widths],
        grid=(bsz, s // tm),
        in_specs=[rows(d), vec, vec, _resident((1, d)), _resident(w_main.shape),
                  _resident(w_ga.shape), _resident(w2.shape), _resident((1, kw))],
        out_specs=[rows(w) for w, _ in widths],
        compiler_params=_params(("parallel", "parallel")),
        name="mixer_proj",
    )(h, sh, sc, nw.reshape(1, d), w_main, w_ga, w2, b2.reshape(1, kw))


def _gla_kernel(q_ref, k_ref, v_ref, r_ref, g_ref, hn_ref, o_ref, st_ref, *, chunk, sub):
    @pl.when(pl.program_id(1) == 0)
    def _():
        st_ref[...] = jnp.zeros_like(st_ref)

    nsub = chunk // sub
    row = lax.broadcasted_iota(jnp.int32, (chunk, chunk), 0)
    col = lax.broadcasted_iota(jnp.int32, (chunk, chunk), 1)
    causal = col <= row
    cum_mat = jnp.concatenate(
        [jnp.where(causal, 1.0, 0.0), jnp.where(causal & (col >= (row // sub) * sub), 1.0, 0.0)],
        axis=0).astype(BF16)
    row_blk = lax.broadcasted_iota(jnp.int32, (chunk, LANES), 0) // sub
    lane_head = lax.broadcasted_iota(jnp.int32, (chunk, LANES), 1) // GLA_DK

    for pair in range(GLA_HEADS // 2):
        lanes = slice(pair * LANES, (pair + 1) * LANES)
        g = g_ref[0, :, lanes]
        g1 = g.astype(BF16)
        e1 = g - g1.astype(F32)
        g2 = e1.astype(BF16)
        g3 = (e1 - g2.astype(F32)).astype(BF16)
        cs = jnp.dot(cum_mat, jnp.concatenate([g1, g2, g3], axis=1), preferred_element_type=F32)
        cs = cs[:, :LANES] + cs[:, LANES:2 * LANES] + cs[:, 2 * LANES:]
        b = cs[:chunk]
        w = cs[chunk:]
        b_last = b[chunk - 1:chunk]
        q = q_ref[0, :, lanes].astype(F32)
        k = k_ref[0, :, lanes].astype(F32)
        q_state = (q * jnp.exp(b)).astype(BF16)
        k_state = k * jnp.exp(b_last - b)
        k_hat = k * jnp.exp(jnp.minimum(-w, EXP_CLAMP))
        q_cat = []
        for j in range(nsub):
            ref_b = b[j * sub - 1:j * sub] if j else jnp.zeros_like(b_last)
            q_cat.append((q * jnp.exp(jnp.minimum(b - ref_b, 0.0))).astype(BF16))
        q_cat = jnp.concatenate(q_cat, axis=1)
        for hh in range(2):
            head = pair * 2 + hh
            in_head = lane_head == hh
            kh = jnp.where(in_head, k_hat, 0.0).astype(BF16)
            k_cat = jnp.concatenate(
                [jnp.where(row_blk == j, kh, jnp.zeros_like(kh)) for j in range(nsub)], axis=1)
            scores = lax.dot_general(q_cat, k_cat, (((1,), (1,)), ((), ())),
                                     preferred_element_type=F32)
            scores = jnp.where(causal, scores, 0.0).astype(BF16)
            vh = v_ref[0, :, head * GLA_DV:(head + 1) * GLA_DV]
            state_t = st_ref[head]
            o = jnp.dot(scores, vh, preferred_element_type=F32)
            o += lax.dot_general(q_state, state_t.astype(BF16), (((1,), (1,)), ((), ())),
                                 preferred_element_type=F32)
            ks = jnp.where(in_head, k_state, 0.0).astype(BF16)
            kv_t = lax.dot_general(vh, ks, (((0,), (0,)), ((), ())), preferred_element_type=F32)
            st_ref[head] = state_t * jnp.exp(b_last) + kv_t
            y = _rmsnorm(o, hn_ref[...]) * r_ref[0, :, head * GLA_DV:(head + 1) * GLA_DV].astype(F32)
            o_ref[0, :, head * GLA_DV:(head + 1) * GLA_DV] = y.astype(BF16)


def _gla(gq, gk, gv, gr, g, head_norm):
    bsz, s, kw = gq.shape
    vw = gv.shape[-1]
    chunk = GLA_CHUNK

    def rows(w):
        return pl.BlockSpec((1, chunk, w), lambda b, c: (b, c, 0))

    return pl.pallas_call(
        functools.partial(_gla_kernel, chunk=chunk, sub=GLA_SUB),
        out_shape=jax.ShapeDtypeStruct((bsz, s, vw), BF16),
        grid=(bsz, s // chunk),
        in_specs=[rows(kw), rows(kw), rows(vw), rows(vw), rows(kw), _resident((1, GLA_DV))],
        out_specs=rows(vw),
        scratch_shapes=[pltpu.VMEM((GLA_HEADS, GLA_DV, LANES), F32)],
        compiler_params=_params(("parallel", "arbitrary")),
        name="gla",
    )(gq, gk, gv, gr, g, head_norm.reshape(1, GLA_DV))


def _attn_kernel(q_ref, k_ref, v_ref, slope_ref, lam_ref, hn_ref, o_ref, kk_ref, qq_ref, s0_ref,
                 s1_ref, m_ref, l_ref, acc_ref, *, tq, rc, lam_init):
    qi = pl.program_id(2)
    seq = k_ref.shape[1]
    c = slope_ref[0] * LOG2E
    c1 = c.astype(BF16).astype(F32)
    c2 = (c - c1).astype(BF16).astype(F32)
    c3 = c - c1 - c2

    @pl.when(qi == 0)
    def _():
        kk_ref[:, :LANES] = k_ref[0]
        pos = lax.broadcasted_iota(jnp.int32, (seq, LANES), 0) & (tq - 1)
        lane = lax.broadcasted_iota(jnp.int32, (seq, LANES), 1)
        feat = jnp.where(lane < 3, pos & ~15, jnp.where(lane < 6, pos & 15, 0))
        kk_ref[:, LANES:] = feat.astype(F32).astype(BF16)
        sub = lax.broadcasted_iota(jnp.int32, (LANES, 2 * tq), 0)
        cf = jnp.where((sub == 0) | (sub == 3), c1, jnp.where((sub == 1) | (sub == 4), c2, c3))
        qq_ref[LANES:, :] = jnp.where(sub < 6, cf, 0.0).astype(BF16)

    q_t = q_ref[0].astype(F32).T
    sub = lax.broadcasted_iota(jnp.int32, (LANES, tq), 0)
    qq_ref[:LANES, :tq] = jnp.where(sub < DIFF_DH, q_t, 0.0).astype(BF16)
    qq_ref[:LANES, tq:] = jnp.where(sub >= DIFF_DH, q_t, 0.0).astype(BF16)
    acc_ref[...] = jnp.zeros_like(acc_ref)
    nchunk = tq // rc

    def logits(j, s_blk):
        start = pl.multiple_of(j * tq, tq)
        s_blk[...] = jnp.dot(kk_ref[pl.ds(start, tq), :], qq_ref[...], preferred_element_type=F32)

    def softmax_pv(j, s_blk, masked):
        m_old = m_ref[...]
        l_old = l_ref[...]
        start = pl.multiple_of(j * tq, tq)
        shift = c * ((qi - j) * tq).astype(F32)

        def load(r):
            blk = s_blk[r * rc:(r + 1) * rc, :]
            if masked:
                key = r * rc + lax.broadcasted_iota(jnp.int32, (rc, 2 * tq), 0)
                qry = lax.broadcasted_iota(jnp.int32, (rc, 2 * tq), 1) & (tq - 1)
                blk = jnp.where(key <= qry, blk, -jnp.inf)
            return blk

        mx = None
        for r in range(nchunk):
            cm = jnp.max(load(r).reshape(rc // 8, 8, 2 * tq), axis=0)
            mx = cm if mx is None else jnp.maximum(mx, cm)
        m_new = jnp.maximum(m_old, jnp.max(mx, axis=0, keepdims=True) - shift)
        alpha = jnp.exp2(m_old - m_new)
        m_shift = m_new + shift
        ls = None
        probs = []
        for r in range(nchunk):
            p = jnp.exp2(load(r) - m_shift)
            ps = jnp.sum(p.reshape(rc // 8, 8, 2 * tq), axis=0)
            ls = ps if ls is None else ls + ps
            probs.append(p.astype(BF16))
        l_ref[...] = alpha * l_old + jnp.sum(ls, axis=0, keepdims=True)
        m_ref[...] = m_new
        pv = lax.dot_general(v_ref[0, pl.ds(start, tq), :], jnp.concatenate(probs, axis=0),
                             (((0,), (0,)), ((), ())), preferred_element_type=F32)
        acc_ref[...] = alpha * acc_ref[...] + pv

    def two_blocks(u, carry):
        j = 2 * u
        logits(j + 1, s1_ref)
        softmax_pv(j, s0_ref, False)
        logits(j + 2, s0_ref)
        softmax_pv(j + 1, s1_ref, False)
        return carry

    m_ref[...] = jnp.full_like(m_ref, -jnp.inf)
    l_ref[...] = jnp.zeros_like(l_ref)
    logits(0, s0_ref)
    lax.fori_loop(0, qi // 2, two_blocks, 0)

    @pl.when(qi % 2 == 0)
    def _():
        softmax_pv(qi, s0_ref, True)

    @pl.when(qi % 2 == 1)
    def _():
        logits(qi, s1_ref)
        softmax_pv(qi - 1, s0_ref, False)
        softmax_pv(qi, s1_ref, True)

    lam = lam_ref[...]
    lam = (jnp.exp(jnp.sum(lam[0:1] * lam[1:2], axis=-1, keepdims=True))
           - jnp.exp(jnp.sum(lam[2:3] * lam[3:4], axis=-1, keepdims=True)) + lam_init)
    out = acc_ref[...] * (1.0 / l_ref[...])
    y_t = out[:, :tq] - lam * out[:, tq:]
    y_t = y_t * lax.rsqrt(jnp.mean(y_t * y_t, axis=0, keepdims=True) + EPS)
    o_ref[0] = (y_t.T * hn_ref[...] * (1.0 - lam_init)).astype(BF16)


def _diff_attn(dq, dk, dv, lam_vecs, head_norm, lam_init, *, tq=512, rc=128):
    bsz, s, _ = dq.shape
    slopes = jnp.asarray(
        [2.0 ** (-ALIBI_MAX_BIAS * (i + 1) / DIFF_HEADS) for i in range(DIFF_HEADS)], F32
    ).reshape(DIFF_HEADS, 1, 1)
    return pl.pallas_call(
        functools.partial(_attn_kernel, tq=tq, rc=rc, lam_init=lam_init),
        out_shape=jax.ShapeDtypeStruct((bsz, s, DIFF_HEADS * DIFF_DV), BF16),
        grid=(bsz, DIFF_HEADS, s // tq),
        in_specs=[
            pl.BlockSpec((1, tq, LANES), lambda b, h, i: (b, i, h)),
            pl.BlockSpec((1, s, LANES), lambda b, h, i: (b, 0, h)),
            pl.BlockSpec((1, s, DIFF_DV), lambda b, h, i: (b, 0, h)),
            pl.BlockSpec((1, 1, 1), lambda b, h, i: (h, 0, 0)),
            _resident(lam_vecs.shape),
            _resident((1, DIFF_DV)),
        ],
        out_specs=pl.BlockSpec((1, tq, DIFF_DV), lambda b, h, i: (b, i, h)),
        scratch_shapes=[
            pltpu.VMEM((s, 2 * LANES), BF16),
            pltpu.VMEM((2 * LANES, 2 * tq), BF16),
            pltpu.VMEM((tq, 2 * tq), F32),
            pltpu.VMEM((tq, 2 * tq), F32),
            pltpu.VMEM((1, 2 * tq), F32),
            pltpu.VMEM((1, 2 * tq), F32),
            pltpu.VMEM((DIFF_DV, 2 * tq), F32),
        ],
        compiler_params=_params(("parallel", "parallel", "arbitrary")),
        name="diff_attn",
    )(dq, dk, dv, slopes, lam_vecs, head_norm.reshape(1, DIFF_DV))


def _merge_kernel(h_ref, za_ref, zb_ref, sg_ref, gt_ref, wa_ref, wb_ref, wo_ref, o_ref):
    d = h_ref.shape[-1]
    ya = jnp.dot(za_ref[0], wa_ref[...], preferred_element_type=F32)
    yb = jnp.dot(zb_ref[0], wb_ref[...], preferred_element_type=F32)
    mix = sg_ref[0, :, :d].astype(F32) * ya + sg_ref[0, :, d:].astype(F32) * yb
    m = jnp.dot(mix.astype(BF16), wo_ref[...], preferred_element_type=F32)
    o_ref[0] = h_ref[0] + gt_ref[0] * m


def _merge(h, za, zb, sg, gt, w_a, w_b, w_o, *, tm=512):
    bsz, s, d = h.shape
    vec = pl.BlockSpec((1, 1, d), lambda b, i: (b, 0, 0))

    def rows(w):
        return pl.BlockSpec((1, tm, w), lambda b, i: (b, i, 0))

    return pl.pallas_call(
        _merge_kernel,
        out_shape=jax.ShapeDtypeStruct((bsz, s, d), F32),
        grid=(bsz, s // tm),
        in_specs=[rows(d), rows(za.shape[-1]), rows(zb.shape[-1]), rows(2 * d), vec,
                  _resident(w_a.shape), _resident(w_b.shape), _resident(w_o.shape)],
        out_specs=rows(d),
        compiler_params=_params(("parallel", "parallel")),
        name="merge",
    )(h, za, zb, sg, gt, w_a, w_b, w_o)


def kernel(x, c, w_ada, b_ada, ffn1_norm, ffn1_w_in, ffn1_w_out, mix_norm, w_in, gla_alpha_w2, gla_alpha_b, gla_head_norm, diff_lq1, diff_lk1, diff_lq2, diff_lk2, diff_head_norm, w_branch_a, w_branch_b, w_out, ffn2_norm, ffn2_w_in, ffn2_w_out, final_norm):
    depth = w_ada.shape[0]
    bsz, _, d = x.shape
    kw = GLA_HEADS * GLA_DK
    vw = GLA_HEADS * GLA_DV
    ga_start = 2 * kw + 2 * vw
    h = x
    for l in range(depth):
        lam_init = 0.8 - 0.6 * math.exp(-0.3 * l)
        mod = _adaln(c, w_ada[l], b_ada[l])
        sh1, sc1, gt1, sh2, sc2, gt2, sh3, sc3, gt3 = [
            mod[:, i * d:(i + 1) * d].reshape(bsz, 1, d) for i in range(N_MOD)]
        last = l == depth - 1
        h = _ffn(h, sh1, sc1, gt1, ffn1_norm[l], ffn1_w_in[l].astype(BF16),
                 ffn1_w_out[l].astype(BF16), final_norm, final_norm=False)
        w = w_in[l]
        w_main = jnp.concatenate([w[:, :ga_start], w[:, ga_start + GLA_RANK:]], axis=1).astype(BF16)
        w_ga = w[:, ga_start:ga_start + GLA_RANK].astype(BF16)
        gq, gk, gv, gr, g, dq, dk, dv, sg = _mixer_proj(
            h, sh2, sc2, mix_norm[l], w_main, w_ga, gla_alpha_w2[l].astype(BF16), gla_alpha_b[l])
        za = _gla(gq, gk, gv, gr, g, gla_head_norm[l])
        lam_vecs = jnp.stack([diff_lq1[l], diff_lk1[l], diff_lq2[l], diff_lk2[l]])
        zb = _diff_attn(dq, dk, dv, lam_vecs, diff_head_norm[l], lam_init)
        h = _merge(h, za, zb, sg, gt2, w_branch_a[l].astype(BF16), w_branch_b[l].astype(BF16),
                   w_out[l].astype(BF16))
        h = _ffn(h, sh3, sc3, gt3, ffn2_norm[l], ffn2_w_in[l].astype(BF16),
                 ffn2_w_out[l].astype(BF16), final_norm, final_norm=last)
    return h
```

```python
import functools
import itertools
import math

import jax
import jax.numpy as jnp
from jax import lax
from jax.experimental import pallas as pl
from jax.experimental.pallas import tpu as pltpu

F32 = jnp.float32
BF16 = jnp.bfloat16

EPS = 1e-6
GLA_HEADS = 4
GLA_DK = 64
GLA_DV = 128
GLA_RANK = 16
GLA_TAU = 16.0
DIFF_HEADS = 4
DIFF_DH = 64
DIFF_DV = 128
ALIBI_MAX_BIAS = 8.0
N_MOD = 9

LOG2E = math.log2(math.e)
LANES = 128
GLA_CHUNK = 128
GLA_SUB = 16
EXP_CLAMP = 80.0
VMEM_LIMIT = 56 * 1024 * 1024


def _params(sem):
    return pltpu.CompilerParams(dimension_semantics=sem, vmem_limit_bytes=VMEM_LIMIT)


def _resident(shape):
    nd = len(shape)
    return pl.BlockSpec(shape, lambda *_: (0,) * nd, pipeline_mode=pl.Buffered(1))


def _rmsnorm(x, g):
    return x * lax.rsqrt(jnp.mean(x * x, axis=-1, keepdims=True) + EPS) * g


def _adaln_kernel(c_ref, w_ref, b_ref, o_ref):
    c = c_ref[...]
    ca = (c * jax.nn.sigmoid(c)).astype(BF16)
    o_ref[...] = jnp.dot(ca, w_ref[...].astype(BF16), preferred_element_type=F32) + b_ref[...]


def _adaln(c, w_ada, b_ada):
    bsz, d = c.shape
    n = w_ada.shape[1]
    tn = n // 4
    return pl.pallas_call(
        _adaln_kernel,
        out_shape=jax.ShapeDtypeStruct((bsz, n), F32),
        grid=(n // tn,),
        in_specs=[
            pl.BlockSpec((bsz, d), lambda j: (0, 0)),
            pl.BlockSpec((d, tn), lambda j: (0, j)),
            pl.BlockSpec((1, tn), lambda j: (0, j)),
        ],
        out_specs=pl.BlockSpec((bsz, tn), lambda j: (0, j)),
        compiler_params=_params(("arbitrary",)),
        name="adaln",
    )(c, w_ada, b_ada.reshape(1, n))


def _ffn_kernel(x_ref, sh_ref, sc_ref, gt_ref, nw_ref, win_ref, wout_ref, fn_ref, o_ref,
                acc_ref, *, d_ff, tf, final_norm):
    x = x_ref[0]
    u = (_rmsnorm(x, nw_ref[...]) * (1.0 + sc_ref[0]) + sh_ref[0]).astype(BF16)
    for i in range(d_ff // tf):
        hg = jnp.dot(u, win_ref[:, i * tf:(i + 1) * tf], preferred_element_type=F32)
        hu = jnp.dot(u, win_ref[:, d_ff + i * tf:d_ff + (i + 1) * tf], preferred_element_type=F32)
        act = (hg * jax.nn.sigmoid(hg) * hu).astype(BF16)
        part = jnp.dot(act, wout_ref[i * tf:(i + 1) * tf, :], preferred_element_type=F32)
        if i == 0:
            acc_ref[...] = part
        else:
            acc_ref[...] += part
    h = x + (0.5 * gt_ref[0]) * acc_ref[...]
    if final_norm:
        h = _rmsnorm(h, fn_ref[...])
    o_ref[0] = h


def _ffn(x, sh, sc, gt, nw, w_in, w_out, fn, *, final_norm, tm=512, tf=256):
    bsz, s, d = x.shape
    d_ff = w_out.shape[0]
    vec = pl.BlockSpec((1, 1, d), lambda b, i: (b, 0, 0))
    row = pl.BlockSpec((1, tm, d), lambda b, i: (b, i, 0))
    return pl.pallas_call(
        functools.partial(_ffn_kernel, d_ff=d_ff, tf=tf, final_norm=final_norm),
        out_shape=jax.ShapeDtypeStruct((bsz, s, d), F32),
        grid=(bsz, s // tm),
        in_specs=[row, vec, vec, vec, _resident((1, d)), _resident((d, 2 * d_ff)),
                  _resident((d_ff, d)), _resident((1, d))],
        out_specs=row,
        scratch_shapes=[pltpu.VMEM((tm, d), F32)],
        compiler_params=_params(("parallel", "parallel")),
        name="ffn_final" if final_norm else "ffn",
    )(x, sh, sc, gt, nw.reshape(1, d), w_in, w_out, fn.reshape(1, d))


def _proj_kernel(x_ref, sh_ref, sc_ref, nw_ref, w_ref, wga_ref, w2_ref, b2_ref,
                 gq_ref, gk_ref, gv_ref, gr_ref, g_ref, dq_ref, dk_ref, dv_ref, sg_ref):
    x = x_ref[0]
    u = (_rmsnorm(x, nw_ref[...]) * (1.0 + sc_ref[0]) + sh_ref[0]).astype(BF16)

    def seg(start, size):
        return jnp.dot(u, w_ref[:, start:start + size], preferred_element_type=F32)

    kw = GLA_HEADS * GLA_DK
    vw = GLA_HEADS * GLA_DV
    qw = DIFF_HEADS * 2 * DIFF_DH
    dvw = DIFF_HEADS * DIFF_DV
    off = 0
    gq_ref[0] = (seg(off, kw) * (GLA_DK ** -0.5)).astype(BF16)
    off += kw
    gk_ref[0] = seg(off, kw).astype(BF16)
    off += kw
    gv_ref[0] = seg(off, vw).astype(BF16)
    off += vw
    r = seg(off, vw)
    gr_ref[0] = (r * jax.nn.sigmoid(r)).astype(BF16)
    off += vw
    dq_ref[0] = (seg(off, qw) * (DIFF_DH ** -0.5 * LOG2E)).astype(BF16)
    off += qw
    dk_ref[0] = seg(off, qw).astype(BF16)
    off += qw
    dv_ref[0] = seg(off, dvw).astype(BF16)
    off += dvw
    n_gate = sg_ref.shape[-1]
    step = 512
    for j in range(n_gate // step):
        sg_ref[0, :, j * step:(j + 1) * step] = jax.nn.sigmoid(seg(off + j * step, step)).astype(BF16)
    a_low = jnp.dot(u, wga_ref[...], preferred_element_type=F32).astype(BF16)
    z = jnp.dot(a_low, w2_ref[...], preferred_element_type=F32) + b2_ref[...]
    log_sig = jnp.minimum(z, 0.0) - jnp.log1p(jnp.exp(-jnp.abs(z)))
    g_ref[0] = log_sig * (1.0 / GLA_TAU)


def _mixer_proj(h, sh, sc, nw, w_main, w_ga, w2, b2, *, tm=512):
    bsz, s, d = h.shape
    kw = GLA_HEADS * GLA_DK
    vw = GLA_HEADS * GLA_DV
    qw = DIFF_HEADS * 2 * DIFF_DH
    dvw = DIFF_HEADS * DIFF_DV
    n_gate = 2 * d
    vec = pl.BlockSpec((1, 1, d), lambda b, i: (b, 0, 0))

    def rows(w):
        return pl.BlockSpec((1, tm, w), lambda b, i: (b, i, 0))

    widths = [(kw, BF16), (kw, BF16), (vw, BF16), (vw, BF16), (kw, F32),
              (qw, BF16), (qw, BF16), (dvw, BF16), (n_gate, BF16)]
    return pl.pallas_call(
        _proj_kernel,
        out_shape=[jax.ShapeDtypeStruct((bsz, s, w), dt) for w, dt in widths],
        grid=(bsz, s // tm),
        in_specs=[rows(d), vec, vec, _resident((1, d)), _resident(w_main.shape),
                  _resident(w_ga.shape), _resident(w2.shape), _resident((1, kw))],
        out_specs=[rows(w) for w, _ in widths],
        compiler_params=_params(("parallel", "parallel")),
        name="mixer_proj",
    )(h, sh, sc, nw.reshape(1, d), w_main, w_ga, w2, b2.reshape(1, kw))


def _gla_kernel(q_ref, k_ref, v_ref, r_ref, g_ref, hn_ref, o_ref, st_ref, *, chunk, sub):
    @pl.when(pl.program_id(1) == 0)
    def _():
        st_ref[...] = jnp.zeros_like(st_ref)

    nsub = chunk // sub
    row = lax.broadcasted_iota(jnp.int32, (chunk, chunk), 0)
    col = lax.broadcasted_iota(jnp.int32, (chunk, chunk), 1)
    causal = col <= row
    cum_mat = jnp.concatenate(
        [jnp.where(causal, 1.0, 0.0), jnp.where(causal & (col >= (row // sub) * sub), 1.0, 0.0)],
        axis=0).astype(BF16)
    row_blk = lax.broadcasted_iota(jnp.int32, (chunk, LANES), 0) // sub
    lane_head = lax.broadcasted_iota(jnp.int32, (chunk, LANES), 1) // GLA_DK

    def cumsum(bi, pair):
        g = g_ref[bi, :, pair * LANES:(pair + 1) * LANES]
        g1 = g.astype(BF16)
        e1 = g - g1.astype(F32)
        g2 = e1.astype(BF16)
        g3 = (e1 - g2.astype(F32)).astype(BF16)
        cs = jnp.dot(cum_mat, jnp.concatenate([g1, g2, g3], axis=1), preferred_element_type=F32)
        return cs[:, :LANES] + cs[:, LANES:2 * LANES] + cs[:, 2 * LANES:]

    def decay(bi, pair, cs):
        lanes = slice(pair * LANES, (pair + 1) * LANES)
        b = cs[:chunk]
        w = cs[chunk:]
        b_last = b[chunk - 1:chunk]
        q = q_ref[bi, :, lanes].astype(F32)
        k = k_ref[bi, :, lanes].astype(F32)
        q_cat = []
        for j in range(nsub):
            ref_b = b[j * sub - 1:j * sub] if j else jnp.zeros_like(b_last)
            q_cat.append((q * jnp.exp(jnp.minimum(b - ref_b, 0.0))).astype(BF16))
        return dict(
            q_cat=jnp.concatenate(q_cat, axis=1),
            q_state=(q * jnp.exp(b)).astype(BF16),
            k_state=k * jnp.exp(b_last - b),
            k_hat=k * jnp.exp(jnp.minimum(-w, EXP_CLAMP)),
            chunk_decay=jnp.exp(b_last))

    def scores(d):
        out = []
        for hh in range(2):
            kh = jnp.where(lane_head == hh, d["k_hat"], 0.0).astype(BF16)
            k_cat = jnp.concatenate(
                [jnp.where(row_blk == j, kh, jnp.zeros_like(kh)) for j in range(nsub)], axis=1)
            out.append(lax.dot_general(d["q_cat"], k_cat, (((1,), (1,)), ((), ())),
                                       preferred_element_type=F32))
        return out

    def finish(bi, pair, d, sc):
        for hh in range(2):
            head = pair * 2 + hh
            cols = slice(head * GLA_DV, (head + 1) * GLA_DV)
            intra = jnp.where(causal, sc[hh], 0.0).astype(BF16)
            vh = v_ref[bi, :, cols]
            state_t = st_ref[bi * GLA_HEADS + head]
            o = jnp.dot(intra, vh, preferred_element_type=F32)
            o += lax.dot_general(d["q_state"], state_t.astype(BF16), (((1,), (1,)), ((), ())),
                                 preferred_element_type=F32)
            ks = jnp.where(lane_head == hh, d["k_state"], 0.0).astype(BF16)
            kv_t = lax.dot_general(vh, ks, (((0,), (0,)), ((), ())), preferred_element_type=F32)
            st_ref[bi * GLA_HEADS + head] = state_t * d["chunk_decay"] + kv_t
            y = _rmsnorm(o, hn_ref[...]) * r_ref[bi, :, cols].astype(F32)
            o_ref[bi, :, cols] = y.astype(BF16)

    chains = list(itertools.product(range(q_ref.shape[0]), range(GLA_HEADS // 2)))
    cums = [cumsum(*ch) for ch in chains]
    decayed, scored = {}, {}
    for t in range(len(chains) + 1):
        if t < len(chains):
            decayed[t] = decay(*chains[t], cums[t])
        if t >= 1:
            finish(*chains[t - 1], decayed.pop(t - 1), scored.pop(t - 1))
        if t < len(chains):
            scored[t] = scores(decayed[t])


def _gla(gq, gk, gv, gr, g, head_norm, *, nb=4):
    bsz, s, kw = gq.shape
    vw = gv.shape[-1]
    chunk = GLA_CHUNK

    def rows(w):
        return pl.BlockSpec((nb, chunk, w), lambda b, c: (b, c, 0))

    return pl.pallas_call(
        functools.partial(_gla_kernel, chunk=chunk, sub=GLA_SUB),
        out_shape=jax.ShapeDtypeStruct((bsz, s, vw), BF16),
        grid=(bsz // nb, s // chunk),
        in_specs=[rows(kw), rows(kw), rows(vw), rows(vw), rows(kw), _resident((1, GLA_DV))],
        out_specs=rows(vw),
        scratch_shapes=[pltpu.VMEM((nb * GLA_HEADS, GLA_DV, LANES), F32)],
        compiler_params=_params(("parallel", "arbitrary")),
        name="gla",
    )(gq, gk, gv, gr, g, head_norm.reshape(1, GLA_DV))


def _attn_kernel(q_ref, k_ref, v_ref, slope_ref, lam_ref, hn_ref, o_ref, kk_ref, qq_ref, s0_ref,
                 s1_ref, m_ref, l_ref, acc_ref, *, tq, rc, lam_init):
    qi = pl.program_id(2)
    seq = k_ref.shape[1]
    c = slope_ref[0] * LOG2E
    c1 = c.astype(BF16).astype(F32)
    c2 = (c - c1).astype(BF16).astype(F32)
    c3 = c - c1 - c2

    @pl.when(qi == 0)
    def _():
        kk_ref[:, :LANES] = k_ref[0]
        pos = lax.broadcasted_iota(jnp.int32, (seq, LANES), 0) & (tq - 1)
        lane = lax.broadcasted_iota(jnp.int32, (seq, LANES), 1)
        feat = jnp.where(lane < 3, pos & ~15, jnp.where(lane < 6, pos & 15, 0))
        kk_ref[:, LANES:] = feat.astype(F32).astype(BF16)
        sub = lax.broadcasted_iota(jnp.int32, (LANES, 2 * tq), 0)
        cf = jnp.where((sub == 0) | (sub == 3), c1, jnp.where((sub == 1) | (sub == 4), c2, c3))
        qq_ref[LANES:, :] = jnp.where(sub < 6, cf, 0.0).astype(BF16)

    q_t = q_ref[0].astype(F32).T
    sub = lax.broadcasted_iota(jnp.int32, (LANES, tq), 0)
    qq_ref[:LANES, :tq] = jnp.where(sub < DIFF_DH, q_t, 0.0).astype(BF16)
    qq_ref[:LANES, tq:] = jnp.where(sub >= DIFF_DH, q_t, 0.0).astype(BF16)
    acc_ref[...] = jnp.zeros_like(acc_ref)
    nchunk = tq // rc

    def logits(j, s_blk):
        start = pl.multiple_of(j * tq, tq)
        s_blk[...] = jnp.dot(kk_ref[pl.ds(start, tq), :], qq_ref[...], preferred_element_type=F32)

    def softmax_pv(j, s_blk, masked):
        m_old = m_ref[...]
        l_old = l_ref[...]
        start = pl.multiple_of(j * tq, tq)
        shift = c * ((qi - j) * tq).astype(F32)

        def load(r):
            blk = s_blk[r * rc:(r + 1) * rc, :]
            if masked:
                key = r * rc + lax.broadcasted_iota(jnp.int32, (rc, 2 * tq), 0)
                qry = lax.broadcasted_iota(jnp.int32, (rc, 2 * tq), 1) & (tq - 1)
                blk = jnp.where(key <= qry, blk, -jnp.inf)
            return blk

        mx = None
        for r in range(nchunk):
            cm = jnp.max(load(r).reshape(rc // 8, 8, 2 * tq), axis=0)
            mx = cm if mx is None else jnp.maximum(mx, cm)
        m_new = jnp.maximum(m_old, jnp.max(mx, axis=0, keepdims=True) - shift)
        alpha = jnp.exp2(m_old - m_new)
        m_shift = m_new + shift
        ls = None
        probs = []
        for r in range(nchunk):
            p = jnp.exp2(load(r) - m_shift)
            ps = jnp.sum(p.reshape(rc // 8, 8, 2 * tq), axis=0)
            ls = ps if ls is None else ls + ps
            probs.append(p.astype(BF16))
        l_ref[...] = alpha * l_old + jnp.sum(ls, axis=0, keepdims=True)
        m_ref[...] = m_new
        pv = lax.dot_general(v_ref[0, pl.ds(start, tq), :], jnp.concatenate(probs, axis=0),
                             (((0,), (0,)), ((), ())), preferred_element_type=F32)
        acc_ref[...] = alpha * acc_ref[...] + pv

    def two_blocks(u, carry):
        j = 2 * u
        logits(j + 1, s1_ref)
        softmax_pv(j, s0_ref, False)
        logits(j + 2, s0_ref)
        softmax_pv(j + 1, s1_ref, False)
        return carry

    m_ref[...] = jnp.full_like(m_ref, -jnp.inf)
    l_ref[...] = jnp.zeros_like(l_ref)
    logits(0, s0_ref)
    lax.fori_loop(0, qi // 2, two_blocks, 0)

    @pl.when(qi % 2 == 0)
    def _():
        softmax_pv(qi, s0_ref, True)

    @pl.when(qi % 2 == 1)
    def _():
        logits(qi, s1_ref)
        softmax_pv(qi - 1, s0_ref, False)
        softmax_pv(qi, s1_ref, True)

    lam = lam_ref[...]
    lam = (jnp.exp(jnp.sum(lam[0:1] * lam[1:2], axis=-1, keepdims=True))
           - jnp.exp(jnp.sum(lam[2:3] * lam[3:4], axis=-1, keepdims=True)) + lam_init)
    out = acc_ref[...] * (1.0 / l_ref[...])
    y_t = out[:, :tq] - lam * out[:, tq:]
    y_t = y_t * lax.rsqrt(jnp.mean(y_t * y_t, axis=0, keepdims=True) + EPS)
    o_ref[0] = (y_t.T * hn_ref[...] * (1.0 - lam_init)).astype(BF16)


def _diff_attn(dq, dk, dv, lam_vecs, head_norm, lam_init, *, tq=512, rc=128):
    bsz, s, _ = dq.shape
    slopes = jnp.asarray(
        [2.0 ** (-ALIBI_MAX_BIAS * (i + 1) / DIFF_HEADS) for i in range(DIFF_HEADS)], F32
    ).reshape(DIFF_HEADS, 1, 1)
    return pl.pallas_call(
        functools.partial(_attn_kernel, tq=tq, rc=rc, lam_init=lam_init),
        out_shape=jax.ShapeDtypeStruct((bsz, s, DIFF_HEADS * DIFF_DV), BF16),
        grid=(bsz, DIFF_HEADS, s // tq),
        in_specs=[
            pl.BlockSpec((1, tq, LANES), lambda b, h, i: (b, i, h)),
            pl.BlockSpec((1, s, LANES), lambda b, h, i: (b, 0, h)),
            pl.BlockSpec((1, s, DIFF_DV), lambda b, h, i: (b, 0, h)),
            pl.BlockSpec((1, 1, 1), lambda b, h, i: (h, 0, 0)),
            _resident(lam_vecs.shape),
            _resident((1, DIFF_DV)),
        ],
        out_specs=pl.BlockSpec((1, tq, DIFF_DV), lambda b, h, i: (b, i, h)),
        scratch_shapes=[
            pltpu.VMEM((s, 2 * LANES), BF16),
            pltpu.VMEM((2 * LANES, 2 * tq), BF16),
            pltpu.VMEM((tq, 2 * tq), F32),
            pltpu.VMEM((tq, 2 * tq), F32),
            pltpu.VMEM((1, 2 * tq), F32),
            pltpu.VMEM((1, 2 * tq), F32),
            pltpu.VMEM((DIFF_DV, 2 * tq), F32),
        ],
        compiler_params=_params(("parallel", "parallel", "arbitrary")),
        name="diff_attn",
    )(dq, dk, dv, slopes, lam_vecs, head_norm.reshape(1, DIFF_DV))


def _merge_kernel(h_ref, za_ref, zb_ref, sg_ref, gt_ref, wa_ref, wb_ref, wo_ref, o_ref):
    d = h_ref.shape[-1]
    ya = jnp.dot(za_ref[0], wa_ref[...], preferred_element_type=F32)
    yb = jnp.dot(zb_ref[0], wb_ref[...], preferred_element_type=F32)
    mix = sg_ref[0, :, :d].astype(F32) * ya + sg_ref[0, :, d:].astype(F32) * yb
    m = jnp.dot(mix.astype(BF16), wo_ref[...], preferred_element_type=F32)
    o_ref[0] = h_ref[0] + gt_ref[0] * m


def _merge(h, za, zb, sg, gt, w_a, w_b, w_o, *, tm=512):
    bsz, s, d = h.shape
    vec = pl.BlockSpec((1, 1, d), lambda b, i: (b, 0, 0))

    def rows(w):
        return pl.BlockSpec((1, tm, w), lambda b, i: (b, i, 0))

    return pl.pallas_call(
        _merge_kernel,
        out_shape=jax.ShapeDtypeStruct((bsz, s, d), F32),
        grid=(bsz, s // tm),
        in_specs=[rows(d), rows(za.shape[-1]), rows(zb.shape[-1]), rows(2 * d), vec,
                  _resident(w_a.shape), _resident(w_b.shape), _resident(w_o.shape)],
        out_specs=rows(d),
        compiler_params=_params(("parallel", "parallel")),
        name="merge",
    )(h, za, zb, sg, gt, w_a, w_b, w_o)


def kernel(x, c, w_ada, b_ada, ffn1_norm, ffn1_w_in, ffn1_w_out, mix_norm, w_in, gla_alpha_w2, gla_alpha_b, gla_head_norm, diff_lq1, diff_lk1, diff_lq2, diff_lk2, diff_head_norm, w_branch_a, w_branch_b, w_out, ffn2_norm, ffn2_w_in, ffn2_w_out, final_norm):
    depth = w_ada.shape[0]
    bsz, _, d = x.shape
    kw = GLA_HEADS * GLA_DK
    vw = GLA_HEADS * GLA_DV
    ga_start = 2 * kw + 2 * vw
    h = x
    for l in range(depth):
        lam_init = 0.8 - 0.6 * math.exp(-0.3 * l)
        mod = _adaln(c, w_ada[l], b_ada[l])
        sh1, sc1, gt1, sh2, sc2, gt2, sh3, sc3, gt3 = [
            mod[:, i * d:(i + 1) * d].reshape(bsz, 1, d) for i in range(N_MOD)]
        last = l == depth - 1
        h = _ffn(h, sh1, sc1, gt1, ffn1_norm[l], ffn1_w_in[l].astype(BF16),
                 ffn1_w_out[l].astype(BF16), final_norm, final_norm=False)
        w = w_in[l]
        w_main = jnp.concatenate([w[:, :ga_start], w[:, ga_start + GLA_RANK:]], axis=1).astype(BF16)
        w_ga = w[:, ga_start:ga_start + GLA_RANK].astype(BF16)
        gq, gk, gv, gr, g, dq, dk, dv, sg = _mixer_proj(
            h, sh2, sc2, mix_norm[l], w_main, w_ga, gla_alpha_w2[l].astype(BF16), gla_alpha_b[l])
        za = _gla(gq, gk, gv, gr, g, gla_head_norm[l])
        lam_vecs = jnp.stack([diff_lq1[l], diff_lk1[l], diff_lq2[l], diff_lk2[l]])
        zb = _diff_attn(dq, dk, dv, lam_vecs, diff_head_norm[l], lam_init)
        h = _merge(h, za, zb, sg, gt2, w_branch_a[l].astype(BF16), w_branch_b[l].astype(BF16),
                   w_out[l].astype(BF16))
        h = _ffn(h, sh3, sc3, gt3, ffn2_norm[l], ffn2_w_in[l].astype(BF16),
                 ffn2_w_out[l].astype(BF16), final_norm, final_norm=last)
    return h
```

```python
import functools
import itertools
import math

import jax
import jax.numpy as jnp
from jax import lax
from jax.experimental import pallas as pl
from jax.experimental.pallas import tpu as pltpu

F32 = jnp.float32
BF16 = jnp.bfloat16

EPS = 1e-6
GLA_HEADS = 4
GLA_DK = 64
GLA_DV = 128
GLA_RANK = 16
GLA_TAU = 16.0
DIFF_HEADS = 4
DIFF_DH = 64
DIFF_DV = 128
ALIBI_MAX_BIAS = 8.0
N_MOD = 9

LOG2E = math.log2(math.e)
LANES = 128
GLA_CHUNK = 128
GLA_SUB = 16
EXP_CLAMP = 80.0
SUM_ROWS = 16
VMEM_LIMIT = 56 * 1024 * 1024


def _params(sem):
    return pltpu.CompilerParams(dimension_semantics=sem, vmem_limit_bytes=VMEM_LIMIT)


def _resident(shape):
    nd = len(shape)
    return pl.BlockSpec(shape, lambda *_: (0,) * nd, pipeline_mode=pl.Buffered(1))


def _rmsnorm(x, g):
    return x * lax.rsqrt(jnp.mean(x * x, axis=-1, keepdims=True) + EPS) * g


def _adaln_kernel(c_ref, w_ref, b_ref, o_ref):
    c = c_ref[...]
    ca = (c * jax.nn.sigmoid(c)).astype(BF16)
    o_ref[...] = jnp.dot(ca, w_ref[...].astype(BF16), preferred_element_type=F32) + b_ref[...]


def _adaln(c, w_ada, b_ada):
    bsz, d = c.shape
    n = w_ada.shape[1]
    tn = n // 4
    return pl.pallas_call(
        _adaln_kernel,
        out_shape=jax.ShapeDtypeStruct((bsz, n), F32),
        grid=(n // tn,),
        in_specs=[
            pl.BlockSpec((bsz, d), lambda j: (0, 0)),
            pl.BlockSpec((d, tn), lambda j: (0, j)),
            pl.BlockSpec((1, tn), lambda j: (0, j)),
        ],
        out_specs=pl.BlockSpec((bsz, tn), lambda j: (0, j)),
        compiler_params=_params(("arbitrary",)),
        name="adaln",
    )(c, w_ada, b_ada.reshape(1, n))


def _ffn_kernel(x_ref, sh_ref, sc_ref, gt_ref, nw_ref, win_ref, wout_ref, fn_ref, o_ref,
                acc_ref, *, d_ff, tf, final_norm):
    x = x_ref[0]
    u = (_rmsnorm(x, nw_ref[...]) * (1.0 + sc_ref[0]) + sh_ref[0]).astype(BF16)
    for i in range(d_ff // tf):
        hg = jnp.dot(u, win_ref[:, i * tf:(i + 1) * tf], preferred_element_type=F32)
        hu = jnp.dot(u, win_ref[:, d_ff + i * tf:d_ff + (i + 1) * tf], preferred_element_type=F32)
        act = (hg * jax.nn.sigmoid(hg) * hu).astype(BF16)
        part = jnp.dot(act, wout_ref[i * tf:(i + 1) * tf, :], preferred_element_type=F32)
        if i == 0:
            acc_ref[...] = part
        else:
            acc_ref[...] += part
    h = x + (0.5 * gt_ref[0]) * acc_ref[...]
    if final_norm:
        h = _rmsnorm(h, fn_ref[...])
    o_ref[0] = h


def _ffn(x, sh, sc, gt, nw, w_in, w_out, fn, *, final_norm, tm=512, tf=256):
    bsz, s, d = x.shape
    d_ff = w_out.shape[0]
    vec = pl.BlockSpec((1, 1, d), lambda b, i: (b, 0, 0))
    row = pl.BlockSpec((1, tm, d), lambda b, i: (b, i, 0))
    return pl.pallas_call(
        functools.partial(_ffn_kernel, d_ff=d_ff, tf=tf, final_norm=final_norm),
        out_shape=jax.ShapeDtypeStruct((bsz, s, d), F32),
        grid=(bsz, s // tm),
        in_specs=[row, vec, vec, vec, _resident((1, d)), _resident((d, 2 * d_ff)),
                  _resident((d_ff, d)), _resident((1, d))],
        out_specs=row,
        scratch_shapes=[pltpu.VMEM((tm, d), F32)],
        compiler_params=_params(("parallel", "parallel")),
        name="ffn_final" if final_norm else "ffn",
    )(x, sh, sc, gt, nw.reshape(1, d), w_in, w_out, fn.reshape(1, d))


def _proj_kernel(x_ref, sh_ref, sc_ref, nw_ref, w_ref, wga_ref, w2_ref, b2_ref,
                 gq_ref, gk_ref, gv_ref, gr_ref, g_ref, dq_ref, dk_ref, dv_ref, sg_ref):
    x = x_ref[0]
    u = (_rmsnorm(x, nw_ref[...]) * (1.0 + sc_ref[0]) + sh_ref[0]).astype(BF16)

    def seg(start, size):
        return jnp.dot(u, w_ref[:, start:start + size], preferred_element_type=F32)

    kw = GLA_HEADS * GLA_DK
    vw = GLA_HEADS * GLA_DV
    qw = DIFF_HEADS * 2 * DIFF_DH
    dvw = DIFF_HEADS * DIFF_DV
    off = 0
    gq_ref[0] = (seg(off, kw) * (GLA_DK ** -0.5)).astype(BF16)
    off += kw
    gk_ref[0] = seg(off, kw).astype(BF16)
    off += kw
    gv_ref[0] = seg(off, vw).astype(BF16)
    off += vw
    r = seg(off, vw)
    gr_ref[0] = (r * jax.nn.sigmoid(r)).astype(BF16)
    off += vw
    dq_ref[0] = (seg(off, qw) * (DIFF_DH ** -0.5 * LOG2E)).astype(BF16)
    off += qw
    dk_ref[0] = seg(off, qw).astype(BF16)
    off += qw
    dv_ref[0] = seg(off, dvw).astype(BF16)
    off += dvw
    n_gate = sg_ref.shape[-1]
    step = 512
    for j in range(n_gate // step):
        sg_ref[0, :, j * step:(j + 1) * step] = jax.nn.sigmoid(seg(off + j * step, step)).astype(BF16)
    a_low = jnp.dot(u, wga_ref[...], preferred_element_type=F32).astype(BF16)
    z = jnp.dot(a_low, w2_ref[...], preferred_element_type=F32) + b2_ref[...]
    log_sig = jnp.minimum(z, 0.0) - jnp.log1p(jnp.exp(-jnp.abs(z)))
    g_ref[0] = log_sig * (1.0 / GLA_TAU)


def _mixer_proj(h, sh, sc, nw, w_main, w_ga, w2, b2, *, tm=512):
    bsz, s, d = h.shape
    kw = GLA_HEADS * GLA_DK
    vw = GLA_HEADS * GLA_DV
    qw = DIFF_HEADS * 2 * DIFF_DH
    dvw = DIFF_HEADS * DIFF_DV
    n_gate = 2 * d
    vec = pl.BlockSpec((1, 1, d), lambda b, i: (b, 0, 0))

    def rows(w):
        return pl.BlockSpec((1, tm, w), lambda b, i: (b, i, 0))

    widths = [(kw, BF16), (kw, BF16), (vw, BF16), (vw, BF16), (kw, F32),
              (qw, BF16), (qw, BF16), (dvw, BF16), (n_gate, BF16)]
    return pl.pallas_call(
        _proj_kernel,
        out_shape=[jax.ShapeDtypeStruct((bsz, s, w), dt) for w, dt in widths],
        grid=(bsz, s // tm),
        in_specs=[rows(d), vec, vec, _resident((1, d)), _resident(w_main.shape),
                  _resident(w_ga.shape), _resident(w2.shape), _resident((1, kw))],
        out_specs=[rows(w) for w, _ in widths],
        compiler_params=_params(("parallel", "parallel")),
        name="mixer_proj",
    )(h, sh, sc, nw.reshape(1, d), w_main, w_ga, w2, b2.reshape(1, kw))


def _gla_kernel(q_ref, k_ref, v_ref, r_ref, g_ref, hn_ref, o_ref, st_ref, *, chunk, sub):
    @pl.when(pl.program_id(1) == 0)
    def _():
        st_ref[...] = jnp.zeros_like(st_ref)

    nsub = chunk // sub
    row = lax.broadcasted_iota(jnp.int32, (chunk, chunk), 0)
    col = lax.broadcasted_iota(jnp.int32, (chunk, chunk), 1)
    causal = col <= row
    cum_mat = jnp.concatenate(
        [jnp.where(causal, 1.0, 0.0), jnp.where(causal & (col >= (row // sub) * sub), 1.0, 0.0)],
        axis=0).astype(BF16)
    row_blk = lax.broadcasted_iota(jnp.int32, (chunk, LANES), 0) // sub
    lane_head = lax.broadcasted_iota(jnp.int32, (chunk, LANES), 1) // GLA_DK

    def cumsum(bi, pair):
        g = g_ref[bi, :, pair * LANES:(pair + 1) * LANES]
        g1 = g.astype(BF16)
        e1 = g - g1.astype(F32)
        g2 = e1.astype(BF16)
        g3 = (e1 - g2.astype(F32)).astype(BF16)
        cs = jnp.dot(cum_mat, jnp.concatenate([g1, g2, g3], axis=1), preferred_element_type=F32)
        return cs[:, :LANES] + cs[:, LANES:2 * LANES] + cs[:, 2 * LANES:]

    def decay(bi, pair, cs):
        lanes = slice(pair * LANES, (pair + 1) * LANES)
        b = cs[:chunk]
        w = cs[chunk:]
        b_last = b[chunk - 1:chunk]
        q = q_ref[bi, :, lanes].astype(F32)
        k = k_ref[bi, :, lanes].astype(F32)
        q_cat = []
        for j in range(nsub):
            ref_b = b[j * sub - 1:j * sub] if j else jnp.zeros_like(b_last)
            q_cat.append((q * jnp.exp(jnp.minimum(b - ref_b, 0.0))).astype(BF16))
        return dict(
            q_cat=jnp.concatenate(q_cat, axis=1),
            q_state=(q * jnp.exp(b)).astype(BF16),
            k_state=k * jnp.exp(b_last - b),
            k_hat=k * jnp.exp(jnp.minimum(-w, EXP_CLAMP)),
            chunk_decay=jnp.exp(b_last))

    def scores(d):
        out = []
        for hh in range(2):
            kh = jnp.where(lane_head == hh, d["k_hat"], 0.0).astype(BF16)
            k_cat = jnp.concatenate(
                [jnp.where(row_blk == j, kh, jnp.zeros_like(kh)) for j in range(nsub)], axis=1)
            out.append(lax.dot_general(d["q_cat"], k_cat, (((1,), (1,)), ((), ())),
                                       preferred_element_type=F32))
        return out

    def finish(bi, pair, d, sc):
        for hh in range(2):
            head = pair * 2 + hh
            cols = slice(head * GLA_DV, (head + 1) * GLA_DV)
            intra = jnp.where(causal, sc[hh], 0.0).astype(BF16)
            vh = v_ref[bi, :, cols]
            state_t = st_ref[bi * GLA_HEADS + head]
            o = jnp.dot(intra, vh, preferred_element_type=F32)
            o += lax.dot_general(d["q_state"], state_t.astype(BF16), (((1,), (1,)), ((), ())),
                                 preferred_element_type=F32)
            ks = jnp.where(lane_head == hh, d["k_state"], 0.0).astype(BF16)
            kv_t = lax.dot_general(vh, ks, (((0,), (0,)), ((), ())), preferred_element_type=F32)
            st_ref[bi * GLA_HEADS + head] = state_t * d["chunk_decay"] + kv_t
            y = _rmsnorm(o, hn_ref[...]) * r_ref[bi, :, cols].astype(F32)
            o_ref[bi, :, cols] = y.astype(BF16)

    chains = list(itertools.product(range(q_ref.shape[0]), range(GLA_HEADS // 2)))
    cums = [cumsum(*ch) for ch in chains]
    decayed, scored = {}, {}
    for t in range(len(chains) + 1):
        if t < len(chains):
            decayed[t] = decay(*chains[t], cums[t])
        if t >= 1:
            finish(*chains[t - 1], decayed.pop(t - 1), scored.pop(t - 1))
        if t < len(chains):
            scored[t] = scores(decayed[t])


def _gla(gq, gk, gv, gr, g, head_norm, *, nb=4):
    bsz, s, kw = gq.shape
    vw = gv.shape[-1]
    chunk = GLA_CHUNK

    def rows(w):
        return pl.BlockSpec((nb, chunk, w), lambda b, c: (b, c, 0))

    return pl.pallas_call(
        functools.partial(_gla_kernel, chunk=chunk, sub=GLA_SUB),
        out_shape=jax.ShapeDtypeStruct((bsz, s, vw), BF16),
        grid=(bsz // nb, s // chunk),
        in_specs=[rows(kw), rows(kw), rows(vw), rows(vw), rows(kw), _resident((1, GLA_DV))],
        out_specs=rows(vw),
        scratch_shapes=[pltpu.VMEM((nb * GLA_HEADS, GLA_DV, LANES), F32)],
        compiler_params=_params(("parallel", "arbitrary")),
        name="gla",
    )(gq, gk, gv, gr, g, head_norm.reshape(1, GLA_DV))


def _attn_kernel(q_ref, k_ref, v_ref, slope_ref, lam_ref, hn_ref, o_ref, kk_ref, qq_ref, vt_ref,
                 s0_ref, s1_ref, m_ref, acc_ref, *, tq, rc, lam_init):
    seq = k_ref.shape[1]
    nq = seq // tq
    c = slope_ref[0] * LOG2E
    c1 = c.astype(BF16).astype(F32)
    c2 = (c - c1).astype(BF16).astype(F32)
    c3 = c - c1 - c2

    kk_ref[:, :LANES] = k_ref[0]
    pos = lax.broadcasted_iota(jnp.int32, (seq, LANES), 0) & (tq - 1)
    lane = lax.broadcasted_iota(jnp.int32, (seq, LANES), 1)
    feat = jnp.where(lane < 3, pos & ~15, jnp.where(lane < 6, pos & 15, 0))
    kk_ref[:, LANES:] = feat.astype(F32).astype(BF16)
    sub = lax.broadcasted_iota(jnp.int32, (LANES, 2 * tq), 0)
    cf = jnp.where((sub == 0) | (sub == 3), c1, jnp.where((sub == 1) | (sub == 4), c2, c3))
    slope_rows = jnp.where(sub < 6, cf, 0.0).astype(BF16)
    qq_ref[0, LANES:, :] = slope_rows
    qq_ref[1, LANES:, :] = slope_rows
    for jb in range(nq):
        vt_ref[jb, :DIFF_DV, :] = v_ref[0, jb * tq:(jb + 1) * tq, :].astype(F32).T.astype(BF16)
        vt_ref[jb, DIFF_DV:, :] = jnp.ones((SUM_ROWS, tq), BF16)
    nchunk = tq // rc
    lam = lam_ref[...]
    lam = (jnp.exp(jnp.sum(lam[0:1] * lam[1:2], axis=-1, keepdims=True))
           - jnp.exp(jnp.sum(lam[2:3] * lam[3:4], axis=-1, keepdims=True)) + lam_init)

    def logits(qs, j, s_blk):
        start = pl.multiple_of(j * tq, tq)
        s_blk[...] = jnp.dot(kk_ref[pl.ds(start, tq), :], qq_ref[qs], preferred_element_type=F32)

    def softmax_pv(qs, qi, j, s_blk, masked):
        m_old = m_ref[qs]
        shift = c * jnp.asarray((qi - j) * tq, F32)

        def load(r):
            blk = s_blk[r * rc:(r + 1) * rc, :]
            if masked:
                key = r * rc + lax.broadcasted_iota(jnp.int32, (rc, 2 * tq), 0)
                qry = lax.broadcasted_iota(jnp.int32, (rc, 2 * tq), 1) & (tq - 1)
                blk = jnp.where(key <= qry, blk, -jnp.inf)
            return blk

        mx = None
        for r in range(nchunk):
            cm = jnp.max(load(r).reshape(rc // 8, 8, 2 * tq), axis=0)
            mx = cm if mx is None else jnp.maximum(mx, cm)
        m_new = jnp.maximum(m_old, jnp.max(mx, axis=0, keepdims=True) - shift)
        alpha = jnp.exp2(m_old - m_new)
        m_shift = m_new + shift
        probs = jnp.concatenate(
            [jnp.exp2(load(r) - m_shift).astype(BF16) for r in range(nchunk)], axis=0)
        m_ref[qs] = m_new
        acc_ref[qs] = alpha * acc_ref[qs] + jnp.dot(vt_ref[j], probs, preferred_element_type=F32)

    def pair(p, carry):
        q_blk = (p, nq - 1 - p)
        for qs in range(2):
            q_t = q_ref[0, pl.ds(pl.multiple_of(q_blk[qs] * tq, tq), tq), :].astype(F32).T
            half = lax.broadcasted_iota(jnp.int32, (LANES, tq), 0)
            qq_ref[qs, :LANES, :tq] = jnp.where(half < DIFF_DH, q_t, 0.0).astype(BF16)
            qq_ref[qs, :LANES, tq:] = jnp.where(half >= DIFF_DH, q_t, 0.0).astype(BF16)
            m_ref[qs] = jnp.full((1, 2 * tq), -jnp.inf, F32)
            acc_ref[qs] = jnp.zeros((DIFF_DV + SUM_ROWS, 2 * tq), F32)
        items = [(1, q_blk[1], q_blk[1], True), (0, q_blk[0], q_blk[0], True)]
        for u in range(nq - 1):
            late = u < q_blk[1]
            items.append((jnp.where(late, 1, 0), jnp.where(late, q_blk[1], q_blk[0]),
                          jnp.where(late, u, u - q_blk[1]), False))
        bufs = (s0_ref, s1_ref)
        logits(items[0][0], items[0][2], bufs[0])
        for t, (qs, qi, j, masked) in enumerate(items):
            if t + 1 < len(items):
                logits(items[t + 1][0], items[t + 1][2], bufs[(t + 1) % 2])
            softmax_pv(qs, qi, j, bufs[t % 2], masked)
        for qs in range(2):
            out = acc_ref[qs, :DIFF_DV, :] * (1.0 / acc_ref[qs, DIFF_DV:DIFF_DV + 1, :])
            y_t = out[:, :tq] - lam * out[:, tq:]
            y_t = y_t * lax.rsqrt(jnp.mean(y_t * y_t, axis=0, keepdims=True) + EPS)
            rows = pl.ds(pl.multiple_of(q_blk[qs] * tq, tq), tq)
            o_ref[0, rows, :] = (y_t.T * hn_ref[...] * (1.0 - lam_init)).astype(BF16)
        return carry

    lax.fori_loop(0, nq // 2, pair, 0)


def _diff_attn(dq, dk, dv, lam_vecs, head_norm, lam_init, *, tq=512, rc=256):
    bsz, s, _ = dq.shape
    slopes = jnp.asarray(
        [2.0 ** (-ALIBI_MAX_BIAS * (i + 1) / DIFF_HEADS) for i in range(DIFF_HEADS)], F32
    ).reshape(DIFF_HEADS, 1, 1)
    return pl.pallas_call(
        functools.partial(_attn_kernel, tq=tq, rc=rc, lam_init=lam_init),
        out_shape=jax.ShapeDtypeStruct((bsz, s, DIFF_HEADS * DIFF_DV), BF16),
        grid=(bsz, DIFF_HEADS),
        in_specs=[
            pl.BlockSpec((1, s, LANES), lambda b, h: (b, 0, h)),
            pl.BlockSpec((1, s, LANES), lambda b, h: (b, 0, h)),
            pl.BlockSpec((1, s, DIFF_DV), lambda b, h: (b, 0, h)),
            pl.BlockSpec((1, 1, 1), lambda b, h: (h, 0, 0)),
            _resident(lam_vecs.shape),
            _resident((1, DIFF_DV)),
        ],
        out_specs=pl.BlockSpec((1, s, DIFF_DV), lambda b, h: (b, 0, h)),
        scratch_shapes=[
            pltpu.VMEM((s, 2 * LANES), BF16),
            pltpu.VMEM((2, 2 * LANES, 2 * tq), BF16),
            pltpu.VMEM((s // tq, DIFF_DV + SUM_ROWS, tq), BF16),
            pltpu.VMEM((tq, 2 * tq), F32),
            pltpu.VMEM((tq, 2 * tq), F32),
            pltpu.VMEM((2, 1, 2 * tq), F32),
            pltpu.VMEM((2, DIFF_DV + SUM_ROWS, 2 * tq), F32),
        ],
        compiler_params=_params(("parallel", "parallel")),
        name="diff_attn",
    )(dq, dk, dv, slopes, lam_vecs, head_norm.reshape(1, DIFF_DV))


def _merge_kernel(h_ref, za_ref, zb_ref, sg_ref, gt_ref, wa_ref, wb_ref, wo_ref, o_ref):
    d = h_ref.shape[-1]
    ya = jnp.dot(za_ref[0], wa_ref[...], preferred_element_type=F32)
    yb = jnp.dot(zb_ref[0], wb_ref[...], preferred_element_type=F32)
    mix = sg_ref[0, :, :d].astype(F32) * ya + sg_ref[0, :, d:].astype(F32) * yb
    m = jnp.dot(mix.astype(BF16), wo_ref[...], preferred_element_type=F32)
    o_ref[0] = h_ref[0] + gt_ref[0] * m


def _merge(h, za, zb, sg, gt, w_a, w_b, w_o, *, tm=512):
    bsz, s, d = h.shape
    vec = pl.BlockSpec((1, 1, d), lambda b, i: (b, 0, 0))

    def rows(w):
        return pl.BlockSpec((1, tm, w), lambda b, i: (b, i, 0))

    return pl.pallas_call(
        _merge_kernel,
        out_shape=jax.ShapeDtypeStruct((bsz, s, d), F32),
        grid=(bsz, s // tm),
        in_specs=[rows(d), rows(za.shape[-1]), rows(zb.shape[-1]), rows(2 * d), vec,
                  _resident(w_a.shape), _resident(w_b.shape), _resident(w_o.shape)],
        out_specs=rows(d),
        compiler_params=_params(("parallel", "parallel")),
        name="merge",
    )(h, za, zb, sg, gt, w_a, w_b, w_o)


def kernel(x, c, w_ada, b_ada, ffn1_norm, ffn1_w_in, ffn1_w_out, mix_norm, w_in, gla_alpha_w2, gla_alpha_b, gla_head_norm, diff_lq1, diff_lk1, diff_lq2, diff_lk2, diff_head_norm, w_branch_a, w_branch_b, w_out, ffn2_norm, ffn2_w_in, ffn2_w_out, final_norm):
    depth = w_ada.shape[0]
    bsz, _, d = x.shape
    kw = GLA_HEADS * GLA_DK
    vw = GLA_HEADS * GLA_DV
    ga_start = 2 * kw + 2 * vw
    h = x
    for l in range(depth):
        lam_init = 0.8 - 0.6 * math.exp(-0.3 * l)
        mod = _adaln(c, w_ada[l], b_ada[l])
        sh1, sc1, gt1, sh2, sc2, gt2, sh3, sc3, gt3 = [
            mod[:, i * d:(i + 1) * d].reshape(bsz, 1, d) for i in range(N_MOD)]
        last = l == depth - 1
        h = _ffn(h, sh1, sc1, gt1, ffn1_norm[l], ffn1_w_in[l].astype(BF16),
                 ffn1_w_out[l].astype(BF16), final_norm, final_norm=False)
        w = w_in[l]
        w_main = jnp.concatenate([w[:, :ga_start], w[:, ga_start + GLA_RANK:]], axis=1).astype(BF16)
        w_ga = w[:, ga_start:ga_start + GLA_RANK].astype(BF16)
        gq, gk, gv, gr, g, dq, dk, dv, sg = _mixer_proj(
            h, sh2, sc2, mix_norm[l], w_main, w_ga, gla_alpha_w2[l].astype(BF16), gla_alpha_b[l])
        za = _gla(gq, gk, gv, gr, g, gla_head_norm[l])
        lam_vecs = jnp.stack([diff_lq1[l], diff_lk1[l], diff_lq2[l], diff_lk2[l]])
        zb = _diff_attn(dq, dk, dv, lam_vecs, diff_head_norm[l], lam_init)
        h = _merge(h, za, zb, sg, gt2, w_branch_a[l].astype(BF16), w_branch_b[l].astype(BF16),
                   w_out[l].astype(BF16))
        h = _ffn(h, sh3, sc3, gt3, ffn2_norm[l], ffn2_w_in[l].astype(BF16),
                 ffn2_w_out[l].astype(BF16), final_norm, final_norm=last)
    return h
```

```python
import functools
import itertools
import math

import jax
import jax.numpy as jnp
import numpy as np
from jax import lax
from jax.experimental import pallas as pl
from jax.experimental.pallas import tpu as pltpu

F32 = jnp.float32
BF16 = jnp.bfloat16

EPS = 1e-6
GLA_HEADS = 4
GLA_DK = 64
GLA_DV = 128
GLA_RANK = 16
GLA_TAU = 16.0
DIFF_HEADS = 4
DIFF_DH = 64
DIFF_DV = 128
ALIBI_MAX_BIAS = 8.0
N_MOD = 9

LOG2E = math.log2(math.e)
LANES = 128
GLA_CHUNK = 128
GLA_SUB = 16
EXP_CLAMP = 80.0
SUM_ROWS = 16
VMEM_LIMIT = 56 * 1024 * 1024


def _params(sem):
    return pltpu.CompilerParams(dimension_semantics=sem, vmem_limit_bytes=VMEM_LIMIT)


def _resident(shape):
    nd = len(shape)
    return pl.BlockSpec(shape, lambda *_: (0,) * nd, pipeline_mode=pl.Buffered(1))


def _rmsnorm(x, g):
    return x * lax.rsqrt(jnp.mean(x * x, axis=-1, keepdims=True) + EPS) * g


def _adaln_kernel(c_ref, w_ref, b_ref, o_ref):
    c = c_ref[...]
    ca = (c * jax.nn.sigmoid(c)).astype(BF16)
    o_ref[...] = jnp.dot(ca, w_ref[...].astype(BF16), preferred_element_type=F32) + b_ref[...]


def _adaln(c, w_ada, b_ada):
    bsz, d = c.shape
    n = w_ada.shape[1]
    tn = n // 4
    return pl.pallas_call(
        _adaln_kernel,
        out_shape=jax.ShapeDtypeStruct((bsz, n), F32),
        grid=(n // tn,),
        in_specs=[
            pl.BlockSpec((bsz, d), lambda j: (0, 0)),
            pl.BlockSpec((d, tn), lambda j: (0, j)),
            pl.BlockSpec((1, tn), lambda j: (0, j)),
        ],
        out_specs=pl.BlockSpec((bsz, tn), lambda j: (0, j)),
        compiler_params=_params(("arbitrary",)),
        name="adaln",
    )(c, w_ada, b_ada.reshape(1, n))


def _ffn_kernel(x_ref, sh_ref, sc_ref, gt_ref, nw_ref, win_ref, wout_ref, fn_ref, o_ref,
                acc_ref, *, d_ff, tf, sub_m, final_norm):
    n_sub = x_ref.shape[1] // sub_m

    def normed(t):
        x = x_ref[0, t * sub_m:(t + 1) * sub_m, :]
        return (_rmsnorm(x, nw_ref[...]) * (1.0 + sc_ref[0]) + sh_ref[0]).astype(BF16)

    def swiglu(t, u):
        rows = slice(t * sub_m, (t + 1) * sub_m)
        for i in range(d_ff // tf):
            cols = slice(i * tf, (i + 1) * tf)
            up = slice(d_ff + i * tf, d_ff + (i + 1) * tf)
            hg = jnp.dot(u, win_ref[:, cols].astype(BF16), preferred_element_type=F32)
            hu = jnp.dot(u, win_ref[:, up].astype(BF16), preferred_element_type=F32)
            act = (hg * jax.nn.sigmoid(hg) * hu).astype(BF16)
            part = jnp.dot(act, wout_ref[cols, :].astype(BF16), preferred_element_type=F32)
            if i == 0:
                acc_ref[rows, :] = part
            else:
                acc_ref[rows, :] += part
        h = x_ref[0, rows, :] + (0.5 * gt_ref[0]) * acc_ref[rows, :]
        if final_norm:
            h = _rmsnorm(h, fn_ref[...])
        o_ref[0, rows, :] = h

    u = normed(0)
    for t in range(n_sub):
        u_next = normed(t + 1) if t + 1 < n_sub else None
        swiglu(t, u)
        u = u_next


def _ffn(x, sh, sc, gt, nw, w_in, w_out, fn, *, final_norm, tm=512, sub_m=512, tf=256):
    bsz, s, d = x.shape
    d_ff = w_out.shape[0]
    vec = pl.BlockSpec((1, 1, d), lambda b, i: (b, 0, 0))
    row = pl.BlockSpec((1, tm, d), lambda b, i: (b, i, 0))
    return pl.pallas_call(
        functools.partial(_ffn_kernel, d_ff=d_ff, tf=tf, sub_m=sub_m, final_norm=final_norm),
        out_shape=jax.ShapeDtypeStruct((bsz, s, d), F32),
        grid=(bsz, s // tm),
        in_specs=[row, vec, vec, vec, _resident((1, d)), _resident((d, 2 * d_ff)),
                  _resident((d_ff, d)), _resident((1, d))],
        out_specs=row,
        scratch_shapes=[pltpu.VMEM((tm, d), F32)],
        compiler_params=_params(("parallel", "parallel")),
        name="ffn_final" if final_norm else "ffn",
    )(x, sh, sc, gt, nw.reshape(1, d), w_in, w_out, fn.reshape(1, d))


def _proj_kernel(x_ref, sh_ref, sc_ref, nw_ref, w_ref, wga_ref, w2_ref, b2_ref,
                 gq_ref, gk_ref, gv_ref, gr_ref, g_ref, dq_ref, dk_ref, dv_ref, sg_ref):
    x = x_ref[0]
    u = (_rmsnorm(x, nw_ref[...]) * (1.0 + sc_ref[0]) + sh_ref[0]).astype(BF16)

    def seg(start, size):
        return jnp.dot(u, w_ref[:, start:start + size].astype(BF16), preferred_element_type=F32)

    kw = GLA_HEADS * GLA_DK
    vw = GLA_HEADS * GLA_DV
    qw = DIFF_HEADS * 2 * DIFF_DH
    dvw = DIFF_HEADS * DIFF_DV
    off = 0
    gq_ref[0] = (seg(off, kw) * (GLA_DK ** -0.5)).astype(BF16)
    off += kw
    gk_ref[0] = seg(off, kw).astype(BF16)
    off += kw
    gv_ref[0] = seg(off, vw).astype(BF16)
    off += vw
    r = seg(off, vw)
    gr_ref[0] = (r * jax.nn.sigmoid(r)).astype(BF16)
    off += vw
    dq_ref[0] = (seg(off, qw) * (DIFF_DH ** -0.5 * LOG2E)).astype(BF16)
    off += qw
    dk_ref[0] = seg(off, qw).astype(BF16)
    off += qw
    dv_ref[0] = seg(off, dvw).astype(BF16)
    off += dvw
    n_gate = sg_ref.shape[-1]
    step = 512
    for j in range(n_gate // step):
        sg_ref[0, :, j * step:(j + 1) * step] = jax.nn.sigmoid(seg(off + j * step, step)).astype(BF16)
    a_low = jnp.dot(u, wga_ref[...].astype(BF16), preferred_element_type=F32).astype(BF16)
    z = jnp.dot(a_low, w2_ref[...].astype(BF16), preferred_element_type=F32) + b2_ref[...]
    log_sig = jnp.minimum(z, 0.0) - jnp.log1p(jnp.exp(-jnp.abs(z)))
    g_ref[0] = log_sig * (1.0 / GLA_TAU)


def _mixer_proj(h, sh, sc, nw, w_main, w_ga, w2, b2, *, tm=512):
    bsz, s, d = h.shape
    kw = GLA_HEADS * GLA_DK
    vw = GLA_HEADS * GLA_DV
    qw = DIFF_HEADS * 2 * DIFF_DH
    dvw = DIFF_HEADS * DIFF_DV
    n_gate = 2 * d
    vec = pl.BlockSpec((1, 1, d), lambda b, i: (b, 0, 0))

    def rows(w):
        return pl.BlockSpec((1, tm, w), lambda b, i: (b, i, 0))

    widths = [(kw, BF16), (kw, BF16), (vw, BF16), (vw, BF16), (kw, F32),
              (qw, BF16), (qw, BF16), (dvw, BF16), (n_gate, BF16)]
    return pl.pallas_call(
        _proj_kernel,
        out_shape=[jax.ShapeDtypeStruct((bsz, s, w), dt) for w, dt in widths],
        grid=(bsz, s // tm),
        in_specs=[rows(d), vec, vec, _resident((1, d)), _resident(w_main.shape),
                  _resident(w_ga.shape), _resident(w2.shape), _resident((1, kw))],
        out_specs=[rows(w) for w, _ in widths],
        compiler_params=_params(("parallel", "parallel")),
        name="mixer_proj",
    )(h, sh, sc, nw.reshape(1, d), w_main, w_ga, w2, b2.reshape(1, kw))


def _gla_kernel(q_ref, k_ref, v_ref, r_ref, g_ref, hn_ref, o_ref, st_ref, *, chunk, sub):
    @pl.when(pl.program_id(1) == 0)
    def _():
        st_ref[...] = jnp.zeros_like(st_ref)

    nsub = chunk // sub
    row = lax.broadcasted_iota(jnp.int32, (chunk, chunk), 0)
    col = lax.broadcasted_iota(jnp.int32, (chunk, chunk), 1)
    causal = col <= row
    cum_mat = jnp.concatenate(
        [jnp.where(causal, 1.0, 0.0), jnp.where(causal & (col >= (row // sub) * sub), 1.0, 0.0)],
        axis=0).astype(BF16)
    row_blk = lax.broadcasted_iota(jnp.int32, (chunk, LANES), 0) // sub
    lane_head = lax.broadcasted_iota(jnp.int32, (chunk, LANES), 1) // GLA_DK

    def cumsum(bi, pair):
        g = g_ref[bi, :, pair * LANES:(pair + 1) * LANES]
        g1 = g.astype(BF16)
        e1 = g - g1.astype(F32)
        g2 = e1.astype(BF16)
        g3 = (e1 - g2.astype(F32)).astype(BF16)
        cs = jnp.dot(cum_mat, jnp.concatenate([g1, g2, g3], axis=1), preferred_element_type=F32)
        return cs[:, :LANES] + cs[:, LANES:2 * LANES] + cs[:, 2 * LANES:]

    def decay(bi, pair, cs):
        lanes = slice(pair * LANES, (pair + 1) * LANES)
        b = cs[:chunk]
        w = cs[chunk:]
        b_last = b[chunk - 1:chunk]
        q = q_ref[bi, :, lanes].astype(F32)
        k = k_ref[bi, :, lanes].astype(F32)
        q_cat = []
        for j in range(nsub):
            ref_b = b[j * sub - 1:j * sub] if j else jnp.zeros_like(b_last)
            q_cat.append((q * jnp.exp(jnp.minimum(b - ref_b, 0.0))).astype(BF16))
        return dict(
            q_cat=jnp.concatenate(q_cat, axis=1),
            q_state=(q * jnp.exp(b)).astype(BF16),
            k_state=k * jnp.exp(b_last - b),
            k_hat=k * jnp.exp(jnp.minimum(-w, EXP_CLAMP)),
            chunk_decay=jnp.exp(b_last))

    def scores(d):
        out = []
        for hh in range(2):
            kh = jnp.where(lane_head == hh, d["k_hat"], 0.0).astype(BF16)
            k_cat = jnp.concatenate(
                [jnp.where(row_blk == j, kh, jnp.zeros_like(kh)) for j in range(nsub)], axis=1)
            out.append(lax.dot_general(d["q_cat"], k_cat, (((1,), (1,)), ((), ())),
                                       preferred_element_type=F32))
        return out

    def finish(bi, pair, d, sc):
        for hh in range(2):
            head = pair * 2 + hh
            cols = slice(head * GLA_DV, (head + 1) * GLA_DV)
            intra = jnp.where(causal, sc[hh], 0.0).astype(BF16)
            vh = v_ref[bi, :, cols]
            state_t = st_ref[bi * GLA_HEADS + head]
            o = jnp.dot(intra, vh, preferred_element_type=F32)
            o += lax.dot_general(d["q_state"], state_t.astype(BF16), (((1,), (1,)), ((), ())),
                                 preferred_element_type=F32)
            ks = jnp.where(lane_head == hh, d["k_state"], 0.0).astype(BF16)
            kv_t = lax.dot_general(vh, ks, (((0,), (0,)), ((), ())), preferred_element_type=F32)
            st_ref[bi * GLA_HEADS + head] = state_t * d["chunk_decay"] + kv_t
            y = _rmsnorm(o, hn_ref[...]) * r_ref[bi, :, cols].astype(F32)
            o_ref[bi, :, cols] = y.astype(BF16)

    chains = list(itertools.product(range(q_ref.shape[0]), range(GLA_HEADS // 2)))
    cums = [cumsum(*ch) for ch in chains]
    decayed, scored = {}, {}
    for t in range(len(chains) + 1):
        if t < len(chains):
            decayed[t] = decay(*chains[t], cums[t])
        if t >= 1:
            finish(*chains[t - 1], decayed.pop(t - 1), scored.pop(t - 1))
        if t < len(chains):
            scored[t] = scores(decayed[t])


def _gla(gq, gk, gv, gr, g, head_norm, *, nb=4):
    bsz, s, kw = gq.shape
    vw = gv.shape[-1]
    chunk = GLA_CHUNK

    def rows(w):
        return pl.BlockSpec((nb, chunk, w), lambda b, c: (b, c, 0))

    return pl.pallas_call(
        functools.partial(_gla_kernel, chunk=chunk, sub=GLA_SUB),
        out_shape=jax.ShapeDtypeStruct((bsz, s, vw), BF16),
        grid=(bsz // nb, s // chunk),
        in_specs=[rows(kw), rows(kw), rows(vw), rows(vw), rows(kw), _resident((1, GLA_DV))],
        out_specs=rows(vw),
        scratch_shapes=[pltpu.VMEM((nb * GLA_HEADS, GLA_DV, LANES), F32)],
        compiler_params=_params(("parallel", "arbitrary")),
        name="gla",
    )(gq, gk, gv, gr, g, head_norm.reshape(1, GLA_DV))


def _attn_kernel(q_ref, k_ref, v_ref, pos_ref, slope_ref, lam_ref, hn_ref, o_ref, kk_ref, qq_ref, vt_ref,
                 s0_ref, s1_ref, m_ref, acc_ref, *, tq, rc, lam_init):
    seq = k_ref.shape[1]
    nq = seq // tq
    c = slope_ref[0] * LOG2E
    c1 = c.astype(BF16).astype(F32)
    c2 = (c - c1).astype(BF16).astype(F32)
    c3 = c - c1 - c2

    kk_ref[:, :LANES] = k_ref[0]
    kk_ref[:, LANES:] = pos_ref[...]
    sub = lax.broadcasted_iota(jnp.int32, (LANES, 2 * tq), 0)
    cf = jnp.where((sub == 0) | (sub == 3), c1, jnp.where((sub == 1) | (sub == 4), c2, c3))
    slope_rows = jnp.where(sub < 6, cf, 0.0).astype(BF16)
    qq_ref[0, LANES:, :] = slope_rows
    qq_ref[1, LANES:, :] = slope_rows
    for jb in range(nq):
        vt_ref[jb, :DIFF_DV, :] = v_ref[0, jb * tq:(jb + 1) * tq, :].astype(F32).T.astype(BF16)
        vt_ref[jb, DIFF_DV:, :] = jnp.ones((SUM_ROWS, tq), BF16)
    nchunk = tq // rc
    lam = lam_ref[...]
    lam = (jnp.exp(jnp.sum(lam[0:1] * lam[1:2], axis=-1, keepdims=True))
           - jnp.exp(jnp.sum(lam[2:3] * lam[3:4], axis=-1, keepdims=True)) + lam_init)

    def logits(qs, j, s_blk):
        start = pl.multiple_of(j * tq, tq)
        s_blk[...] = jnp.dot(kk_ref[pl.ds(start, tq), :], qq_ref[qs], preferred_element_type=F32)

    def softmax_pv(qs, qi, j, s_blk, masked):
        m_old = m_ref[qs]
        shift = c * jnp.asarray((qi - j) * tq, F32)

        def load(r):
            blk = s_blk[r * rc:(r + 1) * rc, :]
            if masked:
                key = r * rc + lax.broadcasted_iota(jnp.int32, (rc, 2 * tq), 0)
                qry = lax.broadcasted_iota(jnp.int32, (rc, 2 * tq), 1) & (tq - 1)
                blk = jnp.where(key <= qry, blk, -jnp.inf)
            return blk

        mx = None
        for r in range(nchunk):
            cm = jnp.max(load(r).reshape(rc // 8, 8, 2 * tq), axis=0)
            mx = cm if mx is None else jnp.maximum(mx, cm)
        m_new = jnp.maximum(m_old, jnp.max(mx, axis=0, keepdims=True) - shift)
        alpha = jnp.exp2(m_old - m_new)
        m_shift = m_new + shift
        probs = jnp.concatenate(
            [jnp.exp2(load(r) - m_shift).astype(BF16) for r in range(nchunk)], axis=0)
        m_ref[qs] = m_new
        acc_ref[qs] = alpha * acc_ref[qs] + jnp.dot(vt_ref[j], probs, preferred_element_type=F32)

    def pair(p, carry):
        q_blk = (p, nq - 1 - p)
        for qs in range(2):
            q_t = q_ref[0, pl.ds(pl.multiple_of(q_blk[qs] * tq, tq), tq), :].astype(F32).T
            half = lax.broadcasted_iota(jnp.int32, (LANES, tq), 0)
            qq_ref[qs, :LANES, :tq] = jnp.where(half < DIFF_DH, q_t, 0.0).astype(BF16)
            qq_ref[qs, :LANES, tq:] = jnp.where(half >= DIFF_DH, q_t, 0.0).astype(BF16)
            m_ref[qs] = jnp.full((1, 2 * tq), -jnp.inf, F32)
            acc_ref[qs] = jnp.zeros((DIFF_DV + SUM_ROWS, 2 * tq), F32)
        items = [(1, q_blk[1], q_blk[1], True), (0, q_blk[0], q_blk[0], True)]
        for u in range(nq - 1):
            late = u < q_blk[1]
            items.append((jnp.where(late, 1, 0), jnp.where(late, q_blk[1], q_blk[0]),
                          jnp.where(late, u, u - q_blk[1]), False))
        bufs = (s0_ref, s1_ref)
        logits(items[0][0], items[0][2], bufs[0])
        for t, (qs, qi, j, masked) in enumerate(items):
            if t + 1 < len(items):
                logits(items[t + 1][0], items[t + 1][2], bufs[(t + 1) % 2])
            softmax_pv(qs, qi, j, bufs[t % 2], masked)
        for qs in range(2):
            out = acc_ref[qs, :DIFF_DV, :] * (1.0 / acc_ref[qs, DIFF_DV:DIFF_DV + 1, :])
            y_t = out[:, :tq] - lam * out[:, tq:]
            y_t = y_t * lax.rsqrt(jnp.mean(y_t * y_t, axis=0, keepdims=True) + EPS)
            rows = pl.ds(pl.multiple_of(q_blk[qs] * tq, tq), tq)
            o_ref[0, rows, :] = (y_t.T * hn_ref[...] * (1.0 - lam_init)).astype(BF16)
        return carry

    lax.fori_loop(0, nq // 2, pair, 0)


def _diff_attn(dq, dk, dv, lam_vecs, head_norm, lam_init, *, tq=512, rc=256):
    bsz, s, _ = dq.shape
    slopes = jnp.asarray(
        [2.0 ** (-ALIBI_MAX_BIAS * (i + 1) / DIFF_HEADS) for i in range(DIFF_HEADS)], F32
    ).reshape(DIFF_HEADS, 1, 1)
    pos = np.arange(s) % tq
    feat = np.zeros((s, LANES), np.float32)
    feat[:, 0:3] = (pos - pos % 16)[:, None]
    feat[:, 3:6] = (pos % 16)[:, None]
    return pl.pallas_call(
        functools.partial(_attn_kernel, tq=tq, rc=rc, lam_init=lam_init),
        out_shape=jax.ShapeDtypeStruct((bsz, s, DIFF_HEADS * DIFF_DV), BF16),
        grid=(bsz, DIFF_HEADS),
        in_specs=[
            pl.BlockSpec((1, s, LANES), lambda b, h: (b, 0, h)),
            pl.BlockSpec((1, s, LANES), lambda b, h: (b, 0, h)),
            pl.BlockSpec((1, s, DIFF_DV), lambda b, h: (b, 0, h)),
            _resident((s, LANES)),
            pl.BlockSpec((1, 1, 1), lambda b, h: (h, 0, 0)),
            _resident(lam_vecs.shape),
            _resident((1, DIFF_DV)),
        ],
        out_specs=pl.BlockSpec((1, s, DIFF_DV), lambda b, h: (b, 0, h)),
        scratch_shapes=[
            pltpu.VMEM((s, 2 * LANES), BF16),
            pltpu.VMEM((2, 2 * LANES, 2 * tq), BF16),
            pltpu.VMEM((s // tq, DIFF_DV + SUM_ROWS, tq), BF16),
            pltpu.VMEM((tq, 2 * tq), F32),
            pltpu.VMEM((tq, 2 * tq), F32),
            pltpu.VMEM((2, 1, 2 * tq), F32),
            pltpu.VMEM((2, DIFF_DV + SUM_ROWS, 2 * tq), F32),
        ],
        compiler_params=_params(("parallel", "parallel")),
        name="diff_attn",
    )(dq, dk, dv, jnp.asarray(feat, BF16), slopes, lam_vecs, head_norm.reshape(1, DIFF_DV))


def _merge_kernel(h_ref, za_ref, zb_ref, sg_ref, gt_ref, wa_ref, wb_ref, wo_ref, o_ref):
    d = h_ref.shape[-1]
    ya = jnp.dot(za_ref[0], wa_ref[...].astype(BF16), preferred_element_type=F32)
    yb = jnp.dot(zb_ref[0], wb_ref[...].astype(BF16), preferred_element_type=F32)
    mix = sg_ref[0, :, :d].astype(F32) * ya + sg_ref[0, :, d:].astype(F32) * yb
    m = jnp.dot(mix.astype(BF16), wo_ref[...].astype(BF16), preferred_element_type=F32)
    o_ref[0] = h_ref[0] + gt_ref[0] * m


def _merge(h, za, zb, sg, gt, w_a, w_b, w_o, *, tm=512):
    bsz, s, d = h.shape
    vec = pl.BlockSpec((1, 1, d), lambda b, i: (b, 0, 0))

    def rows(w):
        return pl.BlockSpec((1, tm, w), lambda b, i: (b, i, 0))

    return pl.pallas_call(
        _merge_kernel,
        out_shape=jax.ShapeDtypeStruct((bsz, s, d), F32),
        grid=(bsz, s // tm),
        in_specs=[rows(d), rows(za.shape[-1]), rows(zb.shape[-1]), rows(2 * d), vec,
                  _resident(w_a.shape), _resident(w_b.shape), _resident(w_o.shape)],
        out_specs=rows(d),
        compiler_params=_params(("parallel", "parallel")),
        name="merge",
    )(h, za, zb, sg, gt, w_a, w_b, w_o)


def kernel(x, c, w_ada, b_ada, ffn1_norm, ffn1_w_in, ffn1_w_out, mix_norm, w_in, gla_alpha_w2, gla_alpha_b, gla_head_norm, diff_lq1, diff_lk1, diff_lq2, diff_lk2, diff_head_norm, w_branch_a, w_branch_b, w_out, ffn2_norm, ffn2_w_in, ffn2_w_out, final_norm):
    depth = w_ada.shape[0]
    bsz, _, d = x.shape
    kw = GLA_HEADS * GLA_DK
    vw = GLA_HEADS * GLA_DV
    ga_start = 2 * kw + 2 * vw
    h = x
    for l in range(depth):
        lam_init = 0.8 - 0.6 * math.exp(-0.3 * l)
        mod = _adaln(c, w_ada[l], b_ada[l])
        sh1, sc1, gt1, sh2, sc2, gt2, sh3, sc3, gt3 = [
            mod[:, i * d:(i + 1) * d].reshape(bsz, 1, d) for i in range(N_MOD)]
        last = l == depth - 1
        h = _ffn(h, sh1, sc1, gt1, ffn1_norm[l], ffn1_w_in[l], ffn1_w_out[l], final_norm,
                 final_norm=False)
        w = w_in[l]
        w_main = jnp.concatenate([w[:, :ga_start], w[:, ga_start + GLA_RANK:]], axis=1)
        w_ga = w[:, ga_start:ga_start + GLA_RANK]
        gq, gk, gv, gr, g, dq, dk, dv, sg = _mixer_proj(
            h, sh2, sc2, mix_norm[l], w_main, w_ga, gla_alpha_w2[l], gla_alpha_b[l])
        za = _gla(gq, gk, gv, gr, g, gla_head_norm[l])
        lam_vecs = jnp.stack([diff_lq1[l], diff_lk1[l], diff_lq2[l], diff_lk2[l]])
        zb = _diff_attn(dq, dk, dv, lam_vecs, diff_head_norm[l], lam_init)
        h = _merge(h, za, zb, sg, gt2, w_branch_a[l], w_branch_b[l], w_out[l])
        h = _ffn(h, sh3, sc3, gt3, ffn2_norm[l], ffn2_w_in[l], ffn2_w_out[l], final_norm,
                 final_norm=last)
    return h
```

```python
import functools
import itertools
import math

import jax
import jax.numpy as jnp
import numpy as np
from jax import lax
from jax.experimental import pallas as pl
from jax.experimental.pallas import tpu as pltpu

F32 = jnp.float32
BF16 = jnp.bfloat16

EPS = 1e-6
GLA_HEADS = 4
GLA_DK = 64
GLA_DV = 128
GLA_RANK = 16
GLA_TAU = 16.0
DIFF_HEADS = 4
DIFF_DH = 64
DIFF_DV = 128
ALIBI_MAX_BIAS = 8.0
N_MOD = 9

LOG2E = math.log2(math.e)
LANES = 128
GLA_CHUNK = 128
GLA_SUB = 16
EXP_CLAMP = 80.0
SUM_ROWS = 16
VMEM_LIMIT = 56 * 1024 * 1024


def _params(sem):
    return pltpu.CompilerParams(dimension_semantics=sem, vmem_limit_bytes=VMEM_LIMIT)


def _resident(shape):
    nd = len(shape)
    return pl.BlockSpec(shape, lambda *_: (0,) * nd, pipeline_mode=pl.Buffered(1))


def _rmsnorm(x, g):
    return x * lax.rsqrt(jnp.mean(x * x, axis=-1, keepdims=True) + EPS) * g


def _adaln_kernel(c_ref, w_ref, b_ref, o_ref):
    c = c_ref[...]
    ca = (c * jax.nn.sigmoid(c)).astype(BF16)
    o_ref[...] = jnp.dot(ca, w_ref[...].astype(BF16), preferred_element_type=F32) + b_ref[...]


def _adaln(c, w_ada, b_ada):
    bsz, d = c.shape
    n = w_ada.shape[1]
    tn = n // 4
    return pl.pallas_call(
        _adaln_kernel,
        out_shape=jax.ShapeDtypeStruct((bsz, n), F32),
        grid=(n // tn,),
        in_specs=[
            pl.BlockSpec((bsz, d), lambda j: (0, 0)),
            pl.BlockSpec((d, tn), lambda j: (0, j)),
            pl.BlockSpec((1, tn), lambda j: (0, j)),
        ],
        out_specs=pl.BlockSpec((bsz, tn), lambda j: (0, j)),
        compiler_params=_params(("arbitrary",)),
        name="adaln",
    )(c, w_ada, b_ada.reshape(1, n))


def _ffn_kernel(x_ref, sh_ref, sc_ref, gt_ref, nw_ref, win_ref, wout_ref, fn_ref, o_ref,
                acc_ref, *, d_ff, tf, sub_m, final_norm):
    n_sub = x_ref.shape[1] // sub_m

    def normed(t):
        x = x_ref[0, t * sub_m:(t + 1) * sub_m, :]
        return (_rmsnorm(x, nw_ref[...]) * (1.0 + sc_ref[0]) + sh_ref[0]).astype(BF16)

    def swiglu(t, u):
        rows = slice(t * sub_m, (t + 1) * sub_m)
        for i in range(d_ff // tf):
            cols = slice(i * tf, (i + 1) * tf)
            up = slice(d_ff + i * tf, d_ff + (i + 1) * tf)
            hg = jnp.dot(u, win_ref[:, cols].astype(BF16), preferred_element_type=F32)
            hu = jnp.dot(u, win_ref[:, up].astype(BF16), preferred_element_type=F32)
            act = (hg * jax.nn.sigmoid(hg) * hu).astype(BF16)
            part = jnp.dot(act, wout_ref[cols, :].astype(BF16), preferred_element_type=F32)
            if i == 0:
                acc_ref[rows, :] = part
            else:
                acc_ref[rows, :] += part
        h = x_ref[0, rows, :] + (0.5 * gt_ref[0]) * acc_ref[rows, :]
        if final_norm:
            h = _rmsnorm(h, fn_ref[...])
        o_ref[0, rows, :] = h

    u = normed(0)
    for t in range(n_sub):
        u_next = normed(t + 1) if t + 1 < n_sub else None
        swiglu(t, u)
        u = u_next


def _ffn(x, sh, sc, gt, nw, w_in, w_out, fn, *, final_norm, tm=512, sub_m=512, tf=256):
    bsz, s, d = x.shape
    d_ff = w_out.shape[0]
    vec = pl.BlockSpec((1, 1, d), lambda b, i: (b, 0, 0))
    row = pl.BlockSpec((1, tm, d), lambda b, i: (b, i, 0))
    return pl.pallas_call(
        functools.partial(_ffn_kernel, d_ff=d_ff, tf=tf, sub_m=sub_m, final_norm=final_norm),
        out_shape=jax.ShapeDtypeStruct((bsz, s, d), F32),
        grid=(bsz, s // tm),
        in_specs=[row, vec, vec, vec, _resident((1, d)), _resident((d, 2 * d_ff)),
                  _resident((d_ff, d)), _resident((1, d))],
        out_specs=row,
        scratch_shapes=[pltpu.VMEM((tm, d), F32)],
        compiler_params=_params(("parallel", "parallel")),
        name="ffn_final" if final_norm else "ffn",
    )(x, sh, sc, gt, nw.reshape(1, d), w_in, w_out, fn.reshape(1, d))


def _split_w_in_kernel(w_ref, main_ref, ga_ref, *, ga_start):
    main_ref[:, :ga_start] = w_ref[:, :ga_start].astype(BF16)
    main_ref[:, ga_start:] = w_ref[:, ga_start + GLA_RANK:].astype(BF16)
    ga_ref[...] = w_ref[:, ga_start:ga_start + GLA_RANK].astype(BF16)


def _split_w_in(w, ga_start, *, tr=128):
    d, n = w.shape
    return pl.pallas_call(
        functools.partial(_split_w_in_kernel, ga_start=ga_start),
        out_shape=[jax.ShapeDtypeStruct((d, n - GLA_RANK), BF16),
                   jax.ShapeDtypeStruct((d, GLA_RANK), BF16)],
        grid=(d // tr,),
        in_specs=[pl.BlockSpec((tr, n), lambda i: (i, 0))],
        out_specs=[pl.BlockSpec((tr, n - GLA_RANK), lambda i: (i, 0)),
                   pl.BlockSpec((tr, GLA_RANK), lambda i: (i, 0))],
        compiler_params=_params(("parallel",)),
        name="split_w_in",
    )(w)


def _proj_kernel(x_ref, sh_ref, sc_ref, nw_ref, w_ref, wga_ref, w2_ref, b2_ref,
                 gq_ref, gk_ref, gv_ref, gr_ref, g_ref, dq_ref, dk_ref, dv_ref, sg_ref):
    x = x_ref[0]
    u = (_rmsnorm(x, nw_ref[...]) * (1.0 + sc_ref[0]) + sh_ref[0]).astype(BF16)

    def seg(start, size):
        return jnp.dot(u, w_ref[:, start:start + size], preferred_element_type=F32)

    kw = GLA_HEADS * GLA_DK
    vw = GLA_HEADS * GLA_DV
    qw = DIFF_HEADS * 2 * DIFF_DH
    dvw = DIFF_HEADS * DIFF_DV
    off = 0
    gq_ref[0] = (seg(off, kw) * (GLA_DK ** -0.5)).astype(BF16)
    off += kw
    gk_ref[0] = seg(off, kw).astype(BF16)
    off += kw
    gv_ref[0] = seg(off, vw).astype(BF16)
    off += vw
    r = seg(off, vw)
    gr_ref[0] = (r * jax.nn.sigmoid(r)).astype(BF16)
    off += vw
    dq_ref[0] = (seg(off, qw) * (DIFF_DH ** -0.5 * LOG2E)).astype(BF16)
    off += qw
    dk_ref[0] = seg(off, qw).astype(BF16)
    off += qw
    dv_ref[0] = seg(off, dvw).astype(BF16)
    off += dvw
    n_gate = sg_ref.shape[-1]
    step = 512
    for j in range(n_gate // step):
        sg_ref[0, :, j * step:(j + 1) * step] = jax.nn.sigmoid(seg(off + j * step, step)).astype(BF16)
    a_low = jnp.dot(u, wga_ref[...], preferred_element_type=F32).astype(BF16)
    z = jnp.dot(a_low, w2_ref[...].astype(BF16), preferred_element_type=F32) + b2_ref[...]
    log_sig = jnp.minimum(z, 0.0) - jnp.log1p(jnp.exp(-jnp.abs(z)))
    g_ref[0] = log_sig * (1.0 / GLA_TAU)


def _mixer_proj(h, sh, sc, nw, w_main, w_ga, w2, b2, *, tm=512):
    bsz, s, d = h.shape
    kw = GLA_HEADS * GLA_DK
    vw = GLA_HEADS * GLA_DV
    qw = DIFF_HEADS * 2 * DIFF_DH
    dvw = DIFF_HEADS * DIFF_DV
    n_gate = 2 * d
    vec = pl.BlockSpec((1, 1, d), lambda b, i: (b, 0, 0))

    def rows(w):
        return pl.BlockSpec((1, tm, w), lambda b, i: (b, i, 0))

    widths = [(kw, BF16), (kw, BF16), (vw, BF16), (vw, BF16), (kw, F32),
              (qw, BF16), (qw, BF16), (dvw, BF16), (n_gate, BF16)]
    return pl.pallas_call(
        _proj_kernel,
        out_shape=[jax.ShapeDtypeStruct((bsz, s, w), dt) for w, dt in widths],
        grid=(bsz, s // tm),
        in_specs=[rows(d), vec, vec, _resident((1, d)), _resident(w_main.shape),
                  _resident(w_ga.shape), _resident(w2.shape), _resident((1, kw))],
        out_specs=[rows(w) for w, _ in widths],
        compiler_params=_params(("parallel", "parallel")),
        name="mixer_proj",
    )(h, sh, sc, nw.reshape(1, d), w_main, w_ga, w2, b2.reshape(1, kw))


def _gla_kernel(q_ref, k_ref, v_ref, r_ref, g_ref, hn_ref, o_ref, st_ref, *, chunk, sub):
    @pl.when(pl.program_id(1) == 0)
    def _():
        st_ref[...] = jnp.zeros_like(st_ref)

    nsub = chunk // sub
    row = lax.broadcasted_iota(jnp.int32, (chunk, chunk), 0)
    col = lax.broadcasted_iota(jnp.int32, (chunk, chunk), 1)
    causal = col <= row
    cum_mat = jnp.concatenate(
        [jnp.where(causal, 1.0, 0.0), jnp.where(causal & (col >= (row // sub) * sub), 1.0, 0.0)],
        axis=0).astype(BF16)
    row_blk = lax.broadcasted_iota(jnp.int32, (chunk, LANES), 0) // sub
    lane_head = lax.broadcasted_iota(jnp.int32, (chunk, LANES), 1) // GLA_DK

    def cumsum(bi, pair):
        g = g_ref[bi, :, pair * LANES:(pair + 1) * LANES]
        g1 = g.astype(BF16)
        e1 = g - g1.astype(F32)
        g2 = e1.astype(BF16)
        g3 = (e1 - g2.astype(F32)).astype(BF16)
        cs = jnp.dot(cum_mat, jnp.concatenate([g1, g2, g3], axis=1), preferred_element_type=F32)
        return cs[:, :LANES] + cs[:, LANES:2 * LANES] + cs[:, 2 * LANES:]

    def decay(bi, pair, cs):
        lanes = slice(pair * LANES, (pair + 1) * LANES)
        b = cs[:chunk]
        w = cs[chunk:]
        b_last = b[chunk - 1:chunk]
        q = q_ref[bi, :, lanes].astype(F32)
        k = k_ref[bi, :, lanes].astype(F32)
        q_cat = []
        for j in range(nsub):
            ref_b = b[j * sub - 1:j * sub] if j else jnp.zeros_like(b_last)
            q_cat.append((q * jnp.exp(jnp.minimum(b - ref_b, 0.0))).astype(BF16))
        return dict(
            q_cat=jnp.concatenate(q_cat, axis=1),
            q_state=(q * jnp.exp(b)).astype(BF16),
            k_state=k * jnp.exp(b_last - b),
            k_hat=k * jnp.exp(jnp.minimum(-w, EXP_CLAMP)),
            chunk_decay=jnp.exp(b_last))

    def scores(d):
        out = []
        for hh in range(2):
            kh = jnp.where(lane_head == hh, d["k_hat"], 0.0).astype(BF16)
            k_cat = jnp.concatenate(
                [jnp.where(row_blk == j, kh, jnp.zeros_like(kh)) for j in range(nsub)], axis=1)
            out.append(lax.dot_general(d["q_cat"], k_cat, (((1,), (1,)), ((), ())),
                                       preferred_element_type=F32))
        return out

    def finish(bi, pair, d, sc):
        for hh in range(2):
            head = pair * 2 + hh
            cols = slice(head * GLA_DV, (head + 1) * GLA_DV)
            intra = jnp.where(causal, sc[hh], 0.0).astype(BF16)
            vh = v_ref[bi, :, cols]
            state_t = st_ref[bi * GLA_HEADS + head]
            o = jnp.dot(intra, vh, preferred_element_type=F32)
            o += lax.dot_general(d["q_state"], state_t.astype(BF16), (((1,), (1,)), ((), ())),
                                 preferred_element_type=F32)
            ks = jnp.where(lane_head == hh, d["k_state"], 0.0).astype(BF16)
            kv_t = lax.dot_general(vh, ks, (((0,), (0,)), ((), ())), preferred_element_type=F32)
            st_ref[bi * GLA_HEADS + head] = state_t * d["chunk_decay"] + kv_t
            y = _rmsnorm(o, hn_ref[...]) * r_ref[bi, :, cols].astype(F32)
            o_ref[bi, :, cols] = y.astype(BF16)

    chains = list(itertools.product(range(q_ref.shape[0]), range(GLA_HEADS // 2)))
    cums = [cumsum(*ch) for ch in chains]
    decayed, scored = {}, {}
    for t in range(len(chains) + 1):
        if t < len(chains):
            decayed[t] = decay(*chains[t], cums[t])
        if t >= 1:
            finish(*chains[t - 1], decayed.pop(t - 1), scored.pop(t - 1))
        if t < len(chains):
            scored[t] = scores(decayed[t])


def _gla(gq, gk, gv, gr, g, head_norm, *, nb=4):
    bsz, s, kw = gq.shape
    vw = gv.shape[-1]
    chunk = GLA_CHUNK

    def rows(w):
        return pl.BlockSpec((nb, chunk, w), lambda b, c: (b, c, 0))

    return pl.pallas_call(
        functools.partial(_gla_kernel, chunk=chunk, sub=GLA_SUB),
        out_shape=jax.ShapeDtypeStruct((bsz, s, vw), BF16),
        grid=(bsz // nb, s // chunk),
        in_specs=[rows(kw), rows(kw), rows(vw), rows(vw), rows(kw), _resident((1, GLA_DV))],
        out_specs=rows(vw),
        scratch_shapes=[pltpu.VMEM((nb * GLA_HEADS, GLA_DV, LANES), F32)],
        compiler_params=_params(("parallel", "arbitrary")),
        name="gla",
    )(gq, gk, gv, gr, g, head_norm.reshape(1, GLA_DV))


def _attn_kernel(q_ref, k_ref, v_ref, pos_ref, slope_ref, lam_ref, hn_ref, o_ref, kk_ref, qq_ref, vt_ref,
                 s0_ref, s1_ref, m_ref, acc_ref, *, tq, rc, lam_init):
    seq = k_ref.shape[1]
    nq = seq // tq
    c = slope_ref[0] * LOG2E
    c1 = c.astype(BF16).astype(F32)
    c2 = (c - c1).astype(BF16).astype(F32)
    c3 = c - c1 - c2

    kk_ref[:, :LANES] = k_ref[0]
    kk_ref[:, LANES:] = pos_ref[...]
    sub = lax.broadcasted_iota(jnp.int32, (LANES, 2 * tq), 0)
    cf = jnp.where((sub == 0) | (sub == 3), c1, jnp.where((sub == 1) | (sub == 4), c2, c3))
    slope_rows = jnp.where(sub < 6, cf, 0.0).astype(BF16)
    qq_ref[0, LANES:, :] = slope_rows
    qq_ref[1, LANES:, :] = slope_rows
    for jb in range(nq):
        vt_ref[jb, :DIFF_DV, :] = v_ref[0, jb * tq:(jb + 1) * tq, :].astype(F32).T.astype(BF16)
        vt_ref[jb, DIFF_DV:, :] = jnp.ones((SUM_ROWS, tq), BF16)
    nchunk = tq // rc
    lam = lam_ref[...]
    lam = (jnp.exp(jnp.sum(lam[0:1] * lam[1:2], axis=-1, keepdims=True))
           - jnp.exp(jnp.sum(lam[2:3] * lam[3:4], axis=-1, keepdims=True)) + lam_init)

    def logits(qs, j, s_blk):
        start = pl.multiple_of(j * tq, tq)
        s_blk[...] = jnp.dot(kk_ref[pl.ds(start, tq), :], qq_ref[qs], preferred_element_type=F32)

    def softmax_pv(qs, qi, j, s_blk, masked):
        m_old = m_ref[qs]
        shift = c * jnp.asarray((qi - j) * tq, F32)

        def load(r):
            blk = s_blk[r * rc:(r + 1) * rc, :]
            if masked:
                key = r * rc + lax.broadcasted_iota(jnp.int32, (rc, 2 * tq), 0)
                qry = lax.broadcasted_iota(jnp.int32, (rc, 2 * tq), 1) & (tq - 1)
                blk = jnp.where(key <= qry, blk, -jnp.inf)
            return blk

        mx = None
        for r in range(nchunk):
            cm = jnp.max(load(r).reshape(rc // 8, 8, 2 * tq), axis=0)
            mx = cm if mx is None else jnp.maximum(mx, cm)
        m_new = jnp.maximum(m_old, jnp.max(mx, axis=0, keepdims=True) - shift)
        alpha = jnp.exp2(m_old - m_new)
        m_shift = m_new + shift
        probs = jnp.concatenate(
            [jnp.exp2(load(r) - m_shift).astype(BF16) for r in range(nchunk)], axis=0)
        m_ref[qs] = m_new
        acc_ref[qs] = alpha * acc_ref[qs] + jnp.dot(vt_ref[j], probs, preferred_element_type=F32)

    def pair(p, carry):
        q_blk = (p, nq - 1 - p)
        for qs in range(2):
            q_t = q_ref[0, pl.ds(pl.multiple_of(q_blk[qs] * tq, tq), tq), :].astype(F32).T
            half = lax.broadcasted_iota(jnp.int32, (LANES, tq), 0)
            qq_ref[qs, :LANES, :tq] = jnp.where(half < DIFF_DH, q_t, 0.0).astype(BF16)
            qq_ref[qs, :LANES, tq:] = jnp.where(half >= DIFF_DH, q_t, 0.0).astype(BF16)
            m_ref[qs] = jnp.full((1, 2 * tq), -jnp.inf, F32)
            acc_ref[qs] = jnp.zeros((DIFF_DV + SUM_ROWS, 2 * tq), F32)
        items = [(1, q_blk[1], q_blk[1], True), (0, q_blk[0], q_blk[0], True)]
        for u in range(nq - 1):
            late = u < q_blk[1]
            items.append((jnp.where(late, 1, 0), jnp.where(late, q_blk[1], q_blk[0]),
                          jnp.where(late, u, u - q_blk[1]), False))
        bufs = (s0_ref, s1_ref)
        logits(items[0][0], items[0][2], bufs[0])
        for t, (qs, qi, j, masked) in enumerate(items):
            if t + 1 < len(items):
                logits(items[t + 1][0], items[t + 1][2], bufs[(t + 1) % 2])
            softmax_pv(qs, qi, j, bufs[t % 2], masked)
        for qs in range(2):
            out = acc_ref[qs, :DIFF_DV, :] * (1.0 / acc_ref[qs, DIFF_DV:DIFF_DV + 1, :])
            y_t = out[:, :tq] - lam * out[:, tq:]
            y_t = y_t * lax.rsqrt(jnp.mean(y_t * y_t, axis=0, keepdims=True) + EPS)
            rows = pl.ds(pl.multiple_of(q_blk[qs] * tq, tq), tq)
            o_ref[0, rows, :] = (y_t.T * hn_ref[...] * (1.0 - lam_init)).astype(BF16)
        return carry

    lax.fori_loop(0, nq // 2, pair, 0)


def _diff_attn(dq, dk, dv, lam_vecs, head_norm, lam_init, *, tq=512, rc=256):
    bsz, s, _ = dq.shape
    slopes = jnp.asarray(
        [2.0 ** (-ALIBI_MAX_BIAS * (i + 1) / DIFF_HEADS) for i in range(DIFF_HEADS)], F32
    ).reshape(DIFF_HEADS, 1, 1)
    pos = np.arange(s) % tq
    feat = np.zeros((s, LANES), np.float32)
    feat[:, 0:3] = (pos - pos % 16)[:, None]
    feat[:, 3:6] = (pos % 16)[:, None]
    return pl.pallas_call(
        functools.partial(_attn_kernel, tq=tq, rc=rc, lam_init=lam_init),
        out_shape=jax.ShapeDtypeStruct((bsz, s, DIFF_HEADS * DIFF_DV), BF16),
        grid=(bsz, DIFF_HEADS),
        in_specs=[
            pl.BlockSpec((1, s, LANES), lambda b, h: (b, 0, h)),
            pl.BlockSpec((1, s, LANES), lambda b, h: (b, 0, h)),
            pl.BlockSpec((1, s, DIFF_DV), lambda b, h: (b, 0, h)),
            _resident((s, LANES)),
            pl.BlockSpec((1, 1, 1), lambda b, h: (h, 0, 0)),
            _resident(lam_vecs.shape),
            _resident((1, DIFF_DV)),
        ],
        out_specs=pl.BlockSpec((1, s, DIFF_DV), lambda b, h: (b, 0, h)),
        scratch_shapes=[
            pltpu.VMEM((s, 2 * LANES), BF16),
            pltpu.VMEM((2, 2 * LANES, 2 * tq), BF16),
            pltpu.VMEM((s // tq, DIFF_DV + SUM_ROWS, tq), BF16),
            pltpu.VMEM((tq, 2 * tq), F32),
            pltpu.VMEM((tq, 2 * tq), F32),
            pltpu.VMEM((2, 1, 2 * tq), F32),
            pltpu.VMEM((2, DIFF_DV + SUM_ROWS, 2 * tq), F32),
        ],
        compiler_params=_params(("parallel", "parallel")),
        name="diff_attn",
    )(dq, dk, dv, jnp.asarray(feat, BF16), slopes, lam_vecs, head_norm.reshape(1, DIFF_DV))


def _merge_kernel(h_ref, za_ref, zb_ref, sg_ref, gt_ref, wa_ref, wb_ref, wo_ref, o_ref):
    d = h_ref.shape[-1]
    ya = jnp.dot(za_ref[0], wa_ref[...].astype(BF16), preferred_element_type=F32)
    yb = jnp.dot(zb_ref[0], wb_ref[...].astype(BF16), preferred_element_type=F32)
    mix = sg_ref[0, :, :d].astype(F32) * ya + sg_ref[0, :, d:].astype(F32) * yb
    m = jnp.dot(mix.astype(BF16), wo_ref[...].astype(BF16), preferred_element_type=F32)
    o_ref[0] = h_ref[0] + gt_ref[0] * m


def _merge(h, za, zb, sg, gt, w_a, w_b, w_o, *, tm=512):
    bsz, s, d = h.shape
    vec = pl.BlockSpec((1, 1, d), lambda b, i: (b, 0, 0))

    def rows(w):
        return pl.BlockSpec((1, tm, w), lambda b, i: (b, i, 0))

    return pl.pallas_call(
        _merge_kernel,
        out_shape=jax.ShapeDtypeStruct((bsz, s, d), F32),
        grid=(bsz, s // tm),
        in_specs=[rows(d), rows(za.shape[-1]), rows(zb.shape[-1]), rows(2 * d), vec,
                  _resident(w_a.shape), _resident(w_b.shape), _resident(w_o.shape)],
        out_specs=rows(d),
        compiler_params=_params(("parallel", "parallel")),
        name="merge",
    )(h, za, zb, sg, gt, w_a, w_b, w_o)


def kernel(x, c, w_ada, b_ada, ffn1_norm, ffn1_w_in, ffn1_w_out, mix_norm, w_in, gla_alpha_w2, gla_alpha_b, gla_head_norm, diff_lq1, diff_lk1, diff_lq2, diff_lk2, diff_head_norm, w_branch_a, w_branch_b, w_out, ffn2_norm, ffn2_w_in, ffn2_w_out, final_norm):
    depth = w_ada.shape[0]
    bsz, _, d = x.shape
    kw = GLA_HEADS * GLA_DK
    vw = GLA_HEADS * GLA_DV
    ga_start = 2 * kw + 2 * vw
    h = x
    for l in range(depth):
        lam_init = 0.8 - 0.6 * math.exp(-0.3 * l)
        mod = _adaln(c, w_ada[l], b_ada[l])
        sh1, sc1, gt1, sh2, sc2, gt2, sh3, sc3, gt3 = [
            mod[:, i * d:(i + 1) * d].reshape(bsz, 1, d) for i in range(N_MOD)]
        last = l == depth - 1
        h = _ffn(h, sh1, sc1, gt1, ffn1_norm[l], ffn1_w_in[l], ffn1_w_out[l], final_norm,
                 final_norm=False)
        w = w_in[l]
        w_main, w_ga = _split_w_in(w, ga_start)
        gq, gk, gv, gr, g, dq, dk, dv, sg = _mixer_proj(
            h, sh2, sc2, mix_norm[l], w_main, w_ga, gla_alpha_w2[l], gla_alpha_b[l])
        za = _gla(gq, gk, gv, gr, g, gla_head_norm[l])
        lam_vecs = jnp.stack([diff_lq1[l], diff_lk1[l], diff_lq2[l], diff_lk2[l]])
        zb = _diff_attn(dq, dk, dv, lam_vecs, diff_head_norm[l], lam_init)
        h = _merge(h, za, zb, sg, gt2, w_branch_a[l], w_branch_b[l], w_out[l])
        h = _ffn(h, sh3, sc3, gt3, ffn2_norm[l], ffn2_w_in[l], ffn2_w_out[l], final_norm,
                 final_norm=last)
    return h
```

```python
import functools
import itertools
import math

import jax
import jax.numpy as jnp
import numpy as np
from jax import lax
from jax.experimental import pallas as pl
from jax.experimental.pallas import tpu as pltpu

F32 = jnp.float32
BF16 = jnp.bfloat16

EPS = 1e-6
GLA_HEADS = 4
GLA_DK = 64
GLA_DV = 128
GLA_RANK = 16
GLA_TAU = 16.0
DIFF_HEADS = 4
DIFF_DH = 64
DIFF_DV = 128
ALIBI_MAX_BIAS = 8.0
N_MOD = 9

LOG2E = math.log2(math.e)
LANES = 128
GLA_CHUNK = 128
GLA_SUB = 16
EXP_CLAMP = 80.0
SUM_ROWS = 16
VMEM_LIMIT = 56 * 1024 * 1024


def _params(sem, **flags):
    return pltpu.CompilerParams(dimension_semantics=sem, vmem_limit_bytes=VMEM_LIMIT,
                                flags=flags or None)


def _resident(shape):
    nd = len(shape)
    return pl.BlockSpec(shape, lambda *_: (0,) * nd, pipeline_mode=pl.Buffered(1))


def _rmsnorm(x, g):
    return x * lax.rsqrt(jnp.mean(x * x, axis=-1, keepdims=True) + EPS) * g


def _adaln_kernel(c_ref, w_ref, b_ref, o_ref):
    c = c_ref[...]
    ca = (c * jax.nn.sigmoid(c)).astype(BF16)
    o_ref[...] = jnp.dot(ca, w_ref[...].astype(BF16), preferred_element_type=F32) + b_ref[...]


def _adaln(c, w_ada, b_ada):
    bsz, d = c.shape
    n = w_ada.shape[1]
    tn = n // 4
    return pl.pallas_call(
        _adaln_kernel,
        out_shape=jax.ShapeDtypeStruct((bsz, n), F32),
        grid=(n // tn,),
        in_specs=[
            pl.BlockSpec((bsz, d), lambda j: (0, 0)),
            pl.BlockSpec((d, tn), lambda j: (0, j)),
            pl.BlockSpec((1, tn), lambda j: (0, j)),
        ],
        out_specs=pl.BlockSpec((bsz, tn), lambda j: (0, j)),
        compiler_params=_params(("arbitrary",)),
        name="adaln",
    )(c, w_ada, b_ada.reshape(1, n))


def _ffn_kernel(x_ref, sh_ref, sc_ref, gt_ref, nw_ref, win_ref, wout_ref, fn_ref, o_ref,
                acc_ref, *, d_ff, tf, sub_m, final_norm):
    n_sub = x_ref.shape[1] // sub_m

    def normed(t):
        x = x_ref[0, t * sub_m:(t + 1) * sub_m, :]
        return (_rmsnorm(x, nw_ref[...]) * (1.0 + sc_ref[0]) + sh_ref[0]).astype(BF16)

    def swiglu(t, u):
        rows = slice(t * sub_m, (t + 1) * sub_m)
        for i in range(d_ff // tf):
            cols = slice(i * tf, (i + 1) * tf)
            up = slice(d_ff + i * tf, d_ff + (i + 1) * tf)
            hg = jnp.dot(u, win_ref[:, cols].astype(BF16), preferred_element_type=F32)
            hu = jnp.dot(u, win_ref[:, up].astype(BF16), preferred_element_type=F32)
            act = (hg * jax.nn.sigmoid(hg) * hu).astype(BF16)
            part = jnp.dot(act, wout_ref[cols, :].astype(BF16), preferred_element_type=F32)
            if i == 0:
                acc_ref[rows, :] = part
            else:
                acc_ref[rows, :] += part
        h = x_ref[0, rows, :] + (0.5 * gt_ref[0]) * acc_ref[rows, :]
        if final_norm:
            h = _rmsnorm(h, fn_ref[...])
        o_ref[0, rows, :] = h

    u = normed(0)
    for t in range(n_sub):
        u_next = normed(t + 1) if t + 1 < n_sub else None
        swiglu(t, u)
        u = u_next


def _ffn(x, sh, sc, gt, nw, w_in, w_out, fn, *, final_norm, tm=512, sub_m=512, tf=256):
    bsz, s, d = x.shape
    d_ff = w_out.shape[0]
    vec = pl.BlockSpec((1, 1, d), lambda b, i: (b, 0, 0))
    row = pl.BlockSpec((1, tm, d), lambda b, i: (b, i, 0))
    return pl.pallas_call(
        functools.partial(_ffn_kernel, d_ff=d_ff, tf=tf, sub_m=sub_m, final_norm=final_norm),
        out_shape=jax.ShapeDtypeStruct((bsz, s, d), F32),
        grid=(bsz, s // tm),
        in_specs=[row, vec, vec, vec, _resident((1, d)), _resident((d, 2 * d_ff)),
                  _resident((d_ff, d)), _resident((1, d))],
        out_specs=row,
        scratch_shapes=[pltpu.VMEM((tm, d), F32)],
        compiler_params=_params(("parallel", "parallel")),
        name="ffn_final" if final_norm else "ffn",
    )(x, sh, sc, gt, nw.reshape(1, d), w_in, w_out, fn.reshape(1, d))


def _split_w_in_kernel(w_ref, main_ref, ga_ref, *, ga_start):
    main_ref[:, :ga_start] = w_ref[:, :ga_start].astype(BF16)
    main_ref[:, ga_start:] = w_ref[:, ga_start + GLA_RANK:].astype(BF16)
    ga_ref[...] = w_ref[:, ga_start:ga_start + GLA_RANK].astype(BF16)


def _split_w_in(w_all, layer, ga_start, *, tr=128):
    _, d, n = w_all.shape
    return pl.pallas_call(
        functools.partial(_split_w_in_kernel, ga_start=ga_start),
        out_shape=[jax.ShapeDtypeStruct((d, n - GLA_RANK), BF16),
                   jax.ShapeDtypeStruct((d, GLA_RANK), BF16)],
        grid=(d // tr,),
        in_specs=[pl.BlockSpec((None, tr, n), lambda i: (layer, i, 0))],
        out_specs=[pl.BlockSpec((tr, n - GLA_RANK), lambda i: (i, 0)),
                   pl.BlockSpec((tr, GLA_RANK), lambda i: (i, 0))],
        compiler_params=_params(("parallel",)),
        name="split_w_in",
    )(w_all)


def _proj_kernel(x_ref, sh_ref, sc_ref, nw_ref, w_ref, wga_ref, w2_ref, b2_ref,
                 gq_ref, gk_ref, gv_ref, gr_ref, g_ref, dq_ref, dk_ref, dv_ref, sg_ref):
    x = x_ref[0]
    u = (_rmsnorm(x, nw_ref[...]) * (1.0 + sc_ref[0]) + sh_ref[0]).astype(BF16)

    def seg(start, size):
        return jnp.dot(u, w_ref[:, start:start + size], preferred_element_type=F32)

    kw = GLA_HEADS * GLA_DK
    vw = GLA_HEADS * GLA_DV
    qw = DIFF_HEADS * 2 * DIFF_DH
    dvw = DIFF_HEADS * DIFF_DV
    off = 0
    gq_ref[0] = (seg(off, kw) * (GLA_DK ** -0.5)).astype(BF16)
    off += kw
    gk_ref[0] = seg(off, kw).astype(BF16)
    off += kw
    gv_ref[0] = seg(off, vw).astype(BF16)
    off += vw
    r = seg(off, vw)
    gr_ref[0] = (r * jax.nn.sigmoid(r)).astype(BF16)
    off += vw
    dq_ref[0] = (seg(off, qw) * (DIFF_DH ** -0.5 * LOG2E)).astype(BF16)
    off += qw
    dk_ref[0] = seg(off, qw).astype(BF16)
    off += qw
    dv_ref[0] = seg(off, dvw).astype(BF16)
    off += dvw
    n_gate = sg_ref.shape[-1]
    step = 512
    for j in range(n_gate // step):
        sg_ref[0, :, j * step:(j + 1) * step] = jax.nn.sigmoid(seg(off + j * step, step)).astype(BF16)
    a_low = jnp.dot(u, wga_ref[...], preferred_element_type=F32).astype(BF16)
    z = jnp.dot(a_low, w2_ref[...].astype(BF16), preferred_element_type=F32) + b2_ref[...]
    log_sig = jnp.minimum(z, 0.0) - jnp.log1p(jnp.exp(-jnp.abs(z)))
    g_ref[0] = log_sig * (1.0 / GLA_TAU)


def _mixer_proj(h, sh, sc, nw, w_main, w_ga, w2, b2, *, tm=512):
    bsz, s, d = h.shape
    kw = GLA_HEADS * GLA_DK
    vw = GLA_HEADS * GLA_DV
    qw = DIFF_HEADS * 2 * DIFF_DH
    dvw = DIFF_HEADS * DIFF_DV
    n_gate = 2 * d
    vec = pl.BlockSpec((1, 1, d), lambda b, i: (b, 0, 0))

    def rows(w):
        return pl.BlockSpec((1, tm, w), lambda b, i: (b, i, 0))

    widths = [(kw, BF16), (kw, BF16), (vw, BF16), (vw, BF16), (kw, F32),
              (qw, BF16), (qw, BF16), (dvw, BF16), (n_gate, BF16)]
    return pl.pallas_call(
        _proj_kernel,
        out_shape=[jax.ShapeDtypeStruct((bsz, s, w), dt) for w, dt in widths],
        grid=(bsz, s // tm),
        in_specs=[rows(d), vec, vec, _resident((1, d)), _resident(w_main.shape),
                  _resident(w_ga.shape), _resident(w2.shape), _resident((1, kw))],
        out_specs=[rows(w) for w, _ in widths],
        compiler_params=_params(("parallel", "parallel")),
        name="mixer_proj",
    )(h, sh, sc, nw.reshape(1, d), w_main, w_ga, w2, b2.reshape(1, kw))


def _gla_kernel(q_ref, k_ref, v_ref, r_ref, g_ref, hn_ref, o_ref, st_ref, *, chunk, sub):
    @pl.when(pl.program_id(1) == 0)
    def _():
        st_ref[...] = jnp.zeros_like(st_ref)

    nsub = chunk // sub
    row = lax.broadcasted_iota(jnp.int32, (chunk, chunk), 0)
    col = lax.broadcasted_iota(jnp.int32, (chunk, chunk), 1)
    causal = col <= row
    cum_mat = jnp.concatenate(
        [jnp.where(causal, 1.0, 0.0), jnp.where(causal & (col >= (row // sub) * sub), 1.0, 0.0)],
        axis=0).astype(BF16)
    row_blk = lax.broadcasted_iota(jnp.int32, (chunk, LANES), 0) // sub
    lane_head = lax.broadcasted_iota(jnp.int32, (chunk, LANES), 1) // GLA_DK

    def cumsum(bi, pair):
        g = g_ref[bi, :, pair * LANES:(pair + 1) * LANES]
        g1 = g.astype(BF16)
        e1 = g - g1.astype(F32)
        g2 = e1.astype(BF16)
        g3 = (e1 - g2.astype(F32)).astype(BF16)
        cs = jnp.dot(cum_mat, jnp.concatenate([g1, g2, g3], axis=1), preferred_element_type=F32)
        return cs[:, :LANES] + cs[:, LANES:2 * LANES] + cs[:, 2 * LANES:]

    def decay(bi, pair, cs):
        lanes = slice(pair * LANES, (pair + 1) * LANES)
        b = cs[:chunk]
        w = cs[chunk:]
        b_last = b[chunk - 1:chunk]
        q = q_ref[bi, :, lanes].astype(F32)
        k = k_ref[bi, :, lanes].astype(F32)
        q_cat = []
        for j in range(nsub):
            ref_b = b[j * sub - 1:j * sub] if j else jnp.zeros_like(b_last)
            q_cat.append((q * jnp.exp(jnp.minimum(b - ref_b, 0.0))).astype(BF16))
        return dict(
            q_cat=jnp.concatenate(q_cat, axis=1),
            q_state=(q * jnp.exp(b)).astype(BF16),
            k_state=k * jnp.exp(b_last - b),
            k_hat=k * jnp.exp(jnp.minimum(-w, EXP_CLAMP)),
            chunk_decay=jnp.exp(b_last))

    def scores(d):
        out = []
        for hh in range(2):
            kh = jnp.where(lane_head == hh, d["k_hat"], 0.0).astype(BF16)
            k_cat = jnp.concatenate(
                [jnp.where(row_blk == j, kh, jnp.zeros_like(kh)) for j in range(nsub)], axis=1)
            out.append(lax.dot_general(d["q_cat"], k_cat, (((1,), (1,)), ((), ())),
                                       preferred_element_type=F32))
        return out

    def finish(bi, pair, d, sc):
        for hh in range(2):
            head = pair * 2 + hh
            cols = slice(head * GLA_DV, (head + 1) * GLA_DV)
            intra = jnp.where(causal, sc[hh], 0.0).astype(BF16)
            vh = v_ref[bi, :, cols]
            state_t = st_ref[bi * GLA_HEADS + head]
            o = jnp.dot(intra, vh, preferred_element_type=F32)
            o += lax.dot_general(d["q_state"], state_t.astype(BF16), (((1,), (1,)), ((), ())),
                                 preferred_element_type=F32)
            ks = jnp.where(lane_head == hh, d["k_state"], 0.0).astype(BF16)
            kv_t = lax.dot_general(vh, ks, (((0,), (0,)), ((), ())), preferred_element_type=F32)
            st_ref[bi * GLA_HEADS + head] = state_t * d["chunk_decay"] + kv_t
            y = _rmsnorm(o, hn_ref[...]) * r_ref[bi, :, cols].astype(F32)
            o_ref[bi, :, cols] = y.astype(BF16)

    chains = list(itertools.product(range(q_ref.shape[0]), range(GLA_HEADS // 2)))
    cums = [cumsum(*ch) for ch in chains]
    decayed, scored = {}, {}
    for t in range(len(chains) + 1):
        if t < len(chains):
            decayed[t] = decay(*chains[t], cums[t])
        if t >= 1:
            finish(*chains[t - 1], decayed.pop(t - 1), scored.pop(t - 1))
        if t < len(chains):
            scored[t] = scores(decayed[t])


def _gla(gq, gk, gv, gr, g, head_norm, *, nb=4):
    bsz, s, kw = gq.shape
    vw = gv.shape[-1]
    chunk = GLA_CHUNK

    def rows(w):
        return pl.BlockSpec((nb, chunk, w), lambda b, c: (b, c, 0))

    return pl.pallas_call(
        functools.partial(_gla_kernel, chunk=chunk, sub=GLA_SUB),
        out_shape=jax.ShapeDtypeStruct((bsz, s, vw), BF16),
        grid=(bsz // nb, s // chunk),
        in_specs=[rows(kw), rows(kw), rows(vw), rows(vw), rows(kw), _resident((1, GLA_DV))],
        out_specs=rows(vw),
        scratch_shapes=[pltpu.VMEM((nb * GLA_HEADS, GLA_DV, LANES), F32)],
        compiler_params=_params(("parallel", "arbitrary")),
        name="gla",
    )(gq, gk, gv, gr, g, head_norm.reshape(1, GLA_DV))


def _attn_kernel(q_ref, k_ref, v_ref, pos_ref, slope_ref, lam_ref, hn_ref, o_ref, kk_ref, qq_ref, vt_ref,
                 s0_ref, s1_ref, s2_ref, m_ref, acc_ref, *, tq, rc, lam_init):
    seq = k_ref.shape[1]
    nq = seq // tq
    c = slope_ref[0] * LOG2E
    c1 = c.astype(BF16).astype(F32)
    c2 = (c - c1).astype(BF16).astype(F32)
    c3 = c - c1 - c2

    kk_ref[:, :LANES] = k_ref[0]
    kk_ref[:, LANES:] = pos_ref[...]
    sub = lax.broadcasted_iota(jnp.int32, (LANES, 2 * tq), 0)
    cf = jnp.where((sub == 0) | (sub == 3), c1, jnp.where((sub == 1) | (sub == 4), c2, c3))
    slope_rows = jnp.where(sub < 6, cf, 0.0).astype(BF16)
    qq_ref[0, LANES:, :] = slope_rows
    qq_ref[1, LANES:, :] = slope_rows
    for jb in range(nq):
        vt_ref[jb, :DIFF_DV, :] = v_ref[0, jb * tq:(jb + 1) * tq, :].astype(F32).T.astype(BF16)
        vt_ref[jb, DIFF_DV:, :] = jnp.ones((SUM_ROWS, tq), BF16)
    nchunk = tq // rc
    lam = lam_ref[...]
    lam = (jnp.exp(jnp.sum(lam[0:1] * lam[1:2], axis=-1, keepdims=True))
           - jnp.exp(jnp.sum(lam[2:3] * lam[3:4], axis=-1, keepdims=True)) + lam_init)

    def logits(qs, j, s_blk):
        start = pl.multiple_of(j * tq, tq)
        s_blk[...] = jnp.dot(kk_ref[pl.ds(start, tq), :], qq_ref[qs], preferred_element_type=F32)

    def softmax(qs, qi, j, s_blk, masked):
        m_old = m_ref[qs]
        shift = c * jnp.asarray((qi - j) * tq, F32)

        def load(r):
            blk = s_blk[r * rc:(r + 1) * rc, :]
            if masked:
                key = r * rc + lax.broadcasted_iota(jnp.int32, (rc, 2 * tq), 0)
                qry = lax.broadcasted_iota(jnp.int32, (rc, 2 * tq), 1) & (tq - 1)
                blk = jnp.where(key <= qry, blk, -jnp.inf)
            return blk

        mx = None
        for r in range(nchunk):
            cm = jnp.max(load(r).reshape(rc // 8, 8, 2 * tq), axis=0)
            mx = cm if mx is None else jnp.maximum(mx, cm)
        m_new = jnp.maximum(m_old, jnp.max(mx, axis=0, keepdims=True) - shift)
        alpha = jnp.exp2(m_old - m_new)
        m_shift = m_new + shift
        probs = jnp.concatenate(
            [jnp.exp2(load(r) - m_shift).astype(BF16) for r in range(nchunk)], axis=0)
        m_ref[qs] = m_new
        return qs, j, alpha, probs

    def accumulate(qs, j, alpha, probs):
        acc_ref[qs] = alpha * acc_ref[qs] + jnp.dot(vt_ref[j], probs, preferred_element_type=F32)

    def pair(p, carry):
        q_blk = (p, nq - 1 - p)
        for qs in range(2):
            q_t = q_ref[0, pl.ds(pl.multiple_of(q_blk[qs] * tq, tq), tq), :].astype(F32).T
            half = lax.broadcasted_iota(jnp.int32, (LANES, tq), 0)
            qq_ref[qs, :LANES, :tq] = jnp.where(half < DIFF_DH, q_t, 0.0).astype(BF16)
            qq_ref[qs, :LANES, tq:] = jnp.where(half >= DIFF_DH, q_t, 0.0).astype(BF16)
            m_ref[qs] = jnp.full((1, 2 * tq), -jnp.inf, F32)
            acc_ref[qs] = jnp.zeros((DIFF_DV + SUM_ROWS, 2 * tq), F32)
        items = [(1, q_blk[1], q_blk[1], True), (0, q_blk[0], q_blk[0], True)]
        for u in range(nq - 1):
            late = u < q_blk[1]
            items.append((jnp.where(late, 1, 0), jnp.where(late, q_blk[1], q_blk[0]),
                          jnp.where(late, u, u - q_blk[1]), False))
        bufs = (s0_ref, s1_ref, s2_ref)
        ahead = len(bufs) - 1
        for t in range(ahead):
            logits(items[t][0], items[t][2], bufs[t])
        for t, (qs, qi, j, masked) in enumerate(items):
            if t + ahead < len(items):
                nxt = items[t + ahead]
                logits(nxt[0], nxt[2], bufs[(t + ahead) % len(bufs)])
            accumulate(*softmax(qs, qi, j, bufs[t % len(bufs)], masked))
        for qs in range(2):
            out = acc_ref[qs, :DIFF_DV, :] * (1.0 / acc_ref[qs, DIFF_DV:DIFF_DV + 1, :])
            y_t = out[:, :tq] - lam * out[:, tq:]
            y_t = y_t * lax.rsqrt(jnp.mean(y_t * y_t, axis=0, keepdims=True) + EPS)
            rows = pl.ds(pl.multiple_of(q_blk[qs] * tq, tq), tq)
            o_ref[0, rows, :] = (y_t.T * hn_ref[...] * (1.0 - lam_init)).astype(BF16)
        return carry

    lax.fori_loop(0, nq // 2, pair, 0)


def _diff_attn(dq, dk, dv, lam_vecs, head_norm, lam_init, *, tq=512, rc=256):
    bsz, s, _ = dq.shape
    slopes = jnp.asarray(
        [2.0 ** (-ALIBI_MAX_BIAS * (i + 1) / DIFF_HEADS) for i in range(DIFF_HEADS)], F32
    ).reshape(DIFF_HEADS, 1, 1)
    pos = np.arange(s) % tq
    feat = np.zeros((s, LANES), np.float32)
    feat[:, 0:3] = (pos - pos % 16)[:, None]
    feat[:, 3:6] = (pos % 16)[:, None]
    return pl.pallas_call(
        functools.partial(_attn_kernel, tq=tq, rc=rc, lam_init=lam_init),
        out_shape=jax.ShapeDtypeStruct((bsz, s, DIFF_HEADS * DIFF_DV), BF16),
        grid=(bsz, DIFF_HEADS),
        in_specs=[
            pl.BlockSpec((1, s, LANES), lambda b, h: (b, 0, h)),
            pl.BlockSpec((1, s, LANES), lambda b, h: (b, 0, h)),
            pl.BlockSpec((1, s, DIFF_DV), lambda b, h: (b, 0, h)),
            _resident((s, LANES)),
            pl.BlockSpec((1, 1, 1), lambda b, h: (h, 0, 0)),
            _resident(lam_vecs.shape),
            _resident((1, DIFF_DV)),
        ],
        out_specs=pl.BlockSpec((1, s, DIFF_DV), lambda b, h: (b, 0, h)),
        scratch_shapes=[
            pltpu.VMEM((s, 2 * LANES), BF16),
            pltpu.VMEM((2, 2 * LANES, 2 * tq), BF16),
            pltpu.VMEM((s // tq, DIFF_DV + SUM_ROWS, tq), BF16),
            pltpu.VMEM((tq, 2 * tq), F32),
            pltpu.VMEM((tq, 2 * tq), F32),
            pltpu.VMEM((tq, 2 * tq), F32),
            pltpu.VMEM((2, 1, 2 * tq), F32),
            pltpu.VMEM((2, DIFF_DV + SUM_ROWS, 2 * tq), F32),
        ],
        compiler_params=_params(("parallel", "parallel")),
        name="diff_attn",
    )(dq, dk, dv, jnp.asarray(feat, BF16), slopes, lam_vecs, head_norm.reshape(1, DIFF_DV))


def _merge_kernel(h_ref, za_ref, zb_ref, sg_ref, gt_ref, wa_ref, wb_ref, wo_ref, o_ref):
    d = h_ref.shape[-1]
    ya = jnp.dot(za_ref[0], wa_ref[...].astype(BF16), preferred_element_type=F32)
    yb = jnp.dot(zb_ref[0], wb_ref[...].astype(BF16), preferred_element_type=F32)
    mix = sg_ref[0, :, :d].astype(F32) * ya + sg_ref[0, :, d:].astype(F32) * yb
    m = jnp.dot(mix.astype(BF16), wo_ref[...].astype(BF16), preferred_element_type=F32)
    o_ref[0] = h_ref[0] + gt_ref[0] * m


def _merge(h, za, zb, sg, gt, w_a, w_b, w_o, *, tm=512):
    bsz, s, d = h.shape
    vec = pl.BlockSpec((1, 1, d), lambda b, i: (b, 0, 0))

    def rows(w):
        return pl.BlockSpec((1, tm, w), lambda b, i: (b, i, 0))

    return pl.pallas_call(
        _merge_kernel,
        out_shape=jax.ShapeDtypeStruct((bsz, s, d), F32),
        grid=(bsz, s // tm),
        in_specs=[rows(d), rows(za.shape[-1]), rows(zb.shape[-1]), rows(2 * d), vec,
                  _resident(w_a.shape), _resident(w_b.shape), _resident(w_o.shape)],
        out_specs=rows(d),
        compiler_params=_params(("parallel", "parallel")),
        name="merge",
    )(h, za, zb, sg, gt, w_a, w_b, w_o)


def kernel(x, c, w_ada, b_ada, ffn1_norm, ffn1_w_in, ffn1_w_out, mix_norm, w_in, gla_alpha_w2, gla_alpha_b, gla_head_norm, diff_lq1, diff_lk1, diff_lq2, diff_lk2, diff_head_norm, w_branch_a, w_branch_b, w_out, ffn2_norm, ffn2_w_in, ffn2_w_out, final_norm):
    depth = w_ada.shape[0]
    bsz, _, d = x.shape
    kw = GLA_HEADS * GLA_DK
    vw = GLA_HEADS * GLA_DV
    ga_start = 2 * kw + 2 * vw
    h = x
    for l in range(depth):
        lam_init = 0.8 - 0.6 * math.exp(-0.3 * l)
        mod = _adaln(c, w_ada[l], b_ada[l])
        sh1, sc1, gt1, sh2, sc2, gt2, sh3, sc3, gt3 = [
            mod[:, i * d:(i + 1) * d].reshape(bsz, 1, d) for i in range(N_MOD)]
        last = l == depth - 1
        h = _ffn(h, sh1, sc1, gt1, ffn1_norm[l], ffn1_w_in[l], ffn1_w_out[l], final_norm,
                 final_norm=False)
        w_main, w_ga = _split_w_in(w_in, l, ga_start)
        gq, gk, gv, gr, g, dq, dk, dv, sg = _mixer_proj(
            h, sh2, sc2, mix_norm[l], w_main, w_ga, gla_alpha_w2[l], gla_alpha_b[l])
        za = _gla(gq, gk, gv, gr, g, gla_head_norm[l])
        lam_vecs = jnp.stack([diff_lq1[l], diff_lk1[l], diff_lq2[l], diff_lk2[l]])
        zb = _diff_attn(dq, dk, dv, lam_vecs, diff_head_norm[l], lam_init)
        h = _merge(h, za, zb, sg, gt2, w_branch_a[l], w_branch_b[l], w_out[l])
        h = _ffn(h, sh3, sc3, gt3, ffn2_norm[l], ffn2_w_in[l], ffn2_w_out[l], final_norm,
                 final_norm=last)
    return h
```

```python
import functools
import itertools
import math

import jax
import jax.numpy as jnp
import numpy as np
from jax import lax
from jax.experimental import pallas as pl
from jax.experimental.pallas import tpu as pltpu

F32 = jnp.float32
BF16 = jnp.bfloat16

EPS = 1e-6
GLA_HEADS = 4
GLA_DK = 64
GLA_DV = 128
GLA_RANK = 16
GLA_TAU = 16.0
DIFF_HEADS = 4
DIFF_DH = 64
DIFF_DV = 128
ALIBI_MAX_BIAS = 8.0
N_MOD = 9

LOG2E = math.log2(math.e)
LANES = 128
GLA_CHUNK = 128
GLA_SUB = 16
EXP_CLAMP = 80.0
SUM_ROWS = 16
VMEM_LIMIT = 56 * 1024 * 1024


def _params(sem, **flags):
    return pltpu.CompilerParams(dimension_semantics=sem, vmem_limit_bytes=VMEM_LIMIT,
                                flags=flags or None)


def _resident(shape):
    nd = len(shape)
    return pl.BlockSpec(shape, lambda *_: (0,) * nd, pipeline_mode=pl.Buffered(1))


def _rmsnorm(x, g):
    return x * lax.rsqrt(jnp.mean(x * x, axis=-1, keepdims=True) + EPS) * g


def _adaln_kernel(c_ref, w_ref, b_ref, o_ref):
    c = c_ref[...]
    ca = (c * jax.nn.sigmoid(c)).astype(BF16)
    o_ref[...] = jnp.dot(ca, w_ref[...].astype(BF16), preferred_element_type=F32) + b_ref[...]


def _adaln(c, w_ada, b_ada):
    bsz, d = c.shape
    n = w_ada.shape[1]
    tn = n // 4
    return pl.pallas_call(
        _adaln_kernel,
        out_shape=jax.ShapeDtypeStruct((bsz, n), F32),
        grid=(n // tn,),
        in_specs=[
            pl.BlockSpec((bsz, d), lambda j: (0, 0)),
            pl.BlockSpec((d, tn), lambda j: (0, j)),
            pl.BlockSpec((1, tn), lambda j: (0, j)),
        ],
        out_specs=pl.BlockSpec((bsz, tn), lambda j: (0, j)),
        compiler_params=_params(("arbitrary",)),
        name="adaln",
    )(c, w_ada, b_ada.reshape(1, n))


def _ffn_kernel(x_ref, sh_ref, sc_ref, gt_ref, nw_ref, win_ref, wout_ref, fn_ref, o_ref,
                acc_ref, *, d_ff, tf, sub_m, final_norm):
    n_sub = x_ref.shape[1] // sub_m

    def normed(t):
        x = x_ref[0, t * sub_m:(t + 1) * sub_m, :]
        return (_rmsnorm(x, nw_ref[...]) * (1.0 + sc_ref[0]) + sh_ref[0]).astype(BF16)

    def swiglu(t, u):
        rows = slice(t * sub_m, (t + 1) * sub_m)
        for i in range(d_ff // tf):
            cols = slice(i * tf, (i + 1) * tf)
            up = slice(d_ff + i * tf, d_ff + (i + 1) * tf)
            hg = jnp.dot(u, win_ref[:, cols].astype(BF16), preferred_element_type=F32)
            hu = jnp.dot(u, win_ref[:, up].astype(BF16), preferred_element_type=F32)
            act = (hg * jax.nn.sigmoid(hg) * hu).astype(BF16)
            part = jnp.dot(act, wout_ref[cols, :].astype(BF16), preferred_element_type=F32)
            if i == 0:
                acc_ref[rows, :] = part
            else:
                acc_ref[rows, :] += part
        h = x_ref[0, rows, :] + (0.5 * gt_ref[0]) * acc_ref[rows, :]
        if final_norm:
            h = _rmsnorm(h, fn_ref[...])
        o_ref[0, rows, :] = h

    u = normed(0)
    for t in range(n_sub):
        u_next = normed(t + 1) if t + 1 < n_sub else None
        swiglu(t, u)
        u = u_next


def _ffn(x, sh, sc, gt, nw, w_in, w_out, fn, *, final_norm, tm=512, sub_m=512, tf=256):
    bsz, s, d = x.shape
    d_ff = w_out.shape[0]
    vec = pl.BlockSpec((1, 1, d), lambda b, i: (b, 0, 0))
    row = pl.BlockSpec((1, tm, d), lambda b, i: (b, i, 0))
    return pl.pallas_call(
        functools.partial(_ffn_kernel, d_ff=d_ff, tf=tf, sub_m=sub_m, final_norm=final_norm),
        out_shape=jax.ShapeDtypeStruct((bsz, s, d), F32),
        grid=(bsz, s // tm),
        in_specs=[row, vec, vec, vec, _resident((1, d)), _resident((d, 2 * d_ff)),
                  _resident((d_ff, d)), _resident((1, d))],
        out_specs=row,
        scratch_shapes=[pltpu.VMEM((tm, d), F32)],
        compiler_params=_params(("parallel", "parallel")),
        name="ffn_final" if final_norm else "ffn",
    )(x, sh, sc, gt, nw.reshape(1, d), w_in, w_out, fn.reshape(1, d))


def _proj_kernel(x_ref, sh_ref, sc_ref, nw_ref, wt_ref, w2_ref, b2_ref,
                 gq_ref, gk_ref, gv_ref, gr_ref, g_ref, dq_ref, dk_ref, dv_ref, sg_ref):
    x = x_ref[0]
    u = (_rmsnorm(x, nw_ref[...]) * (1.0 + sc_ref[0]) + sh_ref[0]).astype(BF16)

    def seg(start, size):
        return lax.dot_general(u, wt_ref[start:start + size, :].astype(BF16),
                               (((1,), (1,)), ((), ())), preferred_element_type=F32)

    kw = GLA_HEADS * GLA_DK
    vw = GLA_HEADS * GLA_DV
    qw = DIFF_HEADS * 2 * DIFF_DH
    dvw = DIFF_HEADS * DIFF_DV
    off = 0
    gq_ref[0] = (seg(off, kw) * (GLA_DK ** -0.5)).astype(BF16)
    off += kw
    gk_ref[0] = seg(off, kw).astype(BF16)
    off += kw
    gv_ref[0] = seg(off, vw).astype(BF16)
    off += vw
    r = seg(off, vw)
    gr_ref[0] = (r * jax.nn.sigmoid(r)).astype(BF16)
    off += vw
    a_low = seg(off, GLA_RANK).astype(BF16)
    z = jnp.dot(a_low, w2_ref[...].astype(BF16), preferred_element_type=F32) + b2_ref[...]
    g_ref[0] = (jnp.minimum(z, 0.0) - jnp.log1p(jnp.exp(-jnp.abs(z)))) * (1.0 / GLA_TAU)
    off += GLA_RANK
    dq_ref[0] = (seg(off, qw) * (DIFF_DH ** -0.5 * LOG2E)).astype(BF16)
    off += qw
    dk_ref[0] = seg(off, qw).astype(BF16)
    off += qw
    dv_ref[0] = seg(off, dvw).astype(BF16)
    off += dvw
    n_gate = sg_ref.shape[-1]
    step = 512
    for j in range(n_gate // step):
        sg_ref[0, :, j * step:(j + 1) * step] = jax.nn.sigmoid(seg(off + j * step, step)).astype(BF16)


def _mixer_proj(h, sh, sc, nw, w_t, w2, b2, *, tm=512):
    bsz, s, d = h.shape
    kw = GLA_HEADS * GLA_DK
    vw = GLA_HEADS * GLA_DV
    qw = DIFF_HEADS * 2 * DIFF_DH
    dvw = DIFF_HEADS * DIFF_DV
    n_gate = 2 * d
    vec = pl.BlockSpec((1, 1, d), lambda b, i: (b, 0, 0))

    def rows(w):
        return pl.BlockSpec((1, tm, w), lambda b, i: (b, i, 0))

    widths = [(kw, BF16), (kw, BF16), (vw, BF16), (vw, BF16), (kw, F32),
              (qw, BF16), (qw, BF16), (dvw, BF16), (n_gate, BF16)]
    return pl.pallas_call(
        _proj_kernel,
        out_shape=[jax.ShapeDtypeStruct((bsz, s, w), dt) for w, dt in widths],
        grid=(bsz, s // tm),
        in_specs=[rows(d), vec, vec, _resident((1, d)), _resident(w_t.shape),
                  _resident(w2.shape), _resident((1, kw))],
        out_specs=[rows(w) for w, _ in widths],
        compiler_params=_params(("parallel", "parallel")),
        name="mixer_proj",
    )(h, sh, sc, nw.reshape(1, d), w_t, w2, b2.reshape(1, kw))


def _gla_kernel(q_ref, k_ref, v_ref, r_ref, g_ref, hn_ref, o_ref, st_ref, *, chunk, sub):
    @pl.when(pl.program_id(1) == 0)
    def _():
        st_ref[...] = jnp.zeros_like(st_ref)

    nsub = chunk // sub
    row = lax.broadcasted_iota(jnp.int32, (chunk, chunk), 0)
    col = lax.broadcasted_iota(jnp.int32, (chunk, chunk), 1)
    causal = col <= row
    cum_mat = jnp.concatenate(
        [jnp.where(causal, 1.0, 0.0), jnp.where(causal & (col >= (row // sub) * sub), 1.0, 0.0)],
        axis=0).astype(BF16)
    row_blk = lax.broadcasted_iota(jnp.int32, (chunk, LANES), 0) // sub
    lane_head = lax.broadcasted_iota(jnp.int32, (chunk, LANES), 1) // GLA_DK

    def cumsum(bi, pair):
        g = g_ref[bi, :, pair * LANES:(pair + 1) * LANES]
        g1 = g.astype(BF16)
        e1 = g - g1.astype(F32)
        g2 = e1.astype(BF16)
        g3 = (e1 - g2.astype(F32)).astype(BF16)
        cs = jnp.dot(cum_mat, jnp.concatenate([g1, g2, g3], axis=1), preferred_element_type=F32)
        return cs[:, :LANES] + cs[:, LANES:2 * LANES] + cs[:, 2 * LANES:]

    def decay(bi, pair, cs):
        lanes = slice(pair * LANES, (pair + 1) * LANES)
        b = cs[:chunk]
        w = cs[chunk:]
        b_last = b[chunk - 1:chunk]
        q = q_ref[bi, :, lanes].astype(F32)
        k = k_ref[bi, :, lanes].astype(F32)
        q_cat = []
        for j in range(nsub):
            ref_b = b[j * sub - 1:j * sub] if j else jnp.zeros_like(b_last)
            q_cat.append((q * jnp.exp(jnp.minimum(b - ref_b, 0.0))).astype(BF16))
        return dict(
            q_cat=jnp.concatenate(q_cat, axis=1),
            q_state=(q * jnp.exp(b)).astype(BF16),
            k_state=k * jnp.exp(b_last - b),
            k_hat=k * jnp.exp(jnp.minimum(-w, EXP_CLAMP)),
            chunk_decay=jnp.exp(b_last))

    def scores(d):
        out = []
        for hh in range(2):
            kh = jnp.where(lane_head == hh, d["k_hat"], 0.0).astype(BF16)
            k_cat = jnp.concatenate(
                [jnp.where(row_blk == j, kh, jnp.zeros_like(kh)) for j in range(nsub)], axis=1)
            out.append(lax.dot_general(d["q_cat"], k_cat, (((1,), (1,)), ((), ())),
                                       preferred_element_type=F32))
        return out

    def finish(bi, pair, d, sc):
        for hh in range(2):
            head = pair * 2 + hh
            cols = slice(head * GLA_DV, (head + 1) * GLA_DV)
            intra = jnp.where(causal, sc[hh], 0.0).astype(BF16)
            vh = v_ref[bi, :, cols]
            state_t = st_ref[bi * GLA_HEADS + head]
            o = jnp.dot(intra, vh, preferred_element_type=F32)
            o += lax.dot_general(d["q_state"], state_t.astype(BF16), (((1,), (1,)), ((), ())),
                                 preferred_element_type=F32)
            ks = jnp.where(lane_head == hh, d["k_state"], 0.0).astype(BF16)
            kv_t = lax.dot_general(vh, ks, (((0,), (0,)), ((), ())), preferred_element_type=F32)
            st_ref[bi * GLA_HEADS + head] = state_t * d["chunk_decay"] + kv_t
            y = _rmsnorm(o, hn_ref[...]) * r_ref[bi, :, cols].astype(F32)
            o_ref[bi, :, cols] = y.astype(BF16)

    chains = list(itertools.product(range(q_ref.shape[0]), range(GLA_HEADS // 2)))
    cums = [cumsum(*ch) for ch in chains]
    decayed, scored = {}, {}
    for t in range(len(chains) + 1):
        if t < len(chains):
            decayed[t] = decay(*chains[t], cums[t])
        if t >= 1:
            finish(*chains[t - 1], decayed.pop(t - 1), scored.pop(t - 1))
        if t < len(chains):
            scored[t] = scores(decayed[t])


def _gla(gq, gk, gv, gr, g, head_norm, *, nb=4):
    bsz, s, kw = gq.shape
    vw = gv.shape[-1]
    chunk = GLA_CHUNK

    def rows(w):
        return pl.BlockSpec((nb, chunk, w), lambda b, c: (b, c, 0))

    return pl.pallas_call(
        functools.partial(_gla_kernel, chunk=chunk, sub=GLA_SUB),
        out_shape=jax.ShapeDtypeStruct((bsz, s, vw), BF16),
        grid=(bsz // nb, s // chunk),
        in_specs=[rows(kw), rows(kw), rows(vw), rows(vw), rows(kw), _resident((1, GLA_DV))],
        out_specs=rows(vw),
        scratch_shapes=[pltpu.VMEM((nb * GLA_HEADS, GLA_DV, LANES), F32)],
        compiler_params=_params(("parallel", "arbitrary")),
        name="gla",
    )(gq, gk, gv, gr, g, head_norm.reshape(1, GLA_DV))


def _attn_kernel(q_ref, k_ref, v_ref, pos_ref, slope_ref, lam_ref, hn_ref, o_ref, kk_ref, qq_ref, vt_ref,
                 s0_ref, s1_ref, s2_ref, m_ref, acc_ref, *, tq, rc, lam_init):
    seq = k_ref.shape[1]
    nq = seq // tq
    c = slope_ref[0] * LOG2E
    c1 = c.astype(BF16).astype(F32)
    c2 = (c - c1).astype(BF16).astype(F32)
    c3 = c - c1 - c2

    kk_ref[:, :LANES] = k_ref[0]
    kk_ref[:, LANES:] = pos_ref[...]
    sub = lax.broadcasted_iota(jnp.int32, (LANES, 2 * tq), 0)
    cf = jnp.where((sub == 0) | (sub == 3), c1, jnp.where((sub == 1) | (sub == 4), c2, c3))
    slope_rows = jnp.where(sub < 6, cf, 0.0).astype(BF16)
    qq_ref[0, LANES:, :] = slope_rows
    qq_ref[1, LANES:, :] = slope_rows
    for jb in range(nq):
        vt_ref[jb, :DIFF_DV, :] = v_ref[0, jb * tq:(jb + 1) * tq, :].astype(F32).T.astype(BF16)
        vt_ref[jb, DIFF_DV:, :] = jnp.ones((SUM_ROWS, tq), BF16)
    nchunk = tq // rc
    lam = lam_ref[...]
    lam = (jnp.exp(jnp.sum(lam[0:1] * lam[1:2], axis=-1, keepdims=True))
           - jnp.exp(jnp.sum(lam[2:3] * lam[3:4], axis=-1, keepdims=True)) + lam_init)

    def logits(qs, j, s_blk):
        start = pl.multiple_of(j * tq, tq)
        s_blk[...] = jnp.dot(kk_ref[pl.ds(start, tq), :], qq_ref[qs], preferred_element_type=F32)

    def softmax(qs, qi, j, s_blk, masked):
        m_old = m_ref[qs]
        shift = c * jnp.asarray((qi - j) * tq, F32)

        def load(r):
            blk = s_blk[r * rc:(r + 1) * rc, :]
            if masked:
                key = r * rc + lax.broadcasted_iota(jnp.int32, (rc, 2 * tq), 0)
                qry = lax.broadcasted_iota(jnp.int32, (rc, 2 * tq), 1) & (tq - 1)
                blk = jnp.where(key <= qry, blk, -jnp.inf)
            return blk

        mx = None
        for r in range(nchunk):
            cm = jnp.max(load(r).reshape(rc // 8, 8, 2 * tq), axis=0)
            mx = cm if mx is None else jnp.maximum(mx, cm)
        m_new = jnp.maximum(m_old, jnp.max(mx, axis=0, keepdims=True) - shift)
        alpha = jnp.exp2(m_old - m_new)
        m_shift = m_new + shift
        probs = jnp.concatenate(
            [jnp.exp2(load(r) - m_shift).astype(BF16) for r in range(nchunk)], axis=0)
        m_ref[qs] = m_new
        return qs, j, alpha, probs

    def accumulate(qs, j, alpha, probs):
        acc_ref[qs] = alpha * acc_ref[qs] + jnp.dot(vt_ref[j], probs, preferred_element_type=F32)

    def pair(p, carry):
        q_blk = (p, nq - 1 - p)
        for qs in range(2):
            q_t = q_ref[0, pl.ds(pl.multiple_of(q_blk[qs] * tq, tq), tq), :].astype(F32).T
            half = lax.broadcasted_iota(jnp.int32, (LANES, tq), 0)
            qq_ref[qs, :LANES, :tq] = jnp.where(half < DIFF_DH, q_t, 0.0).astype(BF16)
            qq_ref[qs, :LANES, tq:] = jnp.where(half >= DIFF_DH, q_t, 0.0).astype(BF16)
            m_ref[qs] = jnp.full((1, 2 * tq), -jnp.inf, F32)
            acc_ref[qs] = jnp.zeros((DIFF_DV + SUM_ROWS, 2 * tq), F32)
        items = [(1, q_blk[1], q_blk[1], True), (0, q_blk[0], q_blk[0], True)]
        for u in range(nq - 1):
            late = u < q_blk[1]
            items.append((jnp.where(late, 1, 0), jnp.where(late, q_blk[1], q_blk[0]),
                          jnp.where(late, u, u - q_blk[1]), False))
        bufs = (s0_ref, s1_ref, s2_ref)
        ahead = len(bufs) - 1
        for t in range(ahead):
            logits(items[t][0], items[t][2], bufs[t])
        for t, (qs, qi, j, masked) in enumerate(items):
            if t + ahead < len(items):
                nxt = items[t + ahead]
                logits(nxt[0], nxt[2], bufs[(t + ahead) % len(bufs)])
            accumulate(*softmax(qs, qi, j, bufs[t % len(bufs)], masked))
        for qs in range(2):
            out = acc_ref[qs, :DIFF_DV, :] * (1.0 / acc_ref[qs, DIFF_DV:DIFF_DV + 1, :])
            y_t = out[:, :tq] - lam * out[:, tq:]
            y_t = y_t * lax.rsqrt(jnp.mean(y_t * y_t, axis=0, keepdims=True) + EPS)
            rows = pl.ds(pl.multiple_of(q_blk[qs] * tq, tq), tq)
            o_ref[0, rows, :] = (y_t.T * hn_ref[...] * (1.0 - lam_init)).astype(BF16)
        return carry

    lax.fori_loop(0, nq // 2, pair, 0)


def _diff_attn(dq, dk, dv, lam_vecs, head_norm, lam_init, *, tq=512, rc=256):
    bsz, s, _ = dq.shape
    slopes = jnp.asarray(
        [2.0 ** (-ALIBI_MAX_BIAS * (i + 1) / DIFF_HEADS) for i in range(DIFF_HEADS)], F32
    ).reshape(DIFF_HEADS, 1, 1)
    pos = np.arange(s) % tq
    feat = np.zeros((s, LANES), np.float32)
    feat[:, 0:3] = (pos - pos % 16)[:, None]
    feat[:, 3:6] = (pos % 16)[:, None]
    return pl.pallas_call(
        functools.partial(_attn_kernel, tq=tq, rc=rc, lam_init=lam_init),
        out_shape=jax.ShapeDtypeStruct((bsz, s, DIFF_HEADS * DIFF_DV), BF16),
        grid=(bsz, DIFF_HEADS),
        in_specs=[
            pl.BlockSpec((1, s, LANES), lambda b, h: (b, 0, h)),
            pl.BlockSpec((1, s, LANES), lambda b, h: (b, 0, h)),
            pl.BlockSpec((1, s, DIFF_DV), lambda b, h: (b, 0, h)),
            _resident((s, LANES)),
            pl.BlockSpec((1, 1, 1), lambda b, h: (h, 0, 0)),
            _resident(lam_vecs.shape),
            _resident((1, DIFF_DV)),
        ],
        out_specs=pl.BlockSpec((1, s, DIFF_DV), lambda b, h: (b, 0, h)),
        scratch_shapes=[
            pltpu.VMEM((s, 2 * LANES), BF16),
            pltpu.VMEM((2, 2 * LANES, 2 * tq), BF16),
            pltpu.VMEM((s // tq, DIFF_DV + SUM_ROWS, tq), BF16),
            pltpu.VMEM((tq, 2 * tq), F32),
            pltpu.VMEM((tq, 2 * tq), F32),
            pltpu.VMEM((tq, 2 * tq), F32),
            pltpu.VMEM((2, 1, 2 * tq), F32),
            pltpu.VMEM((2, DIFF_DV + SUM_ROWS, 2 * tq), F32),
        ],
        compiler_params=_params(("parallel", "parallel")),
        name="diff_attn",
    )(dq, dk, dv, jnp.asarray(feat, BF16), slopes, lam_vecs, head_norm.reshape(1, DIFF_DV))


def _merge_kernel(h_ref, za_ref, zb_ref, sg_ref, gt_ref, wa_ref, wb_ref, wo_ref, o_ref):
    d = h_ref.shape[-1]
    ya = jnp.dot(za_ref[0], wa_ref[...].astype(BF16), preferred_element_type=F32)
    yb = jnp.dot(zb_ref[0], wb_ref[...].astype(BF16), preferred_element_type=F32)
    mix = sg_ref[0, :, :d].astype(F32) * ya + sg_ref[0, :, d:].astype(F32) * yb
    m = jnp.dot(mix.astype(BF16), wo_ref[...].astype(BF16), preferred_element_type=F32)
    o_ref[0] = h_ref[0] + gt_ref[0] * m


def _merge(h, za, zb, sg, gt, w_a, w_b, w_o, *, tm=512):
    bsz, s, d = h.shape
    vec = pl.BlockSpec((1, 1, d), lambda b, i: (b, 0, 0))

    def rows(w):
        return pl.BlockSpec((1, tm, w), lambda b, i: (b, i, 0))

    return pl.pallas_call(
        _merge_kernel,
        out_shape=jax.ShapeDtypeStruct((bsz, s, d), F32),
        grid=(bsz, s // tm),
        in_specs=[rows(d), rows(za.shape[-1]), rows(zb.shape[-1]), rows(2 * d), vec,
                  _resident(w_a.shape), _resident(w_b.shape), _resident(w_o.shape)],
        out_specs=rows(d),
        compiler_params=_params(("parallel", "parallel")),
        name="merge",
    )(h, za, zb, sg, gt, w_a, w_b, w_o)


def kernel(x, c, w_ada, b_ada, ffn1_norm, ffn1_w_in, ffn1_w_out, mix_norm, w_in, gla_alpha_w2, gla_alpha_b, gla_head_norm, diff_lq1, diff_lk1, diff_lq2, diff_lk2, diff_head_norm, w_branch_a, w_branch_b, w_out, ffn2_norm, ffn2_w_in, ffn2_w_out, final_norm):
    depth = w_ada.shape[0]
    bsz, _, d = x.shape
    h = x
    for l in range(depth):
        lam_init = 0.8 - 0.6 * math.exp(-0.3 * l)
        mod = _adaln(c, w_ada[l], b_ada[l])
        sh1, sc1, gt1, sh2, sc2, gt2, sh3, sc3, gt3 = [
            mod[:, i * d:(i + 1) * d].reshape(bsz, 1, d) for i in range(N_MOD)]
        last = l == depth - 1
        h = _ffn(h, sh1, sc1, gt1, ffn1_norm[l], ffn1_w_in[l], ffn1_w_out[l], final_norm,
                 final_norm=False)
        w_t = jnp.swapaxes(w_in, 1, 2)[l]
        gq, gk, gv, gr, g, dq, dk, dv, sg = _mixer_proj(
            h, sh2, sc2, mix_norm[l], w_t, gla_alpha_w2[l], gla_alpha_b[l])
        za = _gla(gq, gk, gv, gr, g, gla_head_norm[l])
        lam_vecs = jnp.stack([diff_lq1[l], diff_lk1[l], diff_lq2[l], diff_lk2[l]])
        zb = _diff_attn(dq, dk, dv, lam_vecs, diff_head_norm[l], lam_init)
        h = _merge(h, za, zb, sg, gt2, w_branch_a[l], w_branch_b[l], w_out[l])
        h = _ffn(h, sh3, sc3, gt3, ffn2_norm[l], ffn2_w_in[l], ffn2_w_out[l], final_norm,
                 final_norm=last)
    return h
```

```python
import functools
import itertools
import math

import jax
import jax.numpy as jnp
import numpy as np
from jax import lax
from jax.experimental import pallas as pl
from jax.experimental.pallas import tpu as pltpu

F32 = jnp.float32
BF16 = jnp.bfloat16

EPS = 1e-6
GLA_HEADS = 4
GLA_DK = 64
GLA_DV = 128
GLA_RANK = 16
GLA_TAU = 16.0
DIFF_HEADS = 4
DIFF_DH = 64
DIFF_DV = 128
ALIBI_MAX_BIAS = 8.0
N_MOD = 9

LOG2E = math.log2(math.e)
LANES = 128
GLA_CHUNK = 128
GLA_SUB = 16
EXP_CLAMP = 80.0
SUM_ROWS = 16
VMEM_LIMIT = 56 * 1024 * 1024


def _params(sem, **flags):
    return pltpu.CompilerParams(dimension_semantics=sem, vmem_limit_bytes=VMEM_LIMIT,
                                flags=flags or None)


def _resident(shape):
    nd = len(shape)
    return pl.BlockSpec(shape, lambda *_: (0,) * nd, pipeline_mode=pl.Buffered(1))


def _rmsnorm(x, g):
    return x * lax.rsqrt(jnp.mean(x * x, axis=-1, keepdims=True) + EPS) * g


def _adaln_kernel(c_ref, w_ref, b_ref, o_ref):
    c = c_ref[...]
    ca = (c * jax.nn.sigmoid(c)).astype(BF16)
    o_ref[...] = jnp.dot(ca, w_ref[...].astype(BF16), preferred_element_type=F32) + b_ref[...]


def _adaln(c, w_ada, b_ada):
    bsz, d = c.shape
    n = w_ada.shape[1]
    tn = n // 4
    return pl.pallas_call(
        _adaln_kernel,
        out_shape=jax.ShapeDtypeStruct((bsz, n), F32),
        grid=(n // tn,),
        in_specs=[
            pl.BlockSpec((bsz, d), lambda j: (0, 0)),
            pl.BlockSpec((d, tn), lambda j: (0, j)),
            pl.BlockSpec((1, tn), lambda j: (0, j)),
        ],
        out_specs=pl.BlockSpec((bsz, tn), lambda j: (0, j)),
        compiler_params=_params(("arbitrary",)),
        name="adaln",
    )(c, w_ada, b_ada.reshape(1, n))


def _ffn_kernel(x_ref, sh_ref, sc_ref, gt_ref, nw_ref, win_ref, wout_ref, fn_ref, o_ref,
                *, d_ff, tf, sub_m, final_norm):
    n_sub = x_ref.shape[1] // sub_m
    acc_ref = o_ref.at[0]

    def normed(t):
        x = x_ref[0, t * sub_m:(t + 1) * sub_m, :]
        return (_rmsnorm(x, nw_ref[...]) * (1.0 + sc_ref[0]) + sh_ref[0]).astype(BF16)

    def swiglu(t, u):
        rows = slice(t * sub_m, (t + 1) * sub_m)
        for i in range(d_ff // tf):
            cols = slice(i * tf, (i + 1) * tf)
            up = slice(d_ff + i * tf, d_ff + (i + 1) * tf)
            hg = jnp.dot(u, win_ref[:, cols].astype(BF16), preferred_element_type=F32)
            hu = jnp.dot(u, win_ref[:, up].astype(BF16), preferred_element_type=F32)
            act = (hg * jax.nn.sigmoid(hg) * hu).astype(BF16)
            part = jnp.dot(act, wout_ref[cols, :].astype(BF16), preferred_element_type=F32)
            if i == 0:
                acc_ref[rows, :] = part
            else:
                acc_ref[rows, :] += part
        h = x_ref[0, rows, :] + (0.5 * gt_ref[0]) * acc_ref[rows, :]
        if final_norm:
            h = _rmsnorm(h, fn_ref[...])
        o_ref[0, rows, :] = h

    u = normed(0)
    for t in range(n_sub):
        u_next = normed(t + 1) if t + 1 < n_sub else None
        swiglu(t, u)
        u = u_next


def _ffn(x, sh, sc, gt, nw, w_in, w_out, fn, *, final_norm, tm=1024, sub_m=512, tf=256):
    bsz, s, d = x.shape
    d_ff = w_out.shape[0]
    vec = pl.BlockSpec((1, 1, d), lambda b, i: (b, 0, 0))
    row = pl.BlockSpec((1, tm, d), lambda b, i: (b, i, 0))
    return pl.pallas_call(
        functools.partial(_ffn_kernel, d_ff=d_ff, tf=tf, sub_m=sub_m, final_norm=final_norm),
        out_shape=jax.ShapeDtypeStruct((bsz, s, d), F32),
        grid=(bsz, s // tm),
        in_specs=[row, vec, vec, vec, _resident((1, d)), _resident((d, 2 * d_ff)),
                  _resident((d_ff, d)), _resident((1, d))],
        out_specs=row,
        compiler_params=_params(("parallel", "parallel")),
        name="ffn_final" if final_norm else "ffn",
    )(x, sh, sc, gt, nw.reshape(1, d), w_in, w_out, fn.reshape(1, d))


def _proj_kernel(x_ref, sh_ref, sc_ref, nw_ref, wt_ref, w2_ref, b2_ref,
                 gq_ref, gk_ref, gv_ref, gr_ref, g_ref, dq_ref, dk_ref, dv_ref, sg_ref):
    x = x_ref[0]
    u = (_rmsnorm(x, nw_ref[...]) * (1.0 + sc_ref[0]) + sh_ref[0]).astype(BF16)

    def seg(start, size):
        return lax.dot_general(u, wt_ref[start:start + size, :].astype(BF16),
                               (((1,), (1,)), ((), ())), preferred_element_type=F32)

    kw = GLA_HEADS * GLA_DK
    vw = GLA_HEADS * GLA_DV
    qw = DIFF_HEADS * 2 * DIFF_DH
    dvw = DIFF_HEADS * DIFF_DV
    off = 0
    gq_ref[0] = (seg(off, kw) * (GLA_DK ** -0.5)).astype(BF16)
    off += kw
    gk_ref[0] = seg(off, kw).astype(BF16)
    off += kw
    gv_ref[0] = seg(off, vw).astype(BF16)
    off += vw
    r = seg(off, vw)
    gr_ref[0] = (r * jax.nn.sigmoid(r)).astype(BF16)
    off += vw
    a_low = seg(off, GLA_RANK).astype(BF16)
    z = jnp.dot(a_low, w2_ref[...].astype(BF16), preferred_element_type=F32) + b2_ref[...]
    g_ref[0] = (jnp.minimum(z, 0.0) - jnp.log1p(jnp.exp(-jnp.abs(z)))) * (1.0 / GLA_TAU)
    off += GLA_RANK
    dq_ref[0] = (seg(off, qw) * (DIFF_DH ** -0.5 * LOG2E)).astype(BF16)
    off += qw
    dk_ref[0] = seg(off, qw).astype(BF16)
    off += qw
    dv_ref[0] = seg(off, dvw).astype(BF16)
    off += dvw
    n_gate = sg_ref.shape[-1]
    step = 512
    for j in range(n_gate // step):
        sg_ref[0, :, j * step:(j + 1) * step] = jax.nn.sigmoid(seg(off + j * step, step)).astype(BF16)


def _mixer_proj(h, sh, sc, nw, w_t, w2, b2, *, tm=512):
    bsz, s, d = h.shape
    kw = GLA_HEADS * GLA_DK
    vw = GLA_HEADS * GLA_DV
    qw = DIFF_HEADS * 2 * DIFF_DH
    dvw = DIFF_HEADS * DIFF_DV
    n_gate = 2 * d
    vec = pl.BlockSpec((1, 1, d), lambda b, i: (b, 0, 0))

    def rows(w):
        return pl.BlockSpec((1, tm, w), lambda b, i: (b, i, 0))

    widths = [(kw, BF16), (kw, BF16), (vw, BF16), (vw, BF16), (kw, F32),
              (qw, BF16), (qw, BF16), (dvw, BF16), (n_gate, BF16)]
    return pl.pallas_call(
        _proj_kernel,
        out_shape=[jax.ShapeDtypeStruct((bsz, s, w), dt) for w, dt in widths],
        grid=(bsz, s // tm),
        in_specs=[rows(d), vec, vec, _resident((1, d)), _resident(w_t.shape),
                  _resident(w2.shape), _resident((1, kw))],
        out_specs=[rows(w) for w, _ in widths],
        compiler_params=_params(("parallel", "parallel")),
        name="mixer_proj",
    )(h, sh, sc, nw.reshape(1, d), w_t, w2, b2.reshape(1, kw))


def _gla_kernel(q_ref, k_ref, v_ref, r_ref, g_ref, hn_ref, o_ref, st_ref, *, chunk, sub):
    @pl.when(pl.program_id(1) == 0)
    def _():
        st_ref[...] = jnp.zeros_like(st_ref)

    nsub = chunk // sub
    row = lax.broadcasted_iota(jnp.int32, (chunk, chunk), 0)
    col = lax.broadcasted_iota(jnp.int32, (chunk, chunk), 1)
    causal = col <= row
    cum_mat = jnp.concatenate(
        [jnp.where(causal, 1.0, 0.0), jnp.where(causal & (col >= (row // sub) * sub), 1.0, 0.0)],
        axis=0).astype(BF16)
    row_blk = lax.broadcasted_iota(jnp.int32, (chunk, LANES), 0) // sub
    lane_head = lax.broadcasted_iota(jnp.int32, (chunk, LANES), 1) // GLA_DK

    def cumsum(bi, pair):
        g = g_ref[bi, :, pair * LANES:(pair + 1) * LANES]
        g1 = g.astype(BF16)
        e1 = g - g1.astype(F32)
        g2 = e1.astype(BF16)
        g3 = (e1 - g2.astype(F32)).astype(BF16)
        cs = jnp.dot(cum_mat, jnp.concatenate([g1, g2, g3], axis=1), preferred_element_type=F32)
        return cs[:, :LANES] + cs[:, LANES:2 * LANES] + cs[:, 2 * LANES:]

    def decay(bi, pair, cs):
        lanes = slice(pair * LANES, (pair + 1) * LANES)
        b = cs[:chunk]
        w = cs[chunk:]
        b_last = b[chunk - 1:chunk]
        q = q_ref[bi, :, lanes].astype(F32)
        k = k_ref[bi, :, lanes].astype(F32)
        q_cat = []
        for j in range(nsub):
            ref_b = b[j * sub - 1:j * sub] if j else jnp.zeros_like(b_last)
            q_cat.append((q * jnp.exp(jnp.minimum(b - ref_b, 0.0))).astype(BF16))
        return dict(
            q_cat=jnp.concatenate(q_cat, axis=1),
            q_state=(q * jnp.exp(b)).astype(BF16),
            k_state=k * jnp.exp(b_last - b),
            k_hat=k * jnp.exp(jnp.minimum(-w, EXP_CLAMP)),
            chunk_decay=jnp.exp(b_last))

    def scores(d):
        out = []
        for hh in range(2):
            kh = jnp.where(lane_head == hh, d["k_hat"], 0.0).astype(BF16)
            k_cat = jnp.concatenate(
                [jnp.where(row_blk == j, kh, jnp.zeros_like(kh)) for j in range(nsub)], axis=1)
            out.append(lax.dot_general(d["q_cat"], k_cat, (((1,), (1,)), ((), ())),
                                       preferred_element_type=F32))
        return out

    def finish(bi, pair, d, sc):
        for hh in range(2):
            head = pair * 2 + hh
            cols = slice(head * GLA_DV, (head + 1) * GLA_DV)
            intra = jnp.where(causal, sc[hh], 0.0).astype(BF16)
            vh = v_ref[bi, :, cols]
            state_t = st_ref[bi * GLA_HEADS + head]
            o = jnp.dot(intra, vh, preferred_element_type=F32)
            o += lax.dot_general(d["q_state"], state_t.astype(BF16), (((1,), (1,)), ((), ())),
                                 preferred_element_type=F32)
            ks = jnp.where(lane_head == hh, d["k_state"], 0.0).astype(BF16)
            kv_t = lax.dot_general(vh, ks, (((0,), (0,)), ((), ())), preferred_element_type=F32)
            st_ref[bi * GLA_HEADS + head] = state_t * d["chunk_decay"] + kv_t
            y = _rmsnorm(o, hn_ref[...]) * r_ref[bi, :, cols].astype(F32)
            o_ref[bi, :, cols] = y.astype(BF16)

    chains = list(itertools.product(range(q_ref.shape[0]), range(GLA_HEADS // 2)))
    cums = [cumsum(*ch) for ch in chains]
    decayed, scored = {}, {}
    for t in range(len(chains) + 1):
        if t < len(chains):
            decayed[t] = decay(*chains[t], cums[t])
        if t >= 1:
            finish(*chains[t - 1], decayed.pop(t - 1), scored.pop(t - 1))
        if t < len(chains):
            scored[t] = scores(decayed[t])


def _gla(gq, gk, gv, gr, g, head_norm, *, nb=4):
    bsz, s, kw = gq.shape
    vw = gv.shape[-1]
    chunk = GLA_CHUNK

    def rows(w):
        return pl.BlockSpec((nb, chunk, w), lambda b, c: (b, c, 0))

    return pl.pallas_call(
        functools.partial(_gla_kernel, chunk=chunk, sub=GLA_SUB),
        out_shape=jax.ShapeDtypeStruct((bsz, s, vw), BF16),
        grid=(bsz // nb, s // chunk),
        in_specs=[rows(kw), rows(kw), rows(vw), rows(vw), rows(kw), _resident((1, GLA_DV))],
        out_specs=rows(vw),
        scratch_shapes=[pltpu.VMEM((nb * GLA_HEADS, GLA_DV, LANES), F32)],
        compiler_params=_params(("parallel", "arbitrary")),
        name="gla",
    )(gq, gk, gv, gr, g, head_norm.reshape(1, GLA_DV))


def _attn_kernel(q_ref, k_ref, v_ref, pos_ref, slope_ref, lam_ref, hn_ref, o_ref, kk_ref, qq_ref, vt_ref,
                 s0_ref, s1_ref, s2_ref, m_ref, acc_ref, *, tq, rc, lam_init):
    seq = k_ref.shape[1]
    nq = seq // tq
    c = slope_ref[0] * LOG2E
    c1 = c.astype(BF16).astype(F32)
    c2 = (c - c1).astype(BF16).astype(F32)
    c3 = c - c1 - c2

    kk_ref[:, :LANES] = k_ref[0]
    kk_ref[:, LANES:] = pos_ref[...]
    sub = lax.broadcasted_iota(jnp.int32, (LANES, 2 * tq), 0)
    cf = jnp.where((sub == 0) | (sub == 3), c1, jnp.where((sub == 1) | (sub == 4), c2, c3))
    slope_rows = jnp.where(sub < 6, cf, 0.0).astype(BF16)
    qq_ref[0, LANES:, :] = slope_rows
    qq_ref[1, LANES:, :] = slope_rows
    for jb in range(nq):
        vt_ref[jb, :DIFF_DV, :] = v_ref[0, jb * tq:(jb + 1) * tq, :].astype(F32).T.astype(BF16)
        vt_ref[jb, DIFF_DV:, :] = jnp.ones((SUM_ROWS, tq), BF16)
    nchunk = tq // rc
    lam = lam_ref[...]
    lam = (jnp.exp(jnp.sum(lam[0:1] * lam[1:2], axis=-1, keepdims=True))
           - jnp.exp(jnp.sum(lam[2:3] * lam[3:4], axis=-1, keepdims=True)) + lam_init)

    def logits(qs, j, s_blk):
        start = pl.multiple_of(j * tq, tq)
        s_blk[...] = jnp.dot(kk_ref[pl.ds(start, tq), :], qq_ref[qs], preferred_element_type=F32)

    def softmax(qs, qi, j, s_blk, masked):
        m_old = m_ref[qs]
        shift = c * jnp.asarray((qi - j) * tq, F32)

        def load(r):
            blk = s_blk[r * rc:(r + 1) * rc, :]
            if masked:
                key = r * rc + lax.broadcasted_iota(jnp.int32, (rc, 2 * tq), 0)
                qry = lax.broadcasted_iota(jnp.int32, (rc, 2 * tq), 1) & (tq - 1)
                blk = jnp.where(key <= qry, blk, -jnp.inf)
            return blk

        mx = None
        for r in range(nchunk):
            cm = jnp.max(load(r).reshape(rc // 8, 8, 2 * tq), axis=0)
            mx = cm if mx is None else jnp.maximum(mx, cm)
        m_new = jnp.maximum(m_old, jnp.max(mx, axis=0, keepdims=True) - shift)
        alpha = jnp.exp2(m_old - m_new)
        m_shift = m_new + shift
        probs = jnp.concatenate(
            [jnp.exp2(load(r) - m_shift).astype(BF16) for r in range(nchunk)], axis=0)
        m_ref[qs] = m_new
        return qs, j, alpha, probs

    def accumulate(qs, j, alpha, probs):
        acc_ref[qs] = alpha * acc_ref[qs] + jnp.dot(vt_ref[j], probs, preferred_element_type=F32)

    def pair(p, carry):
        q_blk = (p, nq - 1 - p)
        for qs in range(2):
            q_t = q_ref[0, pl.ds(pl.multiple_of(q_blk[qs] * tq, tq), tq), :].astype(F32).T
            half = lax.broadcasted_iota(jnp.int32, (LANES, tq), 0)
            qq_ref[qs, :LANES, :tq] = jnp.where(half < DIFF_DH, q_t, 0.0).astype(BF16)
            qq_ref[qs, :LANES, tq:] = jnp.where(half >= DIFF_DH, q_t, 0.0).astype(BF16)
            m_ref[qs] = jnp.full((1, 2 * tq), -jnp.inf, F32)
            acc_ref[qs] = jnp.zeros((DIFF_DV + SUM_ROWS, 2 * tq), F32)
        items = [(1, q_blk[1], q_blk[1], True), (0, q_blk[0], q_blk[0], True)]
        for u in range(nq - 1):
            late = u < q_blk[1]
            items.append((jnp.where(late, 1, 0), jnp.where(late, q_blk[1], q_blk[0]),
                          jnp.where(late, u, u - q_blk[1]), False))
        bufs = (s0_ref, s1_ref, s2_ref)
        ahead = len(bufs) - 1
        for t in range(ahead):
            logits(items[t][0], items[t][2], bufs[t])
        for t, (qs, qi, j, masked) in enumerate(items):
            if t + ahead < len(items):
                nxt = items[t + ahead]
                logits(nxt[0], nxt[2], bufs[(t + ahead) % len(bufs)])
            accumulate(*softmax(qs, qi, j, bufs[t % len(bufs)], masked))
        for qs in range(2):
            out = acc_ref[qs, :DIFF_DV, :] * (1.0 / acc_ref[qs, DIFF_DV:DIFF_DV + 1, :])
            y_t = out[:, :tq] - lam * out[:, tq:]
            y_t = y_t * lax.rsqrt(jnp.mean(y_t * y_t, axis=0, keepdims=True) + EPS)
            rows = pl.ds(pl.multiple_of(q_blk[qs] * tq, tq), tq)
            o_ref[0, rows, :] = (y_t.T * hn_ref[...] * (1.0 - lam_init)).astype(BF16)
        return carry

    lax.fori_loop(0, nq // 2, pair, 0)


def _diff_attn(dq, dk, dv, lam_vecs, head_norm, lam_init, *, tq=512, rc=256):
    bsz, s, _ = dq.shape
    slopes = jnp.asarray(
        [2.0 ** (-ALIBI_MAX_BIAS * (i + 1) / DIFF_HEADS) for i in range(DIFF_HEADS)], F32
    ).reshape(DIFF_HEADS, 1, 1)
    pos = np.arange(s) % tq
    feat = np.zeros((s, LANES), np.float32)
    feat[:, 0:3] = (pos - pos % 16)[:, None]
    feat[:, 3:6] = (pos % 16)[:, None]
    return pl.pallas_call(
        functools.partial(_attn_kernel, tq=tq, rc=rc, lam_init=lam_init),
        out_shape=jax.ShapeDtypeStruct((bsz, s, DIFF_HEADS * DIFF_DV), BF16),
        grid=(bsz, DIFF_HEADS),
        in_specs=[
            pl.BlockSpec((1, s, LANES), lambda b, h: (b, 0, h)),
            pl.BlockSpec((1, s, LANES), lambda b, h: (b, 0, h)),
            pl.BlockSpec((1, s, DIFF_DV), lambda b, h: (b, 0, h)),
            _resident((s, LANES)),
            pl.BlockSpec((1, 1, 1), lambda b, h: (h, 0, 0)),
            _resident(lam_vecs.shape),
            _resident((1, DIFF_DV)),
        ],
        out_specs=pl.BlockSpec((1, s, DIFF_DV), lambda b, h: (b, 0, h)),
        scratch_shapes=[
            pltpu.VMEM((s, 2 * LANES), BF16),
            pltpu.VMEM((2, 2 * LANES, 2 * tq), BF16),
            pltpu.VMEM((s // tq, DIFF_DV + SUM_ROWS, tq), BF16),
            pltpu.VMEM((tq, 2 * tq), F32),
            pltpu.VMEM((tq, 2 * tq), F32),
            pltpu.VMEM((tq, 2 * tq), F32),
            pltpu.VMEM((2, 1, 2 * tq), F32),
            pltpu.VMEM((2, DIFF_DV + SUM_ROWS, 2 * tq), F32),
        ],
        compiler_params=_params(("parallel", "parallel")),
        name="diff_attn",
    )(dq, dk, dv, jnp.asarray(feat, BF16), slopes, lam_vecs, head_norm.reshape(1, DIFF_DV))


def _merge_kernel(h_ref, za_ref, zb_ref, sg_ref, gt_ref, wa_ref, wb_ref, wo_ref, o_ref):
    d = h_ref.shape[-1]
    ya = jnp.dot(za_ref[0], wa_ref[...].astype(BF16), preferred_element_type=F32)
    yb = jnp.dot(zb_ref[0], wb_ref[...].astype(BF16), preferred_element_type=F32)
    mix = sg_ref[0, :, :d].astype(F32) * ya + sg_ref[0, :, d:].astype(F32) * yb
    m = jnp.dot(mix.astype(BF16), wo_ref[...].astype(BF16), preferred_element_type=F32)
    o_ref[0] = h_ref[0] + gt_ref[0] * m


def _merge(h, za, zb, sg, gt, w_a, w_b, w_o, *, tm=1024):
    bsz, s, d = h.shape
    vec = pl.BlockSpec((1, 1, d), lambda b, i: (b, 0, 0))

    def rows(w):
        return pl.BlockSpec((1, tm, w), lambda b, i: (b, i, 0))

    return pl.pallas_call(
        _merge_kernel,
        out_shape=jax.ShapeDtypeStruct((bsz, s, d), F32),
        grid=(bsz, s // tm),
        in_specs=[rows(d), rows(za.shape[-1]), rows(zb.shape[-1]), rows(2 * d), vec,
                  _resident(w_a.shape), _resident(w_b.shape), _resident(w_o.shape)],
        out_specs=rows(d),
        compiler_params=_params(("parallel", "parallel")),
        name="merge",
    )(h, za, zb, sg, gt, w_a, w_b, w_o)


def kernel(x, c, w_ada, b_ada, ffn1_norm, ffn1_w_in, ffn1_w_out, mix_norm, w_in, gla_alpha_w2, gla_alpha_b, gla_head_norm, diff_lq1, diff_lk1, diff_lq2, diff_lk2, diff_head_norm, w_branch_a, w_branch_b, w_out, ffn2_norm, ffn2_w_in, ffn2_w_out, final_norm):
    depth = w_ada.shape[0]
    bsz, _, d = x.shape
    h = x
    for l in range(depth):
        lam_init = 0.8 - 0.6 * math.exp(-0.3 * l)
        mod = _adaln(c, w_ada[l], b_ada[l])
        sh1, sc1, gt1, sh2, sc2, gt2, sh3, sc3, gt3 = [
            mod[:, i * d:(i + 1) * d].reshape(bsz, 1, d) for i in range(N_MOD)]
        last = l == depth - 1
        h = _ffn(h, sh1, sc1, gt1, ffn1_norm[l], ffn1_w_in[l], ffn1_w_out[l], final_norm,
                 final_norm=False)
        w_t = jnp.swapaxes(w_in, 1, 2)[l]
        gq, gk, gv, gr, g, dq, dk, dv, sg = _mixer_proj(
            h, sh2, sc2, mix_norm[l], w_t, gla_alpha_w2[l], gla_alpha_b[l])
        za = _gla(gq, gk, gv, gr, g, gla_head_norm[l])
        lam_vecs = jnp.stack([diff_lq1[l], diff_lk1[l], diff_lq2[l], diff_lk2[l]])
        zb = _diff_attn(dq, dk, dv, lam_vecs, diff_head_norm[l], lam_init)
        h = _merge(h, za, zb, sg, gt2, w_branch_a[l], w_branch_b[l], w_out[l])
        h = _ffn(h, sh3, sc3, gt3, ffn2_norm[l], ffn2_w_in[l], ffn2_w_out[l], final_norm,
                 final_norm=last)
    return h
```

```python
import functools
import itertools
import math

import jax
import jax.numpy as jnp
import numpy as np
from jax import lax
from jax.experimental import pallas as pl
from jax.experimental.pallas import tpu as pltpu

F32 = jnp.float32
BF16 = jnp.bfloat16

EPS = 1e-6
GLA_HEADS = 4
GLA_DK = 64
GLA_DV = 128
GLA_RANK = 16
GLA_TAU = 16.0
DIFF_HEADS = 4
DIFF_DH = 64
DIFF_DV = 128
ALIBI_MAX_BIAS = 8.0
N_MOD = 9

LOG2E = math.log2(math.e)
LANES = 128
GLA_CHUNK = 128
GLA_SUB = 16
EXP_CLAMP = 80.0
SUM_ROWS = 16
VMEM_LIMIT = 56 * 1024 * 1024


def _params(sem, **flags):
    return pltpu.CompilerParams(dimension_semantics=sem, vmem_limit_bytes=VMEM_LIMIT,
                                flags=flags or None)


def _resident(shape):
    nd = len(shape)
    return pl.BlockSpec(shape, lambda *_: (0,) * nd, pipeline_mode=pl.Buffered(1))


def _rmsnorm(x, g):
    return x * lax.rsqrt(jnp.mean(x * x, axis=-1, keepdims=True) + EPS) * g


def _adaln_kernel(c_ref, w_ref, b_ref, o_ref):
    c = c_ref[...]
    ca = (c * jax.nn.sigmoid(c)).astype(BF16)
    o_ref[...] = jnp.dot(ca, w_ref[...].astype(BF16), preferred_element_type=F32) + b_ref[...]


def _adaln(c, w_ada, b_ada):
    bsz, d = c.shape
    n = w_ada.shape[1]
    tn = n // 4
    return pl.pallas_call(
        _adaln_kernel,
        out_shape=jax.ShapeDtypeStruct((bsz, n), F32),
        grid=(n // tn,),
        in_specs=[
            pl.BlockSpec((bsz, d), lambda j: (0, 0)),
            pl.BlockSpec((d, tn), lambda j: (0, j)),
            pl.BlockSpec((1, tn), lambda j: (0, j)),
        ],
        out_specs=pl.BlockSpec((bsz, tn), lambda j: (0, j)),
        compiler_params=_params(("arbitrary",)),
        name="adaln",
    )(c, w_ada, b_ada.reshape(1, n))


def _ffn_kernel(x_ref, sh_ref, sc_ref, gt_ref, nw_ref, win_ref, wout_ref, fn_ref, o_ref,
                *, d_ff, tf, sub_m, final_norm):
    n_sub = x_ref.shape[1] // sub_m
    acc_ref = o_ref.at[0]

    def normed(t):
        x = x_ref[0, t * sub_m:(t + 1) * sub_m, :]
        return (_rmsnorm(x, nw_ref[...]) * (1.0 + sc_ref[0]) + sh_ref[0]).astype(BF16)

    def swiglu(t, u):
        rows = slice(t * sub_m, (t + 1) * sub_m)
        for i in range(d_ff // tf):
            cols = slice(i * tf, (i + 1) * tf)
            up = slice(d_ff + i * tf, d_ff + (i + 1) * tf)
            hg = jnp.dot(u, win_ref[:, cols].astype(BF16), preferred_element_type=F32)
            hu = jnp.dot(u, win_ref[:, up].astype(BF16), preferred_element_type=F32)
            act = (hg * jax.nn.sigmoid(hg) * hu).astype(BF16)
            part = jnp.dot(act, wout_ref[cols, :].astype(BF16), preferred_element_type=F32)
            if i == 0:
                acc_ref[rows, :] = part
            else:
                acc_ref[rows, :] += part
        h = x_ref[0, rows, :] + (0.5 * gt_ref[0]) * acc_ref[rows, :]
        if final_norm:
            h = _rmsnorm(h, fn_ref[...])
        o_ref[0, rows, :] = h

    u = normed(0)
    for t in range(n_sub):
        u_next = normed(t + 1) if t + 1 < n_sub else None
        swiglu(t, u)
        u = u_next


def _ffn(x, sh, sc, gt, nw, w_in, w_out, fn, *, final_norm, tm=512, sub_m=512, tf=256):
    bsz, s, d = x.shape
    d_ff = w_out.shape[0]
    vec = pl.BlockSpec((1, 1, d), lambda b, i: (b, 0, 0))
    row = pl.BlockSpec((1, tm, d), lambda b, i: (b, i, 0))
    return pl.pallas_call(
        functools.partial(_ffn_kernel, d_ff=d_ff, tf=tf, sub_m=sub_m, final_norm=final_norm),
        out_shape=jax.ShapeDtypeStruct((bsz, s, d), F32),
        grid=(bsz, s // tm),
        in_specs=[row, vec, vec, vec, _resident((1, d)), _resident((d, 2 * d_ff)),
                  _resident((d_ff, d)), _resident((1, d))],
        out_specs=row,
        compiler_params=_params(("parallel", "parallel")),
        name="ffn_final" if final_norm else "ffn",
    )(x, sh, sc, gt, nw.reshape(1, d), w_in, w_out, fn.reshape(1, d))


def _proj_kernel(x_ref, sh_ref, sc_ref, nw_ref, wt_ref, w2_ref, b2_ref,
                 gq_ref, gk_ref, gv_ref, gr_ref, g_ref, dq_ref, dk_ref, dv_ref, sg_ref, *, sub_m):
    kw = GLA_HEADS * GLA_DK
    vw = GLA_HEADS * GLA_DV
    qw = DIFF_HEADS * 2 * DIFF_DH
    dvw = DIFF_HEADS * DIFF_DV
    n_gate = sg_ref.shape[-1]
    n_sub = x_ref.shape[1] // sub_m

    def normed(t):
        x = x_ref[0, t * sub_m:(t + 1) * sub_m, :]
        return (_rmsnorm(x, nw_ref[...]) * (1.0 + sc_ref[0]) + sh_ref[0]).astype(BF16)

    def project(t, u):
        rows = slice(t * sub_m, (t + 1) * sub_m)

        def seg(start, size):
            return lax.dot_general(u, wt_ref[start:start + size, :].astype(BF16),
                                   (((1,), (1,)), ((), ())), preferred_element_type=F32)

        off = 0
        gq_ref[0, rows, :] = (seg(off, kw) * (GLA_DK ** -0.5)).astype(BF16)
        off += kw
        gk_ref[0, rows, :] = seg(off, kw).astype(BF16)
        off += kw
        gv_ref[0, rows, :] = seg(off, vw).astype(BF16)
        off += vw
        r = seg(off, vw)
        gr_ref[0, rows, :] = (r * jax.nn.sigmoid(r)).astype(BF16)
        off += vw
        a_low = seg(off, GLA_RANK).astype(BF16)
        z = jnp.dot(a_low, w2_ref[...].astype(BF16), preferred_element_type=F32) + b2_ref[...]
        g_ref[0, rows, :] = (jnp.minimum(z, 0.0) - jnp.log1p(jnp.exp(-jnp.abs(z)))) * (1.0 / GLA_TAU)
        off += GLA_RANK
        dq_ref[0, rows, :] = (seg(off, qw) * (DIFF_DH ** -0.5 * LOG2E)).astype(BF16)
        off += qw
        dk_ref[0, rows, :] = seg(off, qw).astype(BF16)
        off += qw
        dv_ref[0, rows, :] = seg(off, dvw).astype(BF16)
        off += dvw
        step = 512
        for j in range(n_gate // step):
            gate = jax.nn.sigmoid(seg(off + j * step, step))
            sg_ref[0, rows, j * step:(j + 1) * step] = gate.astype(BF16)

    u = normed(0)
    for t in range(n_sub):
        u_next = normed(t + 1) if t + 1 < n_sub else None
        project(t, u)
        u = u_next


def _mixer_proj(h, sh, sc, nw, w_t, w2, b2, *, tm=512, sub_m=512):
    bsz, s, d = h.shape
    kw = GLA_HEADS * GLA_DK
    vw = GLA_HEADS * GLA_DV
    qw = DIFF_HEADS * 2 * DIFF_DH
    dvw = DIFF_HEADS * DIFF_DV
    n_gate = 2 * d
    vec = pl.BlockSpec((1, 1, d), lambda b, i: (b, 0, 0))

    def rows(w):
        return pl.BlockSpec((1, tm, w), lambda b, i: (b, i, 0))

    widths = [(kw, BF16), (kw, BF16), (vw, BF16), (vw, BF16), (kw, F32),
              (qw, BF16), (qw, BF16), (dvw, BF16), (n_gate, BF16)]
    return pl.pallas_call(
        functools.partial(_proj_kernel, sub_m=sub_m),
        out_shape=[jax.ShapeDtypeStruct((bsz, s, w), dt) for w, dt in widths],
        grid=(bsz, s // tm),
        in_specs=[rows(d), vec, vec, _resident((1, d)), _resident(w_t.shape),
                  _resident(w2.shape), _resident((1, kw))],
        out_specs=[rows(w) for w, _ in widths],
        compiler_params=_params(("parallel", "parallel")),
        name="mixer_proj",
    )(h, sh, sc, nw.reshape(1, d), w_t, w2, b2.reshape(1, kw))


def _gla_kernel(q_ref, k_ref, v_ref, r_ref, g_ref, hn_ref, o_ref, st_ref, *, chunk, sub):
    @pl.when(pl.program_id(1) == 0)
    def _():
        st_ref[...] = jnp.zeros_like(st_ref)

    nsub = chunk // sub
    row = lax.broadcasted_iota(jnp.int32, (chunk, chunk), 0)
    col = lax.broadcasted_iota(jnp.int32, (chunk, chunk), 1)
    causal = col <= row
    cum_mat = jnp.concatenate(
        [jnp.where(causal, 1.0, 0.0), jnp.where(causal & (col >= (row // sub) * sub), 1.0, 0.0)],
        axis=0).astype(BF16)
    row_blk = lax.broadcasted_iota(jnp.int32, (chunk, LANES), 0) // sub
    lane_head = lax.broadcasted_iota(jnp.int32, (chunk, LANES), 1) // GLA_DK

    def cumsum(bi, pair):
        g = g_ref[bi, :, pair * LANES:(pair + 1) * LANES]
        g1 = g.astype(BF16)
        e1 = g - g1.astype(F32)
        g2 = e1.astype(BF16)
        g3 = (e1 - g2.astype(F32)).astype(BF16)
        cs = jnp.dot(cum_mat, jnp.concatenate([g1, g2, g3], axis=1), preferred_element_type=F32)
        return cs[:, :LANES] + cs[:, LANES:2 * LANES] + cs[:, 2 * LANES:]

    def decay(bi, pair, cs):
        lanes = slice(pair * LANES, (pair + 1) * LANES)
        b = cs[:chunk]
        w = cs[chunk:]
        b_last = b[chunk - 1:chunk]
        q = q_ref[bi, :, lanes].astype(F32)
        k = k_ref[bi, :, lanes].astype(F32)
        q_cat = []
        for j in range(nsub):
            ref_b = b[j * sub - 1:j * sub] if j else jnp.zeros_like(b_last)
            q_cat.append((q * jnp.exp(jnp.minimum(b - ref_b, 0.0))).astype(BF16))
        return dict(
            q_cat=jnp.concatenate(q_cat, axis=1),
            q_state=(q * jnp.exp(b)).astype(BF16),
            k_state=k * jnp.exp(b_last - b),
            k_hat=k * jnp.exp(jnp.minimum(-w, EXP_CLAMP)),
            chunk_decay=jnp.exp(b_last))

    def scores(d):
        out = []
        for hh in range(2):
            kh = jnp.where(lane_head == hh, d["k_hat"], 0.0).astype(BF16)
            k_cat = jnp.concatenate(
                [jnp.where(row_blk == j, kh, jnp.zeros_like(kh)) for j in range(nsub)], axis=1)
            out.append(lax.dot_general(d["q_cat"], k_cat, (((1,), (1,)), ((), ())),
                                       preferred_element_type=F32))
        return out

    def finish(bi, pair, d, sc):
        for hh in range(2):
            head = pair * 2 + hh
            cols = slice(head * GLA_DV, (head + 1) * GLA_DV)
            intra = jnp.where(causal, sc[hh], 0.0).astype(BF16)
            vh = v_ref[bi, :, cols]
            state_t = st_ref[bi * GLA_HEADS + head]
            o = jnp.dot(intra, vh, preferred_element_type=F32)
            o += lax.dot_general(d["q_state"], state_t.astype(BF16), (((1,), (1,)), ((), ())),
                                 preferred_element_type=F32)
            ks = jnp.where(lane_head == hh, d["k_state"], 0.0).astype(BF16)
            kv_t = lax.dot_general(vh, ks, (((0,), (0,)), ((), ())), preferred_element_type=F32)
            st_ref[bi * GLA_HEADS + head] = state_t * d["chunk_decay"] + kv_t
            y = _rmsnorm(o, hn_ref[...]) * r_ref[bi, :, cols].astype(F32)
            o_ref[bi, :, cols] = y.astype(BF16)

    chains = list(itertools.product(range(q_ref.shape[0]), range(GLA_HEADS // 2)))
    cums = [cumsum(*ch) for ch in chains]
    decayed, scored = {}, {}
    for t in range(len(chains) + 1):
        if t < len(chains):
            decayed[t] = decay(*chains[t], cums[t])
        if t >= 1:
            finish(*chains[t - 1], decayed.pop(t - 1), scored.pop(t - 1))
        if t < len(chains):
            scored[t] = scores(decayed[t])


def _gla(gq, gk, gv, gr, g, head_norm, *, nb=4):
    bsz, s, kw = gq.shape
    vw = gv.shape[-1]
    chunk = GLA_CHUNK

    def rows(w):
        return pl.BlockSpec((nb, chunk, w), lambda b, c: (b, c, 0))

    return pl.pallas_call(
        functools.partial(_gla_kernel, chunk=chunk, sub=GLA_SUB),
        out_shape=jax.ShapeDtypeStruct((bsz, s, vw), BF16),
        grid=(bsz // nb, s // chunk),
        in_specs=[rows(kw), rows(kw), rows(vw), rows(vw), rows(kw), _resident((1, GLA_DV))],
        out_specs=rows(vw),
        scratch_shapes=[pltpu.VMEM((nb * GLA_HEADS, GLA_DV, LANES), F32)],
        compiler_params=_params(("parallel", "arbitrary")),
        name="gla",
    )(gq, gk, gv, gr, g, head_norm.reshape(1, GLA_DV))


def _attn_kernel(q_ref, k_ref, v_ref, pos_ref, slope_ref, lam_ref, hn_ref, o_ref, kk_ref, qq_ref, vt_ref,
                 s0_ref, s1_ref, s2_ref, m_ref, acc_ref, *, tq, rc, lam_init):
    seq = k_ref.shape[1]
    nq = seq // tq
    c = slope_ref[0] * LOG2E
    c1 = c.astype(BF16).astype(F32)
    c2 = (c - c1).astype(BF16).astype(F32)
    c3 = c - c1 - c2

    kk_ref[:, :LANES] = k_ref[0]
    kk_ref[:, LANES:] = pos_ref[...]
    sub = lax.broadcasted_iota(jnp.int32, (LANES, 2 * tq), 0)
    cf = jnp.where((sub == 0) | (sub == 3), c1, jnp.where((sub == 1) | (sub == 4), c2, c3))
    slope_rows = jnp.where(sub < 6, cf, 0.0).astype(BF16)
    qq_ref[0, LANES:, :] = slope_rows
    qq_ref[1, LANES:, :] = slope_rows
    for jb in range(nq):
        vt_ref[jb, :DIFF_DV, :] = v_ref[0, jb * tq:(jb + 1) * tq, :].astype(F32).T.astype(BF16)
        vt_ref[jb, DIFF_DV:, :] = jnp.ones((SUM_ROWS, tq), BF16)
    nchunk = tq // rc
    lam = lam_ref[...]
    lam = (jnp.exp(jnp.sum(lam[0:1] * lam[1:2], axis=-1, keepdims=True))
           - jnp.exp(jnp.sum(lam[2:3] * lam[3:4], axis=-1, keepdims=True)) + lam_init)

    def logits(qs, j, s_blk):
        start = pl.multiple_of(j * tq, tq)
        s_blk[...] = jnp.dot(kk_ref[pl.ds(start, tq), :], qq_ref[qs], preferred_element_type=F32)

    def softmax(qs, qi, j, s_blk):
        m_old = m_ref[qs]
        shift = c * jnp.asarray((qi - j) * tq, F32)

        def load(r):
            return s_blk[r * rc:(r + 1) * rc, :]

        mx = None
        for r in range(nchunk):
            cm = jnp.max(load(r).reshape(rc // 8, 8, 2 * tq), axis=0)
            mx = cm if mx is None else jnp.maximum(mx, cm)
        m_new = jnp.maximum(m_old, jnp.max(mx, axis=0, keepdims=True) - shift)
        alpha = jnp.exp2(m_old - m_new)
        m_shift = m_new + shift
        probs = jnp.concatenate(
            [jnp.exp2(load(r) - m_shift).astype(BF16) for r in range(nchunk)], axis=0)
        m_ref[qs] = m_new
        return qs, j, alpha, probs

    def accumulate(qs, j, alpha, probs):
        acc_ref[qs] = alpha * acc_ref[qs] + jnp.dot(vt_ref[j], probs, preferred_element_type=F32)

    hq = tq // 2

    def late_half(x):
        return jnp.concatenate([x[..., hq:tq], x[..., tq + hq:]], axis=-1)

    def diag_logits(qs, j, s_blk):
        start = pl.multiple_of(j * tq, tq)
        s_blk[:hq, :] = jnp.dot(kk_ref[pl.ds(start, hq), :], qq_ref[qs],
                                preferred_element_type=F32)
        s_blk[hq:, :tq] = jnp.dot(kk_ref[pl.ds(start + hq, hq), :], late_half(qq_ref[qs]),
                                  preferred_element_type=F32)

    def diag_softmax_accumulate(qs, j, s_blk):
        m_old = m_ref[qs]
        key = lax.broadcasted_iota(jnp.int32, (hq, 2 * tq), 0)
        qry = lax.broadcasted_iota(jnp.int32, (hq, 2 * tq), 1) & (tq - 1)
        early = jnp.where(key <= qry, s_blk[:hq, :], -jnp.inf)
        key_l = lax.broadcasted_iota(jnp.int32, (hq, tq), 0)
        qry_l = lax.broadcasted_iota(jnp.int32, (hq, tq), 1) & (hq - 1)
        late = jnp.where(key_l <= qry_l, s_blk[hq:, :tq], -jnp.inf)
        mx_e = jnp.max(jnp.max(early.reshape(hq // 8, 8, 2 * tq), axis=0), axis=0, keepdims=True)
        mx_l = jnp.max(jnp.max(late.reshape(hq // 8, 8, tq), axis=0), axis=0, keepdims=True)
        mx = jnp.concatenate(
            [mx_e[:, :hq], jnp.maximum(mx_e[:, hq:tq], mx_l[:, :hq]),
             mx_e[:, tq:tq + hq], jnp.maximum(mx_e[:, tq + hq:], mx_l[:, hq:])], axis=1)
        m_new = jnp.maximum(m_old, mx)
        alpha = jnp.exp2(m_old - m_new)
        m_ref[qs] = m_new
        p_early = jnp.exp2(early - m_new).astype(BF16)
        p_late = jnp.exp2(late - late_half(m_new)).astype(BF16)
        vt = vt_ref[j]
        acc_ref[qs] = alpha * acc_ref[qs] + jnp.dot(vt[:, :hq], p_early,
                                                    preferred_element_type=F32)
        upd = jnp.dot(vt[:, hq:], p_late, preferred_element_type=F32)
        acc_ref[qs, :, hq:tq] += upd[:, :hq]
        acc_ref[qs, :, tq + hq:] += upd[:, hq:]

    def pair(p, carry):
        q_blk = (p, nq - 1 - p)
        for qs in range(2):
            q_t = q_ref[0, pl.ds(pl.multiple_of(q_blk[qs] * tq, tq), tq), :].astype(F32).T
            half = lax.broadcasted_iota(jnp.int32, (LANES, tq), 0)
            qq_ref[qs, :LANES, :tq] = jnp.where(half < DIFF_DH, q_t, 0.0).astype(BF16)
            qq_ref[qs, :LANES, tq:] = jnp.where(half >= DIFF_DH, q_t, 0.0).astype(BF16)
            m_ref[qs] = jnp.full((1, 2 * tq), -jnp.inf, F32)
            acc_ref[qs] = jnp.zeros((DIFF_DV + SUM_ROWS, 2 * tq), F32)
        items = [(1, q_blk[1], q_blk[1], True), (0, q_blk[0], q_blk[0], True)]
        for u in range(nq - 1):
            late = u < q_blk[1]
            items.append((jnp.where(late, 1, 0), jnp.where(late, q_blk[1], q_blk[0]),
                          jnp.where(late, u, u - q_blk[1]), False))
        bufs = (s0_ref, s1_ref, s2_ref)
        ahead = len(bufs) - 1
        def emit_logits(t):
            qs, _, j, masked = items[t]
            (diag_logits if masked else logits)(qs, j, bufs[t % len(bufs)])

        for t in range(ahead):
            emit_logits(t)
        for t, (qs, qi, j, masked) in enumerate(items):
            if t + ahead < len(items):
                emit_logits(t + ahead)
            if masked:
                diag_softmax_accumulate(qs, j, bufs[t % len(bufs)])
            else:
                accumulate(*softmax(qs, qi, j, bufs[t % len(bufs)]))
        for qs in range(2):
            out = acc_ref[qs, :DIFF_DV, :] * (1.0 / acc_ref[qs, DIFF_DV:DIFF_DV + 1, :])
            y_t = out[:, :tq] - lam * out[:, tq:]
            y_t = y_t * lax.rsqrt(jnp.mean(y_t * y_t, axis=0, keepdims=True) + EPS)
            rows = pl.ds(pl.multiple_of(q_blk[qs] * tq, tq), tq)
            o_ref[0, rows, :] = (y_t.T * hn_ref[...] * (1.0 - lam_init)).astype(BF16)
        return carry

    lax.fori_loop(0, nq // 2, pair, 0)


def _diff_attn(dq, dk, dv, lam_vecs, head_norm, lam_init, *, tq=512, rc=256):
    bsz, s, _ = dq.shape
    slopes = jnp.asarray(
        [2.0 ** (-ALIBI_MAX_BIAS * (i + 1) / DIFF_HEADS) for i in range(DIFF_HEADS)], F32
    ).reshape(DIFF_HEADS, 1, 1)
    pos = np.arange(s) % tq
    feat = np.zeros((s, LANES), np.float32)
    feat[:, 0:3] = (pos - pos % 16)[:, None]
    feat[:, 3:6] = (pos % 16)[:, None]
    return pl.pallas_call(
        functools.partial(_attn_kernel, tq=tq, rc=rc, lam_init=lam_init),
        out_shape=jax.ShapeDtypeStruct((bsz, s, DIFF_HEADS * DIFF_DV), BF16),
        grid=(bsz, DIFF_HEADS),
        in_specs=[
            pl.BlockSpec((1, s, LANES), lambda b, h: (b, 0, h)),
            pl.BlockSpec((1, s, LANES), lambda b, h: (b, 0, h)),
            pl.BlockSpec((1, s, DIFF_DV), lambda b, h: (b, 0, h)),
            _resident((s, LANES)),
            pl.BlockSpec((1, 1, 1), lambda b, h: (h, 0, 0)),
            _resident(lam_vecs.shape),
            _resident((1, DIFF_DV)),
        ],
        out_specs=pl.BlockSpec((1, s, DIFF_DV), lambda b, h: (b, 0, h)),
        scratch_shapes=[
            pltpu.VMEM((s, 2 * LANES), BF16),
            pltpu.VMEM((2, 2 * LANES, 2 * tq), BF16),
            pltpu.VMEM((s // tq, DIFF_DV + SUM_ROWS, tq), BF16),
            pltpu.VMEM((tq, 2 * tq), F32),
            pltpu.VMEM((tq, 2 * tq), F32),
            pltpu.VMEM((tq, 2 * tq), F32),
            pltpu.VMEM((2, 1, 2 * tq), F32),
            pltpu.VMEM((2, DIFF_DV + SUM_ROWS, 2 * tq), F32),
        ],
        compiler_params=_params(("parallel", "parallel")),
        name="diff_attn",
    )(dq, dk, dv, jnp.asarray(feat, BF16), slopes, lam_vecs, head_norm.reshape(1, DIFF_DV))


def _merge_kernel(h_ref, za_ref, zb_ref, sg_ref, gt_ref, wa_ref, wb_ref, wo_ref, o_ref):
    d = h_ref.shape[-1]
    ya = jnp.dot(za_ref[0], wa_ref[...].astype(BF16), preferred_element_type=F32)
    yb = jnp.dot(zb_ref[0], wb_ref[...].astype(BF16), preferred_element_type=F32)
    mix = sg_ref[0, :, :d].astype(F32) * ya + sg_ref[0, :, d:].astype(F32) * yb
    m = jnp.dot(mix.astype(BF16), wo_ref[...].astype(BF16), preferred_element_type=F32)
    o_ref[0] = h_ref[0] + gt_ref[0] * m


def _merge(h, za, zb, sg, gt, w_a, w_b, w_o, *, tm=1024):
    bsz, s, d = h.shape
    vec = pl.BlockSpec((1, 1, d), lambda b, i: (b, 0, 0))

    def rows(w):
        return pl.BlockSpec((1, tm, w), lambda b, i: (b, i, 0))

    return pl.pallas_call(
        _merge_kernel,
        out_shape=jax.ShapeDtypeStruct((bsz, s, d), F32),
        grid=(bsz, s // tm),
        in_specs=[rows(d), rows(za.shape[-1]), rows(zb.shape[-1]), rows(2 * d), vec,
                  _resident(w_a.shape), _resident(w_b.shape), _resident(w_o.shape)],
        out_specs=rows(d),
        compiler_params=_params(("parallel", "parallel")),
        name="merge",
    )(h, za, zb, sg, gt, w_a, w_b, w_o)


def kernel(x, c, w_ada, b_ada, ffn1_norm, ffn1_w_in, ffn1_w_out, mix_norm, w_in, gla_alpha_w2, gla_alpha_b, gla_head_norm, diff_lq1, diff_lk1, diff_lq2, diff_lk2, diff_head_norm, w_branch_a, w_branch_b, w_out, ffn2_norm, ffn2_w_in, ffn2_w_out, final_norm):
    depth = w_ada.shape[0]
    bsz, _, d = x.shape
    h = x
    for l in range(depth):
        lam_init = 0.8 - 0.6 * math.exp(-0.3 * l)
        mod = _adaln(c, w_ada[l], b_ada[l])
        sh1, sc1, gt1, sh2, sc2, gt2, sh3, sc3, gt3 = [
            mod[:, i * d:(i + 1) * d].reshape(bsz, 1, d) for i in range(N_MOD)]
        last = l == depth - 1
        h = _ffn(h, sh1, sc1, gt1, ffn1_norm[l], ffn1_w_in[l], ffn1_w_out[l], final_norm,
                 final_norm=False)
        w_t = jnp.swapaxes(w_in, 1, 2)[l]
        gq, gk, gv, gr, g, dq, dk, dv, sg = _mixer_proj(
            h, sh2, sc2, mix_norm[l], w_t, gla_alpha_w2[l], gla_alpha_b[l])
        za = _gla(gq, gk, gv, gr, g, gla_head_norm[l])
        lam_vecs = jnp.stack([diff_lq1[l], diff_lk1[l], diff_lq2[l], diff_lk2[l]])
        zb = _diff_attn(dq, dk, dv, lam_vecs, diff_head_norm[l], lam_init)
        h = _merge(h, za, zb, sg, gt2, w_branch_a[l], w_branch_b[l], w_out[l])
        h = _ffn(h, sh3, sc3, gt3, ffn2_norm[l], ffn2_w_in[l], ffn2_w_out[l], final_norm,
                 final_norm=last)
    return h
```

```python
import functools
import itertools
import math

import jax
import jax.numpy as jnp
import numpy as np
from jax import lax
from jax.experimental import pallas as pl
from jax.experimental.pallas import tpu as pltpu

F32 = jnp.float32
BF16 = jnp.bfloat16

EPS = 1e-6
GLA_HEADS = 4
GLA_DK = 64
GLA_DV = 128
GLA_RANK = 16
GLA_TAU = 16.0
DIFF_HEADS = 4
DIFF_DH = 64
DIFF_DV = 128
ALIBI_MAX_BIAS = 8.0
N_MOD = 9

LOG2E = math.log2(math.e)
LANES = 128
GLA_CHUNK = 128
GLA_SUB = 16
EXP_CLAMP = 80.0
SUM_ROWS = 16
VMEM_LIMIT = 56 * 1024 * 1024


def _params(sem, **flags):
    return pltpu.CompilerParams(dimension_semantics=sem, vmem_limit_bytes=VMEM_LIMIT,
                                flags=flags or None)


def _resident(shape):
    nd = len(shape)
    return pl.BlockSpec(shape, lambda *_: (0,) * nd, pipeline_mode=pl.Buffered(1))


def _rmsnorm(x, g):
    return x * lax.rsqrt(jnp.mean(x * x, axis=-1, keepdims=True) + EPS) * g


def _adaln_kernel(c_ref, w_ref, b_ref, o_ref):
    c = c_ref[...]
    ca = (c * jax.nn.sigmoid(c)).astype(BF16)
    o_ref[...] = jnp.dot(ca, w_ref[...].astype(BF16), preferred_element_type=F32) + b_ref[...]


def _adaln(c, w_ada, b_ada):
    bsz, d = c.shape
    n = w_ada.shape[1]
    tn = n // 4
    return pl.pallas_call(
        _adaln_kernel,
        out_shape=jax.ShapeDtypeStruct((bsz, n), F32),
        grid=(n // tn,),
        in_specs=[
            pl.BlockSpec((bsz, d), lambda j: (0, 0)),
            pl.BlockSpec((d, tn), lambda j: (0, j)),
            pl.BlockSpec((1, tn), lambda j: (0, j)),
        ],
        out_specs=pl.BlockSpec((bsz, tn), lambda j: (0, j)),
        compiler_params=_params(("arbitrary",)),
        name="adaln",
    )(c, w_ada, b_ada.reshape(1, n))


def _ffn_kernel(x_ref, sh_ref, sc_ref, gt_ref, nw_ref, win_hbm, wout_hbm, fn_ref, o_ref,
                win_ref, wout_ref, sem_ref, *, d_ff, tf, sub_m, final_norm):
    n_sub = x_ref.shape[1] // sub_m
    n_chunk = d_ff // tf
    acc_ref = o_ref.at[0]

    def weight_copies(i):
        cols = pl.ds(i * tf, tf)
        up = pl.ds(d_ff + i * tf, tf)
        return (pltpu.make_async_copy(win_hbm.at[:, cols], win_ref.at[:, cols], sem_ref.at[0, i]),
                pltpu.make_async_copy(win_hbm.at[:, up], win_ref.at[:, up], sem_ref.at[1, i]),
                pltpu.make_async_copy(wout_hbm.at[cols, :], wout_ref.at[cols, :], sem_ref.at[2, i]))

    def normed(t):
        x = x_ref[0, t * sub_m:(t + 1) * sub_m, :]
        return (_rmsnorm(x, nw_ref[...]) * (1.0 + sc_ref[0]) + sh_ref[0]).astype(BF16)

    def swiglu(t, u, wait_for_weights):
        rows = slice(t * sub_m, (t + 1) * sub_m)
        for i in range(n_chunk):
            if wait_for_weights:
                for copy in weight_copies(i):
                    copy.wait()
            cols = slice(i * tf, (i + 1) * tf)
            up = slice(d_ff + i * tf, d_ff + (i + 1) * tf)
            hg = jnp.dot(u, win_ref[:, cols].astype(BF16), preferred_element_type=F32)
            hu = jnp.dot(u, win_ref[:, up].astype(BF16), preferred_element_type=F32)
            act = (hg * jax.nn.sigmoid(hg) * hu).astype(BF16)
            part = jnp.dot(act, wout_ref[cols, :].astype(BF16), preferred_element_type=F32)
            if i == 0:
                acc_ref[rows, :] = part
            else:
                acc_ref[rows, :] += part
        h = x_ref[0, rows, :] + (0.5 * gt_ref[0]) * acc_ref[rows, :]
        if final_norm:
            h = _rmsnorm(h, fn_ref[...])
        o_ref[0, rows, :] = h

    def tile(wait_for_weights):
        u = normed(0)
        for t in range(n_sub):
            u_next = normed(t + 1) if t + 1 < n_sub else None
            swiglu(t, u, wait_for_weights and t == 0)
            u = u_next

    first = (pl.program_id(0) == 0) & (pl.program_id(1) == 0)

    @pl.when(first)
    def _():
        for i in range(n_chunk):
            for copy in weight_copies(i):
                copy.start()
        tile(True)

    @pl.when(jnp.logical_not(first))
    def _():
        tile(False)


def _ffn(x, sh, sc, gt, nw, w_in, w_out, fn, *, final_norm, tm=512, sub_m=512, tf=256):
    bsz, s, d = x.shape
    d_ff = w_out.shape[0]
    vec = pl.BlockSpec((1, 1, d), lambda b, i: (b, 0, 0))
    row = pl.BlockSpec((1, tm, d), lambda b, i: (b, i, 0))
    hbm = pl.BlockSpec(memory_space=pl.ANY)
    return pl.pallas_call(
        functools.partial(_ffn_kernel, d_ff=d_ff, tf=tf, sub_m=sub_m, final_norm=final_norm),
        out_shape=jax.ShapeDtypeStruct((bsz, s, d), F32),
        grid=(bsz, s // tm),
        in_specs=[row, vec, vec, vec, _resident((1, d)), hbm, hbm, _resident((1, d))],
        out_specs=row,
        scratch_shapes=[pltpu.VMEM((d, 2 * d_ff), F32), pltpu.VMEM((d_ff, d), F32),
                        pltpu.SemaphoreType.DMA((3, d_ff // tf))],
        compiler_params=_params(("arbitrary", "arbitrary")),
        name="ffn_final" if final_norm else "ffn",
    )(x, sh, sc, gt, nw.reshape(1, d), w_in, w_out, fn.reshape(1, d))


def _proj_kernel(x_ref, sh_ref, sc_ref, nw_ref, wt_ref, w2_ref, b2_ref,
                 gq_ref, gk_ref, gv_ref, gr_ref, g_ref, dq_ref, dk_ref, dv_ref, sg_ref, *, sub_m):
    kw = GLA_HEADS * GLA_DK
    vw = GLA_HEADS * GLA_DV
    qw = DIFF_HEADS * 2 * DIFF_DH
    dvw = DIFF_HEADS * DIFF_DV
    n_gate = sg_ref.shape[-1]
    n_sub = x_ref.shape[1] // sub_m

    def normed(t):
        x = x_ref[0, t * sub_m:(t + 1) * sub_m, :]
        return (_rmsnorm(x, nw_ref[...]) * (1.0 + sc_ref[0]) + sh_ref[0]).astype(BF16)

    def project(t, u):
        rows = slice(t * sub_m, (t + 1) * sub_m)

        def seg(start, size):
            return lax.dot_general(u, wt_ref[start:start + size, :].astype(BF16),
                                   (((1,), (1,)), ((), ())), preferred_element_type=F32)

        off = 0
        gq_ref[0, rows, :] = (seg(off, kw) * (GLA_DK ** -0.5)).astype(BF16)
        off += kw
        gk_ref[0, rows, :] = seg(off, kw).astype(BF16)
        off += kw
        gv_ref[0, rows, :] = seg(off, vw).astype(BF16)
        off += vw
        r = seg(off, vw)
        gr_ref[0, rows, :] = (r * jax.nn.sigmoid(r)).astype(BF16)
        off += vw
        a_low = seg(off, GLA_RANK).astype(BF16)
        z = jnp.dot(a_low, w2_ref[...].astype(BF16), preferred_element_type=F32) + b2_ref[...]
        g_ref[0, rows, :] = (jnp.minimum(z, 0.0) - jnp.log1p(jnp.exp(-jnp.abs(z)))) * (1.0 / GLA_TAU)
        off += GLA_RANK
        dq_ref[0, rows, :] = (seg(off, qw) * (DIFF_DH ** -0.5 * LOG2E)).astype(BF16)
        off += qw
        dk_ref[0, rows, :] = seg(off, qw).astype(BF16)
        off += qw
        dv_ref[0, rows, :] = seg(off, dvw).astype(BF16)
        off += dvw
        step = 512
        for j in range(n_gate // step):
            gate = jax.nn.sigmoid(seg(off + j * step, step))
            sg_ref[0, rows, j * step:(j + 1) * step] = gate.astype(BF16)

    u = normed(0)
    for t in range(n_sub):
        u_next = normed(t + 1) if t + 1 < n_sub else None
        project(t, u)
        u = u_next


def _mixer_proj(h, sh, sc, nw, w_t, w2, b2, *, tm=512, sub_m=512):
    bsz, s, d = h.shape
    kw = GLA_HEADS * GLA_DK
    vw = GLA_HEADS * GLA_DV
    qw = DIFF_HEADS * 2 * DIFF_DH
    dvw = DIFF_HEADS * DIFF_DV
    n_gate = 2 * d
    vec = pl.BlockSpec((1, 1, d), lambda b, i: (b, 0, 0))

    def rows(w):
        return pl.BlockSpec((1, tm, w), lambda b, i: (b, i, 0))

    widths = [(kw, BF16), (kw, BF16), (vw, BF16), (vw, BF16), (kw, F32),
              (qw, BF16), (qw, BF16), (dvw, BF16), (n_gate, BF16)]
    return pl.pallas_call(
        functools.partial(_proj_kernel, sub_m=sub_m),
        out_shape=[jax.ShapeDtypeStruct((bsz, s, w), dt) for w, dt in widths],
        grid=(bsz, s // tm),
        in_specs=[rows(d), vec, vec, _resident((1, d)), _resident(w_t.shape),
                  _resident(w2.shape), _resident((1, kw))],
        out_specs=[rows(w) for w, _ in widths],
        compiler_params=_params(("parallel", "parallel")),
        name="mixer_proj",
    )(h, sh, sc, nw.reshape(1, d), w_t, w2, b2.reshape(1, kw))


def _gla_kernel(q_ref, k_ref, v_ref, r_ref, g_ref, hn_ref, o_ref, st_ref, *, chunk, sub):
    @pl.when(pl.program_id(1) == 0)
    def _():
        st_ref[...] = jnp.zeros_like(st_ref)

    nsub = chunk // sub
    row = lax.broadcasted_iota(jnp.int32, (chunk, chunk), 0)
    col = lax.broadcasted_iota(jnp.int32, (chunk, chunk), 1)
    causal = col <= row
    cum_mat = jnp.concatenate(
        [jnp.where(causal, 1.0, 0.0), jnp.where(causal & (col >= (row // sub) * sub), 1.0, 0.0)],
        axis=0).astype(BF16)
    row_blk = lax.broadcasted_iota(jnp.int32, (chunk, LANES), 0) // sub
    lane_head = lax.broadcasted_iota(jnp.int32, (chunk, LANES), 1) // GLA_DK

    def cumsum(bi, pair):
        g = g_ref[bi, :, pair * LANES:(pair + 1) * LANES]
        g1 = g.astype(BF16)
        e1 = g - g1.astype(F32)
        g2 = e1.astype(BF16)
        g3 = (e1 - g2.astype(F32)).astype(BF16)
        cs = jnp.dot(cum_mat, jnp.concatenate([g1, g2, g3], axis=1), preferred_element_type=F32)
        return cs[:, :LANES] + cs[:, LANES:2 * LANES] + cs[:, 2 * LANES:]

    def decay(bi, pair, cs):
        lanes = slice(pair * LANES, (pair + 1) * LANES)
        b = cs[:chunk]
        w = cs[chunk:]
        b_last = b[chunk - 1:chunk]
        q = q_ref[bi, :, lanes].astype(F32)
        k = k_ref[bi, :, lanes].astype(F32)
        q_cat = []
        for j in range(nsub):
            ref_b = b[j * sub - 1:j * sub] if j else jnp.zeros_like(b_last)
            q_cat.append((q * jnp.exp(jnp.minimum(b - ref_b, 0.0))).astype(BF16))
        return dict(
            q_cat=jnp.concatenate(q_cat, axis=1),
            q_state=(q * jnp.exp(b)).astype(BF16),
            k_state=k * jnp.exp(b_last - b),
            k_hat=k * jnp.exp(jnp.minimum(-w, EXP_CLAMP)),
            chunk_decay=jnp.exp(b_last))

    def scores(d):
        out = []
        for hh in range(2):
            kh = jnp.where(lane_head == hh, d["k_hat"], 0.0).astype(BF16)
            k_cat = jnp.concatenate(
                [jnp.where(row_blk == j, kh, jnp.zeros_like(kh)) for j in range(nsub)], axis=1)
            out.append(lax.dot_general(d["q_cat"], k_cat, (((1,), (1,)), ((), ())),
                                       preferred_element_type=F32))
        return out

    def finish(bi, pair, d, sc):
        for hh in range(2):
            head = pair * 2 + hh
            cols = slice(head * GLA_DV, (head + 1) * GLA_DV)
            intra = jnp.where(causal, sc[hh], 0.0).astype(BF16)
            vh = v_ref[bi, :, cols]
            state_t = st_ref[bi * GLA_HEADS + head]
            o = jnp.dot(intra, vh, preferred_element_type=F32)
            o += lax.dot_general(d["q_state"], state_t.astype(BF16), (((1,), (1,)), ((), ())),
                                 preferred_element_type=F32)
            ks = jnp.where(lane_head == hh, d["k_state"], 0.0).astype(BF16)
            kv_t = lax.dot_general(vh, ks, (((0,), (0,)), ((), ())), preferred_element_type=F32)
            st_ref[bi * GLA_HEADS + head] = state_t * d["chunk_decay"] + kv_t
            y = _rmsnorm(o, hn_ref[...]) * r_ref[bi, :, cols].astype(F32)
            o_ref[bi, :, cols] = y.astype(BF16)

    chains = list(itertools.product(range(q_ref.shape[0]), range(GLA_HEADS // 2)))
    cums = [cumsum(*ch) for ch in chains]
    decayed, scored = {}, {}
    for t in range(len(chains) + 1):
        if t < len(chains):
            decayed[t] = decay(*chains[t], cums[t])
        if t >= 1:
            finish(*chains[t - 1], decayed.pop(t - 1), scored.pop(t - 1))
        if t < len(chains):
            scored[t] = scores(decayed[t])


def _gla(gq, gk, gv, gr, g, head_norm, *, nb=4):
    bsz, s, kw = gq.shape
    vw = gv.shape[-1]
    chunk = GLA_CHUNK

    def rows(w):
        return pl.BlockSpec((nb, chunk, w), lambda b, c: (b, c, 0))

    return pl.pallas_call(
        functools.partial(_gla_kernel, chunk=chunk, sub=GLA_SUB),
        out_shape=jax.ShapeDtypeStruct((bsz, s, vw), BF16),
        grid=(bsz // nb, s // chunk),
        in_specs=[rows(kw), rows(kw), rows(vw), rows(vw), rows(kw), _resident((1, GLA_DV))],
        out_specs=rows(vw),
        scratch_shapes=[pltpu.VMEM((nb * GLA_HEADS, GLA_DV, LANES), F32)],
        compiler_params=_params(("parallel", "arbitrary")),
        name="gla",
    )(gq, gk, gv, gr, g, head_norm.reshape(1, GLA_DV))


def _attn_kernel(q_ref, k_ref, v_ref, pos_ref, slope_ref, lam_ref, hn_ref, o_ref, kk_ref, qq_ref, vt_ref,
                 s0_ref, s1_ref, s2_ref, m_ref, acc_ref, *, tq, rc, lam_init):
    seq = k_ref.shape[1]
    nq = seq // tq
    c = slope_ref[0] * LOG2E
    c1 = c.astype(BF16).astype(F32)
    c2 = (c - c1).astype(BF16).astype(F32)
    c3 = c - c1 - c2

    kk_ref[:, :LANES] = k_ref[0]
    kk_ref[:, LANES:] = pos_ref[...]
    sub = lax.broadcasted_iota(jnp.int32, (LANES, 2 * tq), 0)
    cf = jnp.where((sub == 0) | (sub == 3), c1, jnp.where((sub == 1) | (sub == 4), c2, c3))
    slope_rows = jnp.where(sub < 6, cf, 0.0).astype(BF16)
    qq_ref[0, LANES:, :] = slope_rows
    qq_ref[1, LANES:, :] = slope_rows
    for jb in range(nq):
        vt_ref[jb, :DIFF_DV, :] = v_ref[0, jb * tq:(jb + 1) * tq, :].astype(F32).T.astype(BF16)
        vt_ref[jb, DIFF_DV:, :] = jnp.ones((SUM_ROWS, tq), BF16)
    nchunk = tq // rc
    lam = lam_ref[...]
    lam = (jnp.exp(jnp.sum(lam[0:1] * lam[1:2], axis=-1, keepdims=True))
           - jnp.exp(jnp.sum(lam[2:3] * lam[3:4], axis=-1, keepdims=True)) + lam_init)

    def logits(qs, j, s_blk):
        start = pl.multiple_of(j * tq, tq)
        s = jnp.dot(kk_ref[pl.ds(start, tq), :], qq_ref[qs], preferred_element_type=F32)
        s_blk[...] = s
        return jnp.max(s.reshape(tq // 8, 8, 2 * tq), axis=0)

    def softmax(qs, qi, j, s_blk, mx):
        m_old = m_ref[qs]
        shift = c * jnp.asarray((qi - j) * tq, F32)

        def load(r):
            return s_blk[r * rc:(r + 1) * rc, :]

        m_new = jnp.maximum(m_old, jnp.max(mx, axis=0, keepdims=True) - shift)
        alpha = jnp.exp2(m_old - m_new)
        m_shift = m_new + shift
        probs = jnp.concatenate(
            [jnp.exp2(load(r) - m_shift).astype(BF16) for r in range(nchunk)], axis=0)
        m_ref[qs] = m_new
        return qs, j, alpha, probs

    def accumulate(qs, j, alpha, probs):
        acc_ref[qs] = alpha * acc_ref[qs] + jnp.dot(vt_ref[j], probs, preferred_element_type=F32)

    hq = tq // 2

    def late_half(x):
        return jnp.concatenate([x[..., hq:tq], x[..., tq + hq:]], axis=-1)

    def diag_logits(qs, j, s_blk):
        start = pl.multiple_of(j * tq, tq)
        s_blk[:hq, :] = jnp.dot(kk_ref[pl.ds(start, hq), :], qq_ref[qs],
                                preferred_element_type=F32)
        s_blk[hq:, :tq] = jnp.dot(kk_ref[pl.ds(start + hq, hq), :], late_half(qq_ref[qs]),
                                  preferred_element_type=F32)

    def diag_softmax_accumulate(qs, j, s_blk):
        m_old = m_ref[qs]
        key = lax.broadcasted_iota(jnp.int32, (hq, 2 * tq), 0)
        qry = lax.broadcasted_iota(jnp.int32, (hq, 2 * tq), 1) & (tq - 1)
        early = jnp.where(key <= qry, s_blk[:hq, :], -jnp.inf)
        key_l = lax.broadcasted_iota(jnp.int32, (hq, tq), 0)
        qry_l = lax.broadcasted_iota(jnp.int32, (hq, tq), 1) & (hq - 1)
        late = jnp.where(key_l <= qry_l, s_blk[hq:, :tq], -jnp.inf)
        mx_e = jnp.max(jnp.max(early.reshape(hq // 8, 8, 2 * tq), axis=0), axis=0, keepdims=True)
        mx_l = jnp.max(jnp.max(late.reshape(hq // 8, 8, tq), axis=0), axis=0, keepdims=True)
        mx = jnp.concatenate(
            [mx_e[:, :hq], jnp.maximum(mx_e[:, hq:tq], mx_l[:, :hq]),
             mx_e[:, tq:tq + hq], jnp.maximum(mx_e[:, tq + hq:], mx_l[:, hq:])], axis=1)
        m_new = jnp.maximum(m_old, mx)
        alpha = jnp.exp2(m_old - m_new)
        m_ref[qs] = m_new
        p_early = jnp.exp2(early - m_new).astype(BF16)
        p_late = jnp.exp2(late - late_half(m_new)).astype(BF16)
        vt = vt_ref[j]
        acc_ref[qs] = alpha * acc_ref[qs] + jnp.dot(vt[:, :hq], p_early,
                                                    preferred_element_type=F32)
        upd = jnp.dot(vt[:, hq:], p_late, preferred_element_type=F32)
        acc_ref[qs, :, hq:tq] += upd[:, :hq]
        acc_ref[qs, :, tq + hq:] += upd[:, hq:]

    def pair(p, carry):
        q_blk = (p, nq - 1 - p)
        for qs in range(2):
            q_t = q_ref[0, pl.ds(pl.multiple_of(q_blk[qs] * tq, tq), tq), :].astype(F32).T
            half = lax.broadcasted_iota(jnp.int32, (LANES, tq), 0)
            qq_ref[qs, :LANES, :tq] = jnp.where(half < DIFF_DH, q_t, 0.0).astype(BF16)
            qq_ref[qs, :LANES, tq:] = jnp.where(half >= DIFF_DH, q_t, 0.0).astype(BF16)
            m_ref[qs] = jnp.full((1, 2 * tq), -jnp.inf, F32)
            acc_ref[qs] = jnp.zeros((DIFF_DV + SUM_ROWS, 2 * tq), F32)
        items = [(1, q_blk[1], q_blk[1], True), (0, q_blk[0], q_blk[0], True)]
        for u in range(nq - 1):
            late = u < q_blk[1]
            items.append((jnp.where(late, 1, 0), jnp.where(late, q_blk[1], q_blk[0]),
                          jnp.where(late, u, u - q_blk[1]), False))
        bufs = (s0_ref, s1_ref, s2_ref)
        ahead = len(bufs) - 1
        col_max = {}

        def emit_logits(t):
            qs, _, j, masked = items[t]
            col_max[t] = (diag_logits if masked else logits)(qs, j, bufs[t % len(bufs)])

        for t in range(ahead):
            emit_logits(t)
        for t, (qs, qi, j, masked) in enumerate(items):
            if t + ahead < len(items):
                emit_logits(t + ahead)
            if masked:
                diag_softmax_accumulate(qs, j, bufs[t % len(bufs)])
            else:
                accumulate(*softmax(qs, qi, j, bufs[t % len(bufs)], col_max.pop(t)))
        for qs in range(2):
            out = acc_ref[qs, :DIFF_DV, :] * (1.0 / acc_ref[qs, DIFF_DV:DIFF_DV + 1, :])
            y_t = out[:, :tq] - lam * out[:, tq:]
            y_t = y_t * lax.rsqrt(jnp.mean(y_t * y_t, axis=0, keepdims=True) + EPS)
            rows = pl.ds(pl.multiple_of(q_blk[qs] * tq, tq), tq)
            o_ref[0, rows, :] = (y_t.T * hn_ref[...] * (1.0 - lam_init)).astype(BF16)
        return carry

    lax.fori_loop(0, nq // 2, pair, 0)


def _diff_attn(dq, dk, dv, lam_vecs, head_norm, lam_init, *, tq=512, rc=256):
    bsz, s, _ = dq.shape
    slopes = jnp.asarray(
        [2.0 ** (-ALIBI_MAX_BIAS * (i + 1) / DIFF_HEADS) for i in range(DIFF_HEADS)], F32
    ).reshape(DIFF_HEADS, 1, 1)
    pos = np.arange(s) % tq
    feat = np.zeros((s, LANES), np.float32)
    feat[:, 0:3] = (pos - pos % 16)[:, None]
    feat[:, 3:6] = (pos % 16)[:, None]
    return pl.pallas_call(
        functools.partial(_attn_kernel, tq=tq, rc=rc, lam_init=lam_init),
        out_shape=jax.ShapeDtypeStruct((bsz, s, DIFF_HEADS * DIFF_DV), BF16),
        grid=(bsz, DIFF_HEADS),
        in_specs=[
            pl.BlockSpec((1, s, LANES), lambda b, h: (b, 0, h)),
            pl.BlockSpec((1, s, LANES), lambda b, h: (b, 0, h)),
            pl.BlockSpec((1, s, DIFF_DV), lambda b, h: (b, 0, h)),
            _resident((s, LANES)),
            pl.BlockSpec((1, 1, 1), lambda b, h: (h, 0, 0)),
            _resident(lam_vecs.shape),
            _resident((1, DIFF_DV)),
        ],
        out_specs=pl.BlockSpec((1, s, DIFF_DV), lambda b, h: (b, 0, h)),
        scratch_shapes=[
            pltpu.VMEM((s, 2 * LANES), BF16),
            pltpu.VMEM((2, 2 * LANES, 2 * tq), BF16),
            pltpu.VMEM((s // tq, DIFF_DV + SUM_ROWS, tq), BF16),
            pltpu.VMEM((tq, 2 * tq), F32),
            pltpu.VMEM((tq, 2 * tq), F32),
            pltpu.VMEM((tq, 2 * tq), F32),
            pltpu.VMEM((2, 1, 2 * tq), F32),
            pltpu.VMEM((2, DIFF_DV + SUM_ROWS, 2 * tq), F32),
        ],
        compiler_params=_params(("parallel", "parallel")),
        name="diff_attn",
    )(dq, dk, dv, jnp.asarray(feat, BF16), slopes, lam_vecs, head_norm.reshape(1, DIFF_DV))


def _merge_kernel(h_ref, za_ref, zb_ref, sg_ref, gt_ref, wa_ref, wb_ref, wo_ref, o_ref):
    d = h_ref.shape[-1]
    ya = jnp.dot(za_ref[0], wa_ref[...].astype(BF16), preferred_element_type=F32)
    yb = jnp.dot(zb_ref[0], wb_ref[...].astype(BF16), preferred_element_type=F32)
    mix = sg_ref[0, :, :d].astype(F32) * ya + sg_ref[0, :, d:].astype(F32) * yb
    m = jnp.dot(mix.astype(BF16), wo_ref[...].astype(BF16), preferred_element_type=F32)
    o_ref[0] = h_ref[0] + gt_ref[0] * m


def _merge(h, za, zb, sg, gt, w_a, w_b, w_o, *, tm=1024):
    bsz, s, d = h.shape
    vec = pl.BlockSpec((1, 1, d), lambda b, i: (b, 0, 0))

    def rows(w):
        return pl.BlockSpec((1, tm, w), lambda b, i: (b, i, 0))

    return pl.pallas_call(
        _merge_kernel,
        out_shape=jax.ShapeDtypeStruct((bsz, s, d), F32),
        grid=(bsz, s // tm),
        in_specs=[rows(d), rows(za.shape[-1]), rows(zb.shape[-1]), rows(2 * d), vec,
                  _resident(w_a.shape), _resident(w_b.shape), _resident(w_o.shape)],
        out_specs=rows(d),
        compiler_params=_params(("parallel", "parallel")),
        name="merge",
    )(h, za, zb, sg, gt, w_a, w_b, w_o)


def kernel(x, c, w_ada, b_ada, ffn1_norm, ffn1_w_in, ffn1_w_out, mix_norm, w_in, gla_alpha_w2, gla_alpha_b, gla_head_norm, diff_lq1, diff_lk1, diff_lq2, diff_lk2, diff_head_norm, w_branch_a, w_branch_b, w_out, ffn2_norm, ffn2_w_in, ffn2_w_out, final_norm):
    depth = w_ada.shape[0]
    bsz, _, d = x.shape
    h = x
    for l in range(depth):
        lam_init = 0.8 - 0.6 * math.exp(-0.3 * l)
        mod = _adaln(c, w_ada[l], b_ada[l])
        sh1, sc1, gt1, sh2, sc2, gt2, sh3, sc3, gt3 = [
            mod[:, i * d:(i + 1) * d].reshape(bsz, 1, d) for i in range(N_MOD)]
        last = l == depth - 1
        h = _ffn(h, sh1, sc1, gt1, ffn1_norm[l], ffn1_w_in[l], ffn1_w_out[l], final_norm,
                 final_norm=False)
        w_t = jnp.swapaxes(w_in, 1, 2)[l]
        gq, gk, gv, gr, g, dq, dk, dv, sg = _mixer_proj(
            h, sh2, sc2, mix_norm[l], w_t, gla_alpha_w2[l], gla_alpha_b[l])
        za = _gla(gq, gk, gv, gr, g, gla_head_norm[l])
        lam_vecs = jnp.stack([diff_lq1[l], diff_lk1[l], diff_lq2[l], diff_lk2[l]])
        zb = _diff_attn(dq, dk, dv, lam_vecs, diff_head_norm[l], lam_init)
        h = _merge(h, za, zb, sg, gt2, w_branch_a[l], w_branch_b[l], w_out[l])
        h = _ffn(h, sh3, sc3, gt3, ffn2_norm[l], ffn2_w_in[l], ffn2_w_out[l], final_norm,
                 final_norm=last)
    return h
```

```python
import functools
import itertools
import math

import jax
import jax.numpy as jnp
import numpy as np
from jax import lax
from jax.experimental import pallas as pl
from jax.experimental.pallas import tpu as pltpu

F32 = jnp.float32
BF16 = jnp.bfloat16

EPS = 1e-6
GLA_HEADS = 4
GLA_DK = 64
GLA_DV = 128
GLA_RANK = 16
GLA_TAU = 16.0
DIFF_HEADS = 4
DIFF_DH = 64
DIFF_DV = 128
ALIBI_MAX_BIAS = 8.0
N_MOD = 9

LOG2E = math.log2(math.e)
LANES = 128
GLA_CHUNK = 128
GLA_SUB = 16
EXP_CLAMP = 80.0
SUM_ROWS = 16
VMEM_LIMIT = 56 * 1024 * 1024


def _params(sem, **flags):
    return pltpu.CompilerParams(dimension_semantics=sem, vmem_limit_bytes=VMEM_LIMIT,
                                flags=flags or None)


def _resident(shape):
    nd = len(shape)
    return pl.BlockSpec(shape, lambda *_: (0,) * nd, pipeline_mode=pl.Buffered(1))


def _rmsnorm(x, g):
    return x * lax.rsqrt(jnp.mean(x * x, axis=-1, keepdims=True) + EPS) * g


def _adaln_kernel(c_ref, w_ref, b_ref, o_ref):
    c = c_ref[...]
    ca = (c * jax.nn.sigmoid(c)).astype(BF16)
    o_ref[...] = jnp.dot(ca, w_ref[...].astype(BF16), preferred_element_type=F32) + b_ref[...]


def _adaln(c, w_ada, b_ada):
    bsz, d = c.shape
    n = w_ada.shape[1]
    tn = n // 4
    return pl.pallas_call(
        _adaln_kernel,
        out_shape=jax.ShapeDtypeStruct((bsz, n), F32),
        grid=(n // tn,),
        in_specs=[
            pl.BlockSpec((bsz, d), lambda j: (0, 0)),
            pl.BlockSpec((d, tn), lambda j: (0, j)),
            pl.BlockSpec((1, tn), lambda j: (0, j)),
        ],
        out_specs=pl.BlockSpec((bsz, tn), lambda j: (0, j)),
        compiler_params=_params(("arbitrary",)),
        name="adaln",
    )(c, w_ada, b_ada.reshape(1, n))


def _ffn_kernel(x_ref, sh_ref, sc_ref, gt_ref, nw_ref, win_hbm, wout_hbm, fn_ref, o_ref,
                win_ref, wout_ref, sem_ref, *, d_ff, tf, sub_m, final_norm):
    n_sub = x_ref.shape[1] // sub_m
    n_chunk = d_ff // tf
    acc_ref = o_ref.at[0]

    def weight_copies(i):
        cols = pl.ds(i * tf, tf)
        up = pl.ds(d_ff + i * tf, tf)
        return (pltpu.make_async_copy(win_hbm.at[:, cols], win_ref.at[:, cols], sem_ref.at[0, i]),
                pltpu.make_async_copy(win_hbm.at[:, up], win_ref.at[:, up], sem_ref.at[1, i]),
                pltpu.make_async_copy(wout_hbm.at[cols, :], wout_ref.at[cols, :], sem_ref.at[2, i]))

    def normed(t):
        x = x_ref[0, t * sub_m:(t + 1) * sub_m, :]
        return (_rmsnorm(x, nw_ref[...]) * (1.0 + sc_ref[0]) + sh_ref[0]).astype(BF16)

    def swiglu(t, u, wait_for_weights):
        rows = slice(t * sub_m, (t + 1) * sub_m)
        for i in range(n_chunk):
            if wait_for_weights:
                for copy in weight_copies(i):
                    copy.wait()
            cols = slice(i * tf, (i + 1) * tf)
            up = slice(d_ff + i * tf, d_ff + (i + 1) * tf)
            hg = jnp.dot(u, win_ref[:, cols].astype(BF16), preferred_element_type=F32)
            hu = jnp.dot(u, win_ref[:, up].astype(BF16), preferred_element_type=F32)
            act = (hg * jax.nn.sigmoid(hg) * hu).astype(BF16)
            part = jnp.dot(act, wout_ref[cols, :].astype(BF16), preferred_element_type=F32)
            if i == 0:
                acc_ref[rows, :] = part
            else:
                acc_ref[rows, :] += part
        h = x_ref[0, rows, :] + (0.5 * gt_ref[0]) * acc_ref[rows, :]
        if final_norm:
            h = _rmsnorm(h, fn_ref[...])
        o_ref[0, rows, :] = h

    def tile(wait_for_weights):
        u = normed(0)
        for t in range(n_sub):
            u_next = normed(t + 1) if t + 1 < n_sub else None
            swiglu(t, u, wait_for_weights and t == 0)
            u = u_next

    first = (pl.program_id(0) == 0) & (pl.program_id(1) == 0)

    @pl.when(first)
    def _():
        for i in range(n_chunk):
            for copy in weight_copies(i):
                copy.start()
        tile(True)

    @pl.when(jnp.logical_not(first))
    def _():
        tile(False)


def _ffn(x, sh, sc, gt, nw, w_in, w_out, fn, *, final_norm, tm=512, sub_m=512, tf=256):
    bsz, s, d = x.shape
    d_ff = w_out.shape[0]
    vec = pl.BlockSpec((1, 1, d), lambda b, i: (b, 0, 0))
    row = pl.BlockSpec((1, tm, d), lambda b, i: (b, i, 0))
    hbm = pl.BlockSpec(memory_space=pl.ANY)
    return pl.pallas_call(
        functools.partial(_ffn_kernel, d_ff=d_ff, tf=tf, sub_m=sub_m, final_norm=final_norm),
        out_shape=jax.ShapeDtypeStruct((bsz, s, d), F32),
        grid=(bsz, s // tm),
        in_specs=[row, vec, vec, vec, _resident((1, d)), hbm, hbm, _resident((1, d))],
        out_specs=row,
        scratch_shapes=[pltpu.VMEM((d, 2 * d_ff), F32), pltpu.VMEM((d_ff, d), F32),
                        pltpu.SemaphoreType.DMA((3, d_ff // tf))],
        compiler_params=_params(("arbitrary", "arbitrary")),
        name="ffn_final" if final_norm else "ffn",
    )(x, sh, sc, gt, nw.reshape(1, d), w_in, w_out, fn.reshape(1, d))


def _proj_kernel(x_ref, sh_ref, sc_ref, nw_ref, wt_ref, w2_ref, b2_ref,
                 gq_ref, gk_ref, gv_ref, gr_ref, g_ref, dq_ref, dk_ref, dv_ref, sg_ref, *, sub_m):
    kw = GLA_HEADS * GLA_DK
    vw = GLA_HEADS * GLA_DV
    qw = DIFF_HEADS * 2 * DIFF_DH
    dvw = DIFF_HEADS * DIFF_DV
    n_gate = sg_ref.shape[-1]
    n_sub = x_ref.shape[1] // sub_m

    def normed(t):
        x = x_ref[0, t * sub_m:(t + 1) * sub_m, :]
        return (_rmsnorm(x, nw_ref[...]) * (1.0 + sc_ref[0]) + sh_ref[0]).astype(BF16)

    def project(t, u):
        rows = slice(t * sub_m, (t + 1) * sub_m)

        def seg(start, size):
            return lax.dot_general(u, wt_ref[start:start + size, :].astype(BF16),
                                   (((1,), (1,)), ((), ())), preferred_element_type=F32)

        off = 0
        gq_ref[0, rows, :] = (seg(off, kw) * (GLA_DK ** -0.5)).astype(BF16)
        off += kw
        gk_ref[0, rows, :] = seg(off, kw).astype(BF16)
        off += kw
        gv_ref[0, rows, :] = seg(off, vw).astype(BF16)
        off += vw
        r = seg(off, vw)
        gr_ref[0, rows, :] = (r * jax.nn.sigmoid(r)).astype(BF16)
        off += vw
        a_low = seg(off, GLA_RANK).astype(BF16)
        z = jnp.dot(a_low, w2_ref[...].astype(BF16), preferred_element_type=F32) + b2_ref[...]
        g_ref[0, rows, :] = (jnp.minimum(z, 0.0) - jnp.log1p(jnp.exp(-jnp.abs(z)))) * (1.0 / GLA_TAU)
        off += GLA_RANK
        dq_ref[0, rows, :] = (seg(off, qw) * (DIFF_DH ** -0.5 * LOG2E)).astype(BF16)
        off += qw
        dk_ref[0, rows, :] = seg(off, qw).astype(BF16)
        off += qw
        dv_ref[0, rows, :] = seg(off, dvw).astype(BF16)
        off += dvw
        step = 512
        for j in range(n_gate // step):
            gate = jax.nn.sigmoid(seg(off + j * step, step))
            sg_ref[0, rows, j * step:(j + 1) * step] = gate.astype(BF16)

    u = normed(0)
    for t in range(n_sub):
        u_next = normed(t + 1) if t + 1 < n_sub else None
        project(t, u)
        u = u_next


def _mixer_proj(h, sh, sc, nw, w_t, w2, b2, *, tm=512, sub_m=512):
    bsz, s, d = h.shape
    kw = GLA_HEADS * GLA_DK
    vw = GLA_HEADS * GLA_DV
    qw = DIFF_HEADS * 2 * DIFF_DH
    dvw = DIFF_HEADS * DIFF_DV
    n_gate = 2 * d
    vec = pl.BlockSpec((1, 1, d), lambda b, i: (b, 0, 0))

    def rows(w):
        return pl.BlockSpec((1, tm, w), lambda b, i: (b, i, 0))

    widths = [(kw, BF16), (kw, BF16), (vw, BF16), (vw, BF16), (kw, F32),
              (qw, BF16), (qw, BF16), (dvw, BF16), (n_gate, BF16)]
    return pl.pallas_call(
        functools.partial(_proj_kernel, sub_m=sub_m),
        out_shape=[jax.ShapeDtypeStruct((bsz, s, w), dt) for w, dt in widths],
        grid=(bsz, s // tm),
        in_specs=[rows(d), vec, vec, _resident((1, d)), _resident(w_t.shape),
                  _resident(w2.shape), _resident((1, kw))],
        out_specs=[rows(w) for w, _ in widths],
        compiler_params=_params(("parallel", "parallel")),
        name="mixer_proj",
    )(h, sh, sc, nw.reshape(1, d), w_t, w2, b2.reshape(1, kw))


def _gla_kernel(q_ref, k_ref, v_ref, r_ref, g_ref, hn_ref, o_ref, st_ref, *, chunk, sub):
    @pl.when(pl.program_id(1) == 0)
    def _():
        st_ref[...] = jnp.zeros_like(st_ref)

    nsub = chunk // sub
    row = lax.broadcasted_iota(jnp.int32, (chunk, chunk), 0)
    col = lax.broadcasted_iota(jnp.int32, (chunk, chunk), 1)
    causal = col <= row
    cum_mat = jnp.where(causal, 1.0, 0.0).astype(BF16)
    row_blk = lax.broadcasted_iota(jnp.int32, (chunk, LANES), 0) // sub
    lane_head = lax.broadcasted_iota(jnp.int32, (chunk, LANES), 1) // GLA_DK

    def cumsum(bi, pair):
        g = g_ref[bi, :, pair * LANES:(pair + 1) * LANES]
        g1 = g.astype(BF16)
        e1 = g - g1.astype(F32)
        g2 = e1.astype(BF16)
        g3 = (e1 - g2.astype(F32)).astype(BF16)
        cs = jnp.dot(cum_mat, jnp.concatenate([g1, g2, g3], axis=1), preferred_element_type=F32)
        return cs[:, :LANES] + cs[:, LANES:2 * LANES] + cs[:, 2 * LANES:]

    def decay(bi, pair, b):
        lanes = slice(pair * LANES, (pair + 1) * LANES)
        b_last = b[chunk - 1:chunk]
        q = q_ref[bi, :, lanes].astype(F32)
        k = k_ref[bi, :, lanes].astype(F32)
        q_cat, w = [], []
        for j in range(nsub):
            ref_b = b[j * sub - 1:j * sub] if j else jnp.zeros_like(b_last)
            q_cat.append((q * jnp.exp(jnp.minimum(b - ref_b, 0.0))).astype(BF16))
            w.append(b[j * sub:(j + 1) * sub] - ref_b)
        w = jnp.concatenate(w, axis=0)
        return dict(
            q_cat=jnp.concatenate(q_cat, axis=1),
            q_state=(q * jnp.exp(b)).astype(BF16),
            k_state=k * jnp.exp(b_last - b),
            k_hat=k * jnp.exp(jnp.minimum(-w, EXP_CLAMP)),
            chunk_decay=jnp.exp(b_last))

    def scores(d):
        out = []
        for hh in range(2):
            kh = jnp.where(lane_head == hh, d["k_hat"], 0.0).astype(BF16)
            k_cat = jnp.concatenate(
                [jnp.where(row_blk == j, kh, jnp.zeros_like(kh)) for j in range(nsub)], axis=1)
            out.append(lax.dot_general(d["q_cat"], k_cat, (((1,), (1,)), ((), ())),
                                       preferred_element_type=F32))
        return out

    def finish(bi, pair, d, sc):
        for hh in range(2):
            head = pair * 2 + hh
            cols = slice(head * GLA_DV, (head + 1) * GLA_DV)
            intra = jnp.where(causal, sc[hh], 0.0).astype(BF16)
            vh = v_ref[bi, :, cols]
            state_t = st_ref[bi * GLA_HEADS + head]
            o = jnp.dot(intra, vh, preferred_element_type=F32)
            o += lax.dot_general(d["q_state"], state_t.astype(BF16), (((1,), (1,)), ((), ())),
                                 preferred_element_type=F32)
            ks = jnp.where(lane_head == hh, d["k_state"], 0.0).astype(BF16)
            kv_t = lax.dot_general(vh, ks, (((0,), (0,)), ((), ())), preferred_element_type=F32)
            st_ref[bi * GLA_HEADS + head] = state_t * d["chunk_decay"] + kv_t
            y = _rmsnorm(o, hn_ref[...]) * r_ref[bi, :, cols].astype(F32)
            o_ref[bi, :, cols] = y.astype(BF16)

    chains = list(itertools.product(range(q_ref.shape[0]), range(GLA_HEADS // 2)))
    cums = [cumsum(*ch) for ch in chains]
    decayed, scored = {}, {}
    for t in range(len(chains) + 1):
        if t < len(chains):
            decayed[t] = decay(*chains[t], cums[t])
        if t >= 1:
            finish(*chains[t - 1], decayed.pop(t - 1), scored.pop(t - 1))
        if t < len(chains):
            scored[t] = scores(decayed[t])


def _gla(gq, gk, gv, gr, g, head_norm, *, nb=4):
    bsz, s, kw = gq.shape
    vw = gv.shape[-1]
    chunk = GLA_CHUNK

    def rows(w):
        return pl.BlockSpec((nb, chunk, w), lambda b, c: (b, c, 0))

    return pl.pallas_call(
        functools.partial(_gla_kernel, chunk=chunk, sub=GLA_SUB),
        out_shape=jax.ShapeDtypeStruct((bsz, s, vw), BF16),
        grid=(bsz // nb, s // chunk),
        in_specs=[rows(kw), rows(kw), rows(vw), rows(vw), rows(kw), _resident((1, GLA_DV))],
        out_specs=rows(vw),
        scratch_shapes=[pltpu.VMEM((nb * GLA_HEADS, GLA_DV, LANES), F32)],
        compiler_params=_params(("parallel", "arbitrary")),
        name="gla",
    )(gq, gk, gv, gr, g, head_norm.reshape(1, GLA_DV))


def _attn_kernel(q_ref, k_ref, v_ref, pos_ref, slope_ref, lam_ref, hn_ref, o_ref, kk_ref, qq_ref, vt_ref,
                 s0_ref, s1_ref, s2_ref, m_ref, acc_ref, *, tq, rc, lam_init):
    seq = k_ref.shape[1]
    nq = seq // tq
    c = slope_ref[0] * LOG2E
    c1 = c.astype(BF16).astype(F32)
    c2 = (c - c1).astype(BF16).astype(F32)
    c3 = c - c1 - c2

    kk_ref[:, :LANES] = k_ref[0]
    kk_ref[:, LANES:] = pos_ref[...]
    sub = lax.broadcasted_iota(jnp.int32, (LANES, 2 * tq), 0)
    cf = jnp.where((sub == 0) | (sub == 3), c1, jnp.where((sub == 1) | (sub == 4), c2, c3))
    slope_rows = jnp.where(sub < 6, cf, 0.0).astype(BF16)
    for i in range(qq_ref.shape[0]):
        qq_ref[i, LANES:, :] = slope_rows
    for jb in range(nq):
        vt_ref[jb, :DIFF_DV, :] = v_ref[0, jb * tq:(jb + 1) * tq, :].astype(F32).T.astype(BF16)
        vt_ref[jb, DIFF_DV:, :] = jnp.ones((SUM_ROWS, tq), BF16)
    nchunk = tq // rc
    lam = lam_ref[...]
    lam = (jnp.exp(jnp.sum(lam[0:1] * lam[1:2], axis=-1, keepdims=True))
           - jnp.exp(jnp.sum(lam[2:3] * lam[3:4], axis=-1, keepdims=True)) + lam_init)

    def logits(qs, j, s_blk):
        start = pl.multiple_of(j * tq, tq)
        s = jnp.dot(kk_ref[pl.ds(start, tq), :], qq_ref[qs], preferred_element_type=F32)
        s_blk[...] = s
        return jnp.max(s.reshape(tq // 8, 8, 2 * tq), axis=0)

    def softmax(qs, qi, j, s_blk, mx):
        m_old = m_ref[qs]
        shift = c * jnp.asarray((qi - j) * tq, F32)

        def load(r):
            return s_blk[r * rc:(r + 1) * rc, :]

        m_new = jnp.maximum(m_old, jnp.max(mx, axis=0, keepdims=True) - shift)
        alpha = jnp.exp2(m_old - m_new)
        m_shift = m_new + shift
        probs = jnp.concatenate(
            [jnp.exp2(load(r) - m_shift).astype(BF16) for r in range(nchunk)], axis=0)
        m_ref[qs] = m_new
        return qs, j, alpha, probs

    def accumulate(qs, j, alpha, probs):
        acc_ref[qs] = alpha * acc_ref[qs] + jnp.dot(vt_ref[j], probs, preferred_element_type=F32)

    hq = tq // 2

    def late_half(x):
        return jnp.concatenate([x[..., hq:tq], x[..., tq + hq:]], axis=-1)

    def diag_logits(qs, j, s_blk):
        start = pl.multiple_of(j * tq, tq)
        s_blk[:hq, :] = jnp.dot(kk_ref[pl.ds(start, hq), :], qq_ref[qs],
                                preferred_element_type=F32)
        s_blk[hq:, :tq] = jnp.dot(kk_ref[pl.ds(start + hq, hq), :], late_half(qq_ref[qs]),
                                  preferred_element_type=F32)

    def diag_softmax_accumulate(qs, j, s_blk):
        m_old = m_ref[qs]
        key = lax.broadcasted_iota(jnp.int32, (hq, 2 * tq), 0)
        qry = lax.broadcasted_iota(jnp.int32, (hq, 2 * tq), 1) & (tq - 1)
        early = jnp.where(key <= qry, s_blk[:hq, :], -jnp.inf)
        key_l = lax.broadcasted_iota(jnp.int32, (hq, tq), 0)
        qry_l = lax.broadcasted_iota(jnp.int32, (hq, tq), 1) & (hq - 1)
        late = jnp.where(key_l <= qry_l, s_blk[hq:, :tq], -jnp.inf)
        mx_e = jnp.max(jnp.max(early.reshape(hq // 8, 8, 2 * tq), axis=0), axis=0, keepdims=True)
        mx_l = jnp.max(jnp.max(late.reshape(hq // 8, 8, tq), axis=0), axis=0, keepdims=True)
        mx = jnp.concatenate(
            [mx_e[:, :hq], jnp.maximum(mx_e[:, hq:tq], mx_l[:, :hq]),
             mx_e[:, tq:tq + hq], jnp.maximum(mx_e[:, tq + hq:], mx_l[:, hq:])], axis=1)
        m_new = jnp.maximum(m_old, mx)
        alpha = jnp.exp2(m_old - m_new)
        m_ref[qs] = m_new
        p_early = jnp.exp2(early - m_new).astype(BF16)
        p_late = jnp.exp2(late - late_half(m_new)).astype(BF16)
        vt = vt_ref[j]
        acc_ref[qs] = alpha * acc_ref[qs] + jnp.dot(vt[:, :hq], p_early,
                                                    preferred_element_type=F32)
        upd = jnp.dot(vt[:, hq:], p_late, preferred_element_type=F32)
        acc_ref[qs, :, hq:tq] += upd[:, :hq]
        acc_ref[qs, :, tq + hq:] += upd[:, hq:]

    bufs = (s0_ref, s1_ref, s2_ref)
    AHEAD = len(bufs) - 1
    assert (nq + 1) % len(bufs) == 0

    def build_queries(p, slot):
        for qs, blk in enumerate((p, nq - 1 - p)):
            q_t = q_ref[0, pl.ds(pl.multiple_of(blk * tq, tq), tq), :].astype(F32).T
            half = lax.broadcasted_iota(jnp.int32, (LANES, tq), 0)
            qq_ref[2 * slot + qs, :LANES, :tq] = jnp.where(half < DIFF_DH, q_t, 0.0).astype(BF16)
            qq_ref[2 * slot + qs, :LANES, tq:] = jnp.where(half >= DIFF_DH, q_t, 0.0).astype(BF16)

    def diag_prefetch(p, slot, t):
        qs, blk = ((1, nq - 1 - p), (0, p))[t]
        diag_logits(2 * slot + qs, blk, bufs[t])

    def pair(p, carry):
        slot = p % 2
        q_blk = (p, nq - 1 - p)
        for qs in range(2):
            m_ref[qs] = jnp.full((1, 2 * tq), -jnp.inf, F32)
            acc_ref[qs] = jnp.zeros((DIFF_DV + SUM_ROWS, 2 * tq), F32)
        items = [(1, q_blk[1], q_blk[1], True), (0, q_blk[0], q_blk[0], True)]
        for u in range(nq - 1):
            late = u < q_blk[1]
            items.append((jnp.where(late, 1, 0), jnp.where(late, q_blk[1], q_blk[0]),
                          jnp.where(late, u, u - q_blk[1]), False))
        col_max = {}
        p_next = jnp.minimum(p + 1, nq // 2 - 1)
        for t, (qs, qi, j, masked) in enumerate(items):
            ta = t + AHEAD
            if ta < len(items):
                col_max[ta] = logits(2 * slot + items[ta][0], items[ta][2], bufs[ta % len(bufs)])
            elif ta == len(items):
                build_queries(p_next, 1 - slot)
                diag_prefetch(p_next, 1 - slot, 0)
            else:
                diag_prefetch(p_next, 1 - slot, 1)
            if masked:
                diag_softmax_accumulate(qs, j, bufs[t % len(bufs)])
            else:
                accumulate(*softmax(qs, qi, j, bufs[t % len(bufs)], col_max.pop(t)))
        for qs in range(2):
            out = acc_ref[qs, :DIFF_DV, :] * (1.0 / acc_ref[qs, DIFF_DV:DIFF_DV + 1, :])
            y_t = out[:, :tq] - lam * out[:, tq:]
            y_t = y_t * lax.rsqrt(jnp.mean(y_t * y_t, axis=0, keepdims=True) + EPS)
            rows = pl.ds(pl.multiple_of(q_blk[qs] * tq, tq), tq)
            o_ref[0, rows, :] = (y_t.T * hn_ref[...] * (1.0 - lam_init)).astype(BF16)
        return carry

    build_queries(0, 0)
    diag_prefetch(0, 0, 0)
    diag_prefetch(0, 0, 1)
    lax.fori_loop(0, nq // 2, pair, 0)


def _diff_attn(dq, dk, dv, lam_vecs, head_norm, lam_init, *, tq=512, rc=256):
    bsz, s, _ = dq.shape
    slopes = jnp.asarray(
        [2.0 ** (-ALIBI_MAX_BIAS * (i + 1) / DIFF_HEADS) for i in range(DIFF_HEADS)], F32
    ).reshape(DIFF_HEADS, 1, 1)
    pos = np.arange(s) % tq
    feat = np.zeros((s, LANES), np.float32)
    feat[:, 0:3] = (pos - pos % 16)[:, None]
    feat[:, 3:6] = (pos % 16)[:, None]
    return pl.pallas_call(
        functools.partial(_attn_kernel, tq=tq, rc=rc, lam_init=lam_init),
        out_shape=jax.ShapeDtypeStruct((bsz, s, DIFF_HEADS * DIFF_DV), BF16),
        grid=(bsz, DIFF_HEADS),
        in_specs=[
            pl.BlockSpec((1, s, LANES), lambda b, h: (b, 0, h)),
            pl.BlockSpec((1, s, LANES), lambda b, h: (b, 0, h)),
            pl.BlockSpec((1, s, DIFF_DV), lambda b, h: (b, 0, h)),
            _resident((s, LANES)),
            pl.BlockSpec((1, 1, 1), lambda b, h: (h, 0, 0)),
            _resident(lam_vecs.shape),
            _resident((1, DIFF_DV)),
        ],
        out_specs=pl.BlockSpec((1, s, DIFF_DV), lambda b, h: (b, 0, h)),
        scratch_shapes=[
            pltpu.VMEM((s, 2 * LANES), BF16),
            pltpu.VMEM((4, 2 * LANES, 2 * tq), BF16),
            pltpu.VMEM((s // tq, DIFF_DV + SUM_ROWS, tq), BF16),
            pltpu.VMEM((tq, 2 * tq), F32),
            pltpu.VMEM((tq, 2 * tq), F32),
            pltpu.VMEM((tq, 2 * tq), F32),
            pltpu.VMEM((2, 1, 2 * tq), F32),
            pltpu.VMEM((2, DIFF_DV + SUM_ROWS, 2 * tq), F32),
        ],
        compiler_params=_params(("parallel", "parallel")),
        name="diff_attn",
    )(dq, dk, dv, jnp.asarray(feat, BF16), slopes, lam_vecs, head_norm.reshape(1, DIFF_DV))


def _merge_kernel(h_ref, za_ref, zb_ref, sg_ref, gt_ref, wa_ref, wb_ref, wo_ref, o_ref):
    d = h_ref.shape[-1]
    ya = jnp.dot(za_ref[0], wa_ref[...].astype(BF16), preferred_element_type=F32)
    yb = jnp.dot(zb_ref[0], wb_ref[...].astype(BF16), preferred_element_type=F32)
    mix = sg_ref[0, :, :d].astype(F32) * ya + sg_ref[0, :, d:].astype(F32) * yb
    m = jnp.dot(mix.astype(BF16), wo_ref[...].astype(BF16), preferred_element_type=F32)
    o_ref[0] = h_ref[0] + gt_ref[0] * m


def _merge(h, za, zb, sg, gt, w_a, w_b, w_o, *, tm=1024):
    bsz, s, d = h.shape
    vec = pl.BlockSpec((1, 1, d), lambda b, i: (b, 0, 0))

    def rows(w):
        return pl.BlockSpec((1, tm, w), lambda b, i: (b, i, 0))

    return pl.pallas_call(
        _merge_kernel,
        out_shape=jax.ShapeDtypeStruct((bsz, s, d), F32),
        grid=(bsz, s // tm),
        in_specs=[rows(d), rows(za.shape[-1]), rows(zb.shape[-1]), rows(2 * d), vec,
                  _resident(w_a.shape), _resident(w_b.shape), _resident(w_o.shape)],
        out_specs=rows(d),
        compiler_params=_params(("parallel", "parallel")),
        name="merge",
    )(h, za, zb, sg, gt, w_a, w_b, w_o)


def kernel(x, c, w_ada, b_ada, ffn1_norm, ffn1_w_in, ffn1_w_out, mix_norm, w_in, gla_alpha_w2, gla_alpha_b, gla_head_norm, diff_lq1, diff_lk1, diff_lq2, diff_lk2, diff_head_norm, w_branch_a, w_branch_b, w_out, ffn2_norm, ffn2_w_in, ffn2_w_out, final_norm):
    depth = w_ada.shape[0]
    bsz, _, d = x.shape
    h = x
    for l in range(depth):
        lam_init = 0.8 - 0.6 * math.exp(-0.3 * l)
        mod = _adaln(c, w_ada[l], b_ada[l])
        sh1, sc1, gt1, sh2, sc2, gt2, sh3, sc3, gt3 = [
            mod[:, i * d:(i + 1) * d].reshape(bsz, 1, d) for i in range(N_MOD)]
        last = l == depth - 1
        h = _ffn(h, sh1, sc1, gt1, ffn1_norm[l], ffn1_w_in[l], ffn1_w_out[l], final_norm,
                 final_norm=False)
        w_t = jnp.swapaxes(w_in, 1, 2)[l]
        gq, gk, gv, gr, g, dq, dk, dv, sg = _mixer_proj(
            h, sh2, sc2, mix_norm[l], w_t, gla_alpha_w2[l], gla_alpha_b[l])
        za = _gla(gq, gk, gv, gr, g, gla_head_norm[l])
        lam_vecs = jnp.stack([diff_lq1[l], diff_lk1[l], diff_lq2[l], diff_lk2[l]])
        zb = _diff_attn(dq, dk, dv, lam_vecs, diff_head_norm[l], lam_init)
        h = _merge(h, za, zb, sg, gt2, w_branch_a[l], w_branch_b[l], w_out[l])
        h = _ffn(h, sh3, sc3, gt3, ffn2_norm[l], ffn2_w_in[l], ffn2_w_out[l], final_norm,
                 final_norm=last)
    return h
```

```python
import functools
import itertools
import math

import jax
import jax.numpy as jnp
import numpy as np
from jax import lax
from jax.experimental import pallas as pl
from jax.experimental.pallas import tpu as pltpu

F32 = jnp.float32
BF16 = jnp.bfloat16

EPS = 1e-6
GLA_HEADS = 4
GLA_DK = 64
GLA_DV = 128
GLA_RANK = 16
GLA_TAU = 16.0
DIFF_HEADS = 4
DIFF_DH = 64
DIFF_DV = 128
ALIBI_MAX_BIAS = 8.0
N_MOD = 9

LOG2E = math.log2(math.e)
LANES = 128
GLA_CHUNK = 128
GLA_SUB = 16
EXP_CLAMP = 80.0
SUM_ROWS = 16
VMEM_LIMIT = 56 * 1024 * 1024


def _params(sem, **flags):
    return pltpu.CompilerParams(dimension_semantics=sem, vmem_limit_bytes=VMEM_LIMIT,
                                flags=flags or None)


def _resident(shape):
    nd = len(shape)
    return pl.BlockSpec(shape, lambda *_: (0,) * nd, pipeline_mode=pl.Buffered(1))


def _rmsnorm(x, g):
    return x * lax.rsqrt(jnp.mean(x * x, axis=-1, keepdims=True) + EPS) * g


def _adaln_kernel(c_ref, w_ref, b_ref, o_ref):
    c = c_ref[...]
    ca = (c * jax.nn.sigmoid(c)).astype(BF16)
    o_ref[...] = jnp.dot(ca, w_ref[...].astype(BF16), preferred_element_type=F32) + b_ref[...]


def _adaln(c, w_ada, b_ada):
    bsz, d = c.shape
    n = w_ada.shape[1]
    tn = n // 4
    return pl.pallas_call(
        _adaln_kernel,
        out_shape=jax.ShapeDtypeStruct((bsz, n), F32),
        grid=(n // tn,),
        in_specs=[
            pl.BlockSpec((bsz, d), lambda j: (0, 0)),
            pl.BlockSpec((d, tn), lambda j: (0, j)),
            pl.BlockSpec((1, tn), lambda j: (0, j)),
        ],
        out_specs=pl.BlockSpec((bsz, tn), lambda j: (0, j)),
        compiler_params=_params(("arbitrary",)),
        name="adaln",
    )(c, w_ada, b_ada.reshape(1, n))


def _ffn_kernel(*refs, d_ff, tf, sub_m, final_norm, has_delta):
    x_ref, dx_ref = (refs[0], refs[1]) if has_delta else (refs[0], None)
    (sh_ref, sc_ref, gt_ref, nw_ref, win_hbm, wout_hbm, fn_ref, o_ref,
     win_ref, wout_ref, sem_ref) = refs[2 if has_delta else 1:]
    n_sub = x_ref.shape[1] // sub_m

    def residual_in(rows):
        x = x_ref[0, rows, :]
        return x if dx_ref is None else x + dx_ref[0, rows, :]

    n_chunk = d_ff // tf
    acc_ref = o_ref.at[0]

    def weight_copies(i):
        cols = pl.ds(i * tf, tf)
        up = pl.ds(d_ff + i * tf, tf)
        return (pltpu.make_async_copy(win_hbm.at[:, cols], win_ref.at[:, cols], sem_ref.at[0, i]),
                pltpu.make_async_copy(win_hbm.at[:, up], win_ref.at[:, up], sem_ref.at[1, i]),
                pltpu.make_async_copy(wout_hbm.at[cols, :], wout_ref.at[cols, :], sem_ref.at[2, i]))

    def normed(t):
        x = residual_in(slice(t * sub_m, (t + 1) * sub_m))
        return (_rmsnorm(x, nw_ref[...]) * (1.0 + sc_ref[0]) + sh_ref[0]).astype(BF16)

    def swiglu(t, u, wait_for_weights):
        rows = slice(t * sub_m, (t + 1) * sub_m)
        for i in range(n_chunk):
            if wait_for_weights:
                for copy in weight_copies(i):
                    copy.wait()
            cols = slice(i * tf, (i + 1) * tf)
            up = slice(d_ff + i * tf, d_ff + (i + 1) * tf)
            hg = jnp.dot(u, win_ref[:, cols].astype(BF16), preferred_element_type=F32)
            hu = jnp.dot(u, win_ref[:, up].astype(BF16), preferred_element_type=F32)
            act = (hg * jax.nn.sigmoid(hg) * hu).astype(BF16)
            part = jnp.dot(act, wout_ref[cols, :].astype(BF16), preferred_element_type=F32)
            if i == 0:
                acc_ref[rows, :] = part
            else:
                acc_ref[rows, :] += part
        h = residual_in(rows) + (0.5 * gt_ref[0]) * acc_ref[rows, :]
        if final_norm:
            h = _rmsnorm(h, fn_ref[...])
        o_ref[0, rows, :] = h

    def tile(wait_for_weights):
        u = normed(0)
        for t in range(n_sub):
            u_next = normed(t + 1) if t + 1 < n_sub else None
            swiglu(t, u, wait_for_weights and t == 0)
            u = u_next

    first = (pl.program_id(0) == 0) & (pl.program_id(1) == 0)

    @pl.when(first)
    def _():
        for i in range(n_chunk):
            for copy in weight_copies(i):
                copy.start()
        tile(True)

    @pl.when(jnp.logical_not(first))
    def _():
        tile(False)


def _ffn(x, dx, sh, sc, gt, nw, w_in, w_out, fn, *, final_norm, tm=512, sub_m=512, tf=256):
    bsz, s, d = x.shape
    d_ff = w_out.shape[0]
    vec = pl.BlockSpec((1, 1, d), lambda b, i: (b, 0, 0))
    row = pl.BlockSpec((1, tm, d), lambda b, i: (b, i, 0))
    hbm = pl.BlockSpec(memory_space=pl.ANY)
    streams = (x,) if dx is None else (x, dx)
    return pl.pallas_call(
        functools.partial(_ffn_kernel, d_ff=d_ff, tf=tf, sub_m=sub_m, final_norm=final_norm,
                          has_delta=dx is not None),
        out_shape=jax.ShapeDtypeStruct((bsz, s, d), F32),
        grid=(bsz, s // tm),
        in_specs=[row] * len(streams) + [vec, vec, vec, _resident((1, d)), hbm, hbm,
                                         _resident((1, d))],
        out_specs=row,
        scratch_shapes=[pltpu.VMEM((d, 2 * d_ff), F32), pltpu.VMEM((d_ff, d), F32),
                        pltpu.SemaphoreType.DMA((3, d_ff // tf))],
        compiler_params=_params(("arbitrary", "arbitrary")),
        name="ffn_final" if final_norm else "ffn",
    )(*streams, sh, sc, gt, nw.reshape(1, d), w_in, w_out, fn.reshape(1, d))


def _proj_kernel(x_ref, sh_ref, sc_ref, nw_ref, wt_ref, w2_ref, b2_ref,
                 gq_ref, gk_ref, gv_ref, gr_ref, g_ref, dq_ref, dk_ref, dv_ref, sg_ref, *, sub_m):
    kw = GLA_HEADS * GLA_DK
    vw = GLA_HEADS * GLA_DV
    qw = DIFF_HEADS * 2 * DIFF_DH
    dvw = DIFF_HEADS * DIFF_DV
    n_gate = sg_ref.shape[-1]
    n_sub = x_ref.shape[1] // sub_m

    def normed(t):
        x = x_ref[0, t * sub_m:(t + 1) * sub_m, :]
        return (_rmsnorm(x, nw_ref[...]) * (1.0 + sc_ref[0]) + sh_ref[0]).astype(BF16)

    def project(t, u):
        rows = slice(t * sub_m, (t + 1) * sub_m)

        def seg(start, size):
            return lax.dot_general(u, wt_ref[start:start + size, :].astype(BF16),
                                   (((1,), (1,)), ((), ())), preferred_element_type=F32)

        off = 0
        gq_ref[0, rows, :] = (seg(off, kw) * (GLA_DK ** -0.5)).astype(BF16)
        off += kw
        gk_ref[0, rows, :] = seg(off, kw).astype(BF16)
        off += kw
        gv_ref[0, rows, :] = seg(off, vw).astype(BF16)
        off += vw
        r = seg(off, vw)
        gr_ref[0, rows, :] = (r * jax.nn.sigmoid(r)).astype(BF16)
        off += vw
        a_low = seg(off, GLA_RANK).astype(BF16)
        z = jnp.dot(a_low, w2_ref[...].astype(BF16), preferred_element_type=F32) + b2_ref[...]
        g_ref[0, rows, :] = (jnp.minimum(z, 0.0) - jnp.log1p(jnp.exp(-jnp.abs(z)))) * (1.0 / GLA_TAU)
        off += GLA_RANK
        dq_ref[0, rows, :] = (seg(off, qw) * (DIFF_DH ** -0.5 * LOG2E)).astype(BF16)
        off += qw
        dk_ref[0, rows, :] = seg(off, qw).astype(BF16)
        off += qw
        dv_ref[0, rows, :] = seg(off, dvw).astype(BF16)
        off += dvw
        step = 512
        for j in range(n_gate // step):
            gate = jax.nn.sigmoid(seg(off + j * step, step))
            sg_ref[0, rows, j * step:(j + 1) * step] = gate.astype(BF16)

    u = normed(0)
    for t in range(n_sub):
        u_next = normed(t + 1) if t + 1 < n_sub else None
        project(t, u)
        u = u_next


def _mixer_proj(h, sh, sc, nw, w_t, w2, b2, *, tm=512, sub_m=512):
    bsz, s, d = h.shape
    kw = GLA_HEADS * GLA_DK
    vw = GLA_HEADS * GLA_DV
    qw = DIFF_HEADS * 2 * DIFF_DH
    dvw = DIFF_HEADS * DIFF_DV
    n_gate = 2 * d
    vec = pl.BlockSpec((1, 1, d), lambda b, i: (b, 0, 0))

    def rows(w):
        return pl.BlockSpec((1, tm, w), lambda b, i: (b, i, 0))

    widths = [(kw, BF16), (kw, BF16), (vw, BF16), (vw, BF16), (kw, F32),
              (qw, BF16), (qw, BF16), (dvw, BF16), (n_gate, BF16)]
    return pl.pallas_call(
        functools.partial(_proj_kernel, sub_m=sub_m),
        out_shape=[jax.ShapeDtypeStruct((bsz, s, w), dt) for w, dt in widths],
        grid=(bsz, s // tm),
        in_specs=[rows(d), vec, vec, _resident((1, d)), _resident(w_t.shape),
                  _resident(w2.shape), _resident((1, kw))],
        out_specs=[rows(w) for w, _ in widths],
        compiler_params=_params(("parallel", "parallel")),
        name="mixer_proj",
    )(h, sh, sc, nw.reshape(1, d), w_t, w2, b2.reshape(1, kw))


def _gla_kernel(q_ref, k_ref, v_ref, r_ref, g_ref, hn_ref, o_ref, st_ref, *, chunk, sub):
    @pl.when(pl.program_id(1) == 0)
    def _():
        st_ref[...] = jnp.zeros_like(st_ref)

    nsub = chunk // sub
    row = lax.broadcasted_iota(jnp.int32, (chunk, chunk), 0)
    col = lax.broadcasted_iota(jnp.int32, (chunk, chunk), 1)
    causal = col <= row
    cum_mat = jnp.where(causal, 1.0, 0.0).astype(BF16)
    row_blk = lax.broadcasted_iota(jnp.int32, (chunk, LANES), 0) // sub
    lane_head = lax.broadcasted_iota(jnp.int32, (chunk, LANES), 1) // GLA_DK

    def cumsum(bi, pair):
        g = g_ref[bi, :, pair * LANES:(pair + 1) * LANES]
        g1 = g.astype(BF16)
        e1 = g - g1.astype(F32)
        g2 = e1.astype(BF16)
        g3 = (e1 - g2.astype(F32)).astype(BF16)
        cs = jnp.dot(cum_mat, jnp.concatenate([g1, g2, g3], axis=1), preferred_element_type=F32)
        return cs[:, :LANES] + cs[:, LANES:2 * LANES] + cs[:, 2 * LANES:]

    def decay(bi, pair, b):
        lanes = slice(pair * LANES, (pair + 1) * LANES)
        b_last = b[chunk - 1:chunk]
        q = q_ref[bi, :, lanes].astype(F32)
        k = k_ref[bi, :, lanes].astype(F32)
        q_cat, w = [], []
        for j in range(nsub):
            ref_b = b[j * sub - 1:j * sub] if j else jnp.zeros_like(b_last)
            q_cat.append((q * jnp.exp(jnp.minimum(b - ref_b, 0.0))).astype(BF16))
            w.append(b[j * sub:(j + 1) * sub] - ref_b)
        w = jnp.concatenate(w, axis=0)
        return dict(
            q_cat=jnp.concatenate(q_cat, axis=1),
            q_state=(q * jnp.exp(b)).astype(BF16),
            k_state=k * jnp.exp(b_last - b),
            k_hat=k * jnp.exp(jnp.minimum(-w, EXP_CLAMP)),
            chunk_decay=jnp.exp(b_last))

    def scores(d):
        out = []
        for hh in range(2):
            kh = jnp.where(lane_head == hh, d["k_hat"], 0.0).astype(BF16)
            k_cat = jnp.concatenate(
                [jnp.where(row_blk == j, kh, jnp.zeros_like(kh)) for j in range(nsub)], axis=1)
            out.append(lax.dot_general(d["q_cat"], k_cat, (((1,), (1,)), ((), ())),
                                       preferred_element_type=F32))
        return out

    def finish(bi, pair, d, sc):
        for hh in range(2):
            head = pair * 2 + hh
            cols = slice(head * GLA_DV, (head + 1) * GLA_DV)
            intra = jnp.where(causal, sc[hh], 0.0).astype(BF16)
            vh = v_ref[bi, :, cols]
            state_t = st_ref[bi * GLA_HEADS + head]
            o = jnp.dot(intra, vh, preferred_element_type=F32)
            o += lax.dot_general(d["q_state"], state_t.astype(BF16), (((1,), (1,)), ((), ())),
                                 preferred_element_type=F32)
            ks = jnp.where(lane_head == hh, d["k_state"], 0.0).astype(BF16)
            kv_t = lax.dot_general(vh, ks, (((0,), (0,)), ((), ())), preferred_element_type=F32)
            st_ref[bi * GLA_HEADS + head] = state_t * d["chunk_decay"] + kv_t
            y = _rmsnorm(o, hn_ref[...]) * r_ref[bi, :, cols].astype(F32)
            o_ref[bi, :, cols] = y.astype(BF16)

    chains = list(itertools.product(range(q_ref.shape[0]), range(GLA_HEADS // 2)))
    cums = [cumsum(*ch) for ch in chains]
    decayed, scored = {}, {}
    for t in range(len(chains) + 1):
        if t < len(chains):
            decayed[t] = decay(*chains[t], cums[t])
        if t >= 1:
            finish(*chains[t - 1], decayed.pop(t - 1), scored.pop(t - 1))
        if t < len(chains):
            scored[t] = scores(decayed[t])


def _gla(gq, gk, gv, gr, g, head_norm, *, nb=4):
    bsz, s, kw = gq.shape
    vw = gv.shape[-1]
    chunk = GLA_CHUNK

    def rows(w):
        return pl.BlockSpec((nb, chunk, w), lambda b, c: (b, c, 0))

    return pl.pallas_call(
        functools.partial(_gla_kernel, chunk=chunk, sub=GLA_SUB),
        out_shape=jax.ShapeDtypeStruct((bsz, s, vw), BF16),
        grid=(bsz // nb, s // chunk),
        in_specs=[rows(kw), rows(kw), rows(vw), rows(vw), rows(kw), _resident((1, GLA_DV))],
        out_specs=rows(vw),
        scratch_shapes=[pltpu.VMEM((nb * GLA_HEADS, GLA_DV, LANES), F32)],
        compiler_params=_params(("parallel", "arbitrary")),
        name="gla",
    )(gq, gk, gv, gr, g, head_norm.reshape(1, GLA_DV))


def _attn_kernel(q_ref, k_ref, v_ref, pos_ref, slope_ref, lam_ref, hn_ref, o_ref, kk_ref, qq_ref, vt_ref,
                 s0_ref, s1_ref, s2_ref, m_ref, acc_ref, *, tq, rc, lam_init):
    seq = k_ref.shape[1]
    nq = seq // tq
    c = slope_ref[0] * LOG2E
    c1 = c.astype(BF16).astype(F32)
    c2 = (c - c1).astype(BF16).astype(F32)
    c3 = c - c1 - c2

    kk_ref[:, :LANES] = k_ref[0]
    kk_ref[:, LANES:] = pos_ref[...]
    sub = lax.broadcasted_iota(jnp.int32, (LANES, 2 * tq), 0)
    cf = jnp.where((sub == 0) | (sub == 3), c1, jnp.where((sub == 1) | (sub == 4), c2, c3))
    slope_rows = jnp.where(sub < 6, cf, 0.0).astype(BF16)
    for i in range(qq_ref.shape[0]):
        qq_ref[i, LANES:, :] = slope_rows
    nchunk = tq // rc
    lam = lam_ref[...]
    lam = (jnp.exp(jnp.sum(lam[0:1] * lam[1:2], axis=-1, keepdims=True))
           - jnp.exp(jnp.sum(lam[2:3] * lam[3:4], axis=-1, keepdims=True)) + lam_init)

    def logits(qs, j, s_blk):
        start = pl.multiple_of(j * tq, tq)
        s = jnp.dot(kk_ref[pl.ds(start, tq), :], qq_ref[qs], preferred_element_type=F32)
        s_blk[...] = s
        return jnp.max(s.reshape(tq // 8, 8, 2 * tq), axis=0)

    def softmax(qs, qi, j, s_blk, mx):
        m_old = m_ref[qs]
        shift = c * jnp.asarray((qi - j) * tq, F32)

        def load(r):
            return s_blk[r * rc:(r + 1) * rc, :]

        m_new = jnp.maximum(m_old, jnp.max(mx, axis=0, keepdims=True) - shift)
        alpha = jnp.exp2(m_old - m_new)
        m_shift = m_new + shift
        probs = jnp.concatenate(
            [jnp.exp2(load(r) - m_shift).astype(BF16) for r in range(nchunk)], axis=0)
        m_ref[qs] = m_new
        return qs, j, alpha, probs

    def accumulate(qs, j, alpha, probs):
        acc_ref[qs] = alpha * acc_ref[qs] + jnp.dot(vt_ref[j], probs, preferred_element_type=F32)

    hq = tq // 2

    def late_half(x):
        return jnp.concatenate([x[..., hq:tq], x[..., tq + hq:]], axis=-1)

    def diag_logits(qs, j, s_blk):
        start = pl.multiple_of(j * tq, tq)
        s_blk[:hq, :] = jnp.dot(kk_ref[pl.ds(start, hq), :], qq_ref[qs],
                                preferred_element_type=F32)
        s_blk[hq:, :tq] = jnp.dot(kk_ref[pl.ds(start + hq, hq), :], late_half(qq_ref[qs]),
                                  preferred_element_type=F32)

    def diag_softmax_accumulate(qs, j, s_blk):
        m_old = m_ref[qs]
        key = lax.broadcasted_iota(jnp.int32, (hq, 2 * tq), 0)
        qry = lax.broadcasted_iota(jnp.int32, (hq, 2 * tq), 1) & (tq - 1)
        early = jnp.where(key <= qry, s_blk[:hq, :], -jnp.inf)
        key_l = lax.broadcasted_iota(jnp.int32, (hq, tq), 0)
        qry_l = lax.broadcasted_iota(jnp.int32, (hq, tq), 1) & (hq - 1)
        late = jnp.where(key_l <= qry_l, s_blk[hq:, :tq], -jnp.inf)
        mx_e = jnp.max(jnp.max(early.reshape(hq // 8, 8, 2 * tq), axis=0), axis=0, keepdims=True)
        mx_l = jnp.max(jnp.max(late.reshape(hq // 8, 8, tq), axis=0), axis=0, keepdims=True)
        mx = jnp.concatenate(
            [mx_e[:, :hq], jnp.maximum(mx_e[:, hq:tq], mx_l[:, :hq]),
             mx_e[:, tq:tq + hq], jnp.maximum(mx_e[:, tq + hq:], mx_l[:, hq:])], axis=1)
        m_new = jnp.maximum(m_old, mx)
        alpha = jnp.exp2(m_old - m_new)
        m_ref[qs] = m_new
        p_early = jnp.exp2(early - m_new).astype(BF16)
        p_late = jnp.exp2(late - late_half(m_new)).astype(BF16)
        vt = vt_ref[j]
        acc_ref[qs] = alpha * acc_ref[qs] + jnp.dot(vt[:, :hq], p_early,
                                                    preferred_element_type=F32)
        upd = jnp.dot(vt[:, hq:], p_late, preferred_element_type=F32)
        acc_ref[qs, :, hq:tq] += upd[:, :hq]
        acc_ref[qs, :, tq + hq:] += upd[:, hq:]

    bufs = (s0_ref, s1_ref, s2_ref)
    AHEAD = len(bufs) - 1
    assert (nq + 1) % len(bufs) == 0

    def build_queries(p, slot):
        for qs, blk in enumerate((p, nq - 1 - p)):
            q_t = q_ref[0, pl.ds(pl.multiple_of(blk * tq, tq), tq), :].astype(F32).T
            half = lax.broadcasted_iota(jnp.int32, (LANES, tq), 0)
            qq_ref[2 * slot + qs, :LANES, :tq] = jnp.where(half < DIFF_DH, q_t, 0.0).astype(BF16)
            qq_ref[2 * slot + qs, :LANES, tq:] = jnp.where(half >= DIFF_DH, q_t, 0.0).astype(BF16)

    def diag_prefetch(p, slot, t):
        qs, blk = ((1, nq - 1 - p), (0, p))[t]
        diag_logits(2 * slot + qs, blk, bufs[t])

    def pair(p, carry):
        slot = p % 2
        q_blk = (p, nq - 1 - p)
        for qs in range(2):
            m_ref[qs] = jnp.full((1, 2 * tq), -jnp.inf, F32)
            acc_ref[qs] = jnp.zeros((DIFF_DV + SUM_ROWS, 2 * tq), F32)
        items = [(1, q_blk[1], q_blk[1], True), (0, q_blk[0], q_blk[0], True)]
        for u in range(nq - 1):
            late = u < q_blk[1]
            items.append((jnp.where(late, 1, 0), jnp.where(late, q_blk[1], q_blk[0]),
                          jnp.where(late, u, u - q_blk[1]), False))
        col_max = {}
        p_next = jnp.minimum(p + 1, nq // 2 - 1)
        for t, (qs, qi, j, masked) in enumerate(items):
            ta = t + AHEAD
            if ta < len(items):
                col_max[ta] = logits(2 * slot + items[ta][0], items[ta][2], bufs[ta % len(bufs)])
            elif ta == len(items):
                build_queries(p_next, 1 - slot)
                diag_prefetch(p_next, 1 - slot, 0)
            else:
                diag_prefetch(p_next, 1 - slot, 1)
            if masked:
                diag_softmax_accumulate(qs, j, bufs[t % len(bufs)])
            else:
                accumulate(*softmax(qs, qi, j, bufs[t % len(bufs)], col_max.pop(t)))
        for qs in range(2):
            out = acc_ref[qs, :DIFF_DV, :] * (1.0 / acc_ref[qs, DIFF_DV:DIFF_DV + 1, :])
            y_t = out[:, :tq] - lam * out[:, tq:]
            y_t = y_t * lax.rsqrt(jnp.mean(y_t * y_t, axis=0, keepdims=True) + EPS)
            rows = pl.ds(pl.multiple_of(q_blk[qs] * tq, tq), tq)
            o_ref[0, rows, :] = (y_t.T * hn_ref[...] * (1.0 - lam_init)).astype(BF16)
        return carry

    build_queries(0, 0)
    diag_prefetch(0, 0, 0)
    diag_prefetch(0, 0, 1)
    for jb in range(nq):
        vt_ref[jb, :DIFF_DV, :] = v_ref[0, jb * tq:(jb + 1) * tq, :].astype(F32).T.astype(BF16)
        vt_ref[jb, DIFF_DV:, :] = jnp.ones((SUM_ROWS, tq), BF16)
    lax.fori_loop(0, nq // 2, pair, 0)


def _diff_attn(dq, dk, dv, lam_vecs, head_norm, lam_init, *, tq=512, rc=256):
    bsz, s, _ = dq.shape
    slopes = jnp.asarray(
        [2.0 ** (-ALIBI_MAX_BIAS * (i + 1) / DIFF_HEADS) for i in range(DIFF_HEADS)], F32
    ).reshape(DIFF_HEADS, 1, 1)
    pos = np.arange(s) % tq
    feat = np.zeros((s, LANES), np.float32)
    feat[:, 0:3] = (pos - pos % 16)[:, None]
    feat[:, 3:6] = (pos % 16)[:, None]
    return pl.pallas_call(
        functools.partial(_attn_kernel, tq=tq, rc=rc, lam_init=lam_init),
        out_shape=jax.ShapeDtypeStruct((bsz, s, DIFF_HEADS * DIFF_DV), BF16),
        grid=(bsz, DIFF_HEADS),
        in_specs=[
            pl.BlockSpec((1, s, LANES), lambda b, h: (b, 0, h)),
            pl.BlockSpec((1, s, LANES), lambda b, h: (b, 0, h)),
            pl.BlockSpec((1, s, DIFF_DV), lambda b, h: (b, 0, h)),
            _resident((s, LANES)),
            pl.BlockSpec((1, 1, 1), lambda b, h: (h, 0, 0)),
            _resident(lam_vecs.shape),
            _resident((1, DIFF_DV)),
        ],
        out_specs=pl.BlockSpec((1, s, DIFF_DV), lambda b, h: (b, 0, h)),
        scratch_shapes=[
            pltpu.VMEM((s, 2 * LANES), BF16),
            pltpu.VMEM((4, 2 * LANES, 2 * tq), BF16),
            pltpu.VMEM((s // tq, DIFF_DV + SUM_ROWS, tq), BF16),
            pltpu.VMEM((tq, 2 * tq), F32),
            pltpu.VMEM((tq, 2 * tq), F32),
            pltpu.VMEM((tq, 2 * tq), F32),
            pltpu.VMEM((2, 1, 2 * tq), F32),
            pltpu.VMEM((2, DIFF_DV + SUM_ROWS, 2 * tq), F32),
        ],
        compiler_params=_params(("parallel", "parallel")),
        name="diff_attn",
    )(dq, dk, dv, jnp.asarray(feat, BF16), slopes, lam_vecs, head_norm.reshape(1, DIFF_DV))


def _merge_kernel(za_ref, zb_ref, sg_ref, gt_ref, wa_ref, wb_ref, wo_ref, o_ref):
    d = o_ref.shape[-1]
    ya = jnp.dot(za_ref[0], wa_ref[...].astype(BF16), preferred_element_type=F32)
    yb = jnp.dot(zb_ref[0], wb_ref[...].astype(BF16), preferred_element_type=F32)
    mix = sg_ref[0, :, :d].astype(F32) * ya + sg_ref[0, :, d:].astype(F32) * yb
    m = jnp.dot(mix.astype(BF16), wo_ref[...].astype(BF16), preferred_element_type=F32)
    o_ref[0] = gt_ref[0] * m


def _merge(za, zb, sg, gt, w_a, w_b, w_o, *, tm=1024):
    bsz, s, _ = za.shape
    d = w_o.shape[-1]
    vec = pl.BlockSpec((1, 1, d), lambda b, i: (b, 0, 0))

    def rows(w):
        return pl.BlockSpec((1, tm, w), lambda b, i: (b, i, 0))

    return pl.pallas_call(
        _merge_kernel,
        out_shape=jax.ShapeDtypeStruct((bsz, s, d), F32),
        grid=(bsz, s // tm),
        in_specs=[rows(za.shape[-1]), rows(zb.shape[-1]), rows(2 * d), vec,
                  _resident(w_a.shape), _resident(w_b.shape), _resident(w_o.shape)],
        out_specs=rows(d),
        compiler_params=_params(("parallel", "parallel")),
        name="merge",
    )(za, zb, sg, gt, w_a, w_b, w_o)


def kernel(x, c, w_ada, b_ada, ffn1_norm, ffn1_w_in, ffn1_w_out, mix_norm, w_in, gla_alpha_w2, gla_alpha_b, gla_head_norm, diff_lq1, diff_lk1, diff_lq2, diff_lk2, diff_head_norm, w_branch_a, w_branch_b, w_out, ffn2_norm, ffn2_w_in, ffn2_w_out, final_norm):
    depth = w_ada.shape[0]
    bsz, _, d = x.shape
    h = x
    for l in range(depth):
        lam_init = 0.8 - 0.6 * math.exp(-0.3 * l)
        mod = _adaln(c, w_ada[l], b_ada[l])
        sh1, sc1, gt1, sh2, sc2, gt2, sh3, sc3, gt3 = [
            mod[:, i * d:(i + 1) * d].reshape(bsz, 1, d) for i in range(N_MOD)]
        last = l == depth - 1
        h = _ffn(h, None, sh1, sc1, gt1, ffn1_norm[l], ffn1_w_in[l], ffn1_w_out[l], final_norm,
                 final_norm=False)
        w_t = jnp.swapaxes(w_in, 1, 2)[l]
        gq, gk, gv, gr, g, dq, dk, dv, sg = _mixer_proj(
            h, sh2, sc2, mix_norm[l], w_t, gla_alpha_w2[l], gla_alpha_b[l])
        za = _gla(gq, gk, gv, gr, g, gla_head_norm[l])
        lam_vecs = jnp.stack([diff_lq1[l], diff_lk1[l], diff_lq2[l], diff_lk2[l]])
        zb = _diff_attn(dq, dk, dv, lam_vecs, diff_head_norm[l], lam_init)
        mixed = _merge(za, zb, sg, gt2, w_branch_a[l], w_branch_b[l], w_out[l])
        h = _ffn(h, mixed, sh3, sc3, gt3, ffn2_norm[l], ffn2_w_in[l], ffn2_w_out[l], final_norm,
                 final_norm=last)
    return h
```

```python
import functools
import itertools
import math

import jax
import jax.numpy as jnp
import numpy as np
from jax import lax
from jax.experimental import pallas as pl
from jax.experimental.pallas import tpu as pltpu

F32 = jnp.float32
BF16 = jnp.bfloat16

EPS = 1e-6
GLA_HEADS = 4
GLA_DK = 64
GLA_DV = 128
GLA_RANK = 16
GLA_TAU = 16.0
DIFF_HEADS = 4
DIFF_DH = 64
DIFF_DV = 128
ALIBI_MAX_BIAS = 8.0
N_MOD = 9

LOG2E = math.log2(math.e)
LANES = 128
MXU_COLS = 256
VMEM_LIMIT = 56 * 1024 * 1024

FFN_ROWS = 512
FFN_HIDDEN_CHUNK = MXU_COLS
PROJ_ROWS = 512
GATE_CHUNK = 512
MERGE_ROWS = 1024
GLA_CHUNK = 128
GLA_SUB = 16
GLA_BATCHES = 4
ATTN_BLOCK = 512
EXP_CLAMP = 80.0
SUM_ROWS = 16


def _params(sem):
    return pltpu.CompilerParams(dimension_semantics=sem, vmem_limit_bytes=VMEM_LIMIT)


def _resident(shape):
    nd = len(shape)
    return pl.BlockSpec(shape, lambda *_: (0,) * nd, pipeline_mode=pl.Buffered(1))


def _rmsnorm(x, g):
    return x * lax.rsqrt(jnp.mean(x * x, axis=-1, keepdims=True) + EPS) * g


def _adaln_kernel(c_ref, w_ref, b_ref, o_ref):
    c = c_ref[...]
    ca = (c * jax.nn.sigmoid(c)).astype(BF16)
    o_ref[...] = jnp.dot(ca, w_ref[...].astype(BF16), preferred_element_type=F32) + b_ref[...]


def _adaln(c, w_ada, b_ada):
    bsz, d = c.shape
    n = w_ada.shape[1]
    tn = n // 4
    return pl.pallas_call(
        _adaln_kernel,
        out_shape=jax.ShapeDtypeStruct((bsz, n), F32),
        grid=(n // tn,),
        in_specs=[
            pl.BlockSpec((bsz, d), lambda j: (0, 0)),
            pl.BlockSpec((d, tn), lambda j: (0, j)),
            pl.BlockSpec((1, tn), lambda j: (0, j)),
        ],
        out_specs=pl.BlockSpec((bsz, tn), lambda j: (0, j)),
        compiler_params=_params(("arbitrary",)),
        name="adaln",
    )(c, w_ada, b_ada.reshape(1, n))


def _ffn_kernel(x_ref, sh_ref, sc_ref, gt_ref, nw_ref, win_hbm, wout_hbm, fn_ref, o_ref,
                win_ref, wout_ref, sem_ref, *, d_ff, tf, final_norm):
    n_chunk = d_ff // tf
    acc_ref = o_ref.at[0]

    def weight_copies(i):
        cols = pl.ds(i * tf, tf)
        up = pl.ds(d_ff + i * tf, tf)
        return (pltpu.make_async_copy(win_hbm.at[:, cols], win_ref.at[:, cols], sem_ref.at[0, i]),
                pltpu.make_async_copy(win_hbm.at[:, up], win_ref.at[:, up], sem_ref.at[1, i]),
                pltpu.make_async_copy(wout_hbm.at[cols, :], wout_ref.at[cols, :], sem_ref.at[2, i]))

    def tile(wait_for_weights):
        x = x_ref[0]
        u = (_rmsnorm(x, nw_ref[...]) * (1.0 + sc_ref[0]) + sh_ref[0]).astype(BF16)
        for i in range(n_chunk):
            if wait_for_weights:
                for copy in weight_copies(i):
                    copy.wait()
            cols = slice(i * tf, (i + 1) * tf)
            up = slice(d_ff + i * tf, d_ff + (i + 1) * tf)
            hg = jnp.dot(u, win_ref[:, cols].astype(BF16), preferred_element_type=F32)
            hu = jnp.dot(u, win_ref[:, up].astype(BF16), preferred_element_type=F32)
            act = (hg * jax.nn.sigmoid(hg) * hu).astype(BF16)
            part = jnp.dot(act, wout_ref[cols, :].astype(BF16), preferred_element_type=F32)
            if i == 0:
                acc_ref[...] = part
            else:
                acc_ref[...] += part
        h = x + (0.5 * gt_ref[0]) * acc_ref[...]
        if final_norm:
            h = _rmsnorm(h, fn_ref[...])
        o_ref[0] = h

    first = (pl.program_id(0) == 0) & (pl.program_id(1) == 0)

    @pl.when(first)
    def _():
        for i in range(n_chunk):
            for copy in weight_copies(i):
                copy.start()
        tile(True)

    @pl.when(jnp.logical_not(first))
    def _():
        tile(False)


def _ffn(x, sh, sc, gt, nw, w_in, w_out, fn, *, final_norm):
    bsz, s, d = x.shape
    d_ff = w_out.shape[0]
    tm, tf = FFN_ROWS, FFN_HIDDEN_CHUNK
    vec = pl.BlockSpec((1, 1, d), lambda b, i: (b, 0, 0))
    row = pl.BlockSpec((1, tm, d), lambda b, i: (b, i, 0))
    hbm = pl.BlockSpec(memory_space=pl.ANY)
    return pl.pallas_call(
        functools.partial(_ffn_kernel, d_ff=d_ff, tf=tf, final_norm=final_norm),
        out_shape=jax.ShapeDtypeStruct((bsz, s, d), F32),
        grid=(bsz, s // tm),
        in_specs=[row, vec, vec, vec, _resident((1, d)), hbm, hbm, _resident((1, d))],
        out_specs=row,
        scratch_shapes=[pltpu.VMEM((d, 2 * d_ff), F32), pltpu.VMEM((d_ff, d), F32),
                        pltpu.SemaphoreType.DMA((3, d_ff // tf))],
        compiler_params=_params(("arbitrary", "arbitrary")),
        name="ffn_final" if final_norm else "ffn",
    )(x, sh, sc, gt, nw.reshape(1, d), w_in, w_out, fn.reshape(1, d))


def _proj_kernel(x_ref, sh_ref, sc_ref, nw_ref, wt_ref, w2_ref, b2_ref,
                 gq_ref, gk_ref, gv_ref, gr_ref, g_ref, dq_ref, dk_ref, dv_ref, sg_ref):
    kw = GLA_HEADS * GLA_DK
    vw = GLA_HEADS * GLA_DV
    qw = DIFF_HEADS * 2 * DIFF_DH
    dvw = DIFF_HEADS * DIFF_DV
    u = (_rmsnorm(x_ref[0], nw_ref[...]) * (1.0 + sc_ref[0]) + sh_ref[0]).astype(BF16)

    def seg(start, size):
        return lax.dot_general(u, wt_ref[start:start + size, :].astype(BF16),
                               (((1,), (1,)), ((), ())), preferred_element_type=F32)

    off = 0
    gq_ref[0] = (seg(off, kw) * (GLA_DK ** -0.5)).astype(BF16)
    off += kw
    gk_ref[0] = seg(off, kw).astype(BF16)
    off += kw
    gv_ref[0] = seg(off, vw).astype(BF16)
    off += vw
    r = seg(off, vw)
    gr_ref[0] = (r * jax.nn.sigmoid(r)).astype(BF16)
    off += vw
    a_low = seg(off, GLA_RANK).astype(BF16)
    z = jnp.dot(a_low, w2_ref[...].astype(BF16), preferred_element_type=F32) + b2_ref[...]
    g_ref[0] = (jnp.minimum(z, 0.0) - jnp.log1p(jnp.exp(-jnp.abs(z)))) * (1.0 / GLA_TAU)
    off += GLA_RANK
    dq_ref[0] = (seg(off, qw) * (DIFF_DH ** -0.5 * LOG2E)).astype(BF16)
    off += qw
    dk_ref[0] = seg(off, qw).astype(BF16)
    off += qw
    dv_ref[0] = seg(off, dvw).astype(BF16)
    off += dvw
    for j in range(sg_ref.shape[-1] // GATE_CHUNK):
        gate = jax.nn.sigmoid(seg(off + j * GATE_CHUNK, GATE_CHUNK))
        sg_ref[0, :, j * GATE_CHUNK:(j + 1) * GATE_CHUNK] = gate.astype(BF16)


def _mixer_proj(h, sh, sc, nw, w_t, w2, b2):
    bsz, s, d = h.shape
    tm = PROJ_ROWS
    kw = GLA_HEADS * GLA_DK
    vw = GLA_HEADS * GLA_DV
    qw = DIFF_HEADS * 2 * DIFF_DH
    dvw = DIFF_HEADS * DIFF_DV
    n_gate = 2 * d
    vec = pl.BlockSpec((1, 1, d), lambda b, i: (b, 0, 0))

    def rows(w):
        return pl.BlockSpec((1, tm, w), lambda b, i: (b, i, 0))

    widths = [(kw, BF16), (kw, BF16), (vw, BF16), (vw, BF16), (kw, F32),
              (qw, BF16), (qw, BF16), (dvw, BF16), (n_gate, BF16)]
    return pl.pallas_call(
        _proj_kernel,
        out_shape=[jax.ShapeDtypeStruct((bsz, s, w), dt) for w, dt in widths],
        grid=(bsz, s // tm),
        in_specs=[rows(d), vec, vec, _resident((1, d)), _resident(w_t.shape),
                  _resident(w2.shape), _resident((1, kw))],
        out_specs=[rows(w) for w, _ in widths],
        compiler_params=_params(("parallel", "parallel")),
        name="mixer_proj",
    )(h, sh, sc, nw.reshape(1, d), w_t, w2, b2.reshape(1, kw))


def _gla_kernel(q_ref, k_ref, v_ref, r_ref, g_ref, hn_ref, o_ref, st_ref, *, chunk, sub):
    @pl.when(pl.program_id(1) == 0)
    def _():
        st_ref[...] = jnp.zeros_like(st_ref)

    nsub = chunk // sub
    row = lax.broadcasted_iota(jnp.int32, (chunk, chunk), 0)
    col = lax.broadcasted_iota(jnp.int32, (chunk, chunk), 1)
    causal = col <= row
    cum_mat = jnp.where(causal, 1.0, 0.0).astype(BF16)
    row_blk = lax.broadcasted_iota(jnp.int32, (chunk, LANES), 0) // sub
    lane_head = lax.broadcasted_iota(jnp.int32, (chunk, LANES), 1) // GLA_DK

    def cumsum(bi, pair):
        g = g_ref[bi, :, pair * LANES:(pair + 1) * LANES]
        g1 = g.astype(BF16)
        e1 = g - g1.astype(F32)
        g2 = e1.astype(BF16)
        g3 = (e1 - g2.astype(F32)).astype(BF16)
        cs = jnp.dot(cum_mat, jnp.concatenate([g1, g2, g3], axis=1), preferred_element_type=F32)
        return cs[:, :LANES] + cs[:, LANES:2 * LANES] + cs[:, 2 * LANES:]

    def decay(bi, pair, b):
        lanes = slice(pair * LANES, (pair + 1) * LANES)
        b_last = b[chunk - 1:chunk]
        q = q_ref[bi, :, lanes].astype(F32)
        k = k_ref[bi, :, lanes].astype(F32)
        q_cat, w = [], []
        for j in range(nsub):
            ref_b = b[j * sub - 1:j * sub] if j else jnp.zeros_like(b_last)
            q_cat.append((q * jnp.exp(jnp.minimum(b - ref_b, 0.0))).astype(BF16))
            w.append(b[j * sub:(j + 1) * sub] - ref_b)
        w = jnp.concatenate(w, axis=0)
        return dict(
            q_cat=jnp.concatenate(q_cat, axis=1),
            q_state=(q * jnp.exp(b)).astype(BF16),
            k_state=k * jnp.exp(b_last - b),
            k_hat=k * jnp.exp(jnp.minimum(-w, EXP_CLAMP)),
            chunk_decay=jnp.exp(b_last))

    def scores(d):
        out = []
        for hh in range(2):
            kh = jnp.where(lane_head == hh, d["k_hat"], 0.0).astype(BF16)
            k_cat = jnp.concatenate(
                [jnp.where(row_blk == j, kh, jnp.zeros_like(kh)) for j in range(nsub)], axis=1)
            out.append(lax.dot_general(d["q_cat"], k_cat, (((1,), (1,)), ((), ())),
                                       preferred_element_type=F32))
        return out

    def finish(bi, pair, d, sc):
        for hh in range(2):
            head = pair * 2 + hh
            cols = slice(head * GLA_DV, (head + 1) * GLA_DV)
            intra = jnp.where(causal, sc[hh], 0.0).astype(BF16)
            vh = v_ref[bi, :, cols]
            state_t = st_ref[bi * GLA_HEADS + head]
            o = jnp.dot(intra, vh, preferred_element_type=F32)
            o += lax.dot_general(d["q_state"], state_t.astype(BF16), (((1,), (1,)), ((), ())),
                                 preferred_element_type=F32)
            ks = jnp.where(lane_head == hh, d["k_state"], 0.0).astype(BF16)
            kv_t = lax.dot_general(vh, ks, (((0,), (0,)), ((), ())), preferred_element_type=F32)
            st_ref[bi * GLA_HEADS + head] = state_t * d["chunk_decay"] + kv_t
            y = _rmsnorm(o, hn_ref[...]) * r_ref[bi, :, cols].astype(F32)
            o_ref[bi, :, cols] = y.astype(BF16)

    chains = list(itertools.product(range(q_ref.shape[0]), range(GLA_HEADS // 2)))
    cums = [cumsum(*ch) for ch in chains]
    decayed, scored = {}, {}
    for t in range(len(chains) + 1):
        if t < len(chains):
            decayed[t] = decay(*chains[t], cums[t])
        if t >= 1:
            finish(*chains[t - 1], decayed.pop(t - 1), scored.pop(t - 1))
        if t < len(chains):
            scored[t] = scores(decayed[t])


def _gla(gq, gk, gv, gr, g, head_norm):
    bsz, s, kw = gq.shape
    vw = gv.shape[-1]
    nb, chunk = GLA_BATCHES, GLA_CHUNK

    def rows(w):
        return pl.BlockSpec((nb, chunk, w), lambda b, c: (b, c, 0))

    return pl.pallas_call(
        functools.partial(_gla_kernel, chunk=chunk, sub=GLA_SUB),
        out_shape=jax.ShapeDtypeStruct((bsz, s, vw), BF16),
        grid=(bsz // nb, s // chunk),
        in_specs=[rows(kw), rows(kw), rows(vw), rows(vw), rows(kw), _resident((1, GLA_DV))],
        out_specs=rows(vw),
        scratch_shapes=[pltpu.VMEM((nb * GLA_HEADS, GLA_DV, LANES), F32)],
        compiler_params=_params(("parallel", "arbitrary")),
        name="gla",
    )(gq, gk, gv, gr, g, head_norm.reshape(1, GLA_DV))


def _attn_kernel(q_ref, k_ref, v_ref, pos_ref, slope_ref, lam_ref, hn_ref, o_ref,
                 kk_ref, qq_ref, vt_ref, s0_ref, s1_ref, s2_ref, m_ref, acc_ref, *, tq, lam_init):
    seq = k_ref.shape[1]
    nq = seq // tq
    hq = tq // 2
    c = slope_ref[0] * LOG2E
    c1 = c.astype(BF16).astype(F32)
    c2 = (c - c1).astype(BF16).astype(F32)
    c3 = c - c1 - c2

    kk_ref[:, :LANES] = k_ref[0]
    kk_ref[:, LANES:] = pos_ref[...]
    sub = lax.broadcasted_iota(jnp.int32, (LANES, 2 * tq), 0)
    cf = jnp.where((sub == 0) | (sub == 3), c1, jnp.where((sub == 1) | (sub == 4), c2, c3))
    slope_rows = jnp.where(sub < 6, cf, 0.0).astype(BF16)
    for i in range(qq_ref.shape[0]):
        qq_ref[i, LANES:, :] = slope_rows
    for jb in range(nq):
        vt_ref[jb, :DIFF_DV, :] = v_ref[0, jb * tq:(jb + 1) * tq, :].astype(F32).T.astype(BF16)
        vt_ref[jb, DIFF_DV:, :] = jnp.ones((SUM_ROWS, tq), BF16)
    lam = lam_ref[...]
    lam = (jnp.exp(jnp.sum(lam[0:1] * lam[1:2], axis=-1, keepdims=True))
           - jnp.exp(jnp.sum(lam[2:3] * lam[3:4], axis=-1, keepdims=True)) + lam_init)

    def logits(qq_slot, j, s_blk):
        start = pl.multiple_of(j * tq, tq)
        s = jnp.dot(kk_ref[pl.ds(start, tq), :], qq_ref[qq_slot], preferred_element_type=F32)
        s_blk[...] = s
        return jnp.max(s.reshape(tq // 8, 8, 2 * tq), axis=0)

    def softmax_accumulate(qs, qi, j, s_blk, mx):
        m_old = m_ref[qs]
        shift = c * jnp.asarray((qi - j) * tq, F32)
        m_new = jnp.maximum(m_old, jnp.max(mx, axis=0, keepdims=True) - shift)
        alpha = jnp.exp2(m_old - m_new)
        m_ref[qs] = m_new
        m_shift = m_new + shift
        probs = jnp.concatenate([jnp.exp2(s_blk[:hq, :] - m_shift).astype(BF16),
                                 jnp.exp2(s_blk[hq:, :] - m_shift).astype(BF16)], axis=0)
        acc_ref[qs] = alpha * acc_ref[qs] + jnp.dot(vt_ref[j], probs, preferred_element_type=F32)

    def late_half(x):
        return jnp.concatenate([x[..., hq:tq], x[..., tq + hq:]], axis=-1)

    def diag_logits(qq_slot, j, s_blk):
        start = pl.multiple_of(j * tq, tq)
        s_blk[:hq, :] = jnp.dot(kk_ref[pl.ds(start, hq), :], qq_ref[qq_slot],
                                preferred_element_type=F32)
        s_blk[hq:, :tq] = jnp.dot(kk_ref[pl.ds(start + hq, hq), :], late_half(qq_ref[qq_slot]),
                                  preferred_element_type=F32)

    def diag_softmax_accumulate(qs, j, s_blk):
        m_old = m_ref[qs]
        key = lax.broadcasted_iota(jnp.int32, (hq, 2 * tq), 0)
        qry = lax.broadcasted_iota(jnp.int32, (hq, 2 * tq), 1) & (tq - 1)
        early = jnp.where(key <= qry, s_blk[:hq, :], -jnp.inf)
        key_l = lax.broadcasted_iota(jnp.int32, (hq, tq), 0)
        qry_l = lax.broadcasted_iota(jnp.int32, (hq, tq), 1) & (hq - 1)
        late = jnp.where(key_l <= qry_l, s_blk[hq:, :tq], -jnp.inf)
        mx_e = jnp.max(jnp.max(early.reshape(hq // 8, 8, 2 * tq), axis=0), axis=0, keepdims=True)
        mx_l = jnp.max(jnp.max(late.reshape(hq // 8, 8, tq), axis=0), axis=0, keepdims=True)
        mx = jnp.concatenate(
            [mx_e[:, :hq], jnp.maximum(mx_e[:, hq:tq], mx_l[:, :hq]),
             mx_e[:, tq:tq + hq], jnp.maximum(mx_e[:, tq + hq:], mx_l[:, hq:])], axis=1)
        m_new = jnp.maximum(m_old, mx)
        alpha = jnp.exp2(m_old - m_new)
        m_ref[qs] = m_new
        p_early = jnp.exp2(early - m_new).astype(BF16)
        p_late = jnp.exp2(late - late_half(m_new)).astype(BF16)
        vt = vt_ref[j]
        acc_ref[qs] = alpha * acc_ref[qs] + jnp.dot(vt[:, :hq], p_early,
                                                    preferred_element_type=F32)
        upd = jnp.dot(vt[:, hq:], p_late, preferred_element_type=F32)
        acc_ref[qs, :, hq:tq] += upd[:, :hq]
        acc_ref[qs, :, tq + hq:] += upd[:, hq:]

    bufs = (s0_ref, s1_ref, s2_ref)
    ahead = len(bufs) - 1
    assert (nq + 1) % len(bufs) == 0

    def build_queries(p, slot):
        for qs, blk in enumerate((p, nq - 1 - p)):
            q_t = q_ref[0, pl.ds(pl.multiple_of(blk * tq, tq), tq), :].astype(F32).T
            half = lax.broadcasted_iota(jnp.int32, (LANES, tq), 0)
            qq_ref[2 * slot + qs, :LANES, :tq] = jnp.where(half < DIFF_DH, q_t, 0.0).astype(BF16)
            qq_ref[2 * slot + qs, :LANES, tq:] = jnp.where(half >= DIFF_DH, q_t, 0.0).astype(BF16)

    def diag_prefetch(p, slot, t):
        qs, blk = ((1, nq - 1 - p), (0, p))[t]
        diag_logits(2 * slot + qs, blk, bufs[t])

    def pair(p, carry):
        slot = p % 2
        q_blk = (p, nq - 1 - p)
        for qs in range(2):
            m_ref[qs] = jnp.full((1, 2 * tq), -jnp.inf, F32)
            acc_ref[qs] = jnp.zeros((DIFF_DV + SUM_ROWS, 2 * tq), F32)
        items = [(1, q_blk[1], q_blk[1], True), (0, q_blk[0], q_blk[0], True)]
        for u in range(nq - 1):
            late = u < q_blk[1]
            items.append((jnp.where(late, 1, 0), jnp.where(late, q_blk[1], q_blk[0]),
                          jnp.where(late, u, u - q_blk[1]), False))
        col_max = {}
        p_next = jnp.minimum(p + 1, nq // 2 - 1)
        for t, (qs, qi, j, masked) in enumerate(items):
            ta = t + ahead
            if ta < len(items):
                col_max[ta] = logits(2 * slot + items[ta][0], items[ta][2], bufs[ta % len(bufs)])
            elif ta == len(items):
                build_queries(p_next, 1 - slot)
                diag_prefetch(p_next, 1 - slot, 0)
            else:
                diag_prefetch(p_next, 1 - slot, 1)
            if masked:
                diag_softmax_accumulate(qs, j, bufs[t % len(bufs)])
            else:
                softmax_accumulate(qs, qi, j, bufs[t % len(bufs)], col_max.pop(t))
        for qs in range(2):
            out = acc_ref[qs, :DIFF_DV, :] * (1.0 / acc_ref[qs, DIFF_DV:DIFF_DV + 1, :])
            y_t = out[:, :tq] - lam * out[:, tq:]
            y_t = y_t * lax.rsqrt(jnp.mean(y_t * y_t, axis=0, keepdims=True) + EPS)
            rows = pl.ds(pl.multiple_of(q_blk[qs] * tq, tq), tq)
            o_ref[0, rows, :] = (y_t.T * hn_ref[...] * (1.0 - lam_init)).astype(BF16)
        return carry

    build_queries(0, 0)
    diag_prefetch(0, 0, 0)
    diag_prefetch(0, 0, 1)
    lax.fori_loop(0, nq // 2, pair, 0)


def _diff_attn(dq, dk, dv, lam_vecs, head_norm, lam_init):
    bsz, s, _ = dq.shape
    tq = ATTN_BLOCK
    slopes = jnp.asarray(
        [2.0 ** (-ALIBI_MAX_BIAS * (i + 1) / DIFF_HEADS) for i in range(DIFF_HEADS)], F32
    ).reshape(DIFF_HEADS, 1, 1)
    pos = np.arange(s) % tq
    feat = np.zeros((s, LANES), np.float32)
    feat[:, 0:3] = (pos - pos % 16)[:, None]
    feat[:, 3:6] = (pos % 16)[:, None]
    return pl.pallas_call(
        functools.partial(_attn_kernel, tq=tq, lam_init=lam_init),
        out_shape=jax.ShapeDtypeStruct((bsz, s, DIFF_HEADS * DIFF_DV), BF16),
        grid=(bsz, DIFF_HEADS),
        in_specs=[
            pl.BlockSpec((1, s, LANES), lambda b, h: (b, 0, h)),
            pl.BlockSpec((1, s, LANES), lambda b, h: (b, 0, h)),
            pl.BlockSpec((1, s, DIFF_DV), lambda b, h: (b, 0, h)),
            _resident((s, LANES)),
            pl.BlockSpec((1, 1, 1), lambda b, h: (h, 0, 0)),
            _resident(lam_vecs.shape),
            _resident((1, DIFF_DV)),
        ],
        out_specs=pl.BlockSpec((1, s, DIFF_DV), lambda b, h: (b, 0, h)),
        scratch_shapes=[
            pltpu.VMEM((s, 2 * LANES), BF16),
            pltpu.VMEM((4, 2 * LANES, 2 * tq), BF16),
            pltpu.VMEM((s // tq, DIFF_DV + SUM_ROWS, tq), BF16),
            pltpu.VMEM((tq, 2 * tq), F32),
            pltpu.VMEM((tq, 2 * tq), F32),
            pltpu.VMEM((tq, 2 * tq), F32),
            pltpu.VMEM((2, 1, 2 * tq), F32),
            pltpu.VMEM((2, DIFF_DV + SUM_ROWS, 2 * tq), F32),
        ],
        compiler_params=_params(("parallel", "parallel")),
        name="diff_attn",
    )(dq, dk, dv, jnp.asarray(feat, BF16), slopes, lam_vecs, head_norm.reshape(1, DIFF_DV))


def _merge_kernel(h_ref, za_ref, zb_ref, sg_ref, gt_ref, wa_ref, wb_ref, wo_ref, o_ref):
    d = h_ref.shape[-1]
    ya = jnp.dot(za_ref[0], wa_ref[...].astype(BF16), preferred_element_type=F32)
    yb = jnp.dot(zb_ref[0], wb_ref[...].astype(BF16), preferred_element_type=F32)
    mix = sg_ref[0, :, :d].astype(F32) * ya + sg_ref[0, :, d:].astype(F32) * yb
    m = jnp.dot(mix.astype(BF16), wo_ref[...].astype(BF16), preferred_element_type=F32)
    o_ref[0] = h_ref[0] + gt_ref[0] * m


def _merge(h, za, zb, sg, gt, w_a, w_b, w_o):
    bsz, s, d = h.shape
    tm = MERGE_ROWS
    vec = pl.BlockSpec((1, 1, d), lambda b, i: (b, 0, 0))

    def rows(w):
        return pl.BlockSpec((1, tm, w), lambda b, i: (b, i, 0))

    return pl.pallas_call(
        _merge_kernel,
        out_shape=jax.ShapeDtypeStruct((bsz, s, d), F32),
        grid=(bsz, s // tm),
        in_specs=[rows(d), rows(za.shape[-1]), rows(zb.shape[-1]), rows(2 * d), vec,
                  _resident(w_a.shape), _resident(w_b.shape), _resident(w_o.shape)],
        out_specs=rows(d),
        compiler_params=_params(("parallel", "parallel")),
        name="merge",
    )(h, za, zb, sg, gt, w_a, w_b, w_o)


def kernel(x, c, w_ada, b_ada, ffn1_norm, ffn1_w_in, ffn1_w_out, mix_norm, w_in, gla_alpha_w2, gla_alpha_b, gla_head_norm, diff_lq1, diff_lk1, diff_lq2, diff_lk2, diff_head_norm, w_branch_a, w_branch_b, w_out, ffn2_norm, ffn2_w_in, ffn2_w_out, final_norm):
    depth = w_ada.shape[0]
    bsz, s, d = x.shape
    assert bsz % GLA_BATCHES == 0 and s % MERGE_ROWS == 0 and (s // ATTN_BLOCK) % 2 == 0
    h = x
    for l in range(depth):
        lam_init = 0.8 - 0.6 * math.exp(-0.3 * l)
        mod = _adaln(c, w_ada[l], b_ada[l])
        sh1, sc1, gt1, sh2, sc2, gt2, sh3, sc3, gt3 = [
            mod[:, i * d:(i + 1) * d].reshape(bsz, 1, d) for i in range(N_MOD)]
        last = l == depth - 1
        h = _ffn(h, sh1, sc1, gt1, ffn1_norm[l], ffn1_w_in[l], ffn1_w_out[l], final_norm,
                 final_norm=False)
        w_t = jnp.swapaxes(w_in, 1, 2)[l]
        gq, gk, gv, gr, g, dq, dk, dv, sg = _mixer_proj(
            h, sh2, sc2, mix_norm[l], w_t, gla_alpha_w2[l], gla_alpha_b[l])
        za = _gla(gq, gk, gv, gr, g, gla_head_norm[l])
        lam_vecs = jnp.stack([diff_lq1[l], diff_lk1[l], diff_lq2[l], diff_lk2[l]])
        zb = _diff_attn(dq, dk, dv, lam_vecs, diff_head_norm[l], lam_init)
        h = _merge(h, za, zb, sg, gt2, w_branch_a[l], w_branch_b[l], w_out[l])
        h = _ffn(h, sh3, sc3, gt3, ffn2_norm[l], ffn2_w_in[l], ffn2_w_out[l], final_norm,
                 final_norm=last)
    return h
```

```python
import functools
import itertools
import math

import jax
import jax.numpy as jnp
import numpy as np
from jax import lax
from jax.experimental import pallas as pl
from jax.experimental.pallas import tpu as pltpu

F32 = jnp.float32
BF16 = jnp.bfloat16

EPS = 1e-6
GLA_HEADS = 4
GLA_DK = 64
GLA_DV = 128
GLA_RANK = 16
GLA_TAU = 16.0
DIFF_HEADS = 4
DIFF_DH = 64
DIFF_DV = 128
ALIBI_MAX_BIAS = 8.0
N_MOD = 9

LOG2E = math.log2(math.e)
LANES = 128
MXU_COLS = 256
VMEM_LIMIT = 56 * 1024 * 1024

FFN_ROWS = 512
FFN_HIDDEN_CHUNK = MXU_COLS
NEXT_PIECES = 8
PROJ_ROWS = 512
GATE_CHUNK = 512
MERGE_ROWS = 1024
GLA_CHUNK = 128
GLA_SUB = 16
GLA_BATCHES = 4
ATTN_BLOCK = 512
EXP_CLAMP = 80.0
SUM_ROWS = 16


def _params(sem):
    return pltpu.CompilerParams(dimension_semantics=sem, vmem_limit_bytes=VMEM_LIMIT)


def _resident(shape):
    nd = len(shape)
    return pl.BlockSpec(shape, lambda *_: (0,) * nd, pipeline_mode=pl.Buffered(1))


def _rmsnorm(x, g):
    return x * lax.rsqrt(jnp.mean(x * x, axis=-1, keepdims=True) + EPS) * g


def _next_input_pieces(xn_ref, un_ref, norm_fn, step):
    piece = un_ref.shape[0] // NEXT_PIECES
    zero = jnp.minimum(step, 0).astype(F32)

    def prepare(i):
        rows = slice(i * piece, (i + 1) * piece)
        un = norm_fn(xn_ref[0, rows, :])
        un_ref[rows, :] = un
        return jnp.sum(un.astype(F32), keepdims=True) * zero

    return prepare


def _next_rows(bsz, nt, tm, d):
    def index(b, i):
        nxt = jnp.minimum(b * nt + i + 1, bsz * nt - 1)
        return nxt // nt, nxt % nt, 0
    return pl.BlockSpec((1, tm, d), index)


def _adaln_kernel(c_ref, w_ref, b_ref, o_ref):
    c = c_ref[...]
    ca = (c * jax.nn.sigmoid(c)).astype(BF16)
    o_ref[...] = jnp.dot(ca, w_ref[...].astype(BF16), preferred_element_type=F32) + b_ref[...]


def _adaln(c, w_ada, b_ada):
    bsz, d = c.shape
    n = w_ada.shape[1]
    tn = n // 4
    return pl.pallas_call(
        _adaln_kernel,
        out_shape=jax.ShapeDtypeStruct((bsz, n), F32),
        grid=(n // tn,),
        in_specs=[
            pl.BlockSpec((bsz, d), lambda j: (0, 0)),
            pl.BlockSpec((d, tn), lambda j: (0, j)),
            pl.BlockSpec((1, tn), lambda j: (0, j)),
        ],
        out_specs=pl.BlockSpec((bsz, tn), lambda j: (0, j)),
        compiler_params=_params(("arbitrary",)),
        name="adaln",
    )(c, w_ada, b_ada.reshape(1, n))


def _ffn_kernel(x_ref, xn_ref, sh_ref, sc_ref, gt_ref, nw_ref, win_hbm, wout_hbm, fn_ref, o_ref,
                u_ref, un_ref, win_ref, wout_ref, sem_ref, *, d_ff, tf, final_norm):
    n_chunk = d_ff // tf
    acc_ref = o_ref.at[0]
    step = pl.program_id(0) * pl.num_programs(1) + pl.program_id(1)
    last_step = pl.num_programs(0) * pl.num_programs(1) - 1

    def normed(x, batch):
        return (_rmsnorm(x, nw_ref[...]) * (1.0 + sc_ref[batch]) + sh_ref[batch]).astype(BF16)

    def weight_copies(i):
        cols = pl.ds(i * tf, tf)
        up = pl.ds(d_ff + i * tf, tf)
        return (pltpu.make_async_copy(win_hbm.at[:, cols], win_ref.at[:, cols], sem_ref.at[0, i]),
                pltpu.make_async_copy(win_hbm.at[:, up], win_ref.at[:, up], sem_ref.at[1, i]),
                pltpu.make_async_copy(wout_hbm.at[cols, :], wout_ref.at[cols, :], sem_ref.at[2, i]))

    def tile(is_first):
        if is_first:
            un_ref[...] = normed(x_ref[0], pl.program_id(0))
        u_ref[...] = un_ref[...]
        u = u_ref[...]
        next_batch = jnp.minimum(step + 1, last_step) // pl.num_programs(1)
        prepare = _next_input_pieces(xn_ref, un_ref, lambda x: normed(x, next_batch), step)
        anchor = None
        for i in range(n_chunk):
            if is_first:
                for copy in weight_copies(i):
                    copy.wait()
            cols = slice(i * tf, (i + 1) * tf)
            up = slice(d_ff + i * tf, d_ff + (i + 1) * tf)
            hg = jnp.dot(u, win_ref[:, cols].astype(BF16), preferred_element_type=F32)
            if anchor is not None:
                hg = hg + anchor
            hu = jnp.dot(u, win_ref[:, up].astype(BF16), preferred_element_type=F32)
            act = (hg * jax.nn.sigmoid(hg) * hu).astype(BF16)
            part = jnp.dot(act, wout_ref[cols, :].astype(BF16), preferred_element_type=F32)
            if i == 0:
                acc_ref[...] = part
            else:
                acc_ref[...] += part
            anchor = prepare(i) if i < NEXT_PIECES else None
        h = x_ref[0] + (0.5 * gt_ref[0]) * acc_ref[...]
        if final_norm:
            h = _rmsnorm(h, fn_ref[...])
        o_ref[0] = h

    first = (pl.program_id(0) == 0) & (pl.program_id(1) == 0)

    @pl.when(first)
    def _():
        for i in range(n_chunk):
            for copy in weight_copies(i):
                copy.start()
        tile(True)

    @pl.when(jnp.logical_not(first))
    def _():
        tile(False)


def _ffn(x, sh, sc, gt, nw, w_in, w_out, fn, *, final_norm):
    bsz, s, d = x.shape
    d_ff = w_out.shape[0]
    tm, tf = FFN_ROWS, FFN_HIDDEN_CHUNK
    nt = s // tm
    vec = pl.BlockSpec((1, 1, d), lambda b, i: (b, 0, 0))
    row = pl.BlockSpec((1, tm, d), lambda b, i: (b, i, 0))
    hbm = pl.BlockSpec(memory_space=pl.ANY)
    return pl.pallas_call(
        functools.partial(_ffn_kernel, d_ff=d_ff, tf=tf, final_norm=final_norm),
        out_shape=jax.ShapeDtypeStruct((bsz, s, d), F32),
        grid=(bsz, nt),
        in_specs=[row, _next_rows(bsz, nt, tm, d), _resident(sh.shape), _resident(sc.shape), vec,
                  _resident((1, d)), hbm, hbm, _resident((1, d))],
        out_specs=row,
        scratch_shapes=[pltpu.VMEM((tm, d), BF16), pltpu.VMEM((tm, d), BF16),
                        pltpu.VMEM((d, 2 * d_ff), F32), pltpu.VMEM((d_ff, d), F32),
                        pltpu.SemaphoreType.DMA((3, d_ff // tf))],
        compiler_params=_params(("arbitrary", "arbitrary")),
        name="ffn_final" if final_norm else "ffn",
    )(x, x, sh, sc, gt, nw.reshape(1, d), w_in, w_out, fn.reshape(1, d))


def _proj_kernel(x_ref, xn_ref, sh_ref, sc_ref, nw_ref, wt_ref, w2_ref, b2_ref,
                 gq_ref, gk_ref, gv_ref, gr_ref, g_ref, dq_ref, dk_ref, dv_ref, sg_ref,
                 u_ref, un_ref):
    kw = GLA_HEADS * GLA_DK
    vw = GLA_HEADS * GLA_DV
    qw = DIFF_HEADS * 2 * DIFF_DH
    dvw = DIFF_HEADS * DIFF_DV
    step = pl.program_id(0) * pl.num_programs(1) + pl.program_id(1)
    last_step = pl.num_programs(0) * pl.num_programs(1) - 1

    def normed(x, batch):
        return (_rmsnorm(x, nw_ref[...]) * (1.0 + sc_ref[batch]) + sh_ref[batch]).astype(BF16)

    @pl.when(step == 0)
    def _():
        un_ref[...] = normed(x_ref[0], 0)

    u_ref[...] = un_ref[...]
    u = u_ref[...]
    next_batch = jnp.minimum(step + 1, last_step) // pl.num_programs(1)
    prepare = _next_input_pieces(xn_ref, un_ref, lambda x: normed(x, next_batch), step)
    pending = {"count": 0, "anchor": None}

    def seg(start, size):
        out = lax.dot_general(u, wt_ref[start:start + size, :].astype(BF16),
                              (((1,), (1,)), ((), ())), preferred_element_type=F32)
        if pending["anchor"] is not None:
            out = out + pending["anchor"]
        pending["anchor"] = prepare(pending["count"]) if pending["count"] < NEXT_PIECES else None
        pending["count"] += 1
        return out

    off = 0
    gq_ref[0] = (seg(off, kw) * (GLA_DK ** -0.5)).astype(BF16)
    off += kw
    gk_ref[0] = seg(off, kw).astype(BF16)
    off += kw
    gv_ref[0] = seg(off, vw).astype(BF16)
    off += vw
    r = seg(off, vw)
    gr_ref[0] = (r * jax.nn.sigmoid(r)).astype(BF16)
    off += vw
    a_low = seg(off, GLA_RANK).astype(BF16)
    z = jnp.dot(a_low, w2_ref[...].astype(BF16), preferred_element_type=F32) + b2_ref[...]
    g_ref[0] = (jnp.minimum(z, 0.0) - jnp.log1p(jnp.exp(-jnp.abs(z)))) * (1.0 / GLA_TAU)
    off += GLA_RANK
    dq_ref[0] = (seg(off, qw) * (DIFF_DH ** -0.5 * LOG2E)).astype(BF16)
    off += qw
    dk_ref[0] = seg(off, qw).astype(BF16)
    off += qw
    dv_ref[0] = seg(off, dvw).astype(BF16)
    off += dvw
    for j in range(sg_ref.shape[-1] // GATE_CHUNK):
        gate = jax.nn.sigmoid(seg(off + j * GATE_CHUNK, GATE_CHUNK))
        sg_ref[0, :, j * GATE_CHUNK:(j + 1) * GATE_CHUNK] = gate.astype(BF16)


def _mixer_proj(h, sh, sc, nw, w_t, w2, b2):
    bsz, s, d = h.shape
    tm = PROJ_ROWS
    kw = GLA_HEADS * GLA_DK
    vw = GLA_HEADS * GLA_DV
    qw = DIFF_HEADS * 2 * DIFF_DH
    dvw = DIFF_HEADS * DIFF_DV
    n_gate = 2 * d
    vec = pl.BlockSpec((1, 1, d), lambda b, i: (b, 0, 0))

    def rows(w):
        return pl.BlockSpec((1, tm, w), lambda b, i: (b, i, 0))

    widths = [(kw, BF16), (kw, BF16), (vw, BF16), (vw, BF16), (kw, F32),
              (qw, BF16), (qw, BF16), (dvw, BF16), (n_gate, BF16)]
    return pl.pallas_call(
        _proj_kernel,
        out_shape=[jax.ShapeDtypeStruct((bsz, s, w), dt) for w, dt in widths],
        grid=(bsz, s // tm),
        in_specs=[rows(d), _next_rows(bsz, s // tm, tm, d), _resident(sh.shape),
                  _resident(sc.shape), _resident((1, d)), _resident(w_t.shape),
                  _resident(w2.shape), _resident((1, kw))],
        out_specs=[rows(w) for w, _ in widths],
        scratch_shapes=[pltpu.VMEM((tm, d), BF16), pltpu.VMEM((tm, d), BF16)],
        compiler_params=_params(("arbitrary", "arbitrary")),
        name="mixer_proj",
    )(h, h, sh, sc, nw.reshape(1, d), w_t, w2, b2.reshape(1, kw))


def _gla_kernel(q_ref, k_ref, v_ref, r_ref, g_ref, hn_ref, o_ref, st_ref, *, chunk, sub):
    @pl.when(pl.program_id(1) == 0)
    def _():
        st_ref[...] = jnp.zeros_like(st_ref)

    nsub = chunk // sub
    row = lax.broadcasted_iota(jnp.int32, (chunk, chunk), 0)
    col = lax.broadcasted_iota(jnp.int32, (chunk, chunk), 1)
    causal = col <= row
    cum_mat = jnp.where(causal, 1.0, 0.0).astype(BF16)
    row_blk = lax.broadcasted_iota(jnp.int32, (chunk, LANES), 0) // sub
    lane_head = lax.broadcasted_iota(jnp.int32, (chunk, LANES), 1) // GLA_DK

    def cumsum(bi, pair):
        g = g_ref[bi, :, pair * LANES:(pair + 1) * LANES]
        g1 = g.astype(BF16)
        e1 = g - g1.astype(F32)
        g2 = e1.astype(BF16)
        g3 = (e1 - g2.astype(F32)).astype(BF16)
        cs = jnp.dot(cum_mat, jnp.concatenate([g1, g2, g3], axis=1), preferred_element_type=F32)
        return cs[:, :LANES] + cs[:, LANES:2 * LANES] + cs[:, 2 * LANES:]

    def decay(bi, pair, b):
        lanes = slice(pair * LANES, (pair + 1) * LANES)
        b_last = b[chunk - 1:chunk]
        q = q_ref[bi, :, lanes].astype(F32)
        k = k_ref[bi, :, lanes].astype(F32)
        q_cat, w = [], []
        for j in range(nsub):
            ref_b = b[j * sub - 1:j * sub] if j else jnp.zeros_like(b_last)
            q_cat.append((q * jnp.exp(jnp.minimum(b - ref_b, 0.0))).astype(BF16))
            w.append(b[j * sub:(j + 1) * sub] - ref_b)
        w = jnp.concatenate(w, axis=0)
        return dict(
            q_cat=jnp.concatenate(q_cat, axis=1),
            q_state=(q * jnp.exp(b)).astype(BF16),
            k_state=k * jnp.exp(b_last - b),
            k_hat=k * jnp.exp(jnp.minimum(-w, EXP_CLAMP)),
            chunk_decay=jnp.exp(b_last))

    def scores(d):
        out = []
        for hh in range(2):
            kh = jnp.where(lane_head == hh, d["k_hat"], 0.0).astype(BF16)
            k_cat = jnp.concatenate(
                [jnp.where(row_blk == j, kh, jnp.zeros_like(kh)) for j in range(nsub)], axis=1)
            out.append(lax.dot_general(d["q_cat"], k_cat, (((1,), (1,)), ((), ())),
                                       preferred_element_type=F32))
        return out

    def finish(bi, pair, d, sc):
        for hh in range(2):
            head = pair * 2 + hh
            cols = slice(head * GLA_DV, (head + 1) * GLA_DV)
            intra = jnp.where(causal, sc[hh], 0.0).astype(BF16)
            vh = v_ref[bi, :, cols]
            state_t = st_ref[bi * GLA_HEADS + head]
            o = jnp.dot(intra, vh, preferred_element_type=F32)
            o += lax.dot_general(d["q_state"], state_t.astype(BF16), (((1,), (1,)), ((), ())),
                                 preferred_element_type=F32)
            ks = jnp.where(lane_head == hh, d["k_state"], 0.0).astype(BF16)
            kv_t = lax.dot_general(vh, ks, (((0,), (0,)), ((), ())), preferred_element_type=F32)
            st_ref[bi * GLA_HEADS + head] = state_t * d["chunk_decay"] + kv_t
            y = _rmsnorm(o, hn_ref[...]) * r_ref[bi, :, cols].astype(F32)
            o_ref[bi, :, cols] = y.astype(BF16)

    chains = list(itertools.product(range(q_ref.shape[0]), range(GLA_HEADS // 2)))
    cums = [cumsum(*ch) for ch in chains]
    decayed, scored = {}, {}
    for t in range(len(chains) + 1):
        if t < len(chains):
            decayed[t] = decay(*chains[t], cums[t])
        if t >= 1:
            finish(*chains[t - 1], decayed.pop(t - 1), scored.pop(t - 1))
        if t < len(chains):
            scored[t] = scores(decayed[t])


def _gla(gq, gk, gv, gr, g, head_norm):
    bsz, s, kw = gq.shape
    vw = gv.shape[-1]
    nb, chunk = GLA_BATCHES, GLA_CHUNK

    def rows(w):
        return pl.BlockSpec((nb, chunk, w), lambda b, c: (b, c, 0))

    return pl.pallas_call(
        functools.partial(_gla_kernel, chunk=chunk, sub=GLA_SUB),
        out_shape=jax.ShapeDtypeStruct((bsz, s, vw), BF16),
        grid=(bsz // nb, s // chunk),
        in_specs=[rows(kw), rows(kw), rows(vw), rows(vw), rows(kw), _resident((1, GLA_DV))],
        out_specs=rows(vw),
        scratch_shapes=[pltpu.VMEM((nb * GLA_HEADS, GLA_DV, LANES), F32)],
        compiler_params=_params(("parallel", "arbitrary")),
        name="gla",
    )(gq, gk, gv, gr, g, head_norm.reshape(1, GLA_DV))


def _attn_kernel(q_ref, k_ref, v_ref, pos_ref, slope_ref, lam_ref, hn_ref, o_ref,
                 kk_ref, qq_ref, vt_ref, s0_ref, s1_ref, s2_ref, m_ref, acc_ref, *, tq, lam_init):
    seq = k_ref.shape[1]
    nq = seq // tq
    hq = tq // 2
    c = slope_ref[0] * LOG2E
    c1 = c.astype(BF16).astype(F32)
    c2 = (c - c1).astype(BF16).astype(F32)
    c3 = c - c1 - c2

    kk_ref[:, :LANES] = k_ref[0]
    kk_ref[:, LANES:] = pos_ref[...]
    sub = lax.broadcasted_iota(jnp.int32, (LANES, 2 * tq), 0)
    cf = jnp.where((sub == 0) | (sub == 3), c1, jnp.where((sub == 1) | (sub == 4), c2, c3))
    slope_rows = jnp.where(sub < 6, cf, 0.0).astype(BF16)
    for i in range(qq_ref.shape[0]):
        qq_ref[i, LANES:, :] = slope_rows
    for jb in range(nq):
        vt_ref[jb, :DIFF_DV, :] = v_ref[0, jb * tq:(jb + 1) * tq, :].astype(F32).T.astype(BF16)
        vt_ref[jb, DIFF_DV:, :] = jnp.ones((SUM_ROWS, tq), BF16)
    lam = lam_ref[...]
    lam = (jnp.exp(jnp.sum(lam[0:1] * lam[1:2], axis=-1, keepdims=True))
           - jnp.exp(jnp.sum(lam[2:3] * lam[3:4], axis=-1, keepdims=True)) + lam_init)

    def logits(qq_slot, j, s_blk):
        start = pl.multiple_of(j * tq, tq)
        s = jnp.dot(kk_ref[pl.ds(start, tq), :], qq_ref[qq_slot], preferred_element_type=F32)
        s_blk[...] = s
        return jnp.max(s.reshape(tq // 8, 8, 2 * tq), axis=0)

    def softmax_accumulate(qs, qi, j, s_blk, mx):
        m_old = m_ref[qs]
        shift = c * jnp.asarray((qi - j) * tq, F32)
        m_new = jnp.maximum(m_old, jnp.max(mx, axis=0, keepdims=True) - shift)
        alpha = jnp.exp2(m_old - m_new)
        m_ref[qs] = m_new
        m_shift = m_new + shift
        probs = jnp.concatenate([jnp.exp2(s_blk[:hq, :] - m_shift).astype(BF16),
                                 jnp.exp2(s_blk[hq:, :] - m_shift).astype(BF16)], axis=0)
        acc_ref[qs] = alpha * acc_ref[qs] + jnp.dot(vt_ref[j], probs, preferred_element_type=F32)

    def late_half(x):
        return jnp.concatenate([x[..., hq:tq], x[..., tq + hq:]], axis=-1)

    def diag_logits(qq_slot, j, s_blk):
        start = pl.multiple_of(j * tq, tq)
        s_blk[:hq, :] = jnp.dot(kk_ref[pl.ds(start, hq), :], qq_ref[qq_slot],
                                preferred_element_type=F32)
        s_blk[hq:, :tq] = jnp.dot(kk_ref[pl.ds(start + hq, hq), :], late_half(qq_ref[qq_slot]),
                                  preferred_element_type=F32)

    def diag_softmax_accumulate(qs, j, s_blk):
        m_old = m_ref[qs]
        key = lax.broadcasted_iota(jnp.int32, (hq, 2 * tq), 0)
        qry = lax.broadcasted_iota(jnp.int32, (hq, 2 * tq), 1) & (tq - 1)
        early = jnp.where(key <= qry, s_blk[:hq, :], -jnp.inf)
        key_l = lax.broadcasted_iota(jnp.int32, (hq, tq), 0)
        qry_l = lax.broadcasted_iota(jnp.int32, (hq, tq), 1) & (hq - 1)
        late = jnp.where(key_l <= qry_l, s_blk[hq:, :tq], -jnp.inf)
        mx_e = jnp.max(jnp.max(early.reshape(hq // 8, 8, 2 * tq), axis=0), axis=0, keepdims=True)
        mx_l = jnp.max(jnp.max(late.reshape(hq // 8, 8, tq), axis=0), axis=0, keepdims=True)
        mx = jnp.concatenate(
            [mx_e[:, :hq], jnp.maximum(mx_e[:, hq:tq], mx_l[:, :hq]),
             mx_e[:, tq:tq + hq], jnp.maximum(mx_e[:, tq + hq:], mx_l[:, hq:])], axis=1)
        m_new = jnp.maximum(m_old, mx)
        alpha = jnp.exp2(m_old - m_new)
        m_ref[qs] = m_new
        p_early = jnp.exp2(early - m_new).astype(BF16)
        p_late = jnp.exp2(late - late_half(m_new)).astype(BF16)
        vt = vt_ref[j]
        acc_ref[qs] = alpha * acc_ref[qs] + jnp.dot(vt[:, :hq], p_early,
                                                    preferred_element_type=F32)
        upd = jnp.dot(vt[:, hq:], p_late, preferred_element_type=F32)
        acc_ref[qs, :, hq:tq] += upd[:, :hq]
        acc_ref[qs, :, tq + hq:] += upd[:, hq:]

    bufs = (s0_ref, s1_ref, s2_ref)
    ahead = len(bufs) - 1
    assert (nq + 1) % len(bufs) == 0

    def build_queries(p, slot):
        for qs, blk in enumerate((p, nq - 1 - p)):
            q_t = q_ref[0, pl.ds(pl.multiple_of(blk * tq, tq), tq), :].astype(F32).T
            half = lax.broadcasted_iota(jnp.int32, (LANES, tq), 0)
            qq_ref[2 * slot + qs, :LANES, :tq] = jnp.where(half < DIFF_DH, q_t, 0.0).astype(BF16)
            qq_ref[2 * slot + qs, :LANES, tq:] = jnp.where(half >= DIFF_DH, q_t, 0.0).astype(BF16)

    def diag_prefetch(p, slot, t):
        qs, blk = ((1, nq - 1 - p), (0, p))[t]
        diag_logits(2 * slot + qs, blk, bufs[t])

    def pair(p, carry):
        slot = p % 2
        q_blk = (p, nq - 1 - p)
        for qs in range(2):
            m_ref[qs] = jnp.full((1, 2 * tq), -jnp.inf, F32)
            acc_ref[qs] = jnp.zeros((DIFF_DV + SUM_ROWS, 2 * tq), F32)
        items = [(1, q_blk[1], q_blk[1], True), (0, q_blk[0], q_blk[0], True)]
        for u in range(nq - 1):
            late = u < q_blk[1]
            items.append((jnp.where(late, 1, 0), jnp.where(late, q_blk[1], q_blk[0]),
                          jnp.where(late, u, u - q_blk[1]), False))
        col_max = {}
        p_next = jnp.minimum(p + 1, nq // 2 - 1)
        for t, (qs, qi, j, masked) in enumerate(items):
            ta = t + ahead
            if ta < len(items):
                col_max[ta] = logits(2 * slot + items[ta][0], items[ta][2], bufs[ta % len(bufs)])
            elif ta == len(items):
                build_queries(p_next, 1 - slot)
                diag_prefetch(p_next, 1 - slot, 0)
            else:
                diag_prefetch(p_next, 1 - slot, 1)
            if masked:
                diag_softmax_accumulate(qs, j, bufs[t % len(bufs)])
            else:
                softmax_accumulate(qs, qi, j, bufs[t % len(bufs)], col_max.pop(t))
        for qs in range(2):
            out = acc_ref[qs, :DIFF_DV, :] * (1.0 / acc_ref[qs, DIFF_DV:DIFF_DV + 1, :])
            y_t = out[:, :tq] - lam * out[:, tq:]
            y_t = y_t * lax.rsqrt(jnp.mean(y_t * y_t, axis=0, keepdims=True) + EPS)
            rows = pl.ds(pl.multiple_of(q_blk[qs] * tq, tq), tq)
            o_ref[0, rows, :] = (y_t.T * hn_ref[...] * (1.0 - lam_init)).astype(BF16)
        return carry

    build_queries(0, 0)
    diag_prefetch(0, 0, 0)
    diag_prefetch(0, 0, 1)
    lax.fori_loop(0, nq // 2, pair, 0)


def _diff_attn(dq, dk, dv, lam_vecs, head_norm, lam_init):
    bsz, s, _ = dq.shape
    tq = ATTN_BLOCK
    slopes = jnp.asarray(
        [2.0 ** (-ALIBI_MAX_BIAS * (i + 1) / DIFF_HEADS) for i in range(DIFF_HEADS)], F32
    ).reshape(DIFF_HEADS, 1, 1)
    pos = np.arange(s) % tq
    feat = np.zeros((s, LANES), np.float32)
    feat[:, 0:3] = (pos - pos % 16)[:, None]
    feat[:, 3:6] = (pos % 16)[:, None]
    return pl.pallas_call(
        functools.partial(_attn_kernel, tq=tq, lam_init=lam_init),
        out_shape=jax.ShapeDtypeStruct((bsz, s, DIFF_HEADS * DIFF_DV), BF16),
        grid=(bsz, DIFF_HEADS),
        in_specs=[
            pl.BlockSpec((1, s, LANES), lambda b, h: (b, 0, h)),
            pl.BlockSpec((1, s, LANES), lambda b, h: (b, 0, h)),
            pl.BlockSpec((1, s, DIFF_DV), lambda b, h: (b, 0, h)),
            _resident((s, LANES)),
            pl.BlockSpec((1, 1, 1), lambda b, h: (h, 0, 0)),
            _resident(lam_vecs.shape),
            _resident((1, DIFF_DV)),
        ],
        out_specs=pl.BlockSpec((1, s, DIFF_DV), lambda b, h: (b, 0, h)),
        scratch_shapes=[
            pltpu.VMEM((s, 2 * LANES), BF16),
            pltpu.VMEM((4, 2 * LANES, 2 * tq), BF16),
            pltpu.VMEM((s // tq, DIFF_DV + SUM_ROWS, tq), BF16),
            pltpu.VMEM((tq, 2 * tq), F32),
            pltpu.VMEM((tq, 2 * tq), F32),
            pltpu.VMEM((tq, 2 * tq), F32),
            pltpu.VMEM((2, 1, 2 * tq), F32),
            pltpu.VMEM((2, DIFF_DV + SUM_ROWS, 2 * tq), F32),
        ],
        compiler_params=_params(("parallel", "parallel")),
        name="diff_attn",
    )(dq, dk, dv, jnp.asarray(feat, BF16), slopes, lam_vecs, head_norm.reshape(1, DIFF_DV))


def _merge_kernel(h_ref, za_ref, zb_ref, sg_ref, gt_ref, wa_ref, wb_ref, wo_ref, o_ref):
    d = h_ref.shape[-1]
    ya = jnp.dot(za_ref[0], wa_ref[...].astype(BF16), preferred_element_type=F32)
    yb = jnp.dot(zb_ref[0], wb_ref[...].astype(BF16), preferred_element_type=F32)
    mix = sg_ref[0, :, :d].astype(F32) * ya + sg_ref[0, :, d:].astype(F32) * yb
    m = jnp.dot(mix.astype(BF16), wo_ref[...].astype(BF16), preferred_element_type=F32)
    o_ref[0] = h_ref[0] + gt_ref[0] * m


def _merge(h, za, zb, sg, gt, w_a, w_b, w_o):
    bsz, s, d = h.shape
    tm = MERGE_ROWS
    vec = pl.BlockSpec((1, 1, d), lambda b, i: (b, 0, 0))

    def rows(w):
        return pl.BlockSpec((1, tm, w), lambda b, i: (b, i, 0))

    return pl.pallas_call(
        _merge_kernel,
        out_shape=jax.ShapeDtypeStruct((bsz, s, d), F32),
        grid=(bsz, s // tm),
        in_specs=[rows(d), rows(za.shape[-1]), rows(zb.shape[-1]), rows(2 * d), vec,
                  _resident(w_a.shape), _resident(w_b.shape), _resident(w_o.shape)],
        out_specs=rows(d),
        compiler_params=_params(("parallel", "parallel")),
        name="merge",
    )(h, za, zb, sg, gt, w_a, w_b, w_o)


def kernel(x, c, w_ada, b_ada, ffn1_norm, ffn1_w_in, ffn1_w_out, mix_norm, w_in, gla_alpha_w2, gla_alpha_b, gla_head_norm, diff_lq1, diff_lk1, diff_lq2, diff_lk2, diff_head_norm, w_branch_a, w_branch_b, w_out, ffn2_norm, ffn2_w_in, ffn2_w_out, final_norm):
    depth = w_ada.shape[0]
    bsz, s, d = x.shape
    assert bsz % GLA_BATCHES == 0 and s % MERGE_ROWS == 0 and (s // ATTN_BLOCK) % 2 == 0
    h = x
    for l in range(depth):
        lam_init = 0.8 - 0.6 * math.exp(-0.3 * l)
        mod = _adaln(c, w_ada[l], b_ada[l])
        sh1, sc1, gt1, sh2, sc2, gt2, sh3, sc3, gt3 = [
            mod[:, i * d:(i + 1) * d].reshape(bsz, 1, d) for i in range(N_MOD)]
        last = l == depth - 1
        h = _ffn(h, sh1, sc1, gt1, ffn1_norm[l], ffn1_w_in[l], ffn1_w_out[l], final_norm,
                 final_norm=False)
        w_t = jnp.swapaxes(w_in, 1, 2)[l]
        gq, gk, gv, gr, g, dq, dk, dv, sg = _mixer_proj(
            h, sh2, sc2, mix_norm[l], w_t, gla_alpha_w2[l], gla_alpha_b[l])
        za = _gla(gq, gk, gv, gr, g, gla_head_norm[l])
        lam_vecs = jnp.stack([diff_lq1[l], diff_lk1[l], diff_lq2[l], diff_lk2[l]])
        zb = _diff_attn(dq, dk, dv, lam_vecs, diff_head_norm[l], lam_init)
        h = _merge(h, za, zb, sg, gt2, w_branch_a[l], w_branch_b[l], w_out[l])
        h = _ffn(h, sh3, sc3, gt3, ffn2_norm[l], ffn2_w_in[l], ffn2_w_out[l], final_norm,
                 final_norm=last)
    return h
```

```python
import functools
import itertools
import math

import jax
import jax.numpy as jnp
import numpy as np
from jax import lax
from jax.experimental import pallas as pl
from jax.experimental.pallas import tpu as pltpu

F32 = jnp.float32
BF16 = jnp.bfloat16

EPS = 1e-6
GLA_HEADS = 4
GLA_DK = 64
GLA_DV = 128
GLA_RANK = 16
GLA_TAU = 16.0
DIFF_HEADS = 4
DIFF_DH = 64
DIFF_DV = 128
ALIBI_MAX_BIAS = 8.0
N_MOD = 9

LOG2E = math.log2(math.e)
LANES = 128
MXU_COLS = 256
VMEM_LIMIT = 56 * 1024 * 1024

ADALN_STEPS = 8
FFN_ROWS = 512
FFN_HIDDEN_CHUNK = MXU_COLS
NEXT_PIECES = 8
PROJ_ROWS = 512
GATE_CHUNK = 512
MERGE_ROWS = 1024
GLA_CHUNK = 128
GLA_SUB = 16
GLA_BATCHES = 4
GLA_STEP_CHUNKS = 4
ATTN_BLOCK = 512
EXP2_CLAMP = 115.0
SUM_ROWS = 16


def _params(sem):
    return pltpu.CompilerParams(dimension_semantics=sem, vmem_limit_bytes=VMEM_LIMIT)


def _resident(shape):
    nd = len(shape)
    return pl.BlockSpec(shape, lambda *_: (0,) * nd, pipeline_mode=pl.Buffered(1))


def _rmsnorm(x, g):
    return x * lax.rsqrt(jnp.mean(x * x, axis=-1, keepdims=True) + EPS) * g


def _next_input_pieces(xn_ref, un_ref, norm_fn, step):
    piece = un_ref.shape[0] // NEXT_PIECES
    zero = jnp.minimum(step, 0).astype(F32)

    def prepare(i):
        rows = slice(i * piece, (i + 1) * piece)
        un = norm_fn(xn_ref[0, rows, :])
        un_ref[rows, :] = un
        return jnp.sum(un.astype(F32), keepdims=True) * zero

    return prepare


def _next_rows(bsz, nt, tm, d):
    def index(b, i):
        nxt = jnp.minimum(b * nt + i + 1, bsz * nt - 1)
        return nxt // nt, nxt % nt, 0
    return pl.BlockSpec((1, tm, d), index)


def _adaln_kernel(c_ref, w_ref, b_ref, o_ref):
    c = c_ref[...]
    ca = (c * jax.nn.sigmoid(c)).astype(BF16)
    o_ref[...] = jnp.dot(ca, w_ref[...].astype(BF16), preferred_element_type=F32) + b_ref[...]


def _adaln(c, w_ada, b_ada):
    bsz, d = c.shape
    n = w_ada.shape[1]
    tn = n // ADALN_STEPS
    return pl.pallas_call(
        _adaln_kernel,
        out_shape=jax.ShapeDtypeStruct((bsz, n), F32),
        grid=(n // tn,),
        in_specs=[
            pl.BlockSpec((bsz, d), lambda j: (0, 0)),
            pl.BlockSpec((d, tn), lambda j: (0, j)),
            pl.BlockSpec((1, tn), lambda j: (0, j)),
        ],
        out_specs=pl.BlockSpec((bsz, tn), lambda j: (0, j)),
        compiler_params=_params(("arbitrary",)),
        name="adaln",
    )(c, w_ada, b_ada.reshape(1, n))


def _ffn_kernel(x_ref, xn_ref, sh_ref, sc_ref, gt_ref, nw_ref, win_hbm, wout_hbm, fn_ref, o_ref,
                u_ref, un_ref, win_ref, wout_ref, sem_ref, *, d_ff, tf, final_norm):
    n_chunk = d_ff // tf
    acc_ref = o_ref.at[0]
    step = pl.program_id(0) * pl.num_programs(1) + pl.program_id(1)
    last_step = pl.num_programs(0) * pl.num_programs(1) - 1

    def normed(x, batch):
        return (_rmsnorm(x, nw_ref[...]) * (1.0 + sc_ref[batch]) + sh_ref[batch]).astype(BF16)

    def weight_copies(i):
        cols = pl.ds(i * tf, tf)
        up = pl.ds(d_ff + i * tf, tf)
        return (pltpu.make_async_copy(win_hbm.at[:, cols], win_ref.at[:, cols], sem_ref.at[0, i]),
                pltpu.make_async_copy(win_hbm.at[:, up], win_ref.at[:, up], sem_ref.at[1, i]),
                pltpu.make_async_copy(wout_hbm.at[cols, :], wout_ref.at[cols, :], sem_ref.at[2, i]))

    def tile(is_first):
        if is_first:
            un_ref[...] = normed(x_ref[0], pl.program_id(0))
        u_ref[...] = un_ref[...]
        u = u_ref[...]
        next_batch = jnp.minimum(step + 1, last_step) // pl.num_programs(1)
        prepare = _next_input_pieces(xn_ref, un_ref, lambda x: normed(x, next_batch), step)
        anchor = None
        for i in range(n_chunk):
            if is_first:
                for copy in weight_copies(i):
                    copy.wait()
            cols = slice(i * tf, (i + 1) * tf)
            up = slice(d_ff + i * tf, d_ff + (i + 1) * tf)
            hg = jnp.dot(u, win_ref[:, cols].astype(BF16), preferred_element_type=F32)
            if anchor is not None:
                hg = hg + anchor
            hu = jnp.dot(u, win_ref[:, up].astype(BF16), preferred_element_type=F32)
            act = (hg * jax.nn.sigmoid(hg) * hu).astype(BF16)
            part = jnp.dot(act, wout_ref[cols, :].astype(BF16), preferred_element_type=F32)
            if i == 0:
                acc_ref[...] = part
            else:
                acc_ref[...] += part
            anchor = prepare(i) if i < NEXT_PIECES else None
        h = x_ref[0] + (0.5 * gt_ref[0]) * acc_ref[...]
        if final_norm:
            h = _rmsnorm(h, fn_ref[...])
        o_ref[0] = h

    first = (pl.program_id(0) == 0) & (pl.program_id(1) == 0)

    @pl.when(first)
    def _():
        for i in range(n_chunk):
            for copy in weight_copies(i):
                copy.start()
        tile(True)

    @pl.when(jnp.logical_not(first))
    def _():
        tile(False)


def _ffn(x, sh, sc, gt, nw, w_in, w_out, fn, *, final_norm):
    bsz, s, d = x.shape
    d_ff = w_out.shape[0]
    tm, tf = FFN_ROWS, FFN_HIDDEN_CHUNK
    nt = s // tm
    vec = pl.BlockSpec((1, 1, d), lambda b, i: (b, 0, 0))
    row = pl.BlockSpec((1, tm, d), lambda b, i: (b, i, 0))
    hbm = pl.BlockSpec(memory_space=pl.ANY)
    return pl.pallas_call(
        functools.partial(_ffn_kernel, d_ff=d_ff, tf=tf, final_norm=final_norm),
        out_shape=jax.ShapeDtypeStruct((bsz, s, d), F32),
        grid=(bsz, nt),
        in_specs=[row, _next_rows(bsz, nt, tm, d), _resident(sh.shape), _resident(sc.shape), vec,
                  _resident((1, d)), hbm, hbm, _resident((1, d))],
        out_specs=row,
        scratch_shapes=[pltpu.VMEM((tm, d), BF16), pltpu.VMEM((tm, d), BF16),
                        pltpu.VMEM((d, 2 * d_ff), F32), pltpu.VMEM((d_ff, d), F32),
                        pltpu.SemaphoreType.DMA((3, d_ff // tf))],
        compiler_params=_params(("arbitrary", "arbitrary")),
        name="ffn_final" if final_norm else "ffn",
    )(x, x, sh, sc, gt, nw.reshape(1, d), w_in, w_out, fn.reshape(1, d))


def _proj_kernel(x_ref, xn_ref, sh_ref, sc_ref, nw_ref, wt_ref, w2_ref, b2_ref,
                 gq_ref, gk_ref, gv_ref, gr_ref, g_ref, dq_ref, dk_ref, dv_ref, sg_ref,
                 u_ref, un_ref, w_ref):
    kw = GLA_HEADS * GLA_DK
    vw = GLA_HEADS * GLA_DV
    qw = DIFF_HEADS * 2 * DIFF_DH
    dvw = DIFF_HEADS * DIFF_DV
    step = pl.program_id(0) * pl.num_programs(1) + pl.program_id(1)
    last_step = pl.num_programs(0) * pl.num_programs(1) - 1

    def normed(x, batch):
        return (_rmsnorm(x, nw_ref[...]) * (1.0 + sc_ref[batch]) + sh_ref[batch]).astype(BF16)

    ga_start = 2 * kw + 2 * vw

    @pl.when(step == 0)
    def _():
        un_ref[...] = normed(x_ref[0], 0)
        for col in range(0, w_ref.shape[1], MXU_COLS):
            row = col if col < ga_start else col + GLA_RANK
            w_ref[:, col:col + MXU_COLS] = wt_ref[row:row + MXU_COLS, :].T.astype(BF16)

    u_ref[...] = un_ref[...]
    u = u_ref[...]
    next_batch = jnp.minimum(step + 1, last_step) // pl.num_programs(1)
    prepare = _next_input_pieces(xn_ref, un_ref, lambda x: normed(x, next_batch), step)
    pending = {"count": 0, "anchor": None}

    def seg(start, size):
        if size == GLA_RANK:
            out = lax.dot_general(u, wt_ref[start:start + size, :].astype(BF16),
                                  (((1,), (1,)), ((), ())), preferred_element_type=F32)
        else:
            col = start if start < ga_start else start - GLA_RANK
            out = jnp.dot(u, w_ref[:, col:col + size], preferred_element_type=F32)
        if pending["anchor"] is not None:
            out = out + pending["anchor"]
        pending["anchor"] = prepare(pending["count"]) if pending["count"] < NEXT_PIECES else None
        pending["count"] += 1
        return out

    off = 0
    gq_ref[0] = (seg(off, kw) * (GLA_DK ** -0.5)).astype(BF16)
    off += kw
    gk_ref[0] = seg(off, kw).astype(BF16)
    off += kw
    gv_ref[0] = seg(off, vw).astype(BF16)
    off += vw
    r = seg(off, vw)
    gr_ref[0] = (r * jax.nn.sigmoid(r)).astype(BF16)
    off += vw
    a_low = seg(off, GLA_RANK).astype(BF16)
    z = jnp.dot(a_low, w2_ref[...].astype(BF16), preferred_element_type=F32) + b2_ref[...]
    g_ref[0] = (jnp.minimum(z, 0.0) - jnp.log1p(jnp.exp(-jnp.abs(z)))) * (LOG2E / GLA_TAU)
    off += GLA_RANK
    dq_ref[0] = (seg(off, qw) * (DIFF_DH ** -0.5 * LOG2E)).astype(BF16)
    off += qw
    dk_ref[0] = seg(off, qw).astype(BF16)
    off += qw
    dv_ref[0] = seg(off, dvw).astype(BF16)
    off += dvw
    for j in range(sg_ref.shape[-1] // GATE_CHUNK):
        gate = jax.nn.sigmoid(seg(off + j * GATE_CHUNK, GATE_CHUNK))
        sg_ref[0, :, j * GATE_CHUNK:(j + 1) * GATE_CHUNK] = gate.astype(BF16)


def _mixer_proj(h, sh, sc, nw, w_t, w2, b2):
    bsz, s, d = h.shape
    tm = PROJ_ROWS
    kw = GLA_HEADS * GLA_DK
    vw = GLA_HEADS * GLA_DV
    qw = DIFF_HEADS * 2 * DIFF_DH
    dvw = DIFF_HEADS * DIFF_DV
    n_gate = 2 * d
    vec = pl.BlockSpec((1, 1, d), lambda b, i: (b, 0, 0))

    def rows(w):
        return pl.BlockSpec((1, tm, w), lambda b, i: (b, i, 0))

    widths = [(kw, BF16), (kw, BF16), (vw, BF16), (vw, BF16), (kw, F32),
              (qw, BF16), (qw, BF16), (dvw, BF16), (n_gate, BF16)]
    return pl.pallas_call(
        _proj_kernel,
        out_shape=[jax.ShapeDtypeStruct((bsz, s, w), dt) for w, dt in widths],
        grid=(bsz, s // tm),
        in_specs=[rows(d), _next_rows(bsz, s // tm, tm, d), _resident(sh.shape),
                  _resident(sc.shape), _resident((1, d)), _resident(w_t.shape),
                  _resident(w2.shape), _resident((1, kw))],
        out_specs=[rows(w) for w, _ in widths],
        scratch_shapes=[pltpu.VMEM((tm, d), BF16), pltpu.VMEM((tm, d), BF16),
                        pltpu.VMEM((d, w_t.shape[0] - GLA_RANK), BF16)],
        compiler_params=_params(("arbitrary", "arbitrary")),
        name="mixer_proj",
    )(h, h, sh, sc, nw.reshape(1, d), w_t, w2, b2.reshape(1, kw))


def _gla_kernel(q_ref, k_ref, v_ref, r_ref, g_ref, hn_ref, o_ref, st_ref, *, chunk, sub):
    @pl.when(pl.program_id(1) == 0)
    def _():
        st_ref[...] = jnp.zeros_like(st_ref)

    nsub = chunk // sub
    row = lax.broadcasted_iota(jnp.int32, (chunk, chunk), 0)
    col = lax.broadcasted_iota(jnp.int32, (chunk, chunk), 1)
    causal = col <= row
    cum_mat = jnp.where(causal, 1.0, 0.0).astype(BF16)
    row_blk = lax.broadcasted_iota(jnp.int32, (chunk, LANES), 0) // sub
    lane_head = lax.broadcasted_iota(jnp.int32, (chunk, LANES), 1) // GLA_DK

    def cumsum(ck, bi, pair):
        g = g_ref[bi, ck * chunk:(ck + 1) * chunk, pair * LANES:(pair + 1) * LANES]
        g1 = g.astype(BF16)
        e1 = g - g1.astype(F32)
        g2 = e1.astype(BF16)
        g3 = (e1 - g2.astype(F32)).astype(BF16)
        cs = jnp.dot(cum_mat, jnp.concatenate([g1, g2, g3], axis=1), preferred_element_type=F32)
        return cs[:, :LANES] + cs[:, LANES:2 * LANES] + cs[:, 2 * LANES:]

    def decay(ck, bi, pair, b):
        lanes = slice(pair * LANES, (pair + 1) * LANES)
        b_last = b[chunk - 1:chunk]
        rows = slice(ck * chunk, (ck + 1) * chunk)
        q = q_ref[bi, rows, lanes].astype(F32)
        k = k_ref[bi, rows, lanes].astype(F32)
        q_cat, w = [], []
        for j in range(nsub):
            ref_b = b[j * sub - 1:j * sub] if j else jnp.zeros_like(b_last)
            q_cat.append((q * jnp.exp2(jnp.minimum(b - ref_b, 0.0))).astype(BF16))
            w.append(b[j * sub:(j + 1) * sub] - ref_b)
        w = jnp.concatenate(w, axis=0)
        return dict(
            q_cat=jnp.concatenate(q_cat, axis=1),
            q_state=(q * jnp.exp2(b)).astype(BF16),
            k_state=k * jnp.exp2(b_last - b),
            k_hat=k * jnp.exp2(jnp.minimum(-w, EXP2_CLAMP)),
            chunk_decay=jnp.exp2(b_last))

    def scores(d):
        out = []
        for hh in range(2):
            kh = jnp.where(lane_head == hh, d["k_hat"], 0.0).astype(BF16)
            k_cat = jnp.concatenate(
                [jnp.where(row_blk == j, kh, jnp.zeros_like(kh)) for j in range(nsub)], axis=1)
            out.append(lax.dot_general(d["q_cat"], k_cat, (((1,), (1,)), ((), ())),
                                       preferred_element_type=F32))
        return out

    def finish(ck, bi, pair, d, sc):
        rows = slice(ck * chunk, (ck + 1) * chunk)
        for hh in range(2):
            head = pair * 2 + hh
            cols = slice(head * GLA_DV, (head + 1) * GLA_DV)
            intra = jnp.where(causal, sc[hh], 0.0).astype(BF16)
            vh = v_ref[bi, rows, cols]
            state_t = st_ref[bi * GLA_HEADS + head]
            o = jnp.dot(intra, vh, preferred_element_type=F32)
            o += lax.dot_general(d["q_state"], state_t.astype(BF16), (((1,), (1,)), ((), ())),
                                 preferred_element_type=F32)
            ks = jnp.where(lane_head == hh, d["k_state"], 0.0).astype(BF16)
            kv_t = lax.dot_general(vh, ks, (((0,), (0,)), ((), ())), preferred_element_type=F32)
            st_ref[bi * GLA_HEADS + head] = state_t * d["chunk_decay"] + kv_t
            y = _rmsnorm(o, hn_ref[...]) * r_ref[bi, rows, cols].astype(F32)
            o_ref[bi, rows, cols] = y.astype(BF16)

    per_chunk = list(itertools.product(range(q_ref.shape[0]), range(GLA_HEADS // 2)))
    chains = [(ck, bi, pair) for ck in range(q_ref.shape[1] // chunk) for bi, pair in per_chunk]
    cums = {t: cumsum(*chains[t]) for t in range(len(per_chunk))}
    decayed, scored = {}, {}
    for t in range(len(chains) + 1):
        if t + len(per_chunk) < len(chains):
            cums[t + len(per_chunk)] = cumsum(*chains[t + len(per_chunk)])
        if t < len(chains):
            decayed[t] = decay(*chains[t], cums.pop(t))
        if t >= 1:
            finish(*chains[t - 1], decayed.pop(t - 1), scored.pop(t - 1))
        if t < len(chains):
            scored[t] = scores(decayed[t])


def _gla(gq, gk, gv, gr, g, head_norm):
    bsz, s, kw = gq.shape
    vw = gv.shape[-1]
    nb, chunk = GLA_BATCHES, GLA_CHUNK
    step_rows = GLA_STEP_CHUNKS * chunk

    def rows(w):
        return pl.BlockSpec((nb, step_rows, w), lambda b, c: (b, c, 0))

    return pl.pallas_call(
        functools.partial(_gla_kernel, chunk=chunk, sub=GLA_SUB),
        out_shape=jax.ShapeDtypeStruct((bsz, s, vw), BF16),
        grid=(bsz // nb, s // step_rows),
        in_specs=[rows(kw), rows(kw), rows(vw), rows(vw), rows(kw), _resident((1, GLA_DV))],
        out_specs=rows(vw),
        scratch_shapes=[pltpu.VMEM((nb * GLA_HEADS, GLA_DV, LANES), F32)],
        compiler_params=_params(("parallel", "arbitrary")),
        name="gla",
    )(gq, gk, gv, gr, g, head_norm.reshape(1, GLA_DV))


def _attn_kernel(q_ref, k_ref, v_ref, pos_ref, slope_ref, lam_ref, hn_ref, o_ref,
                 kk_ref, qq_ref, vt_ref, s0_ref, s1_ref, s2_ref, m_ref, acc_ref, *, tq, lam_init):
    seq = k_ref.shape[1]
    nq = seq // tq
    hq = tq // 2
    c = slope_ref[0] * LOG2E
    c1 = c.astype(BF16).astype(F32)
    c2 = (c - c1).astype(BF16).astype(F32)
    c3 = c - c1 - c2

    kk_ref[:, :LANES] = k_ref[0]
    kk_ref[:, LANES:] = pos_ref[...]
    sub = lax.broadcasted_iota(jnp.int32, (LANES, 2 * tq), 0)
    cf = jnp.where((sub == 0) | (sub == 3), c1, jnp.where((sub == 1) | (sub == 4), c2, c3))
    slope_rows = jnp.where(sub < 6, cf, 0.0).astype(BF16)
    for i in range(qq_ref.shape[0]):
        qq_ref[i, LANES:, :] = slope_rows
    for jb in range(nq):
        vt_ref[jb, :DIFF_DV, :] = v_ref[0, jb * tq:(jb + 1) * tq, :].astype(F32).T.astype(BF16)
        vt_ref[jb, DIFF_DV:, :] = jnp.ones((SUM_ROWS, tq), BF16)
    lam = lam_ref[...]
    lam = (jnp.exp(jnp.sum(lam[0:1] * lam[1:2], axis=-1, keepdims=True))
           - jnp.exp(jnp.sum(lam[2:3] * lam[3:4], axis=-1, keepdims=True)) + lam_init)

    def logits(qq_slot, j, s_blk):
        start = pl.multiple_of(j * tq, tq)
        s = jnp.dot(kk_ref[pl.ds(start, tq), :], qq_ref[qq_slot], preferred_element_type=F32)
        s_blk[...] = s
        return jnp.max(s.reshape(tq // 8, 8, 2 * tq), axis=0)

    def softmax_accumulate(qs, qi, j, s_blk, mx):
        m_old = m_ref[qs]
        shift = c * jnp.asarray((qi - j) * tq, F32)
        m_new = jnp.maximum(m_old, jnp.max(mx, axis=0, keepdims=True) - shift)
        alpha = jnp.exp2(m_old - m_new)
        m_ref[qs] = m_new
        m_shift = m_new + shift
        probs = jnp.concatenate([jnp.exp2(s_blk[:hq, :] - m_shift).astype(BF16),
                                 jnp.exp2(s_blk[hq:, :] - m_shift).astype(BF16)], axis=0)
        acc_ref[qs] = alpha * acc_ref[qs] + jnp.dot(vt_ref[j], probs, preferred_element_type=F32)

    def late_half(x):
        return jnp.concatenate([x[..., hq:tq], x[..., tq + hq:]], axis=-1)

    def diag_logits(qq_slot, j, s_blk):
        start = pl.multiple_of(j * tq, tq)
        s_blk[:hq, :] = jnp.dot(kk_ref[pl.ds(start, hq), :], qq_ref[qq_slot],
                                preferred_element_type=F32)
        s_blk[hq:, :tq] = jnp.dot(kk_ref[pl.ds(start + hq, hq), :], late_half(qq_ref[qq_slot]),
                                  preferred_element_type=F32)

    def diag_softmax_accumulate(qs, j, s_blk):
        m_old = m_ref[qs]
        key = lax.broadcasted_iota(jnp.int32, (hq, 2 * tq), 0)
        qry = lax.broadcasted_iota(jnp.int32, (hq, 2 * tq), 1) & (tq - 1)
        early = jnp.where(key <= qry, s_blk[:hq, :], -jnp.inf)
        key_l = lax.broadcasted_iota(jnp.int32, (hq, tq), 0)
        qry_l = lax.broadcasted_iota(jnp.int32, (hq, tq), 1) & (hq - 1)
        late = jnp.where(key_l <= qry_l, s_blk[hq:, :tq], -jnp.inf)
        mx_e = jnp.max(jnp.max(early.reshape(hq // 8, 8, 2 * tq), axis=0), axis=0, keepdims=True)
        mx_l = jnp.max(jnp.max(late.reshape(hq // 8, 8, tq), axis=0), axis=0, keepdims=True)
        mx = jnp.concatenate(
            [mx_e[:, :hq], jnp.maximum(mx_e[:, hq:tq], mx_l[:, :hq]),
             mx_e[:, tq:tq + hq], jnp.maximum(mx_e[:, tq + hq:], mx_l[:, hq:])], axis=1)
        m_new = jnp.maximum(m_old, mx)
        alpha = jnp.exp2(m_old - m_new)
        m_ref[qs] = m_new
        p_early = jnp.exp2(early - m_new).astype(BF16)
        p_late = jnp.exp2(late - late_half(m_new)).astype(BF16)
        vt = vt_ref[j]
        acc_ref[qs] = alpha * acc_ref[qs] + jnp.dot(vt[:, :hq], p_early,
                                                    preferred_element_type=F32)
        upd = jnp.dot(vt[:, hq:], p_late, preferred_element_type=F32)
        acc_ref[qs, :, hq:tq] += upd[:, :hq]
        acc_ref[qs, :, tq + hq:] += upd[:, hq:]

    bufs = (s0_ref, s1_ref, s2_ref)
    ahead = len(bufs) - 1
    assert (nq + 1) % len(bufs) == 0

    def build_queries(p, slot):
        for qs, blk in enumerate((p, nq - 1 - p)):
            q_t = q_ref[0, pl.ds(pl.multiple_of(blk * tq, tq), tq), :].astype(F32).T
            half = lax.broadcasted_iota(jnp.int32, (LANES, tq), 0)
            qq_ref[2 * slot + qs, :LANES, :tq] = jnp.where(half < DIFF_DH, q_t, 0.0).astype(BF16)
            qq_ref[2 * slot + qs, :LANES, tq:] = jnp.where(half >= DIFF_DH, q_t, 0.0).astype(BF16)

    def diag_prefetch(p, slot, t):
        qs, blk = ((1, nq - 1 - p), (0, p))[t]
        diag_logits(2 * slot + qs, blk, bufs[t])

    def pair(p, carry):
        slot = p % 2
        q_blk = (p, nq - 1 - p)
        for qs in range(2):
            m_ref[qs] = jnp.full((1, 2 * tq), -jnp.inf, F32)
            acc_ref[qs] = jnp.zeros((DIFF_DV + SUM_ROWS, 2 * tq), F32)
        items = [(1, q_blk[1], q_blk[1], True), (0, q_blk[0], q_blk[0], True)]
        for u in range(nq - 1):
            late = u < q_blk[1]
            items.append((jnp.where(late, 1, 0), jnp.where(late, q_blk[1], q_blk[0]),
                          jnp.where(late, u, u - q_blk[1]), False))
        col_max = {}
        p_next = jnp.minimum(p + 1, nq // 2 - 1)
        for t, (qs, qi, j, masked) in enumerate(items):
            ta = t + ahead
            if ta < len(items):
                col_max[ta] = logits(2 * slot + items[ta][0], items[ta][2], bufs[ta % len(bufs)])
            elif ta == len(items):
                build_queries(p_next, 1 - slot)
                diag_prefetch(p_next, 1 - slot, 0)
            else:
                diag_prefetch(p_next, 1 - slot, 1)
            if masked:
                diag_softmax_accumulate(qs, j, bufs[t % len(bufs)])
            else:
                softmax_accumulate(qs, qi, j, bufs[t % len(bufs)], col_max.pop(t))
        for qs in range(2):
            out = acc_ref[qs, :DIFF_DV, :] * (1.0 / acc_ref[qs, DIFF_DV:DIFF_DV + 1, :])
            y_t = out[:, :tq] - lam * out[:, tq:]
            y_t = y_t * lax.rsqrt(jnp.mean(y_t * y_t, axis=0, keepdims=True) + EPS)
            rows = pl.ds(pl.multiple_of(q_blk[qs] * tq, tq), tq)
            o_ref[0, rows, :] = (y_t.T * hn_ref[...] * (1.0 - lam_init)).astype(BF16)
        return carry

    build_queries(0, 0)
    diag_prefetch(0, 0, 0)
    diag_prefetch(0, 0, 1)
    lax.fori_loop(0, nq // 2, pair, 0)


def _diff_attn(dq, dk, dv, lam_vecs, head_norm, lam_init):
    bsz, s, _ = dq.shape
    tq = ATTN_BLOCK
    slopes = jnp.asarray(
        [2.0 ** (-ALIBI_MAX_BIAS * (i + 1) / DIFF_HEADS) for i in range(DIFF_HEADS)], F32
    ).reshape(DIFF_HEADS, 1, 1)
    pos = np.arange(s) % tq
    feat = np.zeros((s, LANES), np.float32)
    feat[:, 0:3] = (pos - pos % 16)[:, None]
    feat[:, 3:6] = (pos % 16)[:, None]
    return pl.pallas_call(
        functools.partial(_attn_kernel, tq=tq, lam_init=lam_init),
        out_shape=jax.ShapeDtypeStruct((bsz, s, DIFF_HEADS * DIFF_DV), BF16),
        grid=(bsz, DIFF_HEADS),
        in_specs=[
            pl.BlockSpec((1, s, LANES), lambda b, h: (b, 0, h)),
            pl.BlockSpec((1, s, LANES), lambda b, h: (b, 0, h)),
            pl.BlockSpec((1, s, DIFF_DV), lambda b, h: (b, 0, h)),
            _resident((s, LANES)),
            pl.BlockSpec((1, 1, 1), lambda b, h: (h, 0, 0)),
            _resident(lam_vecs.shape),
            _resident((1, DIFF_DV)),
        ],
        out_specs=pl.BlockSpec((1, s, DIFF_DV), lambda b, h: (b, 0, h)),
        scratch_shapes=[
            pltpu.VMEM((s, 2 * LANES), BF16),
            pltpu.VMEM((4, 2 * LANES, 2 * tq), BF16),
            pltpu.VMEM((s // tq, DIFF_DV + SUM_ROWS, tq), BF16),
            pltpu.VMEM((tq, 2 * tq), F32),
            pltpu.VMEM((tq, 2 * tq), F32),
            pltpu.VMEM((tq, 2 * tq), F32),
            pltpu.VMEM((2, 1, 2 * tq), F32),
            pltpu.VMEM((2, DIFF_DV + SUM_ROWS, 2 * tq), F32),
        ],
        compiler_params=_params(("parallel", "parallel")),
        name="diff_attn",
    )(dq, dk, dv, jnp.asarray(feat, BF16), slopes, lam_vecs, head_norm.reshape(1, DIFF_DV))


def _merge_kernel(h_ref, za_ref, zb_ref, sg_ref, gt_ref, wa_ref, wb_ref, wo_ref, o_ref):
    d = h_ref.shape[-1]
    ya = jnp.dot(za_ref[0], wa_ref[...].astype(BF16), preferred_element_type=F32)
    yb = jnp.dot(zb_ref[0], wb_ref[...].astype(BF16), preferred_element_type=F32)
    mix = sg_ref[0, :, :d].astype(F32) * ya + sg_ref[0, :, d:].astype(F32) * yb
    m = jnp.dot(mix.astype(BF16), wo_ref[...].astype(BF16), preferred_element_type=F32)
    o_ref[0] = h_ref[0] + gt_ref[0] * m


def _merge(h, za, zb, sg, gt, w_a, w_b, w_o):
    bsz, s, d = h.shape
    tm = MERGE_ROWS
    vec = pl.BlockSpec((1, 1, d), lambda b, i: (b, 0, 0))

    def rows(w):
        return pl.BlockSpec((1, tm, w), lambda b, i: (b, i, 0))

    return pl.pallas_call(
        _merge_kernel,
        out_shape=jax.ShapeDtypeStruct((bsz, s, d), F32),
        grid=(bsz, s // tm),
        in_specs=[rows(d), rows(za.shape[-1]), rows(zb.shape[-1]), rows(2 * d), vec,
                  _resident(w_a.shape), _resident(w_b.shape), _resident(w_o.shape)],
        out_specs=rows(d),
        compiler_params=_params(("parallel", "parallel")),
        name="merge",
    )(h, za, zb, sg, gt, w_a, w_b, w_o)


def kernel(x, c, w_ada, b_ada, ffn1_norm, ffn1_w_in, ffn1_w_out, mix_norm, w_in, gla_alpha_w2, gla_alpha_b, gla_head_norm, diff_lq1, diff_lk1, diff_lq2, diff_lk2, diff_head_norm, w_branch_a, w_branch_b, w_out, ffn2_norm, ffn2_w_in, ffn2_w_out, final_norm):
    depth = w_ada.shape[0]
    bsz, s, d = x.shape
    assert bsz % GLA_BATCHES == 0 and s % (GLA_STEP_CHUNKS * GLA_CHUNK) == 0
    assert s % MERGE_ROWS == 0 and s % FFN_ROWS == 0 and s % PROJ_ROWS == 0
    assert (s // ATTN_BLOCK) % 2 == 0 and (N_MOD * d) % (ADALN_STEPS * LANES) == 0
    h = x
    for l in range(depth):
        lam_init = 0.8 - 0.6 * math.exp(-0.3 * l)
        mod = _adaln(c, w_ada[l], b_ada[l])
        sh1, sc1, gt1, sh2, sc2, gt2, sh3, sc3, gt3 = [
            mod[:, i * d:(i + 1) * d].reshape(bsz, 1, d) for i in range(N_MOD)]
        last = l == depth - 1
        h = _ffn(h, sh1, sc1, gt1, ffn1_norm[l], ffn1_w_in[l], ffn1_w_out[l], final_norm,
                 final_norm=False)
        w_t = jnp.swapaxes(w_in, 1, 2)[l]
        gq, gk, gv, gr, g, dq, dk, dv, sg = _mixer_proj(
            h, sh2, sc2, mix_norm[l], w_t, gla_alpha_w2[l], gla_alpha_b[l])
        za = _gla(gq, gk, gv, gr, g, gla_head_norm[l])
        lam_vecs = jnp.stack([diff_lq1[l], diff_lk1[l], diff_lq2[l], diff_lk2[l]])
        zb = _diff_attn(dq, dk, dv, lam_vecs, diff_head_norm[l], lam_init)
        h = _merge(h, za, zb, sg, gt2, w_branch_a[l], w_branch_b[l], w_out[l])
        h = _ffn(h, sh3, sc3, gt3, ffn2_norm[l], ffn2_w_in[l], ffn2_w_out[l], final_norm,
                 final_norm=last)
    return h
```

```python
import functools
import itertools
import math

import jax
import jax.numpy as jnp
import numpy as np
from jax import lax
from jax.experimental import pallas as pl
from jax.experimental.pallas import tpu as pltpu

F32 = jnp.float32
BF16 = jnp.bfloat16

EPS = 1e-6
GLA_HEADS = 4
GLA_DK = 64
GLA_DV = 128
GLA_RANK = 16
GLA_TAU = 16.0
DIFF_HEADS = 4
DIFF_DH = 64
DIFF_DV = 128
ALIBI_MAX_BIAS = 8.0
N_MOD = 9

LOG2E = math.log2(math.e)
LANES = 128
MXU_COLS = 256
VMEM_LIMIT = 56 * 1024 * 1024

ADALN_STEPS = 8
FFN_ROWS = 512
FFN_HIDDEN_CHUNK = MXU_COLS
NEXT_PIECES = 8
PROJ_ROWS = 512
GATE_CHUNK = 512
MERGE_ROWS = 1024
GLA_CHUNK = 128
GLA_SUB = 16
GLA_BATCHES = 4
GLA_STEP_CHUNKS = 4
ATTN_BLOCK = 512
EXP2_CLAMP = 115.0
SUM_ROWS = 16


def _params(sem):
    return pltpu.CompilerParams(dimension_semantics=sem, vmem_limit_bytes=VMEM_LIMIT)


def _resident(shape):
    nd = len(shape)
    return pl.BlockSpec(shape, lambda *_: (0,) * nd, pipeline_mode=pl.Buffered(1))


def _rmsnorm(x, g):
    return x * lax.rsqrt(jnp.mean(x * x, axis=-1, keepdims=True) + EPS) * g


def _next_input_pieces(xn_ref, un_ref, norm_fn, step):
    piece = un_ref.shape[0] // NEXT_PIECES
    zero = jnp.minimum(step, 0).astype(F32)

    def prepare(i):
        rows = slice(i * piece, (i + 1) * piece)
        un = norm_fn(xn_ref[0, rows, :])
        un_ref[rows, :] = un
        return jnp.sum(un.astype(F32), keepdims=True) * zero

    return prepare


def _next_rows(bsz, nt, tm, d):
    def index(b, i):
        nxt = jnp.minimum(b * nt + i + 1, bsz * nt - 1)
        return nxt // nt, nxt % nt, 0
    return pl.BlockSpec((1, tm, d), index)


def _adaln_kernel(c_ref, w_ref, b_ref, o_ref):
    c = c_ref[...]
    ca = (c * jax.nn.sigmoid(c)).astype(BF16)
    o_ref[...] = jnp.dot(ca, w_ref[...].astype(BF16), preferred_element_type=F32) + b_ref[...]


def _adaln(c, w_ada, b_ada):
    bsz, d = c.shape
    n = w_ada.shape[1]
    tn = n // ADALN_STEPS
    return pl.pallas_call(
        _adaln_kernel,
        out_shape=jax.ShapeDtypeStruct((bsz, n), F32),
        grid=(n // tn,),
        in_specs=[
            pl.BlockSpec((bsz, d), lambda j: (0, 0)),
            pl.BlockSpec((d, tn), lambda j: (0, j)),
            pl.BlockSpec((1, tn), lambda j: (0, j)),
        ],
        out_specs=pl.BlockSpec((bsz, tn), lambda j: (0, j)),
        compiler_params=_params(("arbitrary",)),
        name="adaln",
    )(c, w_ada, b_ada.reshape(1, n))


def _ffn_kernel(x_ref, xn_ref, sh_ref, sc_ref, gt_ref, nw_ref, win_hbm, wout_hbm, fn_ref, o_ref,
                u_ref, un_ref, win_ref, wout_ref, sem_ref, *, d_ff, tf, final_norm):
    n_chunk = d_ff // tf
    acc_ref = o_ref.at[0]
    step = pl.program_id(0) * pl.num_programs(1) + pl.program_id(1)
    last_step = pl.num_programs(0) * pl.num_programs(1) - 1

    def normed(x, batch):
        return (_rmsnorm(x, nw_ref[...]) * (1.0 + sc_ref[batch]) + sh_ref[batch]).astype(BF16)

    def weight_copies(i):
        cols = pl.ds(i * tf, tf)
        up = pl.ds(d_ff + i * tf, tf)
        return (pltpu.make_async_copy(win_hbm.at[:, cols], win_ref.at[:, cols], sem_ref.at[0, i]),
                pltpu.make_async_copy(win_hbm.at[:, up], win_ref.at[:, up], sem_ref.at[1, i]),
                pltpu.make_async_copy(wout_hbm.at[cols, :], wout_ref.at[cols, :], sem_ref.at[2, i]))

    def tile(is_first):
        if is_first:
            un_ref[...] = normed(x_ref[0], pl.program_id(0))
        u_ref[...] = un_ref[...]
        u = u_ref[...]
        next_batch = jnp.minimum(step + 1, last_step) // pl.num_programs(1)
        prepare = _next_input_pieces(xn_ref, un_ref, lambda x: normed(x, next_batch), step)
        anchor = None
        for i in range(n_chunk):
            if is_first:
                for copy in weight_copies(i):
                    copy.wait()
            cols = slice(i * tf, (i + 1) * tf)
            up = slice(d_ff + i * tf, d_ff + (i + 1) * tf)
            hg = jnp.dot(u, win_ref[:, cols].astype(BF16), preferred_element_type=F32)
            if anchor is not None:
                hg = hg + anchor
            hu = jnp.dot(u, win_ref[:, up].astype(BF16), preferred_element_type=F32)
            act = (hg * jax.nn.sigmoid(hg) * hu).astype(BF16)
            part = jnp.dot(act, wout_ref[cols, :].astype(BF16), preferred_element_type=F32)
            if i == 0:
                acc_ref[...] = part
            else:
                acc_ref[...] += part
            anchor = prepare(i) if i < NEXT_PIECES else None
        h = x_ref[0] + (0.5 * gt_ref[0]) * acc_ref[...]
        if final_norm:
            h = _rmsnorm(h, fn_ref[...])
        o_ref[0] = h

    first = (pl.program_id(0) == 0) & (pl.program_id(1) == 0)

    @pl.when(first)
    def _():
        for i in range(n_chunk):
            for copy in weight_copies(i):
                copy.start()
        tile(True)

    @pl.when(jnp.logical_not(first))
    def _():
        tile(False)


def _ffn(x, sh, sc, gt, nw, w_in, w_out, fn, *, final_norm):
    bsz, s, d = x.shape
    d_ff = w_out.shape[0]
    tm, tf = FFN_ROWS, FFN_HIDDEN_CHUNK
    nt = s // tm
    vec = pl.BlockSpec((1, 1, d), lambda b, i: (b, 0, 0))
    row = pl.BlockSpec((1, tm, d), lambda b, i: (b, i, 0))
    hbm = pl.BlockSpec(memory_space=pl.ANY)
    return pl.pallas_call(
        functools.partial(_ffn_kernel, d_ff=d_ff, tf=tf, final_norm=final_norm),
        out_shape=jax.ShapeDtypeStruct((bsz, s, d), F32),
        grid=(bsz, nt),
        in_specs=[row, _next_rows(bsz, nt, tm, d), _resident(sh.shape), _resident(sc.shape), vec,
                  _resident((1, d)), hbm, hbm, _resident((1, d))],
        out_specs=row,
        scratch_shapes=[pltpu.VMEM((tm, d), BF16), pltpu.VMEM((tm, d), BF16),
                        pltpu.VMEM((d, 2 * d_ff), F32), pltpu.VMEM((d_ff, d), F32),
                        pltpu.SemaphoreType.DMA((3, d_ff // tf))],
        compiler_params=_params(("arbitrary", "arbitrary")),
        name="ffn_final" if final_norm else "ffn",
    )(x, x, sh, sc, gt, nw.reshape(1, d), w_in, w_out, fn.reshape(1, d))


def _proj_kernel(x_ref, xn_ref, sh_ref, sc_ref, nw_ref, wt_ref, w2_ref, b2_ref,
                 gq_ref, gk_ref, gv_ref, gr_ref, g_ref, dq_ref, dk_ref, dv_ref, sg_ref,
                 u_ref, un_ref):
    kw = GLA_HEADS * GLA_DK
    vw = GLA_HEADS * GLA_DV
    qw = DIFF_HEADS * 2 * DIFF_DH
    dvw = DIFF_HEADS * DIFF_DV
    step = pl.program_id(0) * pl.num_programs(1) + pl.program_id(1)
    last_step = pl.num_programs(0) * pl.num_programs(1) - 1

    def normed(x, batch):
        return (_rmsnorm(x, nw_ref[...]) * (1.0 + sc_ref[batch]) + sh_ref[batch]).astype(BF16)

    @pl.when(step == 0)
    def _():
        un_ref[...] = normed(x_ref[0], 0)

    u_ref[...] = un_ref[...]
    u = u_ref[...]
    next_batch = jnp.minimum(step + 1, last_step) // pl.num_programs(1)
    prepare = _next_input_pieces(xn_ref, un_ref, lambda x: normed(x, next_batch), step)
    pending = {"count": 0, "anchor": None}

    def seg(start, size):
        out = lax.dot_general(u, wt_ref[start:start + size, :].astype(BF16),
                              (((1,), (1,)), ((), ())), preferred_element_type=F32)
        if pending["anchor"] is not None:
            out = out + pending["anchor"]
        pending["anchor"] = prepare(pending["count"]) if pending["count"] < NEXT_PIECES else None
        pending["count"] += 1
        return out

    off = 0
    gq_ref[0] = (seg(off, kw) * (GLA_DK ** -0.5)).astype(BF16)
    off += kw
    gk_ref[0] = seg(off, kw).astype(BF16)
    off += kw
    gv_ref[0] = seg(off, vw).astype(BF16)
    off += vw
    r = seg(off, vw)
    gr_ref[0] = (r * jax.nn.sigmoid(r)).astype(BF16)
    off += vw
    a_low = seg(off, GLA_RANK).astype(BF16)
    z = jnp.dot(a_low, w2_ref[...].astype(BF16), preferred_element_type=F32) + b2_ref[...]
    g_ref[0] = (jnp.minimum(z, 0.0) - jnp.log1p(jnp.exp(-jnp.abs(z)))) * (LOG2E / GLA_TAU)
    off += GLA_RANK
    dq_ref[0] = (seg(off, qw) * (DIFF_DH ** -0.5 * LOG2E)).astype(BF16)
    off += qw
    dk_ref[0] = seg(off, qw).astype(BF16)
    off += qw
    dv_ref[0] = seg(off, dvw).astype(BF16)
    off += dvw
    for j in range(sg_ref.shape[-1] // GATE_CHUNK):
        gate = jax.nn.sigmoid(seg(off + j * GATE_CHUNK, GATE_CHUNK))
        sg_ref[0, :, j * GATE_CHUNK:(j + 1) * GATE_CHUNK] = gate.astype(BF16)


def _mixer_proj(h, sh, sc, nw, w_t, w2, b2):
    bsz, s, d = h.shape
    tm = PROJ_ROWS
    kw = GLA_HEADS * GLA_DK
    vw = GLA_HEADS * GLA_DV
    qw = DIFF_HEADS * 2 * DIFF_DH
    dvw = DIFF_HEADS * DIFF_DV
    n_gate = 2 * d
    vec = pl.BlockSpec((1, 1, d), lambda b, i: (b, 0, 0))

    def rows(w):
        return pl.BlockSpec((1, tm, w), lambda b, i: (b, i, 0))

    widths = [(kw, BF16), (kw, BF16), (vw, BF16), (vw, BF16), (kw, F32),
              (qw, BF16), (qw, BF16), (dvw, BF16), (n_gate, BF16)]
    return pl.pallas_call(
        _proj_kernel,
        out_shape=[jax.ShapeDtypeStruct((bsz, s, w), dt) for w, dt in widths],
        grid=(bsz, s // tm),
        in_specs=[rows(d), _next_rows(bsz, s // tm, tm, d), _resident(sh.shape),
                  _resident(sc.shape), _resident((1, d)), _resident(w_t.shape),
                  _resident(w2.shape), _resident((1, kw))],
        out_specs=[rows(w) for w, _ in widths],
        scratch_shapes=[pltpu.VMEM((tm, d), BF16), pltpu.VMEM((tm, d), BF16)],
        compiler_params=_params(("arbitrary", "arbitrary")),
        name="mixer_proj",
    )(h, h, sh, sc, nw.reshape(1, d), w_t, w2, b2.reshape(1, kw))


def _gla_kernel(q_ref, k_ref, v_ref, r_ref, g_ref, hn_ref, o_ref, st_ref, inter_ref, *, chunk, sub):
    @pl.when(pl.program_id(1) == 0)
    def _():
        st_ref[...] = jnp.zeros_like(st_ref)

    nsub = chunk // sub
    row = lax.broadcasted_iota(jnp.int32, (chunk, chunk), 0)
    col = lax.broadcasted_iota(jnp.int32, (chunk, chunk), 1)
    causal = col <= row
    cum_mat = jnp.where(causal, 1.0, 0.0).astype(BF16)
    row_blk = lax.broadcasted_iota(jnp.int32, (chunk, LANES), 0) // sub
    lane_head = lax.broadcasted_iota(jnp.int32, (chunk, LANES), 1) // GLA_DK

    def cumsum(ck, bi, pair):
        return prefix_sums(g_ref[bi, ck * chunk:(ck + 1) * chunk, pair * LANES:(pair + 1) * LANES])

    def prefix_sums(g):
        g1 = g.astype(BF16)
        e1 = g - g1.astype(F32)
        g2 = e1.astype(BF16)
        g3 = (e1 - g2.astype(F32)).astype(BF16)
        cs = jnp.dot(cum_mat, jnp.concatenate([g1, g2, g3], axis=1), preferred_element_type=F32)
        return cs[:, :LANES] + cs[:, LANES:2 * LANES] + cs[:, 2 * LANES:]

    def decay(ck, bi, pair, b):
        lanes = slice(pair * LANES, (pair + 1) * LANES)
        b_last = b[chunk - 1:chunk]
        rows = slice(ck * chunk, (ck + 1) * chunk)
        q = q_ref[bi, rows, lanes].astype(F32)
        k = k_ref[bi, rows, lanes].astype(F32)
        q_cat, w = [], []
        for j in range(nsub):
            ref_b = b[j * sub - 1:j * sub] if j else jnp.zeros_like(b_last)
            q_cat.append((q * jnp.exp2(jnp.minimum(b - ref_b, 0.0))).astype(BF16))
            w.append(b[j * sub:(j + 1) * sub] - ref_b)
        w = jnp.concatenate(w, axis=0)
        return dict(
            q_cat=jnp.concatenate(q_cat, axis=1),
            q_state=(q * jnp.exp2(b)).astype(BF16),
            k_state=k * jnp.exp2(b_last - b),
            k_hat=k * jnp.exp2(jnp.minimum(-w, EXP2_CLAMP)),
            chunk_decay=jnp.exp2(b_last),
            key_exponent=jnp.max(-w, axis=0, keepdims=True))

    def scores(d):
        out = []
        for hh in range(2):
            kh = jnp.where(lane_head == hh, d["k_hat"], 0.0).astype(BF16)
            k_cat = jnp.concatenate(
                [jnp.where(row_blk == j, kh, jnp.zeros_like(kh)) for j in range(nsub)], axis=1)
            out.append(lax.dot_general(d["q_cat"], k_cat, (((1,), (1,)), ((), ())),
                                       preferred_element_type=F32))
        return out

    def finish(ck, bi, pair, d, sc):
        rows = slice(ck * chunk, (ck + 1) * chunk)
        for hh in range(2):
            head = pair * 2 + hh
            cols = slice(head * GLA_DV, (head + 1) * GLA_DV)
            intra = jnp.where(causal, sc[hh], 0.0).astype(BF16)
            vh = v_ref[bi, rows, cols]
            state_t = st_ref[bi * GLA_HEADS + head]
            o_inter = lax.dot_general(d["q_state"], state_t.astype(BF16), (((1,), (1,)), ((), ())),
                                      preferred_element_type=F32)
            inter_ref[(ck * n_batch + bi) * GLA_HEADS + head] = o_inter
            o = jnp.dot(intra, vh, preferred_element_type=F32) + o_inter
            ks = jnp.where(lane_head == hh, d["k_state"], 0.0).astype(BF16)
            kv_t = lax.dot_general(vh, ks, (((0,), (0,)), ((), ())), preferred_element_type=F32)
            st_ref[bi * GLA_HEADS + head] = state_t * d["chunk_decay"] + kv_t
            y = _rmsnorm(o, hn_ref[...]) * r_ref[bi, rows, cols].astype(F32)
            o_ref[bi, rows, cols] = y.astype(BF16)

    n_batch = q_ref.shape[0]
    n_chunk = q_ref.shape[1] // chunk
    per_chunk = list(itertools.product(range(n_batch), range(GLA_HEADS // 2)))
    chains = [(ck, bi, pair) for ck in range(n_chunk) for bi, pair in per_chunk]
    cums = {t: cumsum(*chains[t]) for t in range(len(per_chunk))}
    decayed, scored = {}, {}
    key_exponent = jnp.zeros((1, LANES), F32)
    for t in range(len(chains) + 1):
        if t + len(per_chunk) < len(chains):
            cums[t + len(per_chunk)] = cumsum(*chains[t + len(per_chunk)])
        if t < len(chains):
            decayed[t] = decay(*chains[t], cums.pop(t))
            key_exponent = jnp.maximum(key_exponent, decayed[t]["key_exponent"])
        if t >= 1:
            finish(*chains[t - 1], decayed.pop(t - 1), scored.pop(t - 1))
        if t < len(chains):
            scored[t] = scores(decayed[t])

    @pl.when(jnp.max(key_exponent) > EXP2_CLAMP)
    def _():
        def redo(i, carry):
            ck, bi = i // n_batch, i % n_batch
            rows = pl.ds(pl.multiple_of(ck * chunk, chunk), chunk)
            for pair in range(GLA_HEADS // 2):
                lanes = slice(pair * LANES, (pair + 1) * LANES)
                q = q_ref[bi, rows, lanes].astype(F32)
                k = k_ref[bi, rows, lanes].astype(F32)
                b = prefix_sums(g_ref[bi, rows, lanes])
                q_cat, to_end = [], []
                for j in range(nsub):
                    end_b = b[(j + 1) * sub - 1:(j + 1) * sub]
                    q_cat.append((q * jnp.exp2(jnp.minimum(b - end_b, 0.0))).astype(BF16))
                    to_end.append(end_b - b[j * sub:(j + 1) * sub])
                q_cat = jnp.concatenate(q_cat, axis=1)
                k_end = k * jnp.exp2(jnp.concatenate(to_end, axis=0))
                for hh in range(2):
                    head = pair * 2 + hh
                    cols = slice(head * GLA_DV, (head + 1) * GLA_DV)
                    vh = v_ref[bi, rows, cols]
                    kh = jnp.where(lane_head == hh, k_end, 0.0).astype(BF16)
                    k_cat = jnp.concatenate(
                        [jnp.where(row_blk == j, kh, jnp.zeros_like(kh)) for j in range(nsub)],
                        axis=1)
                    cross = lax.dot_general(q_cat, k_cat, (((1,), (1,)), ((), ())),
                                            preferred_element_type=F32)
                    cross = jnp.where(row // sub > col // sub, cross, 0.0).astype(BF16)
                    o = jnp.dot(cross, vh, preferred_element_type=F32)
                    v32 = vh.astype(F32)
                    in_blk = lax.broadcasted_iota(jnp.int32, (chunk, LANES), 0) % sub
                    for off in range(sub):
                        k_o = pltpu.roll(k, off, 0) if off else k
                        b_o = pltpu.roll(b, off, 0) if off else b
                        v_o = pltpu.roll(v32, off, 0) if off else v32
                        ok = (in_blk >= off) & (lane_head == hh)
                        term = jnp.where(ok, q * k_o * jnp.exp2(jnp.minimum(b - b_o, 0.0)), 0.0)
                        o += jnp.sum(term, axis=1, keepdims=True) * v_o
                    o += inter_ref[(ck * n_batch + bi) * GLA_HEADS + head]
                    y = _rmsnorm(o, hn_ref[...]) * r_ref[bi, rows, cols].astype(F32)
                    o_ref[bi, rows, cols] = y.astype(BF16)
            return carry

        lax.fori_loop(0, n_chunk * n_batch, redo, 0)


def _gla(gq, gk, gv, gr, g, head_norm):
    bsz, s, kw = gq.shape
    vw = gv.shape[-1]
    nb, chunk = GLA_BATCHES, GLA_CHUNK
    step_rows = GLA_STEP_CHUNKS * chunk

    def rows(w):
        return pl.BlockSpec((nb, step_rows, w), lambda b, c: (b, c, 0))

    return pl.pallas_call(
        functools.partial(_gla_kernel, chunk=chunk, sub=GLA_SUB),
        out_shape=jax.ShapeDtypeStruct((bsz, s, vw), BF16),
        grid=(bsz // nb, s // step_rows),
        in_specs=[rows(kw), rows(kw), rows(vw), rows(vw), rows(kw), _resident((1, GLA_DV))],
        out_specs=rows(vw),
        scratch_shapes=[pltpu.VMEM((nb * GLA_HEADS, GLA_DV, LANES), F32),
                        pltpu.VMEM((GLA_STEP_CHUNKS * nb * GLA_HEADS, chunk, GLA_DV), F32)],
        compiler_params=_params(("parallel", "arbitrary")),
        name="gla",
    )(gq, gk, gv, gr, g, head_norm.reshape(1, GLA_DV))


def _attn_kernel(q_ref, k_ref, v_ref, pos_ref, slope_ref, lam_ref, hn_ref, o_ref,
                 kk_ref, qq_ref, vt_ref, s0_ref, s1_ref, s2_ref, m_ref, acc_ref, *, tq, lam_init):
    seq = k_ref.shape[1]
    nq = seq // tq
    hq = tq // 2
    c = slope_ref[0] * LOG2E
    c1 = c.astype(BF16).astype(F32)
    c2 = (c - c1).astype(BF16).astype(F32)
    c3 = c - c1 - c2

    kk_ref[:, :LANES] = k_ref[0]
    kk_ref[:, LANES:] = pos_ref[...]
    sub = lax.broadcasted_iota(jnp.int32, (LANES, 2 * tq), 0)
    cf = jnp.where((sub == 0) | (sub == 3), c1, jnp.where((sub == 1) | (sub == 4), c2, c3))
    slope_rows = jnp.where(sub < 6, cf, 0.0).astype(BF16)
    for i in range(qq_ref.shape[0]):
        qq_ref[i, LANES:, :] = slope_rows
    for jb in range(nq):
        vt_ref[jb, :DIFF_DV, :] = v_ref[0, jb * tq:(jb + 1) * tq, :].astype(F32).T.astype(BF16)
        vt_ref[jb, DIFF_DV:, :] = jnp.ones((SUM_ROWS, tq), BF16)
    lam = lam_ref[...]
    lam = (jnp.exp(jnp.sum(lam[0:1] * lam[1:2], axis=-1, keepdims=True))
           - jnp.exp(jnp.sum(lam[2:3] * lam[3:4], axis=-1, keepdims=True)) + lam_init)

    def logits(qq_slot, j, s_blk):
        start = pl.multiple_of(j * tq, tq)
        s = jnp.dot(kk_ref[pl.ds(start, tq), :], qq_ref[qq_slot], preferred_element_type=F32)
        s_blk[...] = s
        return jnp.max(s.reshape(tq // 8, 8, 2 * tq), axis=0)

    def softmax_accumulate(qs, qi, j, s_blk, mx):
        m_old = m_ref[qs]
        shift = c * jnp.asarray((qi - j) * tq, F32)
        m_new = jnp.maximum(m_old, jnp.max(mx, axis=0, keepdims=True) - shift)
        alpha = jnp.exp2(m_old - m_new)
        m_ref[qs] = m_new
        m_shift = m_new + shift
        probs = jnp.concatenate([jnp.exp2(s_blk[:hq, :] - m_shift).astype(BF16),
                                 jnp.exp2(s_blk[hq:, :] - m_shift).astype(BF16)], axis=0)
        acc_ref[qs] = alpha * acc_ref[qs] + jnp.dot(vt_ref[j], probs, preferred_element_type=F32)

    def late_half(x):
        return jnp.concatenate([x[..., hq:tq], x[..., tq + hq:]], axis=-1)

    def diag_logits(qq_slot, j, s_blk):
        start = pl.multiple_of(j * tq, tq)
        s_blk[:hq, :] = jnp.dot(kk_ref[pl.ds(start, hq), :], qq_ref[qq_slot],
                                preferred_element_type=F32)
        s_blk[hq:, :tq] = jnp.dot(kk_ref[pl.ds(start + hq, hq), :], late_half(qq_ref[qq_slot]),
                                  preferred_element_type=F32)

    def diag_softmax_accumulate(qs, j, s_blk):
        m_old = m_ref[qs]
        key = lax.broadcasted_iota(jnp.int32, (hq, 2 * tq), 0)
        qry = lax.broadcasted_iota(jnp.int32, (hq, 2 * tq), 1) & (tq - 1)
        early = jnp.where(key <= qry, s_blk[:hq, :], -jnp.inf)
        key_l = lax.broadcasted_iota(jnp.int32, (hq, tq), 0)
        qry_l = lax.broadcasted_iota(jnp.int32, (hq, tq), 1) & (hq - 1)
        late = jnp.where(key_l <= qry_l, s_blk[hq:, :tq], -jnp.inf)
        mx_e = jnp.max(jnp.max(early.reshape(hq // 8, 8, 2 * tq), axis=0), axis=0, keepdims=True)
        mx_l = jnp.max(jnp.max(late.reshape(hq // 8, 8, tq), axis=0), axis=0, keepdims=True)
        mx = jnp.concatenate(
            [mx_e[:, :hq], jnp.maximum(mx_e[:, hq:tq], mx_l[:, :hq]),
             mx_e[:, tq:tq + hq], jnp.maximum(mx_e[:, tq + hq:], mx_l[:, hq:])], axis=1)
        m_new = jnp.maximum(m_old, mx)
        alpha = jnp.exp2(m_old - m_new)
        m_ref[qs] = m_new
        p_early = jnp.exp2(early - m_new).astype(BF16)
        p_late = jnp.exp2(late - late_half(m_new)).astype(BF16)
        vt = vt_ref[j]
        acc_ref[qs] = alpha * acc_ref[qs] + jnp.dot(vt[:, :hq], p_early,
                                                    preferred_element_type=F32)
        upd = jnp.dot(vt[:, hq:], p_late, preferred_element_type=F32)
        acc_ref[qs, :, hq:tq] += upd[:, :hq]
        acc_ref[qs, :, tq + hq:] += upd[:, hq:]

    bufs = (s0_ref, s1_ref, s2_ref)
    ahead = len(bufs) - 1
    assert (nq + 1) % len(bufs) == 0

    def build_queries(p, slot):
        for qs, blk in enumerate((p, nq - 1 - p)):
            q_t = q_ref[0, pl.ds(pl.multiple_of(blk * tq, tq), tq), :].astype(F32).T
            half = lax.broadcasted_iota(jnp.int32, (LANES, tq), 0)
            qq_ref[2 * slot + qs, :LANES, :tq] = jnp.where(half < DIFF_DH, q_t, 0.0).astype(BF16)
            qq_ref[2 * slot + qs, :LANES, tq:] = jnp.where(half >= DIFF_DH, q_t, 0.0).astype(BF16)

    def diag_prefetch(p, slot, t):
        qs, blk = ((1, nq - 1 - p), (0, p))[t]
        diag_logits(2 * slot + qs, blk, bufs[t])

    def pair(p, carry):
        slot = p % 2
        q_blk = (p, nq - 1 - p)
        for qs in range(2):
            m_ref[qs] = jnp.full((1, 2 * tq), -jnp.inf, F32)
            acc_ref[qs] = jnp.zeros((DIFF_DV + SUM_ROWS, 2 * tq), F32)
        items = [(1, q_blk[1], q_blk[1], True), (0, q_blk[0], q_blk[0], True)]
        for u in range(nq - 1):
            late = u < q_blk[1]
            items.append((jnp.where(late, 1, 0), jnp.where(late, q_blk[1], q_blk[0]),
                          jnp.where(late, u, u - q_blk[1]), False))
        col_max = {}
        p_next = jnp.minimum(p + 1, nq // 2 - 1)
        for t, (qs, qi, j, masked) in enumerate(items):
            ta = t + ahead
            if ta < len(items):
                col_max[ta] = logits(2 * slot + items[ta][0], items[ta][2], bufs[ta % len(bufs)])
            elif ta == len(items):
                build_queries(p_next, 1 - slot)
                diag_prefetch(p_next, 1 - slot, 0)
            else:
                diag_prefetch(p_next, 1 - slot, 1)
            if masked:
                diag_softmax_accumulate(qs, j, bufs[t % len(bufs)])
            else:
                softmax_accumulate(qs, qi, j, bufs[t % len(bufs)], col_max.pop(t))
        for qs in range(2):
            out = acc_ref[qs, :DIFF_DV, :] * (1.0 / acc_ref[qs, DIFF_DV:DIFF_DV + 1, :])
            y_t = out[:, :tq] - lam * out[:, tq:]
            y_t = y_t * lax.rsqrt(jnp.mean(y_t * y_t, axis=0, keepdims=True) + EPS)
            rows = pl.ds(pl.multiple_of(q_blk[qs] * tq, tq), tq)
            o_ref[0, rows, :] = (y_t.T * hn_ref[...] * (1.0 - lam_init)).astype(BF16)
        return carry

    build_queries(0, 0)
    diag_prefetch(0, 0, 0)
    diag_prefetch(0, 0, 1)
    lax.fori_loop(0, nq // 2, pair, 0)


def _diff_attn(dq, dk, dv, lam_vecs, head_norm, lam_init):
    bsz, s, _ = dq.shape
    tq = ATTN_BLOCK
    slopes = jnp.asarray(
        [2.0 ** (-ALIBI_MAX_BIAS * (i + 1) / DIFF_HEADS) for i in range(DIFF_HEADS)], F32
    ).reshape(DIFF_HEADS, 1, 1)
    pos = np.arange(s) % tq
    feat = np.zeros((s, LANES), np.float32)
    feat[:, 0:3] = (pos - pos % 16)[:, None]
    feat[:, 3:6] = (pos % 16)[:, None]
    return pl.pallas_call(
        functools.partial(_attn_kernel, tq=tq, lam_init=lam_init),
        out_shape=jax.ShapeDtypeStruct((bsz, s, DIFF_HEADS * DIFF_DV), BF16),
        grid=(bsz, DIFF_HEADS),
        in_specs=[
            pl.BlockSpec((1, s, LANES), lambda b, h: (b, 0, h)),
            pl.BlockSpec((1, s, LANES), lambda b, h: (b, 0, h)),
            pl.BlockSpec((1, s, DIFF_DV), lambda b, h: (b, 0, h)),
            _resident((s, LANES)),
            pl.BlockSpec((1, 1, 1), lambda b, h: (h, 0, 0)),
            _resident(lam_vecs.shape),
            _resident((1, DIFF_DV)),
        ],
        out_specs=pl.BlockSpec((1, s, DIFF_DV), lambda b, h: (b, 0, h)),
        scratch_shapes=[
            pltpu.VMEM((s, 2 * LANES), BF16),
            pltpu.VMEM((4, 2 * LANES, 2 * tq), BF16),
            pltpu.VMEM((s // tq, DIFF_DV + SUM_ROWS, tq), BF16),
            pltpu.VMEM((tq, 2 * tq), F32),
            pltpu.VMEM((tq, 2 * tq), F32),
            pltpu.VMEM((tq, 2 * tq), F32),
            pltpu.VMEM((2, 1, 2 * tq), F32),
            pltpu.VMEM((2, DIFF_DV + SUM_ROWS, 2 * tq), F32),
        ],
        compiler_params=_params(("parallel", "parallel")),
        name="diff_attn",
    )(dq, dk, dv, jnp.asarray(feat, BF16), slopes, lam_vecs, head_norm.reshape(1, DIFF_DV))


def _merge_kernel(h_ref, za_ref, zb_ref, sg_ref, gt_ref, wa_ref, wb_ref, wo_ref, o_ref):
    d = h_ref.shape[-1]
    ya = jnp.dot(za_ref[0], wa_ref[...].astype(BF16), preferred_element_type=F32)
    yb = jnp.dot(zb_ref[0], wb_ref[...].astype(BF16), preferred_element_type=F32)
    mix = sg_ref[0, :, :d].astype(F32) * ya + sg_ref[0, :, d:].astype(F32) * yb
    m = jnp.dot(mix.astype(BF16), wo_ref[...].astype(BF16), preferred_element_type=F32)
    o_ref[0] = h_ref[0] + gt_ref[0] * m


def _merge(h, za, zb, sg, gt, w_a, w_b, w_o):
    bsz, s, d = h.shape
    tm = MERGE_ROWS
    vec = pl.BlockSpec((1, 1, d), lambda b, i: (b, 0, 0))

    def rows(w):
        return pl.BlockSpec((1, tm, w), lambda b, i: (b, i, 0))

    return pl.pallas_call(
        _merge_kernel,
        out_shape=jax.ShapeDtypeStruct((bsz, s, d), F32),
        grid=(bsz, s // tm),
        in_specs=[rows(d), rows(za.shape[-1]), rows(zb.shape[-1]), rows(2 * d), vec,
                  _resident(w_a.shape), _resident(w_b.shape), _resident(w_o.shape)],
        out_specs=rows(d),
        compiler_params=_params(("parallel", "parallel")),
        name="merge",
    )(h, za, zb, sg, gt, w_a, w_b, w_o)


def kernel(x, c, w_ada, b_ada, ffn1_norm, ffn1_w_in, ffn1_w_out, mix_norm, w_in, gla_alpha_w2, gla_alpha_b, gla_head_norm, diff_lq1, diff_lk1, diff_lq2, diff_lk2, diff_head_norm, w_branch_a, w_branch_b, w_out, ffn2_norm, ffn2_w_in, ffn2_w_out, final_norm):
    depth = w_ada.shape[0]
    bsz, s, d = x.shape
    assert bsz % GLA_BATCHES == 0 and s % (GLA_STEP_CHUNKS * GLA_CHUNK) == 0
    assert s % MERGE_ROWS == 0 and s % FFN_ROWS == 0 and s % PROJ_ROWS == 0
    assert (s // ATTN_BLOCK) % 2 == 0 and (N_MOD * d) % (ADALN_STEPS * LANES) == 0
    h = x
    for l in range(depth):
        lam_init = 0.8 - 0.6 * math.exp(-0.3 * l)
        mod = _adaln(c, w_ada[l], b_ada[l])
        sh1, sc1, gt1, sh2, sc2, gt2, sh3, sc3, gt3 = [
            mod[:, i * d:(i + 1) * d].reshape(bsz, 1, d) for i in range(N_MOD)]
        last = l == depth - 1
        h = _ffn(h, sh1, sc1, gt1, ffn1_norm[l], ffn1_w_in[l], ffn1_w_out[l], final_norm,
                 final_norm=False)
        w_t = jnp.swapaxes(w_in, 1, 2)[l]
        gq, gk, gv, gr, g, dq, dk, dv, sg = _mixer_proj(
            h, sh2, sc2, mix_norm[l], w_t, gla_alpha_w2[l], gla_alpha_b[l])
        za = _gla(gq, gk, gv, gr, g, gla_head_norm[l])
        lam_vecs = jnp.stack([diff_lq1[l], diff_lk1[l], diff_lq2[l], diff_lk2[l]])
        zb = _diff_attn(dq, dk, dv, lam_vecs, diff_head_norm[l], lam_init)
        h = _merge(h, za, zb, sg, gt2, w_branch_a[l], w_branch_b[l], w_out[l])
        h = _ffn(h, sh3, sc3, gt3, ffn2_norm[l], ffn2_w_in[l], ffn2_w_out[l], final_norm,
                 final_norm=last)
    return h
```

```python
import functools
import itertools
import math

import jax
import jax.numpy as jnp
import numpy as np
from jax import lax
from jax.experimental import pallas as pl
from jax.experimental.pallas import tpu as pltpu

F32 = jnp.float32
BF16 = jnp.bfloat16

EPS = 1e-6
GLA_HEADS = 4
GLA_DK = 64
GLA_DV = 128
GLA_RANK = 16
GLA_TAU = 16.0
DIFF_HEADS = 4
DIFF_DH = 64
DIFF_DV = 128
ALIBI_MAX_BIAS = 8.0
N_MOD = 9

LOG2E = math.log2(math.e)
LANES = 128
MXU_COLS = 256
VMEM_LIMIT = 56 * 1024 * 1024

ADALN_STEPS = 8
FFN_ROWS = 512
FFN_HIDDEN_CHUNK = MXU_COLS
NEXT_PIECES = 8
PROJ_ROWS = 512
GATE_CHUNK = 512
MERGE_ROWS = 1024
GLA_CHUNK = 128
GLA_SUB = 16
GLA_BATCHES = 4
GLA_STEP_CHUNKS = 4
ATTN_BLOCK = 512
EXP2_CLAMP = 115.0
SUM_ROWS = 16


def _params(sem):
    return pltpu.CompilerParams(dimension_semantics=sem, vmem_limit_bytes=VMEM_LIMIT)


def _resident(shape):
    nd = len(shape)
    return pl.BlockSpec(shape, lambda *_: (0,) * nd, pipeline_mode=pl.Buffered(1))


def _rmsnorm(x, g):
    return x * lax.rsqrt(jnp.mean(x * x, axis=-1, keepdims=True) + EPS) * g


def _next_input_pieces(xn_ref, un_ref, norm_fn, step):
    piece = un_ref.shape[0] // NEXT_PIECES
    zero = jnp.minimum(step, 0).astype(F32)

    def prepare(i):
        rows = slice(i * piece, (i + 1) * piece)
        un = norm_fn(xn_ref[0, rows, :])
        un_ref[rows, :] = un
        return jnp.sum(un.astype(F32), keepdims=True) * zero

    return prepare


def _next_rows(bsz, nt, tm, d):
    def index(b, i):
        nxt = jnp.minimum(b * nt + i + 1, bsz * nt - 1)
        return nxt // nt, nxt % nt, 0
    return pl.BlockSpec((1, tm, d), index)


def _adaln_kernel(c_ref, w_ref, b_ref, o_ref):
    c = c_ref[...]
    ca = (c * jax.nn.sigmoid(c)).astype(BF16)
    o_ref[...] = jnp.dot(ca, w_ref[...].astype(BF16), preferred_element_type=F32) + b_ref[...]


def _adaln(c, w_ada, b_ada):
    bsz, d = c.shape
    n = w_ada.shape[1]
    tn = n // ADALN_STEPS
    return pl.pallas_call(
        _adaln_kernel,
        out_shape=jax.ShapeDtypeStruct((bsz, n), F32),
        grid=(n // tn,),
        in_specs=[
            pl.BlockSpec((bsz, d), lambda j: (0, 0)),
            pl.BlockSpec((d, tn), lambda j: (0, j)),
            pl.BlockSpec((1, tn), lambda j: (0, j)),
        ],
        out_specs=pl.BlockSpec((bsz, tn), lambda j: (0, j)),
        compiler_params=_params(("arbitrary",)),
        name="adaln",
    )(c, w_ada, b_ada.reshape(1, n))


def _ffn_kernel(x_ref, xn_ref, sh_ref, sc_ref, gt_ref, nw_ref, win_hbm, wout_hbm, fn_ref, o_ref,
                u_ref, un_ref, win_ref, wout_ref, sem_ref, *, d_ff, tf, final_norm):
    n_chunk = d_ff // tf
    acc_ref = o_ref.at[0]
    step = pl.program_id(0) * pl.num_programs(1) + pl.program_id(1)
    last_step = pl.num_programs(0) * pl.num_programs(1) - 1

    def normed(x, batch):
        return (_rmsnorm(x, nw_ref[...]) * (1.0 + sc_ref[batch]) + sh_ref[batch]).astype(BF16)

    def weight_copies(i):
        cols = pl.ds(i * tf, tf)
        up = pl.ds(d_ff + i * tf, tf)
        return (pltpu.make_async_copy(win_hbm.at[:, cols], win_ref.at[:, cols], sem_ref.at[0, i]),
                pltpu.make_async_copy(win_hbm.at[:, up], win_ref.at[:, up], sem_ref.at[1, i]),
                pltpu.make_async_copy(wout_hbm.at[cols, :], wout_ref.at[cols, :], sem_ref.at[2, i]))

    def tile(is_first):
        if is_first:
            un_ref[...] = normed(x_ref[0], pl.program_id(0))
        u_ref[...] = un_ref[...]
        u = u_ref[...]
        next_batch = jnp.minimum(step + 1, last_step) // pl.num_programs(1)
        prepare = _next_input_pieces(xn_ref, un_ref, lambda x: normed(x, next_batch), step)
        anchor = None
        for i in range(n_chunk):
            if is_first:
                for copy in weight_copies(i):
                    copy.wait()
            cols = slice(i * tf, (i + 1) * tf)
            up = slice(d_ff + i * tf, d_ff + (i + 1) * tf)
            hg = jnp.dot(u, win_ref[:, cols].astype(BF16), preferred_element_type=F32)
            if anchor is not None:
                hg = hg + anchor
            hu = jnp.dot(u, win_ref[:, up].astype(BF16), preferred_element_type=F32)
            act = (hg * jax.nn.sigmoid(hg) * hu).astype(BF16)
            part = jnp.dot(act, wout_ref[cols, :].astype(BF16), preferred_element_type=F32)
            if i == 0:
                acc_ref[...] = part
            else:
                acc_ref[...] += part
            anchor = prepare(i) if i < NEXT_PIECES else None
        h = x_ref[0] + (0.5 * gt_ref[0]) * acc_ref[...]
        if final_norm:
            h = _rmsnorm(h, fn_ref[...])
        o_ref[0] = h

    first = (pl.program_id(0) == 0) & (pl.program_id(1) == 0)

    @pl.when(first)
    def _():
        for i in range(n_chunk):
            for copy in weight_copies(i):
                copy.start()
        tile(True)

    @pl.when(jnp.logical_not(first))
    def _():
        tile(False)


def _ffn(x, sh, sc, gt, nw, w_in, w_out, fn, *, final_norm):
    bsz, s, d = x.shape
    d_ff = w_out.shape[0]
    tm, tf = FFN_ROWS, FFN_HIDDEN_CHUNK
    nt = s // tm
    vec = pl.BlockSpec((1, 1, d), lambda b, i: (b, 0, 0))
    row = pl.BlockSpec((1, tm, d), lambda b, i: (b, i, 0))
    hbm = pl.BlockSpec(memory_space=pl.ANY)
    return pl.pallas_call(
        functools.partial(_ffn_kernel, d_ff=d_ff, tf=tf, final_norm=final_norm),
        out_shape=jax.ShapeDtypeStruct((bsz, s, d), F32),
        grid=(bsz, nt),
        in_specs=[row, _next_rows(bsz, nt, tm, d), _resident(sh.shape), _resident(sc.shape), vec,
                  _resident((1, d)), hbm, hbm, _resident((1, d))],
        out_specs=row,
        scratch_shapes=[pltpu.VMEM((tm, d), BF16), pltpu.VMEM((tm, d), BF16),
                        pltpu.VMEM((d, 2 * d_ff), F32), pltpu.VMEM((d_ff, d), F32),
                        pltpu.SemaphoreType.DMA((3, d_ff // tf))],
        compiler_params=_params(("arbitrary", "arbitrary")),
        name="ffn_final" if final_norm else "ffn",
    )(x, x, sh, sc, gt, nw.reshape(1, d), w_in, w_out, fn.reshape(1, d))


def _proj_kernel(x_ref, xn_ref, sh_ref, sc_ref, nw_ref, wt_ref, w2_ref, b2_ref,
                 gq_ref, gk_ref, gv_ref, gr_ref, g_ref, dq_ref, dk_ref, dv_ref, sg_ref,
                 u_ref, un_ref):
    kw = GLA_HEADS * GLA_DK
    vw = GLA_HEADS * GLA_DV
    qw = DIFF_HEADS * 2 * DIFF_DH
    dvw = DIFF_HEADS * DIFF_DV
    step = pl.program_id(0) * pl.num_programs(1) + pl.program_id(1)
    last_step = pl.num_programs(0) * pl.num_programs(1) - 1

    def normed(x, batch):
        return (_rmsnorm(x, nw_ref[...]) * (1.0 + sc_ref[batch]) + sh_ref[batch]).astype(BF16)

    @pl.when(step == 0)
    def _():
        un_ref[...] = normed(x_ref[0], 0)

    u_ref[...] = un_ref[...]
    u = u_ref[...]
    next_batch = jnp.minimum(step + 1, last_step) // pl.num_programs(1)
    prepare = _next_input_pieces(xn_ref, un_ref, lambda x: normed(x, next_batch), step)
    pending = {"count": 0, "anchor": None}

    def seg(start, size):
        out = lax.dot_general(u, wt_ref[start:start + size, :].astype(BF16),
                              (((1,), (1,)), ((), ())), preferred_element_type=F32)
        if pending["anchor"] is not None:
            out = out + pending["anchor"]
        pending["anchor"] = prepare(pending["count"]) if pending["count"] < NEXT_PIECES else None
        pending["count"] += 1
        return out

    off = 0
    gq_ref[0] = (seg(off, kw) * (GLA_DK ** -0.5)).astype(BF16)
    off += kw
    gk_ref[0] = seg(off, kw).astype(BF16)
    off += kw
    gv_ref[0] = seg(off, vw).astype(BF16)
    off += vw
    r = seg(off, vw)
    gr_ref[0] = (r * jax.nn.sigmoid(r)).astype(BF16)
    off += vw
    a_low = seg(off, GLA_RANK).astype(BF16)
    z = jnp.dot(a_low, w2_ref[...].astype(BF16), preferred_element_type=F32) + b2_ref[...]
    g_ref[0] = (jnp.minimum(z, 0.0) - jnp.log1p(jnp.exp(-jnp.abs(z)))) * (LOG2E / GLA_TAU)
    off += GLA_RANK
    dq_ref[0] = (seg(off, qw) * (DIFF_DH ** -0.5 * LOG2E)).astype(BF16)
    off += qw
    dk_ref[0] = seg(off, qw).astype(BF16)
    off += qw
    dv_ref[0] = seg(off, dvw).astype(BF16)
    off += dvw
    for j in range(sg_ref.shape[-1] // GATE_CHUNK):
        gate = jax.nn.sigmoid(seg(off + j * GATE_CHUNK, GATE_CHUNK))
        sg_ref[0, :, j * GATE_CHUNK:(j + 1) * GATE_CHUNK] = gate.astype(BF16)


def _mixer_proj(h, sh, sc, nw, w_t, w2, b2):
    bsz, s, d = h.shape
    tm = PROJ_ROWS
    kw = GLA_HEADS * GLA_DK
    vw = GLA_HEADS * GLA_DV
    qw = DIFF_HEADS * 2 * DIFF_DH
    dvw = DIFF_HEADS * DIFF_DV
    n_gate = 2 * d
    vec = pl.BlockSpec((1, 1, d), lambda b, i: (b, 0, 0))

    def rows(w):
        return pl.BlockSpec((1, tm, w), lambda b, i: (b, i, 0))

    widths = [(kw, BF16), (kw, BF16), (vw, BF16), (vw, BF16), (kw, F32),
              (qw, BF16), (qw, BF16), (dvw, BF16), (n_gate, BF16)]
    return pl.pallas_call(
        _proj_kernel,
        out_shape=[jax.ShapeDtypeStruct((bsz, s, w), dt) for w, dt in widths],
        grid=(bsz, s // tm),
        in_specs=[rows(d), _next_rows(bsz, s // tm, tm, d), _resident(sh.shape),
                  _resident(sc.shape), _resident((1, d)), _resident(w_t.shape),
                  _resident(w2.shape), _resident((1, kw))],
        out_specs=[rows(w) for w, _ in widths],
        scratch_shapes=[pltpu.VMEM((tm, d), BF16), pltpu.VMEM((tm, d), BF16)],
        compiler_params=_params(("arbitrary", "arbitrary")),
        name="mixer_proj",
    )(h, h, sh, sc, nw.reshape(1, d), w_t, w2, b2.reshape(1, kw))


def _gla_kernel(q_ref, k_ref, v_ref, r_ref, g_ref, hn_ref, o_ref, st_ref, inter_ref,
                *, chunk, sub):
    @pl.when(pl.program_id(1) == 0)
    def _():
        st_ref[...] = jnp.zeros_like(st_ref)

    nsub = chunk // sub
    row = lax.broadcasted_iota(jnp.int32, (chunk, chunk), 0)
    col = lax.broadcasted_iota(jnp.int32, (chunk, chunk), 1)
    causal = col <= row
    cum_mat = jnp.where(causal, 1.0, 0.0).astype(BF16)
    row_blk = lax.broadcasted_iota(jnp.int32, (chunk, LANES), 0) // sub
    lane_head = lax.broadcasted_iota(jnp.int32, (chunk, LANES), 1) // GLA_DK

    def cumsum(ck, bi, pair):
        return prefix_sums(g_ref[bi, ck * chunk:(ck + 1) * chunk, pair * LANES:(pair + 1) * LANES])

    def prefix_sums(g):
        g1 = g.astype(BF16)
        e1 = g - g1.astype(F32)
        g2 = e1.astype(BF16)
        g3 = (e1 - g2.astype(F32)).astype(BF16)
        cs = jnp.dot(cum_mat, jnp.concatenate([g1, g2, g3], axis=1), preferred_element_type=F32)
        return cs[:, :LANES] + cs[:, LANES:2 * LANES] + cs[:, 2 * LANES:]

    def decay(ck, bi, pair, b):
        lanes = slice(pair * LANES, (pair + 1) * LANES)
        b_last = b[chunk - 1:chunk]
        rows = slice(ck * chunk, (ck + 1) * chunk)
        q = q_ref[bi, rows, lanes].astype(F32)
        k = k_ref[bi, rows, lanes].astype(F32)
        q_cat, w = [], []
        for j in range(nsub):
            ref_b = b[j * sub - 1:j * sub] if j else jnp.zeros_like(b_last)
            q_cat.append((q * jnp.exp2(jnp.minimum(b - ref_b, 0.0))).astype(BF16))
            w.append(b[j * sub:(j + 1) * sub] - ref_b)
        w = jnp.concatenate(w, axis=0)
        return dict(
            q_cat=jnp.concatenate(q_cat, axis=1),
            q_state=(q * jnp.exp2(b)).astype(BF16),
            k_state=k * jnp.exp2(b_last - b),
            k_hat=k * jnp.exp2(jnp.minimum(-w, EXP2_CLAMP)),
            chunk_decay=jnp.exp2(b_last),
            key_exponent=jnp.max(-w, axis=0, keepdims=True))

    def scores(d):
        out = []
        for hh in range(2):
            kh = jnp.where(lane_head == hh, d["k_hat"], 0.0).astype(BF16)
            k_cat = jnp.concatenate(
                [jnp.where(row_blk == j, kh, jnp.zeros_like(kh)) for j in range(nsub)], axis=1)
            out.append(lax.dot_general(d["q_cat"], k_cat, (((1,), (1,)), ((), ())),
                                       preferred_element_type=F32))
        return out

    def finish(ck, bi, pair, d, sc):
        rows = slice(ck * chunk, (ck + 1) * chunk)
        for hh in range(2):
            head = pair * 2 + hh
            cols = slice(head * GLA_DV, (head + 1) * GLA_DV)
            intra = jnp.where(causal, sc[hh], 0.0).astype(BF16)
            vh = v_ref[bi, rows, cols]
            state_t = st_ref[bi * GLA_HEADS + head]
            o_inter = lax.dot_general(d["q_state"], state_t.astype(BF16), (((1,), (1,)), ((), ())),
                                      preferred_element_type=F32)
            inter_ref[(ck * n_batch + bi) * GLA_HEADS + head] = o_inter
            o = jnp.dot(intra, vh, preferred_element_type=F32) + o_inter
            ks = jnp.where(lane_head == hh, d["k_state"], 0.0).astype(BF16)
            kv_t = lax.dot_general(vh, ks, (((0,), (0,)), ((), ())), preferred_element_type=F32)
            st_ref[bi * GLA_HEADS + head] = state_t * d["chunk_decay"] + kv_t
            y = _rmsnorm(o, hn_ref[...]) * r_ref[bi, rows, cols].astype(F32)
            o_ref[bi, rows, cols] = y.astype(BF16)

    n_batch = q_ref.shape[0]
    n_chunk = q_ref.shape[1] // chunk
    per_chunk = list(itertools.product(range(n_batch), range(GLA_HEADS // 2)))
    chains = [(ck, bi, pair) for ck in range(n_chunk) for bi, pair in per_chunk]
    cums = {t: cumsum(*chains[t]) for t in range(len(per_chunk))}
    decayed, scored = {}, {}
    key_exponent = jnp.zeros((1, LANES), F32)
    for t in range(len(chains) + 1):
        if t + len(per_chunk) < len(chains):
            cums[t + len(per_chunk)] = cumsum(*chains[t + len(per_chunk)])
        if t < len(chains):
            decayed[t] = decay(*chains[t], cums.pop(t))
            key_exponent = jnp.maximum(key_exponent, decayed[t]["key_exponent"])
        if t >= 1:
            finish(*chains[t - 1], decayed.pop(t - 1), scored.pop(t - 1))
        if t < len(chains):
            scored[t] = scores(decayed[t])

    @pl.when(jnp.max(key_exponent) > EXP2_CLAMP)
    def _():
        def redo(i, carry):
            ck, bi = i // n_batch, i % n_batch
            rows = pl.ds(pl.multiple_of(ck * chunk, chunk), chunk)
            for pair in range(GLA_HEADS // 2):
                lanes = slice(pair * LANES, (pair + 1) * LANES)
                q = q_ref[bi, rows, lanes].astype(F32)
                k = k_ref[bi, rows, lanes].astype(F32)
                b = prefix_sums(g_ref[bi, rows, lanes])
                q_cat, to_end = [], []
                for j in range(nsub):
                    end_b = b[(j + 1) * sub - 1:(j + 1) * sub]
                    q_cat.append((q * jnp.exp2(jnp.minimum(b - end_b, 0.0))).astype(BF16))
                    to_end.append(end_b - b[j * sub:(j + 1) * sub])
                q_cat = jnp.concatenate(q_cat, axis=1)
                k_end = k * jnp.exp2(jnp.concatenate(to_end, axis=0))
                for hh in range(2):
                    head = pair * 2 + hh
                    cols = slice(head * GLA_DV, (head + 1) * GLA_DV)
                    vh = v_ref[bi, rows, cols]
                    kh = jnp.where(lane_head == hh, k_end, 0.0).astype(BF16)
                    k_cat = jnp.concatenate(
                        [jnp.where(row_blk == j, kh, jnp.zeros_like(kh)) for j in range(nsub)],
                        axis=1)
                    cross = lax.dot_general(q_cat, k_cat, (((1,), (1,)), ((), ())),
                                            preferred_element_type=F32)
                    cross = jnp.where(row // sub > col // sub, cross, 0.0).astype(BF16)
                    o = jnp.dot(cross, vh, preferred_element_type=F32)
                    v32 = vh.astype(F32)
                    in_blk = lax.broadcasted_iota(jnp.int32, (chunk, LANES), 0) % sub
                    for off in range(sub):
                        k_o = pltpu.roll(k, off, 0) if off else k
                        b_o = pltpu.roll(b, off, 0) if off else b
                        v_o = pltpu.roll(v32, off, 0) if off else v32
                        ok = (in_blk >= off) & (lane_head == hh)
                        term = jnp.where(ok, q * k_o * jnp.exp2(jnp.minimum(b - b_o, 0.0)), 0.0)
                        o += jnp.sum(term, axis=1, keepdims=True) * v_o
                    o += inter_ref[(ck * n_batch + bi) * GLA_HEADS + head]
                    y = _rmsnorm(o, hn_ref[...]) * r_ref[bi, rows, cols].astype(F32)
                    o_ref[bi, rows, cols] = y.astype(BF16)
            return carry

        lax.fori_loop(0, n_chunk * n_batch, redo, 0)


def _gla(gq, gk, gv, gr, g, head_norm):
    bsz, s, kw = gq.shape
    vw = gv.shape[-1]
    nb, chunk = GLA_BATCHES, GLA_CHUNK
    step_rows = GLA_STEP_CHUNKS * chunk

    def rows(w):
        return pl.BlockSpec((nb, step_rows, w), lambda b, c: (b, c, 0))

    return pl.pallas_call(
        functools.partial(_gla_kernel, chunk=chunk, sub=GLA_SUB),
        out_shape=jax.ShapeDtypeStruct((bsz, s, vw), BF16),
        grid=(bsz // nb, s // step_rows),
        in_specs=[rows(kw), rows(kw), rows(vw), rows(vw), rows(kw), _resident((1, GLA_DV))],
        out_specs=rows(vw),
        scratch_shapes=[pltpu.VMEM((nb * GLA_HEADS, GLA_DV, LANES), F32),
                        pltpu.VMEM((GLA_STEP_CHUNKS * nb * GLA_HEADS, chunk, GLA_DV), F32)],
        compiler_params=_params(("parallel", "arbitrary")),
        name="gla",
    )(gq, gk, gv, gr, g, head_norm.reshape(1, GLA_DV))


def _attn_kernel(q_ref, k_ref, v_ref, pos_ref, slope_ref, lam_ref, hn_ref, o_ref,
                 kk_ref, qq_ref, vt_ref, s0_ref, s1_ref, s2_ref, m_ref, acc_ref, *, tq, lam_init):
    seq = k_ref.shape[1]
    nq = seq // tq
    hq = tq // 2
    c = slope_ref[0] * LOG2E
    c1 = c.astype(BF16).astype(F32)
    c2 = (c - c1).astype(BF16).astype(F32)
    c3 = c - c1 - c2

    kk_ref[:, :LANES] = k_ref[0]
    kk_ref[:, LANES:] = pos_ref[...]
    sub = lax.broadcasted_iota(jnp.int32, (LANES, 2 * tq), 0)
    cf = jnp.where((sub == 0) | (sub == 3), c1, jnp.where((sub == 1) | (sub == 4), c2, c3))
    slope_rows = jnp.where(sub < 6, cf, 0.0).astype(BF16)
    for i in range(qq_ref.shape[0]):
        qq_ref[i, LANES:, :] = slope_rows
    for jb in range(nq):
        vt_ref[jb, :DIFF_DV, :] = v_ref[0, jb * tq:(jb + 1) * tq, :].astype(F32).T.astype(BF16)
        vt_ref[jb, DIFF_DV:, :] = jnp.ones((SUM_ROWS, tq), BF16)
    lam = lam_ref[...]
    lam = (jnp.exp(jnp.sum(lam[0:1] * lam[1:2], axis=-1, keepdims=True))
           - jnp.exp(jnp.sum(lam[2:3] * lam[3:4], axis=-1, keepdims=True)) + lam_init)

    def logits(qq_slot, j, s_blk):
        start = pl.multiple_of(j * tq, tq)
        s = jnp.dot(kk_ref[pl.ds(start, tq), :], qq_ref[qq_slot], preferred_element_type=F32)
        s_blk[...] = s
        return jnp.max(s.reshape(tq // 8, 8, 2 * tq), axis=0)

    def softmax_accumulate(qs, qi, j, s_blk, mx):
        m_old = m_ref[qs]
        shift = c * jnp.asarray((qi - j) * tq, F32)
        m_new = jnp.maximum(m_old, jnp.max(mx, axis=0, keepdims=True) - shift)
        alpha = jnp.exp2(m_old - m_new)
        m_ref[qs] = m_new
        m_shift = m_new + shift
        probs = jnp.concatenate([jnp.exp2(s_blk[:hq, :] - m_shift).astype(BF16),
                                 jnp.exp2(s_blk[hq:, :] - m_shift).astype(BF16)], axis=0)
        acc_ref[qs] = alpha * acc_ref[qs] + jnp.dot(vt_ref[j], probs, preferred_element_type=F32)

    def late_half(x):
        return jnp.concatenate([x[..., hq:tq], x[..., tq + hq:]], axis=-1)

    def diag_logits(qq_slot, j, s_blk):
        start = pl.multiple_of(j * tq, tq)
        s_blk[:hq, :] = jnp.dot(kk_ref[pl.ds(start, hq), :], qq_ref[qq_slot],
                                preferred_element_type=F32)
        s_blk[hq:, :tq] = jnp.dot(kk_ref[pl.ds(start + hq, hq), :], late_half(qq_ref[qq_slot]),
                                  preferred_element_type=F32)

    def diag_softmax_accumulate(qs, j, s_blk):
        m_old = m_ref[qs]
        key = lax.broadcasted_iota(jnp.int32, (hq, 2 * tq), 0)
        qry = lax.broadcasted_iota(jnp.int32, (hq, 2 * tq), 1) & (tq - 1)
        early = jnp.where(key <= qry, s_blk[:hq, :], -jnp.inf)
        key_l = lax.broadcasted_iota(jnp.int32, (hq, tq), 0)
        qry_l = lax.broadcasted_iota(jnp.int32, (hq, tq), 1) & (hq - 1)
        late = jnp.where(key_l <= qry_l, s_blk[hq:, :tq], -jnp.inf)
        mx_e = jnp.max(jnp.max(early.reshape(hq // 8, 8, 2 * tq), axis=0), axis=0, keepdims=True)
        mx_l = jnp.max(jnp.max(late.reshape(hq // 8, 8, tq), axis=0), axis=0, keepdims=True)
        mx = jnp.concatenate(
            [mx_e[:, :hq], jnp.maximum(mx_e[:, hq:tq], mx_l[:, :hq]),
             mx_e[:, tq:tq + hq], jnp.maximum(mx_e[:, tq + hq:], mx_l[:, hq:])], axis=1)
        m_new = jnp.maximum(m_old, mx)
        alpha = jnp.exp2(m_old - m_new)
        m_ref[qs] = m_new
        p_early = jnp.exp2(early - m_new).astype(BF16)
        p_late = jnp.exp2(late - late_half(m_new)).astype(BF16)
        vt = vt_ref[j]
        acc_ref[qs] = alpha * acc_ref[qs] + jnp.dot(vt[:, :hq], p_early,
                                                    preferred_element_type=F32)
        upd = jnp.dot(vt[:, hq:], p_late, preferred_element_type=F32)
        acc_ref[qs, :, hq:tq] += upd[:, :hq]
        acc_ref[qs, :, tq + hq:] += upd[:, hq:]

    bufs = (s0_ref, s1_ref, s2_ref)
    ahead = len(bufs) - 1
    assert (nq + 1) % len(bufs) == 0

    def build_queries(p, slot):
        for qs, blk in enumerate((p, nq - 1 - p)):
            q_t = q_ref[0, pl.ds(pl.multiple_of(blk * tq, tq), tq), :].astype(F32).T
            half = lax.broadcasted_iota(jnp.int32, (LANES, tq), 0)
            qq_ref[2 * slot + qs, :LANES, :tq] = jnp.where(half < DIFF_DH, q_t, 0.0).astype(BF16)
            qq_ref[2 * slot + qs, :LANES, tq:] = jnp.where(half >= DIFF_DH, q_t, 0.0).astype(BF16)

    def diag_prefetch(p, slot, t):
        qs, blk = ((1, nq - 1 - p), (0, p))[t]
        diag_logits(2 * slot + qs, blk, bufs[t])

    def pair(p, carry):
        slot = p % 2
        q_blk = (p, nq - 1 - p)
        for qs in range(2):
            m_ref[qs] = jnp.full((1, 2 * tq), -jnp.inf, F32)
            acc_ref[qs] = jnp.zeros((DIFF_DV + SUM_ROWS, 2 * tq), F32)
        items = [(1, q_blk[1], q_blk[1], True), (0, q_blk[0], q_blk[0], True)]
        for u in range(nq - 1):
            late = u < q_blk[1]
            items.append((jnp.where(late, 1, 0), jnp.where(late, q_blk[1], q_blk[0]),
                          jnp.where(late, u, u - q_blk[1]), False))
        col_max = {}
        p_next = jnp.minimum(p + 1, nq // 2 - 1)
        for t, (qs, qi, j, masked) in enumerate(items):
            ta = t + ahead
            if ta < len(items):
                col_max[ta] = logits(2 * slot + items[ta][0], items[ta][2], bufs[ta % len(bufs)])
            elif ta == len(items):
                build_queries(p_next, 1 - slot)
                diag_prefetch(p_next, 1 - slot, 0)
            else:
                diag_prefetch(p_next, 1 - slot, 1)
            if masked:
                diag_softmax_accumulate(qs, j, bufs[t % len(bufs)])
            else:
                softmax_accumulate(qs, qi, j, bufs[t % len(bufs)], col_max.pop(t))
        for qs in range(2):
            out = acc_ref[qs, :DIFF_DV, :] * (1.0 / acc_ref[qs, DIFF_DV:DIFF_DV + 1, :])
            y_t = out[:, :tq] - lam * out[:, tq:]
            y_t = y_t * lax.rsqrt(jnp.mean(y_t * y_t, axis=0, keepdims=True) + EPS)
            rows = pl.ds(pl.multiple_of(q_blk[qs] * tq, tq), tq)
            o_ref[0, rows, :] = (y_t.T * hn_ref[...] * (1.0 - lam_init)).astype(BF16)
        return carry

    build_queries(0, 0)
    diag_prefetch(0, 0, 0)
    diag_prefetch(0, 0, 1)
    lax.fori_loop(0, nq // 2, pair, 0)


def _diff_attn(dq, dk, dv, lam_vecs, head_norm, lam_init):
    bsz, s, _ = dq.shape
    tq = ATTN_BLOCK
    slopes = jnp.asarray(
        [2.0 ** (-ALIBI_MAX_BIAS * (i + 1) / DIFF_HEADS) for i in range(DIFF_HEADS)], F32
    ).reshape(DIFF_HEADS, 1, 1)
    pos = np.arange(s) % tq
    feat = np.zeros((s, LANES), np.float32)
    feat[:, 0:3] = (pos - pos % 16)[:, None]
    feat[:, 3:6] = (pos % 16)[:, None]
    return pl.pallas_call(
        functools.partial(_attn_kernel, tq=tq, lam_init=lam_init),
        out_shape=jax.ShapeDtypeStruct((bsz, s, DIFF_HEADS * DIFF_DV), BF16),
        grid=(bsz, DIFF_HEADS),
        in_specs=[
            pl.BlockSpec((1, s, LANES), lambda b, h: (b, 0, h)),
            pl.BlockSpec((1, s, LANES), lambda b, h: (b, 0, h)),
            pl.BlockSpec((1, s, DIFF_DV), lambda b, h: (b, 0, h)),
            _resident((s, LANES)),
            pl.BlockSpec((1, 1, 1), lambda b, h: (h, 0, 0)),
            _resident(lam_vecs.shape),
            _resident((1, DIFF_DV)),
        ],
        out_specs=pl.BlockSpec((1, s, DIFF_DV), lambda b, h: (b, 0, h)),
        scratch_shapes=[
            pltpu.VMEM((s, 2 * LANES), BF16),
            pltpu.VMEM((4, 2 * LANES, 2 * tq), BF16),
            pltpu.VMEM((s // tq, DIFF_DV + SUM_ROWS, tq), BF16),
            pltpu.VMEM((tq, 2 * tq), F32),
            pltpu.VMEM((tq, 2 * tq), F32),
            pltpu.VMEM((tq, 2 * tq), F32),
            pltpu.VMEM((2, 1, 2 * tq), F32),
            pltpu.VMEM((2, DIFF_DV + SUM_ROWS, 2 * tq), F32),
        ],
        compiler_params=_params(("parallel", "parallel")),
        name="diff_attn",
    )(dq, dk, dv, jnp.asarray(feat, BF16), slopes, lam_vecs, head_norm.reshape(1, DIFF_DV))


def _merge_kernel(h_ref, za_ref, zb_ref, sg_ref, gt_ref, wa_ref, wb_ref, wo_ref, o_ref):
    d = h_ref.shape[-1]
    ya = jnp.dot(za_ref[0], wa_ref[...].astype(BF16), preferred_element_type=F32)
    yb = jnp.dot(zb_ref[0], wb_ref[...].astype(BF16), preferred_element_type=F32)
    mix = sg_ref[0, :, :d].astype(F32) * ya + sg_ref[0, :, d:].astype(F32) * yb
    m = jnp.dot(mix.astype(BF16), wo_ref[...].astype(BF16), preferred_element_type=F32)
    o_ref[0] = h_ref[0] + gt_ref[0] * m


def _merge(h, za, zb, sg, gt, w_a, w_b, w_o):
    bsz, s, d = h.shape
    tm = MERGE_ROWS
    vec = pl.BlockSpec((1, 1, d), lambda b, i: (b, 0, 0))

    def rows(w):
        return pl.BlockSpec((1, tm, w), lambda b, i: (b, i, 0))

    return pl.pallas_call(
        _merge_kernel,
        out_shape=jax.ShapeDtypeStruct((bsz, s, d), F32),
        grid=(bsz, s // tm),
        in_specs=[rows(d), rows(za.shape[-1]), rows(zb.shape[-1]), rows(2 * d), vec,
                  _resident(w_a.shape), _resident(w_b.shape), _resident(w_o.shape)],
        out_specs=rows(d),
        compiler_params=_params(("parallel", "parallel")),
        name="merge",
    )(h, za, zb, sg, gt, w_a, w_b, w_o)


def kernel(x, c, w_ada, b_ada, ffn1_norm, ffn1_w_in, ffn1_w_out, mix_norm, w_in, gla_alpha_w2, gla_alpha_b, gla_head_norm, diff_lq1, diff_lk1, diff_lq2, diff_lk2, diff_head_norm, w_branch_a, w_branch_b, w_out, ffn2_norm, ffn2_w_in, ffn2_w_out, final_norm):
    depth = w_ada.shape[0]
    bsz, s, d = x.shape
    assert bsz % GLA_BATCHES == 0 and s % (GLA_STEP_CHUNKS * GLA_CHUNK) == 0
    assert s % MERGE_ROWS == 0 and s % FFN_ROWS == 0 and s % PROJ_ROWS == 0
    assert (s // ATTN_BLOCK) % 2 == 0 and (N_MOD * d) % (ADALN_STEPS * LANES) == 0
    h = x
    for l in range(depth):
        lam_init = 0.8 - 0.6 * math.exp(-0.3 * l)
        mod = _adaln(c, w_ada[l], b_ada[l])
        sh1, sc1, gt1, sh2, sc2, gt2, sh3, sc3, gt3 = [
            mod[:, i * d:(i + 1) * d].reshape(bsz, 1, d) for i in range(N_MOD)]
        last = l == depth - 1
        h = _ffn(h, sh1, sc1, gt1, ffn1_norm[l], ffn1_w_in[l], ffn1_w_out[l], final_norm,
                 final_norm=False)
        w_t = jnp.swapaxes(w_in, 1, 2)[l]
        gq, gk, gv, gr, g, dq, dk, dv, sg = _mixer_proj(
            h, sh2, sc2, mix_norm[l], w_t, gla_alpha_w2[l], gla_alpha_b[l])
        za = _gla(gq, gk, gv, gr, g, gla_head_norm[l])
        lam_vecs = jnp.stack([diff_lq1[l], diff_lk1[l], diff_lq2[l], diff_lk2[l]])
        zb = _diff_attn(dq, dk, dv, lam_vecs, diff_head_norm[l], lam_init)
        h = _merge(h, za, zb, sg, gt2, w_branch_a[l], w_branch_b[l], w_out[l])
        h = _ffn(h, sh3, sc3, gt3, ffn2_norm[l], ffn2_w_in[l], ffn2_w_out[l], final_norm,
                 final_norm=last)
    return h
```

```python
import functools
import itertools
import math

import jax
import jax.numpy as jnp
import numpy as np
from jax import lax
from jax.experimental import pallas as pl
from jax.experimental.pallas import tpu as pltpu

F32 = jnp.float32
BF16 = jnp.bfloat16

EPS = 1e-6
GLA_HEADS = 4
GLA_DK = 64
GLA_DV = 128
GLA_RANK = 16
GLA_TAU = 16.0
DIFF_HEADS = 4
DIFF_DH = 64
DIFF_DV = 128
ALIBI_MAX_BIAS = 8.0
N_MOD = 9

LOG2E = math.log2(math.e)
LANES = 128
MXU_COLS = 256
VMEM_LIMIT = 56 * 1024 * 1024

ADALN_STEPS = 8
FFN_ROWS = 512
FFN_HIDDEN_CHUNK = MXU_COLS
NEXT_PIECES = 8
PROJ_ROWS = 512
GATE_CHUNK = 512
MERGE_ROWS = 1024
GLA_CHUNK = 128
GLA_SUB = 16
GLA_BATCHES = 4
GLA_STEP_CHUNKS = 4
ATTN_BLOCK = 512
EXP2_CLAMP = 115.0
SUM_ROWS = 16


def _params(sem):
    return pltpu.CompilerParams(dimension_semantics=sem, vmem_limit_bytes=VMEM_LIMIT)


def _resident(shape):
    nd = len(shape)
    return pl.BlockSpec(shape, lambda *_: (0,) * nd, pipeline_mode=pl.Buffered(1))


def _rmsnorm(x, g):
    return x * lax.rsqrt(jnp.mean(x * x, axis=-1, keepdims=True) + EPS) * g


def _next_input_pieces(xn_ref, un_ref, norm_fn, step):
    piece = un_ref.shape[0] // NEXT_PIECES
    zero = jnp.minimum(step, 0).astype(F32)

    def prepare(i):
        rows = slice(i * piece, (i + 1) * piece)
        un = norm_fn(xn_ref[0, rows, :])
        un_ref[rows, :] = un
        return jnp.sum(un.astype(F32), keepdims=True) * zero

    return prepare


def _next_rows(bsz, nt, tm, d):
    def index(b, i):
        nxt = jnp.minimum(b * nt + i + 1, bsz * nt - 1)
        return nxt // nt, nxt % nt, 0
    return pl.BlockSpec((1, tm, d), index)


def _adaln_kernel(c_ref, w_ref, b_ref, o_ref):
    c = c_ref[...]
    ca = (c * jax.nn.sigmoid(c)).astype(BF16)
    o_ref[...] = jnp.dot(ca, w_ref[...].astype(BF16), preferred_element_type=F32) + b_ref[...]


def _adaln(c, w_ada, b_ada):
    bsz, d = c.shape
    n = w_ada.shape[1]
    tn = n // ADALN_STEPS
    return pl.pallas_call(
        _adaln_kernel,
        out_shape=jax.ShapeDtypeStruct((bsz, n), F32),
        grid=(n // tn,),
        in_specs=[
            pl.BlockSpec((bsz, d), lambda j: (0, 0)),
            pl.BlockSpec((d, tn), lambda j: (0, j)),
            pl.BlockSpec((1, tn), lambda j: (0, j)),
        ],
        out_specs=pl.BlockSpec((bsz, tn), lambda j: (0, j)),
        compiler_params=_params(("arbitrary",)),
        name="adaln",
    )(c, w_ada, b_ada.reshape(1, n))


def _ffn_kernel(x_ref, xn_ref, sh_ref, sc_ref, gt_ref, nw_ref, win_hbm, wout_hbm, fn_ref, o_ref,
                u_ref, un_ref, win_ref, wout_ref, sem_ref, *, d_ff, tf, final_norm):
    n_chunk = d_ff // tf
    assert n_chunk >= NEXT_PIECES
    acc_ref = o_ref.at[0]
    step = pl.program_id(0) * pl.num_programs(1) + pl.program_id(1)
    last_step = pl.num_programs(0) * pl.num_programs(1) - 1

    def normed(x, batch):
        return (_rmsnorm(x, nw_ref[...]) * (1.0 + sc_ref[batch]) + sh_ref[batch]).astype(BF16)

    def weight_copies(i):
        cols = pl.ds(i * tf, tf)
        up = pl.ds(d_ff + i * tf, tf)
        return (pltpu.make_async_copy(win_hbm.at[:, cols], win_ref.at[:, cols], sem_ref.at[0, i]),
                pltpu.make_async_copy(win_hbm.at[:, up], win_ref.at[:, up], sem_ref.at[1, i]),
                pltpu.make_async_copy(wout_hbm.at[cols, :], wout_ref.at[cols, :], sem_ref.at[2, i]))

    def tile(is_first):
        if is_first:
            un_ref[...] = normed(x_ref[0], pl.program_id(0))
        u_ref[...] = un_ref[...]
        u = u_ref[...]
        next_batch = jnp.minimum(step + 1, last_step) // pl.num_programs(1)
        prepare = _next_input_pieces(xn_ref, un_ref, lambda x: normed(x, next_batch), step)
        anchor = None
        for i in range(n_chunk):
            if is_first:
                for copy in weight_copies(i):
                    copy.wait()
            cols = slice(i * tf, (i + 1) * tf)
            up = slice(d_ff + i * tf, d_ff + (i + 1) * tf)
            hg = jnp.dot(u, win_ref[:, cols].astype(BF16), preferred_element_type=F32)
            if anchor is not None:
                hg = hg + anchor
            hu = jnp.dot(u, win_ref[:, up].astype(BF16), preferred_element_type=F32)
            act = (hg * jax.nn.sigmoid(hg) * hu).astype(BF16)
            part = jnp.dot(act, wout_ref[cols, :].astype(BF16), preferred_element_type=F32)
            if i == 0:
                acc_ref[...] = part
            else:
                acc_ref[...] += part
            anchor = prepare(i) if i < NEXT_PIECES else None
        h = x_ref[0] + (0.5 * gt_ref[0]) * acc_ref[...]
        if final_norm:
            h = _rmsnorm(h, fn_ref[...])
        o_ref[0] = h

    first = (pl.program_id(0) == 0) & (pl.program_id(1) == 0)

    @pl.when(first)
    def _():
        for i in range(n_chunk):
            for copy in weight_copies(i):
                copy.start()
        tile(True)

    @pl.when(jnp.logical_not(first))
    def _():
        tile(False)


def _ffn(x, sh, sc, gt, nw, w_in, w_out, fn, *, final_norm):
    bsz, s, d = x.shape
    d_ff = w_out.shape[0]
    tm, tf = FFN_ROWS, FFN_HIDDEN_CHUNK
    nt = s // tm
    vec = pl.BlockSpec((1, 1, d), lambda b, i: (b, 0, 0))
    row = pl.BlockSpec((1, tm, d), lambda b, i: (b, i, 0))
    hbm = pl.BlockSpec(memory_space=pl.ANY)
    return pl.pallas_call(
        functools.partial(_ffn_kernel, d_ff=d_ff, tf=tf, final_norm=final_norm),
        out_shape=jax.ShapeDtypeStruct((bsz, s, d), F32),
        grid=(bsz, nt),
        in_specs=[row, _next_rows(bsz, nt, tm, d), _resident(sh.shape), _resident(sc.shape), vec,
                  _resident((1, d)), hbm, hbm, _resident((1, d))],
        out_specs=row,
        scratch_shapes=[pltpu.VMEM((tm, d), BF16), pltpu.VMEM((tm, d), BF16),
                        pltpu.VMEM((d, 2 * d_ff), F32), pltpu.VMEM((d_ff, d), F32),
                        pltpu.SemaphoreType.DMA((3, d_ff // tf))],
        compiler_params=_params(("arbitrary", "arbitrary")),
        name="ffn_final" if final_norm else "ffn",
    )(x, x, sh, sc, gt, nw.reshape(1, d), w_in, w_out, fn.reshape(1, d))


def _proj_kernel(x_ref, xn_ref, sh_ref, sc_ref, nw_ref, wt_ref, w2_ref, b2_ref,
                 gq_ref, gk_ref, gv_ref, gr_ref, g_ref, dq_ref, dk_ref, dv_ref, sg_ref,
                 u_ref, un_ref):
    kw = GLA_HEADS * GLA_DK
    vw = GLA_HEADS * GLA_DV
    qw = DIFF_HEADS * 2 * DIFF_DH
    dvw = DIFF_HEADS * DIFF_DV
    step = pl.program_id(0) * pl.num_programs(1) + pl.program_id(1)
    last_step = pl.num_programs(0) * pl.num_programs(1) - 1

    def normed(x, batch):
        return (_rmsnorm(x, nw_ref[...]) * (1.0 + sc_ref[batch]) + sh_ref[batch]).astype(BF16)

    @pl.when(step == 0)
    def _():
        un_ref[...] = normed(x_ref[0], 0)

    u_ref[...] = un_ref[...]
    u = u_ref[...]
    next_batch = jnp.minimum(step + 1, last_step) // pl.num_programs(1)
    prepare = _next_input_pieces(xn_ref, un_ref, lambda x: normed(x, next_batch), step)
    pending = {"count": 0, "anchor": None}

    def seg(start, size):
        out = lax.dot_general(u, wt_ref[start:start + size, :].astype(BF16),
                              (((1,), (1,)), ((), ())), preferred_element_type=F32)
        if pending["anchor"] is not None:
            out = out + pending["anchor"]
        pending["anchor"] = prepare(pending["count"]) if pending["count"] < NEXT_PIECES else None
        pending["count"] += 1
        return out

    off = 0
    gq_ref[0] = (seg(off, kw) * (GLA_DK ** -0.5)).astype(BF16)
    off += kw
    gk_ref[0] = seg(off, kw).astype(BF16)
    off += kw
    gv_ref[0] = seg(off, vw).astype(BF16)
    off += vw
    r = seg(off, vw)
    gr_ref[0] = (r * jax.nn.sigmoid(r)).astype(BF16)
    off += vw
    a_low = seg(off, GLA_RANK).astype(BF16)
    z = jnp.dot(a_low, w2_ref[...].astype(BF16), preferred_element_type=F32) + b2_ref[...]
    g_ref[0] = (jnp.minimum(z, 0.0) - jnp.log1p(jnp.exp(-jnp.abs(z)))) * (LOG2E / GLA_TAU)
    off += GLA_RANK
    dq_ref[0] = (seg(off, qw) * (DIFF_DH ** -0.5 * LOG2E)).astype(BF16)
    off += qw
    dk_ref[0] = seg(off, qw).astype(BF16)
    off += qw
    dv_ref[0] = seg(off, dvw).astype(BF16)
    off += dvw
    for j in range(sg_ref.shape[-1] // GATE_CHUNK):
        gate = jax.nn.sigmoid(seg(off + j * GATE_CHUNK, GATE_CHUNK))
        sg_ref[0, :, j * GATE_CHUNK:(j + 1) * GATE_CHUNK] = gate.astype(BF16)
    assert pending["count"] >= NEXT_PIECES


def _mixer_proj(h, sh, sc, nw, w_t, w2, b2):
    bsz, s, d = h.shape
    tm = PROJ_ROWS
    kw = GLA_HEADS * GLA_DK
    vw = GLA_HEADS * GLA_DV
    qw = DIFF_HEADS * 2 * DIFF_DH
    dvw = DIFF_HEADS * DIFF_DV
    n_gate = 2 * d
    vec = pl.BlockSpec((1, 1, d), lambda b, i: (b, 0, 0))

    def rows(w):
        return pl.BlockSpec((1, tm, w), lambda b, i: (b, i, 0))

    widths = [(kw, BF16), (kw, BF16), (vw, BF16), (vw, BF16), (kw, F32),
              (qw, BF16), (qw, BF16), (dvw, BF16), (n_gate, BF16)]
    return pl.pallas_call(
        _proj_kernel,
        out_shape=[jax.ShapeDtypeStruct((bsz, s, w), dt) for w, dt in widths],
        grid=(bsz, s // tm),
        in_specs=[rows(d), _next_rows(bsz, s // tm, tm, d), _resident(sh.shape),
                  _resident(sc.shape), _resident((1, d)), _resident(w_t.shape),
                  _resident(w2.shape), _resident((1, kw))],
        out_specs=[rows(w) for w, _ in widths],
        scratch_shapes=[pltpu.VMEM((tm, d), BF16), pltpu.VMEM((tm, d), BF16)],
        compiler_params=_params(("arbitrary", "arbitrary")),
        name="mixer_proj",
    )(h, h, sh, sc, nw.reshape(1, d), w_t, w2, b2.reshape(1, kw))


def _gla_kernel(q_ref, k_ref, v_ref, r_ref, g_ref, hn_ref, o_ref, st_ref, inter_ref,
                *, chunk, sub):
    @pl.when(pl.program_id(1) == 0)
    def _():
        st_ref[...] = jnp.zeros_like(st_ref)

    nsub = chunk // sub
    row = lax.broadcasted_iota(jnp.int32, (chunk, chunk), 0)
    col = lax.broadcasted_iota(jnp.int32, (chunk, chunk), 1)
    causal = col <= row
    cum_mat = jnp.where(causal, 1.0, 0.0).astype(BF16)
    row_blk = lax.broadcasted_iota(jnp.int32, (chunk, LANES), 0) // sub
    lane_head = lax.broadcasted_iota(jnp.int32, (chunk, LANES), 1) // GLA_DK

    def cumsum(ck, bi, pair):
        return prefix_sums(g_ref[bi, ck * chunk:(ck + 1) * chunk, pair * LANES:(pair + 1) * LANES])

    def prefix_sums(g):
        g1 = g.astype(BF16)
        e1 = g - g1.astype(F32)
        g2 = e1.astype(BF16)
        g3 = (e1 - g2.astype(F32)).astype(BF16)
        cs = jnp.dot(cum_mat, jnp.concatenate([g1, g2, g3], axis=1), preferred_element_type=F32)
        return cs[:, :LANES] + cs[:, LANES:2 * LANES] + cs[:, 2 * LANES:]

    def decay(ck, bi, pair, b):
        lanes = slice(pair * LANES, (pair + 1) * LANES)
        b_last = b[chunk - 1:chunk]
        rows = slice(ck * chunk, (ck + 1) * chunk)
        q = q_ref[bi, rows, lanes].astype(F32)
        k = k_ref[bi, rows, lanes].astype(F32)
        q_cat, w = [], []
        for j in range(nsub):
            ref_b = b[j * sub - 1:j * sub] if j else jnp.zeros_like(b_last)
            q_cat.append((q * jnp.exp2(jnp.minimum(b - ref_b, 0.0))).astype(BF16))
            w.append(b[j * sub:(j + 1) * sub] - ref_b)
        w = jnp.concatenate(w, axis=0)
        return dict(
            q_cat=jnp.concatenate(q_cat, axis=1),
            q_state=(q * jnp.exp2(b)).astype(BF16),
            k_state=k * jnp.exp2(b_last - b),
            k_hat=k * jnp.exp2(jnp.minimum(-w, EXP2_CLAMP)),
            chunk_decay=jnp.exp2(b_last),
            key_exponent=jnp.max(-w, axis=0, keepdims=True))

    def scores(d):
        out = []
        for hh in range(2):
            kh = jnp.where(lane_head == hh, d["k_hat"], 0.0).astype(BF16)
            k_cat = jnp.concatenate(
                [jnp.where(row_blk == j, kh, jnp.zeros_like(kh)) for j in range(nsub)], axis=1)
            out.append(lax.dot_general(d["q_cat"], k_cat, (((1,), (1,)), ((), ())),
                                       preferred_element_type=F32))
        return out

    def finish(ck, bi, pair, d, sc):
        rows = slice(ck * chunk, (ck + 1) * chunk)
        for hh in range(2):
            head = pair * 2 + hh
            cols = slice(head * GLA_DV, (head + 1) * GLA_DV)
            intra = jnp.where(causal, sc[hh], 0.0).astype(BF16)
            vh = v_ref[bi, rows, cols]
            state_t = st_ref[bi * GLA_HEADS + head]
            o_inter = lax.dot_general(d["q_state"], state_t.astype(BF16), (((1,), (1,)), ((), ())),
                                      preferred_element_type=F32)
            inter_ref[(ck * n_batch + bi) * GLA_HEADS + head] = o_inter
            o = jnp.dot(intra, vh, preferred_element_type=F32) + o_inter
            ks = jnp.where(lane_head == hh, d["k_state"], 0.0).astype(BF16)
            kv_t = lax.dot_general(vh, ks, (((0,), (0,)), ((), ())), preferred_element_type=F32)
            st_ref[bi * GLA_HEADS + head] = state_t * d["chunk_decay"] + kv_t
            y = _rmsnorm(o, hn_ref[...]) * r_ref[bi, rows, cols].astype(F32)
            o_ref[bi, rows, cols] = y.astype(BF16)

    n_batch = q_ref.shape[0]
    n_chunk = q_ref.shape[1] // chunk
    per_chunk = list(itertools.product(range(n_batch), range(GLA_HEADS // 2)))
    chains = [(ck, bi, pair) for ck in range(n_chunk) for bi, pair in per_chunk]
    cums = {t: cumsum(*chains[t]) for t in range(len(per_chunk))}
    decayed, scored = {}, {}
    key_exponent = jnp.zeros((1, LANES), F32)
    for t in range(len(chains) + 1):
        if t + len(per_chunk) < len(chains):
            cums[t + len(per_chunk)] = cumsum(*chains[t + len(per_chunk)])
        if t < len(chains):
            decayed[t] = decay(*chains[t], cums.pop(t))
            key_exponent = jnp.maximum(key_exponent, decayed[t]["key_exponent"])
        if t >= 1:
            finish(*chains[t - 1], decayed.pop(t - 1), scored.pop(t - 1))
        if t < len(chains):
            scored[t] = scores(decayed[t])

    @pl.when(jnp.max(key_exponent) > EXP2_CLAMP)
    def _():
        def redo(i, carry):
            ck, bi = i // n_batch, i % n_batch
            rows = pl.ds(pl.multiple_of(ck * chunk, chunk), chunk)
            for pair in range(GLA_HEADS // 2):
                lanes = slice(pair * LANES, (pair + 1) * LANES)
                q = q_ref[bi, rows, lanes].astype(F32)
                k = k_ref[bi, rows, lanes].astype(F32)
                b = prefix_sums(g_ref[bi, rows, lanes])
                q_cat, to_end = [], []
                for j in range(nsub):
                    end_b = b[(j + 1) * sub - 1:(j + 1) * sub]
                    q_cat.append((q * jnp.exp2(jnp.minimum(b - end_b, 0.0))).astype(BF16))
                    to_end.append(end_b - b[j * sub:(j + 1) * sub])
                q_cat = jnp.concatenate(q_cat, axis=1)
                k_end = k * jnp.exp2(jnp.concatenate(to_end, axis=0))
                for hh in range(2):
                    head = pair * 2 + hh
                    cols = slice(head * GLA_DV, (head + 1) * GLA_DV)
                    vh = v_ref[bi, rows, cols]
                    kh = jnp.where(lane_head == hh, k_end, 0.0).astype(BF16)
                    k_cat = jnp.concatenate(
                        [jnp.where(row_blk == j, kh, jnp.zeros_like(kh)) for j in range(nsub)],
                        axis=1)
                    cross = lax.dot_general(q_cat, k_cat, (((1,), (1,)), ((), ())),
                                            preferred_element_type=F32)
                    cross = jnp.where(row // sub > col // sub, cross, 0.0).astype(BF16)
                    o = jnp.dot(cross, vh, preferred_element_type=F32)
                    v32 = vh.astype(F32)
                    in_blk = lax.broadcasted_iota(jnp.int32, (chunk, LANES), 0) % sub
                    for off in range(sub):
                        k_o = pltpu.roll(k, off, 0) if off else k
                        b_o = pltpu.roll(b, off, 0) if off else b
                        v_o = pltpu.roll(v32, off, 0) if off else v32
                        ok = (in_blk >= off) & (lane_head == hh)
                        term = jnp.where(ok, q * k_o * jnp.exp2(jnp.minimum(b - b_o, 0.0)), 0.0)
                        o += jnp.sum(term, axis=1, keepdims=True) * v_o
                    o += inter_ref[(ck * n_batch + bi) * GLA_HEADS + head]
                    y = _rmsnorm(o, hn_ref[...]) * r_ref[bi, rows, cols].astype(F32)
                    o_ref[bi, rows, cols] = y.astype(BF16)
            return carry

        lax.fori_loop(0, n_chunk * n_batch, redo, 0)


def _gla(gq, gk, gv, gr, g, head_norm):
    bsz, s, kw = gq.shape
    vw = gv.shape[-1]
    nb, chunk = GLA_BATCHES, GLA_CHUNK
    step_rows = GLA_STEP_CHUNKS * chunk

    def rows(w):
        return pl.BlockSpec((nb, step_rows, w), lambda b, c: (b, c, 0))

    return pl.pallas_call(
        functools.partial(_gla_kernel, chunk=chunk, sub=GLA_SUB),
        out_shape=jax.ShapeDtypeStruct((bsz, s, vw), BF16),
        grid=(bsz // nb, s // step_rows),
        in_specs=[rows(kw), rows(kw), rows(vw), rows(vw), rows(kw), _resident((1, GLA_DV))],
        out_specs=rows(vw),
        scratch_shapes=[pltpu.VMEM((nb * GLA_HEADS, GLA_DV, LANES), F32),
                        pltpu.VMEM((GLA_STEP_CHUNKS * nb * GLA_HEADS, chunk, GLA_DV), F32)],
        compiler_params=_params(("parallel", "arbitrary")),
        name="gla",
    )(gq, gk, gv, gr, g, head_norm.reshape(1, GLA_DV))


def _attn_kernel(q_ref, k_ref, v_ref, pos_ref, slope_ref, lam_ref, hn_ref, o_ref,
                 kk_ref, qq_ref, vt_ref, s0_ref, s1_ref, s2_ref, m_ref, acc_ref, *, tq, lam_init):
    seq = k_ref.shape[1]
    nq = seq // tq
    hq = tq // 2
    c = slope_ref[0] * LOG2E
    c1 = c.astype(BF16).astype(F32)
    c2 = (c - c1).astype(BF16).astype(F32)
    c3 = c - c1 - c2

    kk_ref[:, :LANES] = k_ref[0]
    kk_ref[:, LANES:] = pos_ref[...]
    sub = lax.broadcasted_iota(jnp.int32, (LANES, 2 * tq), 0)
    cf = jnp.where((sub == 0) | (sub == 3), c1, jnp.where((sub == 1) | (sub == 4), c2, c3))
    slope_rows = jnp.where(sub < 6, cf, 0.0).astype(BF16)
    for i in range(qq_ref.shape[0]):
        qq_ref[i, LANES:, :] = slope_rows
    for jb in range(nq):
        vt_ref[jb, :DIFF_DV, :] = v_ref[0, jb * tq:(jb + 1) * tq, :].astype(F32).T.astype(BF16)
        vt_ref[jb, DIFF_DV:, :] = jnp.ones((SUM_ROWS, tq), BF16)
    lam = lam_ref[...]
    lam = (jnp.exp(jnp.sum(lam[0:1] * lam[1:2], axis=-1, keepdims=True))
           - jnp.exp(jnp.sum(lam[2:3] * lam[3:4], axis=-1, keepdims=True)) + lam_init)

    def logits(qq_slot, j, s_blk):
        start = pl.multiple_of(j * tq, tq)
        s = jnp.dot(kk_ref[pl.ds(start, tq), :], qq_ref[qq_slot], preferred_element_type=F32)
        s_blk[...] = s
        return jnp.max(s.reshape(tq // 8, 8, 2 * tq), axis=0)

    def softmax_accumulate(qs, qi, j, s_blk, mx):
        m_old = m_ref[qs]
        shift = c * jnp.asarray((qi - j) * tq, F32)
        m_new = jnp.maximum(m_old, jnp.max(mx, axis=0, keepdims=True) - shift)
        alpha = jnp.exp2(m_old - m_new)
        m_ref[qs] = m_new
        m_shift = m_new + shift
        probs = jnp.concatenate([jnp.exp2(s_blk[:hq, :] - m_shift).astype(BF16),
                                 jnp.exp2(s_blk[hq:, :] - m_shift).astype(BF16)], axis=0)
        acc_ref[qs] = alpha * acc_ref[qs] + jnp.dot(vt_ref[j], probs, preferred_element_type=F32)

    def late_half(x):
        return jnp.concatenate([x[..., hq:tq], x[..., tq + hq:]], axis=-1)

    def diag_logits(qq_slot, j, s_blk):
        start = pl.multiple_of(j * tq, tq)
        s_blk[:hq, :] = jnp.dot(kk_ref[pl.ds(start, hq), :], qq_ref[qq_slot],
                                preferred_element_type=F32)
        s_blk[hq:, :tq] = jnp.dot(kk_ref[pl.ds(start + hq, hq), :], late_half(qq_ref[qq_slot]),
                                  preferred_element_type=F32)

    def diag_softmax_accumulate(qs, j, s_blk):
        m_old = m_ref[qs]
        key = lax.broadcasted_iota(jnp.int32, (hq, 2 * tq), 0)
        qry = lax.broadcasted_iota(jnp.int32, (hq, 2 * tq), 1) & (tq - 1)
        early = jnp.where(key <= qry, s_blk[:hq, :], -jnp.inf)
        key_l = lax.broadcasted_iota(jnp.int32, (hq, tq), 0)
        qry_l = lax.broadcasted_iota(jnp.int32, (hq, tq), 1) & (hq - 1)
        late = jnp.where(key_l <= qry_l, s_blk[hq:, :tq], -jnp.inf)
        mx_e = jnp.max(jnp.max(early.reshape(hq // 8, 8, 2 * tq), axis=0), axis=0, keepdims=True)
        mx_l = jnp.max(jnp.max(late.reshape(hq // 8, 8, tq), axis=0), axis=0, keepdims=True)
        mx = jnp.concatenate(
            [mx_e[:, :hq], jnp.maximum(mx_e[:, hq:tq], mx_l[:, :hq]),
             mx_e[:, tq:tq + hq], jnp.maximum(mx_e[:, tq + hq:], mx_l[:, hq:])], axis=1)
        m_new = jnp.maximum(m_old, mx)
        alpha = jnp.exp2(m_old - m_new)
        m_ref[qs] = m_new
        p_early = jnp.exp2(early - m_new).astype(BF16)
        p_late = jnp.exp2(late - late_half(m_new)).astype(BF16)
        vt = vt_ref[j]
        acc_ref[qs] = alpha * acc_ref[qs] + jnp.dot(vt[:, :hq], p_early,
                                                    preferred_element_type=F32)
        upd = jnp.dot(vt[:, hq:], p_late, preferred_element_type=F32)
        acc_ref[qs, :, hq:tq] += upd[:, :hq]
        acc_ref[qs, :, tq + hq:] += upd[:, hq:]

    bufs = (s0_ref, s1_ref, s2_ref)
    ahead = len(bufs) - 1
    assert (nq + 1) % len(bufs) == 0

    def build_queries(p, slot):
        for qs, blk in enumerate((p, nq - 1 - p)):
            q_t = q_ref[0, pl.ds(pl.multiple_of(blk * tq, tq), tq), :].astype(F32).T
            half = lax.broadcasted_iota(jnp.int32, (LANES, tq), 0)
            qq_ref[2 * slot + qs, :LANES, :tq] = jnp.where(half < DIFF_DH, q_t, 0.0).astype(BF16)
            qq_ref[2 * slot + qs, :LANES, tq:] = jnp.where(half >= DIFF_DH, q_t, 0.0).astype(BF16)

    def diag_prefetch(p, slot, t):
        qs, blk = ((1, nq - 1 - p), (0, p))[t]
        diag_logits(2 * slot + qs, blk, bufs[t])

    def pair(p, carry):
        slot = p % 2
        q_blk = (p, nq - 1 - p)
        for qs in range(2):
            m_ref[qs] = jnp.full((1, 2 * tq), -jnp.inf, F32)
            acc_ref[qs] = jnp.zeros((DIFF_DV + SUM_ROWS, 2 * tq), F32)
        items = [(1, q_blk[1], q_blk[1], True), (0, q_blk[0], q_blk[0], True)]
        for u in range(nq - 1):
            late = u < q_blk[1]
            items.append((jnp.where(late, 1, 0), jnp.where(late, q_blk[1], q_blk[0]),
                          jnp.where(late, u, u - q_blk[1]), False))
        col_max = {}
        p_next = jnp.minimum(p + 1, nq // 2 - 1)
        for t, (qs, qi, j, masked) in enumerate(items):
            ta = t + ahead
            if ta < len(items):
                col_max[ta] = logits(2 * slot + items[ta][0], items[ta][2], bufs[ta % len(bufs)])
            elif ta == len(items):
                build_queries(p_next, 1 - slot)
                diag_prefetch(p_next, 1 - slot, 0)
            else:
                diag_prefetch(p_next, 1 - slot, 1)
            if masked:
                diag_softmax_accumulate(qs, j, bufs[t % len(bufs)])
            else:
                softmax_accumulate(qs, qi, j, bufs[t % len(bufs)], col_max.pop(t))
        for qs in range(2):
            out = acc_ref[qs, :DIFF_DV, :] * (1.0 / acc_ref[qs, DIFF_DV:DIFF_DV + 1, :])
            y_t = out[:, :tq] - lam * out[:, tq:]
            y_t = y_t * lax.rsqrt(jnp.mean(y_t * y_t, axis=0, keepdims=True) + EPS)
            rows = pl.ds(pl.multiple_of(q_blk[qs] * tq, tq), tq)
            o_ref[0, rows, :] = (y_t.T * hn_ref[...] * (1.0 - lam_init)).astype(BF16)
        return carry

    build_queries(0, 0)
    diag_prefetch(0, 0, 0)
    diag_prefetch(0, 0, 1)
    lax.fori_loop(0, nq // 2, pair, 0)


def _diff_attn(dq, dk, dv, lam_vecs, head_norm, lam_init):
    bsz, s, _ = dq.shape
    tq = ATTN_BLOCK
    slopes = jnp.asarray(
        [2.0 ** (-ALIBI_MAX_BIAS * (i + 1) / DIFF_HEADS) for i in range(DIFF_HEADS)], F32
    ).reshape(DIFF_HEADS, 1, 1)
    pos = np.arange(s) % tq
    feat = np.zeros((s, LANES), np.float32)
    feat[:, 0:3] = (pos - pos % 16)[:, None]
    feat[:, 3:6] = (pos % 16)[:, None]
    return pl.pallas_call(
        functools.partial(_attn_kernel, tq=tq, lam_init=lam_init),
        out_shape=jax.ShapeDtypeStruct((bsz, s, DIFF_HEADS * DIFF_DV), BF16),
        grid=(bsz, DIFF_HEADS),
        in_specs=[
            pl.BlockSpec((1, s, LANES), lambda b, h: (b, 0, h)),
            pl.BlockSpec((1, s, LANES), lambda b, h: (b, 0, h)),
            pl.BlockSpec((1, s, DIFF_DV), lambda b, h: (b, 0, h)),
            _resident((s, LANES)),
            pl.BlockSpec((1, 1, 1), lambda b, h: (h, 0, 0)),
            _resident(lam_vecs.shape),
            _resident((1, DIFF_DV)),
        ],
        out_specs=pl.BlockSpec((1, s, DIFF_DV), lambda b, h: (b, 0, h)),
        scratch_shapes=[
            pltpu.VMEM((s, 2 * LANES), BF16),
            pltpu.VMEM((4, 2 * LANES, 2 * tq), BF16),
            pltpu.VMEM((s // tq, DIFF_DV + SUM_ROWS, tq), BF16),
            pltpu.VMEM((tq, 2 * tq), F32),
            pltpu.VMEM((tq, 2 * tq), F32),
            pltpu.VMEM((tq, 2 * tq), F32),
            pltpu.VMEM((2, 1, 2 * tq), F32),
            pltpu.VMEM((2, DIFF_DV + SUM_ROWS, 2 * tq), F32),
        ],
        compiler_params=_params(("parallel", "parallel")),
        name="diff_attn",
    )(dq, dk, dv, jnp.asarray(feat, BF16), slopes, lam_vecs, head_norm.reshape(1, DIFF_DV))


def _merge_kernel(h_ref, za_ref, zb_ref, sg_ref, gt_ref, wa_ref, wb_ref, wo_ref, o_ref):
    d = h_ref.shape[-1]
    ya = jnp.dot(za_ref[0], wa_ref[...].astype(BF16), preferred_element_type=F32)
    yb = jnp.dot(zb_ref[0], wb_ref[...].astype(BF16), preferred_element_type=F32)
    mix = sg_ref[0, :, :d].astype(F32) * ya + sg_ref[0, :, d:].astype(F32) * yb
    m = jnp.dot(mix.astype(BF16), wo_ref[...].astype(BF16), preferred_element_type=F32)
    o_ref[0] = h_ref[0] + gt_ref[0] * m


def _merge(h, za, zb, sg, gt, w_a, w_b, w_o):
    bsz, s, d = h.shape
    tm = MERGE_ROWS
    vec = pl.BlockSpec((1, 1, d), lambda b, i: (b, 0, 0))

    def rows(w):
        return pl.BlockSpec((1, tm, w), lambda b, i: (b, i, 0))

    return pl.pallas_call(
        _merge_kernel,
        out_shape=jax.ShapeDtypeStruct((bsz, s, d), F32),
        grid=(bsz, s // tm),
        in_specs=[rows(d), rows(za.shape[-1]), rows(zb.shape[-1]), rows(2 * d), vec,
                  _resident(w_a.shape), _resident(w_b.shape), _resident(w_o.shape)],
        out_specs=rows(d),
        compiler_params=_params(("parallel", "parallel")),
        name="merge",
    )(h, za, zb, sg, gt, w_a, w_b, w_o)


def kernel(x, c, w_ada, b_ada, ffn1_norm, ffn1_w_in, ffn1_w_out, mix_norm, w_in, gla_alpha_w2, gla_alpha_b, gla_head_norm, diff_lq1, diff_lk1, diff_lq2, diff_lk2, diff_head_norm, w_branch_a, w_branch_b, w_out, ffn2_norm, ffn2_w_in, ffn2_w_out, final_norm):
    depth = w_ada.shape[0]
    bsz, s, d = x.shape
    assert bsz % GLA_BATCHES == 0 and s % (GLA_STEP_CHUNKS * GLA_CHUNK) == 0
    assert s % MERGE_ROWS == 0 and s % FFN_ROWS == 0 and s % PROJ_ROWS == 0
    assert (s // ATTN_BLOCK) % 2 == 0 and (N_MOD * d) % (ADALN_STEPS * LANES) == 0
    h = x
    for l in range(depth):
        lam_init = 0.8 - 0.6 * math.exp(-0.3 * l)
        mod = _adaln(c, w_ada[l], b_ada[l])
        sh1, sc1, gt1, sh2, sc2, gt2, sh3, sc3, gt3 = [
            mod[:, i * d:(i + 1) * d].reshape(bsz, 1, d) for i in range(N_MOD)]
        last = l == depth - 1
        h = _ffn(h, sh1, sc1, gt1, ffn1_norm[l], ffn1_w_in[l], ffn1_w_out[l], final_norm,
                 final_norm=False)
        w_t = jnp.swapaxes(w_in, 1, 2)[l]
        gq, gk, gv, gr, g, dq, dk, dv, sg = _mixer_proj(
            h, sh2, sc2, mix_norm[l], w_t, gla_alpha_w2[l], gla_alpha_b[l])
        za = _gla(gq, gk, gv, gr, g, gla_head_norm[l])
        lam_vecs = jnp.stack([diff_lq1[l], diff_lk1[l], diff_lq2[l], diff_lk2[l]])
        zb = _diff_attn(dq, dk, dv, lam_vecs, diff_head_norm[l], lam_init)
        h = _merge(h, za, zb, sg, gt2, w_branch_a[l], w_branch_b[l], w_out[l])
        h = _ffn(h, sh3, sc3, gt3, ffn2_norm[l], ffn2_w_in[l], ffn2_w_out[l], final_norm,
                 final_norm=last)
    return h
```

```python
import functools
import itertools
import math

import jax
import jax.numpy as jnp
import numpy as np
from jax import lax
from jax.experimental import pallas as pl
from jax.experimental.pallas import tpu as pltpu

F32 = jnp.float32
BF16 = jnp.bfloat16

EPS = 1e-6
GLA_HEADS = 4
GLA_DK = 64
GLA_DV = 128
GLA_RANK = 16
GLA_TAU = 16.0
DIFF_HEADS = 4
DIFF_DH = 64
DIFF_DV = 128
ALIBI_MAX_BIAS = 8.0
N_MOD = 9

LOG2E = math.log2(math.e)
LANES = 128
MXU_COLS = 256
VMEM_LIMIT = 56 * 1024 * 1024

ADALN_STEPS = 8
FFN_ROWS = 512
FFN_HIDDEN_CHUNK = MXU_COLS
NEXT_PIECES = 8
PROJ_ROWS = 512
GATE_CHUNK = 512
MERGE_ROWS = 1024
MERGE_RING = 3
GLA_CHUNK = 128
GLA_SUB = 16
GLA_BATCHES = 4
GLA_STEP_CHUNKS = 4
ATTN_BLOCK = 512
EXP2_CLAMP = 115.0
SUM_ROWS = 16


def _params(sem):
    return pltpu.CompilerParams(dimension_semantics=sem, vmem_limit_bytes=VMEM_LIMIT)


def _resident(shape):
    nd = len(shape)
    return pl.BlockSpec(shape, lambda *_: (0,) * nd, pipeline_mode=pl.Buffered(1))


def _rmsnorm(x, g):
    return x * lax.rsqrt(jnp.mean(x * x, axis=-1, keepdims=True) + EPS) * g


def _next_input_pieces(xn_ref, un_ref, norm_fn, step):
    piece = un_ref.shape[0] // NEXT_PIECES
    zero = jnp.minimum(step, 0).astype(F32)

    def prepare(i):
        rows = slice(i * piece, (i + 1) * piece)
        un = norm_fn(xn_ref[0, rows, :])
        un_ref[rows, :] = un
        return jnp.sum(un.astype(F32), keepdims=True) * zero

    return prepare


def _next_rows(bsz, nt, tm, d):
    def index(b, i):
        nxt = jnp.minimum(b * nt + i + 1, bsz * nt - 1)
        return nxt // nt, nxt % nt, 0
    return pl.BlockSpec((1, tm, d), index)


def _adaln_kernel(c_ref, w_ref, b_ref, o_ref):
    c = c_ref[...]
    ca = (c * jax.nn.sigmoid(c)).astype(BF16)
    o_ref[...] = jnp.dot(ca, w_ref[...].astype(BF16), preferred_element_type=F32) + b_ref[...]


def _adaln(c, w_ada, b_ada):
    bsz, d = c.shape
    n = w_ada.shape[1]
    tn = n // ADALN_STEPS
    return pl.pallas_call(
        _adaln_kernel,
        out_shape=jax.ShapeDtypeStruct((bsz, n), F32),
        grid=(n // tn,),
        in_specs=[
            pl.BlockSpec((bsz, d), lambda j: (0, 0)),
            pl.BlockSpec((d, tn), lambda j: (0, j)),
            pl.BlockSpec((1, tn), lambda j: (0, j)),
        ],
        out_specs=pl.BlockSpec((bsz, tn), lambda j: (0, j)),
        compiler_params=_params(("arbitrary",)),
        name="adaln",
    )(c, w_ada, b_ada.reshape(1, n))


def _ffn_kernel(x_ref, xn_ref, sh_ref, sc_ref, gt_ref, nw_ref, win_hbm, wout_hbm, fn_ref, o_ref,
                u_ref, un_ref, win_ref, wout_ref, sem_ref, *, d_ff, tf, final_norm):
    n_chunk = d_ff // tf
    assert n_chunk >= NEXT_PIECES
    acc_ref = o_ref.at[0]
    step = pl.program_id(0) * pl.num_programs(1) + pl.program_id(1)
    last_step = pl.num_programs(0) * pl.num_programs(1) - 1

    def normed(x, batch):
        return (_rmsnorm(x, nw_ref[...]) * (1.0 + sc_ref[batch]) + sh_ref[batch]).astype(BF16)

    def weight_copies(i):
        cols = pl.ds(i * tf, tf)
        up = pl.ds(d_ff + i * tf, tf)
        return (pltpu.make_async_copy(win_hbm.at[:, cols], win_ref.at[:, cols], sem_ref.at[0, i]),
                pltpu.make_async_copy(win_hbm.at[:, up], win_ref.at[:, up], sem_ref.at[1, i]),
                pltpu.make_async_copy(wout_hbm.at[cols, :], wout_ref.at[cols, :], sem_ref.at[2, i]))

    def tile(is_first):
        if is_first:
            un_ref[...] = normed(x_ref[0], pl.program_id(0))
        u_ref[...] = un_ref[...]
        u = u_ref[...]
        next_batch = jnp.minimum(step + 1, last_step) // pl.num_programs(1)
        prepare = _next_input_pieces(xn_ref, un_ref, lambda x: normed(x, next_batch), step)
        anchor = None
        for i in range(n_chunk):
            if is_first:
                for copy in weight_copies(i):
                    copy.wait()
            cols = slice(i * tf, (i + 1) * tf)
            up = slice(d_ff + i * tf, d_ff + (i + 1) * tf)
            hg = jnp.dot(u, win_ref[:, cols].astype(BF16), preferred_element_type=F32)
            if anchor is not None:
                hg = hg + anchor
            hu = jnp.dot(u, win_ref[:, up].astype(BF16), preferred_element_type=F32)
            act = (hg * jax.nn.sigmoid(hg) * hu).astype(BF16)
            part = jnp.dot(act, wout_ref[cols, :].astype(BF16), preferred_element_type=F32)
            if i == 0:
                acc_ref[...] = part
            else:
                acc_ref[...] += part
            anchor = prepare(i) if i < NEXT_PIECES else None
        h = x_ref[0] + (0.5 * gt_ref[0]) * acc_ref[...]
        if final_norm:
            h = _rmsnorm(h, fn_ref[...])
        o_ref[0] = h

    first = (pl.program_id(0) == 0) & (pl.program_id(1) == 0)

    @pl.when(first)
    def _():
        for i in range(n_chunk):
            for copy in weight_copies(i):
                copy.start()
        tile(True)

    @pl.when(jnp.logical_not(first))
    def _():
        tile(False)


def _ffn(x, sh, sc, gt, nw, w_in, w_out, fn, *, final_norm):
    bsz, s, d = x.shape
    d_ff = w_out.shape[0]
    tm, tf = FFN_ROWS, FFN_HIDDEN_CHUNK
    nt = s // tm
    vec = pl.BlockSpec((1, 1, d), lambda b, i: (b, 0, 0))
    row = pl.BlockSpec((1, tm, d), lambda b, i: (b, i, 0))
    hbm = pl.BlockSpec(memory_space=pl.ANY)
    return pl.pallas_call(
        functools.partial(_ffn_kernel, d_ff=d_ff, tf=tf, final_norm=final_norm),
        out_shape=jax.ShapeDtypeStruct((bsz, s, d), F32),
        grid=(bsz, nt),
        in_specs=[row, _next_rows(bsz, nt, tm, d), _resident(sh.shape), _resident(sc.shape), vec,
                  _resident((1, d)), hbm, hbm, _resident((1, d))],
        out_specs=row,
        scratch_shapes=[pltpu.VMEM((tm, d), BF16), pltpu.VMEM((tm, d), BF16),
                        pltpu.VMEM((d, 2 * d_ff), F32), pltpu.VMEM((d_ff, d), F32),
                        pltpu.SemaphoreType.DMA((3, d_ff // tf))],
        compiler_params=_params(("arbitrary", "arbitrary")),
        name="ffn_final" if final_norm else "ffn",
    )(x, x, sh, sc, gt, nw.reshape(1, d), w_in, w_out, fn.reshape(1, d))


def _proj_kernel(x_ref, xn_ref, sh_ref, sc_ref, nw_ref, wt_ref, w2_ref, b2_ref,
                 gq_ref, gk_ref, gv_ref, gr_ref, g_ref, dq_ref, dk_ref, dv_ref, sg_ref,
                 u_ref, un_ref):
    kw = GLA_HEADS * GLA_DK
    vw = GLA_HEADS * GLA_DV
    qw = DIFF_HEADS * 2 * DIFF_DH
    dvw = DIFF_HEADS * DIFF_DV
    step = pl.program_id(0) * pl.num_programs(1) + pl.program_id(1)
    last_step = pl.num_programs(0) * pl.num_programs(1) - 1

    def normed(x, batch):
        return (_rmsnorm(x, nw_ref[...]) * (1.0 + sc_ref[batch]) + sh_ref[batch]).astype(BF16)

    @pl.when(step == 0)
    def _():
        un_ref[...] = normed(x_ref[0], 0)

    u_ref[...] = un_ref[...]
    u = u_ref[...]
    next_batch = jnp.minimum(step + 1, last_step) // pl.num_programs(1)
    prepare = _next_input_pieces(xn_ref, un_ref, lambda x: normed(x, next_batch), step)
    pending = {"count": 0, "anchor": None}

    def seg(start, size):
        out = lax.dot_general(u, wt_ref[start:start + size, :].astype(BF16),
                              (((1,), (1,)), ((), ())), preferred_element_type=F32)
        if pending["anchor"] is not None:
            out = out + pending["anchor"]
        pending["anchor"] = prepare(pending["count"]) if pending["count"] < NEXT_PIECES else None
        pending["count"] += 1
        return out

    off = 0
    gq_ref[0] = (seg(off, kw) * (GLA_DK ** -0.5)).astype(BF16)
    off += kw
    gk_ref[0] = seg(off, kw).astype(BF16)
    off += kw
    gv_ref[0] = seg(off, vw).astype(BF16)
    off += vw
    r = seg(off, vw)
    gr_ref[0] = (r * jax.nn.sigmoid(r)).astype(BF16)
    off += vw
    a_low = seg(off, GLA_RANK).astype(BF16)
    z = jnp.dot(a_low, w2_ref[...].astype(BF16), preferred_element_type=F32) + b2_ref[...]
    g_ref[0] = (jnp.minimum(z, 0.0) - jnp.log1p(jnp.exp(-jnp.abs(z)))) * (LOG2E / GLA_TAU)
    off += GLA_RANK
    dq_ref[0] = (seg(off, qw) * (DIFF_DH ** -0.5 * LOG2E)).astype(BF16)
    off += qw
    dk_ref[0] = seg(off, qw).astype(BF16)
    off += qw
    dv_ref[0] = seg(off, dvw).astype(BF16)
    off += dvw
    for j in range(sg_ref.shape[-1] // GATE_CHUNK):
        gate = jax.nn.sigmoid(seg(off + j * GATE_CHUNK, GATE_CHUNK))
        sg_ref[0, :, j * GATE_CHUNK:(j + 1) * GATE_CHUNK] = gate.astype(BF16)
    assert pending["count"] >= NEXT_PIECES


def _mixer_proj(h, sh, sc, nw, w_t, w2, b2):
    bsz, s, d = h.shape
    tm = PROJ_ROWS
    kw = GLA_HEADS * GLA_DK
    vw = GLA_HEADS * GLA_DV
    qw = DIFF_HEADS * 2 * DIFF_DH
    dvw = DIFF_HEADS * DIFF_DV
    n_gate = 2 * d
    vec = pl.BlockSpec((1, 1, d), lambda b, i: (b, 0, 0))

    def rows(w):
        return pl.BlockSpec((1, tm, w), lambda b, i: (b, i, 0))

    widths = [(kw, BF16), (kw, BF16), (vw, BF16), (vw, BF16), (kw, F32),
              (qw, BF16), (qw, BF16), (dvw, BF16), (n_gate, BF16)]
    return pl.pallas_call(
        _proj_kernel,
        out_shape=[jax.ShapeDtypeStruct((bsz, s, w), dt) for w, dt in widths],
        grid=(bsz, s // tm),
        in_specs=[rows(d), _next_rows(bsz, s // tm, tm, d), _resident(sh.shape),
                  _resident(sc.shape), _resident((1, d)), _resident(w_t.shape),
                  _resident(w2.shape), _resident((1, kw))],
        out_specs=[rows(w) for w, _ in widths],
        scratch_shapes=[pltpu.VMEM((tm, d), BF16), pltpu.VMEM((tm, d), BF16)],
        compiler_params=_params(("arbitrary", "arbitrary")),
        name="mixer_proj",
    )(h, h, sh, sc, nw.reshape(1, d), w_t, w2, b2.reshape(1, kw))


def _gla_kernel(q_ref, k_ref, v_ref, r_ref, g_ref, hn_ref, o_ref, st_ref, inter_ref,
                *, chunk, sub):
    @pl.when(pl.program_id(1) == 0)
    def _():
        st_ref[...] = jnp.zeros_like(st_ref)

    nsub = chunk // sub
    row = lax.broadcasted_iota(jnp.int32, (chunk, chunk), 0)
    col = lax.broadcasted_iota(jnp.int32, (chunk, chunk), 1)
    causal = col <= row
    cum_mat = jnp.where(causal, 1.0, 0.0).astype(BF16)
    row_blk = lax.broadcasted_iota(jnp.int32, (chunk, LANES), 0) // sub
    lane_head = lax.broadcasted_iota(jnp.int32, (chunk, LANES), 1) // GLA_DK

    def cumsum(ck, bi, pair):
        return prefix_sums(g_ref[bi, ck * chunk:(ck + 1) * chunk, pair * LANES:(pair + 1) * LANES])

    def prefix_sums(g):
        g1 = g.astype(BF16)
        e1 = g - g1.astype(F32)
        g2 = e1.astype(BF16)
        g3 = (e1 - g2.astype(F32)).astype(BF16)
        cs = jnp.dot(cum_mat, jnp.concatenate([g1, g2, g3], axis=1), preferred_element_type=F32)
        return cs[:, :LANES] + cs[:, LANES:2 * LANES] + cs[:, 2 * LANES:]

    def decay(ck, bi, pair, b):
        lanes = slice(pair * LANES, (pair + 1) * LANES)
        b_last = b[chunk - 1:chunk]
        rows = slice(ck * chunk, (ck + 1) * chunk)
        q = q_ref[bi, rows, lanes].astype(F32)
        k = k_ref[bi, rows, lanes].astype(F32)
        q_cat, w = [], []
        for j in range(nsub):
            ref_b = b[j * sub - 1:j * sub] if j else jnp.zeros_like(b_last)
            q_cat.append((q * jnp.exp2(jnp.minimum(b - ref_b, 0.0))).astype(BF16))
            w.append(b[j * sub:(j + 1) * sub] - ref_b)
        w = jnp.concatenate(w, axis=0)
        return dict(
            q_cat=jnp.concatenate(q_cat, axis=1),
            q_state=(q * jnp.exp2(b)).astype(BF16),
            k_state=k * jnp.exp2(b_last - b),
            k_hat=k * jnp.exp2(jnp.minimum(-w, EXP2_CLAMP)),
            chunk_decay=jnp.exp2(b_last),
            key_exponent=jnp.max(-w, axis=0, keepdims=True))

    def scores(d):
        out = []
        for hh in range(2):
            kh = jnp.where(lane_head == hh, d["k_hat"], 0.0).astype(BF16)
            k_cat = jnp.concatenate(
                [jnp.where(row_blk == j, kh, jnp.zeros_like(kh)) for j in range(nsub)], axis=1)
            out.append(lax.dot_general(d["q_cat"], k_cat, (((1,), (1,)), ((), ())),
                                       preferred_element_type=F32))
        return out

    def finish(ck, bi, pair, d, sc):
        rows = slice(ck * chunk, (ck + 1) * chunk)
        for hh in range(2):
            head = pair * 2 + hh
            cols = slice(head * GLA_DV, (head + 1) * GLA_DV)
            intra = jnp.where(causal, sc[hh], 0.0).astype(BF16)
            vh = v_ref[bi, rows, cols]
            state_t = st_ref[bi * GLA_HEADS + head]
            o_inter = lax.dot_general(d["q_state"], state_t.astype(BF16), (((1,), (1,)), ((), ())),
                                      preferred_element_type=F32)
            inter_ref[(ck * n_batch + bi) * GLA_HEADS + head] = o_inter
            o = jnp.dot(intra, vh, preferred_element_type=F32) + o_inter
            ks = jnp.where(lane_head == hh, d["k_state"], 0.0).astype(BF16)
            kv_t = lax.dot_general(vh, ks, (((0,), (0,)), ((), ())), preferred_element_type=F32)
            st_ref[bi * GLA_HEADS + head] = state_t * d["chunk_decay"] + kv_t
            y = _rmsnorm(o, hn_ref[...]) * r_ref[bi, rows, cols].astype(F32)
            o_ref[bi, rows, cols] = y.astype(BF16)

    n_batch = q_ref.shape[0]
    n_chunk = q_ref.shape[1] // chunk
    per_chunk = list(itertools.product(range(n_batch), range(GLA_HEADS // 2)))
    chains = [(ck, bi, pair) for ck in range(n_chunk) for bi, pair in per_chunk]
    cums = {t: cumsum(*chains[t]) for t in range(len(per_chunk))}
    decayed, scored = {}, {}
    key_exponent = jnp.zeros((1, LANES), F32)
    for t in range(len(chains) + 1):
        if t + len(per_chunk) < len(chains):
            cums[t + len(per_chunk)] = cumsum(*chains[t + len(per_chunk)])
        if t < len(chains):
            decayed[t] = decay(*chains[t], cums.pop(t))
            key_exponent = jnp.maximum(key_exponent, decayed[t]["key_exponent"])
        if t >= 1:
            finish(*chains[t - 1], decayed.pop(t - 1), scored.pop(t - 1))
        if t < len(chains):
            scored[t] = scores(decayed[t])

    @pl.when(jnp.max(key_exponent) > EXP2_CLAMP)
    def _():
        def redo(i, carry):
            ck, bi = i // n_batch, i % n_batch
            rows = pl.ds(pl.multiple_of(ck * chunk, chunk), chunk)
            for pair in range(GLA_HEADS // 2):
                lanes = slice(pair * LANES, (pair + 1) * LANES)
                q = q_ref[bi, rows, lanes].astype(F32)
                k = k_ref[bi, rows, lanes].astype(F32)
                b = prefix_sums(g_ref[bi, rows, lanes])
                q_cat, to_end = [], []
                for j in range(nsub):
                    end_b = b[(j + 1) * sub - 1:(j + 1) * sub]
                    q_cat.append((q * jnp.exp2(jnp.minimum(b - end_b, 0.0))).astype(BF16))
                    to_end.append(end_b - b[j * sub:(j + 1) * sub])
                q_cat = jnp.concatenate(q_cat, axis=1)
                k_end = k * jnp.exp2(jnp.concatenate(to_end, axis=0))
                for hh in range(2):
                    head = pair * 2 + hh
                    cols = slice(head * GLA_DV, (head + 1) * GLA_DV)
                    vh = v_ref[bi, rows, cols]
                    kh = jnp.where(lane_head == hh, k_end, 0.0).astype(BF16)
                    k_cat = jnp.concatenate(
                        [jnp.where(row_blk == j, kh, jnp.zeros_like(kh)) for j in range(nsub)],
                        axis=1)
                    cross = lax.dot_general(q_cat, k_cat, (((1,), (1,)), ((), ())),
                                            preferred_element_type=F32)
                    cross = jnp.where(row // sub > col // sub, cross, 0.0).astype(BF16)
                    o = jnp.dot(cross, vh, preferred_element_type=F32)
                    v32 = vh.astype(F32)
                    in_blk = lax.broadcasted_iota(jnp.int32, (chunk, LANES), 0) % sub
                    for off in range(sub):
                        k_o = pltpu.roll(k, off, 0) if off else k
                        b_o = pltpu.roll(b, off, 0) if off else b
                        v_o = pltpu.roll(v32, off, 0) if off else v32
                        ok = (in_blk >= off) & (lane_head == hh)
                        term = jnp.where(ok, q * k_o * jnp.exp2(jnp.minimum(b - b_o, 0.0)), 0.0)
                        o += jnp.sum(term, axis=1, keepdims=True) * v_o
                    o += inter_ref[(ck * n_batch + bi) * GLA_HEADS + head]
                    y = _rmsnorm(o, hn_ref[...]) * r_ref[bi, rows, cols].astype(F32)
                    o_ref[bi, rows, cols] = y.astype(BF16)
            return carry

        lax.fori_loop(0, n_chunk * n_batch, redo, 0)


def _gla(gq, gk, gv, gr, g, head_norm):
    bsz, s, kw = gq.shape
    vw = gv.shape[-1]
    nb, chunk = GLA_BATCHES, GLA_CHUNK
    step_rows = GLA_STEP_CHUNKS * chunk

    def rows(w):
        return pl.BlockSpec((nb, step_rows, w), lambda b, c: (b, c, 0))

    return pl.pallas_call(
        functools.partial(_gla_kernel, chunk=chunk, sub=GLA_SUB),
        out_shape=jax.ShapeDtypeStruct((bsz, s, vw), BF16),
        grid=(bsz // nb, s // step_rows),
        in_specs=[rows(kw), rows(kw), rows(vw), rows(vw), rows(kw), _resident((1, GLA_DV))],
        out_specs=rows(vw),
        scratch_shapes=[pltpu.VMEM((nb * GLA_HEADS, GLA_DV, LANES), F32),
                        pltpu.VMEM((GLA_STEP_CHUNKS * nb * GLA_HEADS, chunk, GLA_DV), F32)],
        compiler_params=_params(("parallel", "arbitrary")),
        name="gla",
    )(gq, gk, gv, gr, g, head_norm.reshape(1, GLA_DV))


def _attn_kernel(q_ref, k_ref, v_ref, pos_ref, slope_ref, lam_ref, hn_ref, o_ref,
                 kk_ref, qq_ref, vt_ref, s0_ref, s1_ref, s2_ref, m_ref, acc_ref, *, tq, lam_init):
    seq = k_ref.shape[1]
    nq = seq // tq
    hq = tq // 2
    c = slope_ref[0] * LOG2E
    c1 = c.astype(BF16).astype(F32)
    c2 = (c - c1).astype(BF16).astype(F32)
    c3 = c - c1 - c2

    kk_ref[:, :LANES] = k_ref[0]
    kk_ref[:, LANES:] = pos_ref[...]
    sub = lax.broadcasted_iota(jnp.int32, (LANES, 2 * tq), 0)
    cf = jnp.where((sub == 0) | (sub == 3), c1, jnp.where((sub == 1) | (sub == 4), c2, c3))
    slope_rows = jnp.where(sub < 6, cf, 0.0).astype(BF16)
    for i in range(qq_ref.shape[0]):
        qq_ref[i, LANES:, :] = slope_rows
    for jb in range(nq):
        vt_ref[jb, :DIFF_DV, :] = v_ref[0, jb * tq:(jb + 1) * tq, :].astype(F32).T.astype(BF16)
        vt_ref[jb, DIFF_DV:, :] = jnp.ones((SUM_ROWS, tq), BF16)
    lam = lam_ref[...]
    lam = (jnp.exp(jnp.sum(lam[0:1] * lam[1:2], axis=-1, keepdims=True))
           - jnp.exp(jnp.sum(lam[2:3] * lam[3:4], axis=-1, keepdims=True)) + lam_init)

    def logits(qq_slot, j, s_blk):
        start = pl.multiple_of(j * tq, tq)
        s = jnp.dot(kk_ref[pl.ds(start, tq), :], qq_ref[qq_slot], preferred_element_type=F32)
        s_blk[...] = s
        return jnp.max(s.reshape(tq // 8, 8, 2 * tq), axis=0)

    def softmax_accumulate(qs, qi, j, s_blk, mx):
        m_old = m_ref[qs]
        shift = c * jnp.asarray((qi - j) * tq, F32)
        m_new = jnp.maximum(m_old, jnp.max(mx, axis=0, keepdims=True) - shift)
        alpha = jnp.exp2(m_old - m_new)
        m_ref[qs] = m_new
        m_shift = m_new + shift
        probs = jnp.concatenate([jnp.exp2(s_blk[:hq, :] - m_shift).astype(BF16),
                                 jnp.exp2(s_blk[hq:, :] - m_shift).astype(BF16)], axis=0)
        acc_ref[qs] = alpha * acc_ref[qs] + jnp.dot(vt_ref[j], probs, preferred_element_type=F32)

    def late_half(x):
        return jnp.concatenate([x[..., hq:tq], x[..., tq + hq:]], axis=-1)

    def diag_logits(qq_slot, j, s_blk):
        start = pl.multiple_of(j * tq, tq)
        s_blk[:hq, :] = jnp.dot(kk_ref[pl.ds(start, hq), :], qq_ref[qq_slot],
                                preferred_element_type=F32)
        s_blk[hq:, :tq] = jnp.dot(kk_ref[pl.ds(start + hq, hq), :], late_half(qq_ref[qq_slot]),
                                  preferred_element_type=F32)

    def diag_softmax_accumulate(qs, j, s_blk):
        m_old = m_ref[qs]
        key = lax.broadcasted_iota(jnp.int32, (hq, 2 * tq), 0)
        qry = lax.broadcasted_iota(jnp.int32, (hq, 2 * tq), 1) & (tq - 1)
        early = jnp.where(key <= qry, s_blk[:hq, :], -jnp.inf)
        key_l = lax.broadcasted_iota(jnp.int32, (hq, tq), 0)
        qry_l = lax.broadcasted_iota(jnp.int32, (hq, tq), 1) & (hq - 1)
        late = jnp.where(key_l <= qry_l, s_blk[hq:, :tq], -jnp.inf)
        mx_e = jnp.max(jnp.max(early.reshape(hq // 8, 8, 2 * tq), axis=0), axis=0, keepdims=True)
        mx_l = jnp.max(jnp.max(late.reshape(hq // 8, 8, tq), axis=0), axis=0, keepdims=True)
        mx = jnp.concatenate(
            [mx_e[:, :hq], jnp.maximum(mx_e[:, hq:tq], mx_l[:, :hq]),
             mx_e[:, tq:tq + hq], jnp.maximum(mx_e[:, tq + hq:], mx_l[:, hq:])], axis=1)
        m_new = jnp.maximum(m_old, mx)
        alpha = jnp.exp2(m_old - m_new)
        m_ref[qs] = m_new
        p_early = jnp.exp2(early - m_new).astype(BF16)
        p_late = jnp.exp2(late - late_half(m_new)).astype(BF16)
        vt = vt_ref[j]
        acc_ref[qs] = alpha * acc_ref[qs] + jnp.dot(vt[:, :hq], p_early,
                                                    preferred_element_type=F32)
        upd = jnp.dot(vt[:, hq:], p_late, preferred_element_type=F32)
        acc_ref[qs, :, hq:tq] += upd[:, :hq]
        acc_ref[qs, :, tq + hq:] += upd[:, hq:]

    bufs = (s0_ref, s1_ref, s2_ref)
    ahead = len(bufs) - 1
    assert (nq + 1) % len(bufs) == 0

    def build_queries(p, slot):
        for qs, blk in enumerate((p, nq - 1 - p)):
            q_t = q_ref[0, pl.ds(pl.multiple_of(blk * tq, tq), tq), :].astype(F32).T
            half = lax.broadcasted_iota(jnp.int32, (LANES, tq), 0)
            qq_ref[2 * slot + qs, :LANES, :tq] = jnp.where(half < DIFF_DH, q_t, 0.0).astype(BF16)
            qq_ref[2 * slot + qs, :LANES, tq:] = jnp.where(half >= DIFF_DH, q_t, 0.0).astype(BF16)

    def diag_prefetch(p, slot, t):
        qs, blk = ((1, nq - 1 - p), (0, p))[t]
        diag_logits(2 * slot + qs, blk, bufs[t])

    def pair(p, carry):
        slot = p % 2
        q_blk = (p, nq - 1 - p)
        for qs in range(2):
            m_ref[qs] = jnp.full((1, 2 * tq), -jnp.inf, F32)
            acc_ref[qs] = jnp.zeros((DIFF_DV + SUM_ROWS, 2 * tq), F32)
        items = [(1, q_blk[1], q_blk[1], True), (0, q_blk[0], q_blk[0], True)]
        for u in range(nq - 1):
            late = u < q_blk[1]
            items.append((jnp.where(late, 1, 0), jnp.where(late, q_blk[1], q_blk[0]),
                          jnp.where(late, u, u - q_blk[1]), False))
        col_max = {}
        p_next = jnp.minimum(p + 1, nq // 2 - 1)
        for t, (qs, qi, j, masked) in enumerate(items):
            ta = t + ahead
            if ta < len(items):
                col_max[ta] = logits(2 * slot + items[ta][0], items[ta][2], bufs[ta % len(bufs)])
            elif ta == len(items):
                build_queries(p_next, 1 - slot)
                diag_prefetch(p_next, 1 - slot, 0)
            else:
                diag_prefetch(p_next, 1 - slot, 1)
            if masked:
                diag_softmax_accumulate(qs, j, bufs[t % len(bufs)])
            else:
                softmax_accumulate(qs, qi, j, bufs[t % len(bufs)], col_max.pop(t))
        for qs in range(2):
            out = acc_ref[qs, :DIFF_DV, :] * (1.0 / acc_ref[qs, DIFF_DV:DIFF_DV + 1, :])
            y_t = out[:, :tq] - lam * out[:, tq:]
            y_t = y_t * lax.rsqrt(jnp.mean(y_t * y_t, axis=0, keepdims=True) + EPS)
            rows = pl.ds(pl.multiple_of(q_blk[qs] * tq, tq), tq)
            o_ref[0, rows, :] = (y_t.T * hn_ref[...] * (1.0 - lam_init)).astype(BF16)
        return carry

    build_queries(0, 0)
    diag_prefetch(0, 0, 0)
    diag_prefetch(0, 0, 1)
    lax.fori_loop(0, nq // 2, pair, 0)


def _diff_attn(dq, dk, dv, lam_vecs, head_norm, lam_init):
    bsz, s, _ = dq.shape
    tq = ATTN_BLOCK
    slopes = jnp.asarray(
        [2.0 ** (-ALIBI_MAX_BIAS * (i + 1) / DIFF_HEADS) for i in range(DIFF_HEADS)], F32
    ).reshape(DIFF_HEADS, 1, 1)
    pos = np.arange(s) % tq
    feat = np.zeros((s, LANES), np.float32)
    feat[:, 0:3] = (pos - pos % 16)[:, None]
    feat[:, 3:6] = (pos % 16)[:, None]
    return pl.pallas_call(
        functools.partial(_attn_kernel, tq=tq, lam_init=lam_init),
        out_shape=jax.ShapeDtypeStruct((bsz, s, DIFF_HEADS * DIFF_DV), BF16),
        grid=(bsz, DIFF_HEADS),
        in_specs=[
            pl.BlockSpec((1, s, LANES), lambda b, h: (b, 0, h)),
            pl.BlockSpec((1, s, LANES), lambda b, h: (b, 0, h)),
            pl.BlockSpec((1, s, DIFF_DV), lambda b, h: (b, 0, h)),
            _resident((s, LANES)),
            pl.BlockSpec((1, 1, 1), lambda b, h: (h, 0, 0)),
            _resident(lam_vecs.shape),
            _resident((1, DIFF_DV)),
        ],
        out_specs=pl.BlockSpec((1, s, DIFF_DV), lambda b, h: (b, 0, h)),
        scratch_shapes=[
            pltpu.VMEM((s, 2 * LANES), BF16),
            pltpu.VMEM((4, 2 * LANES, 2 * tq), BF16),
            pltpu.VMEM((s // tq, DIFF_DV + SUM_ROWS, tq), BF16),
            pltpu.VMEM((tq, 2 * tq), F32),
            pltpu.VMEM((tq, 2 * tq), F32),
            pltpu.VMEM((tq, 2 * tq), F32),
            pltpu.VMEM((2, 1, 2 * tq), F32),
            pltpu.VMEM((2, DIFF_DV + SUM_ROWS, 2 * tq), F32),
        ],
        compiler_params=_params(("parallel", "parallel")),
        name="diff_attn",
    )(dq, dk, dv, jnp.asarray(feat, BF16), slopes, lam_vecs, head_norm.reshape(1, DIFF_DV))


def _merge_kernel(h_hbm, za_hbm, zb_hbm, sg_hbm, gt_ref, wa_ref, wb_ref, wo_ref, o_ref,
                  h_buf, za_buf, zb_buf, sg_buf, sem_ref, *, tm):
    d = o_ref.shape[-1]
    nt = pl.num_programs(1)
    step = pl.program_id(0) * nt + pl.program_id(1)
    n_steps = pl.num_programs(0) * nt
    streams = ((h_hbm, h_buf), (za_hbm, za_buf), (zb_hbm, zb_buf), (sg_hbm, sg_buf))

    def copies(s):
        rows = pl.ds(pl.multiple_of((s % nt) * tm, tm), tm)
        return [pltpu.make_async_copy(src.at[s // nt, rows, :], buf.at[s % MERGE_RING],
                                      sem_ref.at[k, s % MERGE_RING])
                for k, (src, buf) in enumerate(streams)]

    @pl.when(step == 0)
    def _():
        for s in range(MERGE_RING - 1):
            for copy in copies(jnp.int32(s)):
                copy.start()

    @pl.when(step + (MERGE_RING - 1) < n_steps)
    def _():
        for copy in copies(step + (MERGE_RING - 1)):
            copy.start()

    for copy in copies(step):
        copy.wait()
    slot = step % MERGE_RING
    ya = jnp.dot(za_buf[slot], wa_ref[...].astype(BF16), preferred_element_type=F32)
    yb = jnp.dot(zb_buf[slot], wb_ref[...].astype(BF16), preferred_element_type=F32)
    mix = sg_buf[slot, :, :d].astype(F32) * ya + sg_buf[slot, :, d:].astype(F32) * yb
    m = jnp.dot(mix.astype(BF16), wo_ref[...].astype(BF16), preferred_element_type=F32)
    o_ref[0] = h_buf[slot] + gt_ref[0] * m


def _merge(h, za, zb, sg, gt, w_a, w_b, w_o):
    bsz, s, d = h.shape
    tm = MERGE_ROWS
    assert bsz * (s // tm) >= MERGE_RING - 1
    vec = pl.BlockSpec((1, 1, d), lambda b, i: (b, 0, 0))
    hbm = pl.BlockSpec(memory_space=pl.ANY)
    return pl.pallas_call(
        functools.partial(_merge_kernel, tm=tm),
        out_shape=jax.ShapeDtypeStruct((bsz, s, d), F32),
        grid=(bsz, s // tm),
        in_specs=[hbm, hbm, hbm, hbm, vec,
                  _resident(w_a.shape), _resident(w_b.shape), _resident(w_o.shape)],
        out_specs=pl.BlockSpec((1, tm, d), lambda b, i: (b, i, 0)),
        scratch_shapes=[pltpu.VMEM((MERGE_RING, tm, x.shape[-1]), x.dtype) for x in (h, za, zb, sg)]
        + [pltpu.SemaphoreType.DMA((4, MERGE_RING))],
        compiler_params=_params(("arbitrary", "arbitrary")),
        name="merge",
    )(h, za, zb, sg, gt, w_a, w_b, w_o)


def kernel(x, c, w_ada, b_ada, ffn1_norm, ffn1_w_in, ffn1_w_out, mix_norm, w_in, gla_alpha_w2, gla_alpha_b, gla_head_norm, diff_lq1, diff_lk1, diff_lq2, diff_lk2, diff_head_norm, w_branch_a, w_branch_b, w_out, ffn2_norm, ffn2_w_in, ffn2_w_out, final_norm):
    depth = w_ada.shape[0]
    bsz, s, d = x.shape
    assert bsz % GLA_BATCHES == 0 and s % (GLA_STEP_CHUNKS * GLA_CHUNK) == 0
    assert s % MERGE_ROWS == 0 and s % FFN_ROWS == 0 and s % PROJ_ROWS == 0
    assert (s // ATTN_BLOCK) % 2 == 0 and (N_MOD * d) % (ADALN_STEPS * LANES) == 0
    h = x
    for l in range(depth):
        lam_init = 0.8 - 0.6 * math.exp(-0.3 * l)
        mod = _adaln(c, w_ada[l], b_ada[l])
        sh1, sc1, gt1, sh2, sc2, gt2, sh3, sc3, gt3 = [
            mod[:, i * d:(i + 1) * d].reshape(bsz, 1, d) for i in range(N_MOD)]
        last = l == depth - 1
        h = _ffn(h, sh1, sc1, gt1, ffn1_norm[l], ffn1_w_in[l], ffn1_w_out[l], final_norm,
                 final_norm=False)
        w_t = jnp.swapaxes(w_in, 1, 2)[l]
        gq, gk, gv, gr, g, dq, dk, dv, sg = _mixer_proj(
            h, sh2, sc2, mix_norm[l], w_t, gla_alpha_w2[l], gla_alpha_b[l])
        za = _gla(gq, gk, gv, gr, g, gla_head_norm[l])
        lam_vecs = jnp.stack([diff_lq1[l], diff_lk1[l], diff_lq2[l], diff_lk2[l]])
        zb = _diff_attn(dq, dk, dv, lam_vecs, diff_head_norm[l], lam_init)
        h = _merge(h, za, zb, sg, gt2, w_branch_a[l], w_branch_b[l], w_out[l])
        h = _ffn(h, sh3, sc3, gt3, ffn2_norm[l], ffn2_w_in[l], ffn2_w_out[l], final_norm,
                 final_norm=last)
    return h
```

```python
import functools
import itertools
import math

import jax
import jax.numpy as jnp
import numpy as np
from jax import lax
from jax.experimental import pallas as pl
from jax.experimental.pallas import tpu as pltpu

F32 = jnp.float32
BF16 = jnp.bfloat16

EPS = 1e-6
GLA_HEADS = 4
GLA_DK = 64
GLA_DV = 128
GLA_RANK = 16
GLA_TAU = 16.0
DIFF_HEADS = 4
DIFF_DH = 64
DIFF_DV = 128
ALIBI_MAX_BIAS = 8.0
N_MOD = 9

LOG2E = math.log2(math.e)
LANES = 128
MXU_COLS = 256
VMEM_LIMIT = 56 * 1024 * 1024

ADALN_STEPS = 8
FFN_ROWS = 512
FFN_HIDDEN_CHUNK = MXU_COLS
NEXT_PIECES = 8
PROJ_ROWS = 512
GATE_CHUNK = 512
MERGE_ROWS = 1024
MERGE_RING = 3
GLA_CHUNK = 128
GLA_SUB = 16
GLA_BATCHES = 4
GLA_STEP_CHUNKS = 4
ATTN_BLOCK = 512
EXP2_CLAMP = 115.0
SUM_ROWS = 16


def _params(sem):
    return pltpu.CompilerParams(dimension_semantics=sem, vmem_limit_bytes=VMEM_LIMIT)


def _resident(shape):
    nd = len(shape)
    return pl.BlockSpec(shape, lambda *_: (0,) * nd, pipeline_mode=pl.Buffered(1))


def _rmsnorm(x, g):
    return x * lax.rsqrt(jnp.mean(x * x, axis=-1, keepdims=True) + EPS) * g


def _next_input_pieces(xn_ref, un_ref, norm_fn, step):
    piece = un_ref.shape[0] // NEXT_PIECES
    zero = jnp.minimum(step, 0).astype(F32)

    def prepare(i):
        rows = slice(i * piece, (i + 1) * piece)
        un = norm_fn(xn_ref[0, rows, :])
        un_ref[rows, :] = un
        return jnp.sum(un.astype(F32), keepdims=True) * zero

    return prepare


def _next_rows(bsz, nt, tm, d):
    def index(b, i):
        nxt = jnp.minimum(b * nt + i + 1, bsz * nt - 1)
        return nxt // nt, nxt % nt, 0
    return pl.BlockSpec((1, tm, d), index)


def _adaln_kernel(c_ref, w_hbm, b_ref, o_ref, w_ref, sem_ref):
    tn = w_ref.shape[1] // ADALN_STEPS
    copies = [pltpu.make_async_copy(w_hbm.at[:, pl.ds(j * tn, tn)], w_ref.at[:, pl.ds(j * tn, tn)],
                                    sem_ref.at[j]) for j in range(ADALN_STEPS)]
    for copy in copies:
        copy.start()
    c = c_ref[...]
    ca = (c * jax.nn.sigmoid(c)).astype(BF16)
    for j, copy in enumerate(copies):
        copy.wait()
        cols = slice(j * tn, (j + 1) * tn)
        o_ref[:, cols] = (jnp.dot(ca, w_ref[:, cols].astype(BF16), preferred_element_type=F32)
                          + b_ref[:, cols])


def _adaln(c, w_ada, b_ada):
    bsz, d = c.shape
    n = w_ada.shape[1]
    vmem = pl.BlockSpec(memory_space=pltpu.VMEM)
    return pl.pallas_call(
        _adaln_kernel,
        out_shape=jax.ShapeDtypeStruct((bsz, n), F32),
        in_specs=[vmem, pl.BlockSpec(memory_space=pl.ANY), vmem],
        out_specs=vmem,
        scratch_shapes=[pltpu.VMEM((d, n), F32), pltpu.SemaphoreType.DMA((ADALN_STEPS,))],
        compiler_params=pltpu.CompilerParams(vmem_limit_bytes=VMEM_LIMIT),
        name="adaln",
    )(c, w_ada, b_ada.reshape(1, n))


def _ffn_kernel(x_ref, xn_ref, sh_ref, sc_ref, gt_ref, nw_ref, win_hbm, wout_hbm, fn_ref, o_ref,
                u_ref, un_ref, win_ref, wout_ref, sem_ref, *, d_ff, tf, final_norm):
    n_chunk = d_ff // tf
    assert n_chunk >= NEXT_PIECES
    acc_ref = o_ref.at[0]
    step = pl.program_id(0) * pl.num_programs(1) + pl.program_id(1)
    last_step = pl.num_programs(0) * pl.num_programs(1) - 1

    def normed(x, batch):
        return (_rmsnorm(x, nw_ref[...]) * (1.0 + sc_ref[batch]) + sh_ref[batch]).astype(BF16)

    def weight_copies(i):
        cols = pl.ds(i * tf, tf)
        up = pl.ds(d_ff + i * tf, tf)
        return (pltpu.make_async_copy(win_hbm.at[:, cols], win_ref.at[:, cols], sem_ref.at[0, i]),
                pltpu.make_async_copy(win_hbm.at[:, up], win_ref.at[:, up], sem_ref.at[1, i]),
                pltpu.make_async_copy(wout_hbm.at[cols, :], wout_ref.at[cols, :], sem_ref.at[2, i]))

    def tile(is_first):
        if is_first:
            un_ref[...] = normed(x_ref[0], pl.program_id(0))
        u_ref[...] = un_ref[...]
        u = u_ref[...]
        next_batch = jnp.minimum(step + 1, last_step) // pl.num_programs(1)
        prepare = _next_input_pieces(xn_ref, un_ref, lambda x: normed(x, next_batch), step)
        anchor = None
        for i in range(n_chunk):
            if is_first:
                for copy in weight_copies(i):
                    copy.wait()
            cols = slice(i * tf, (i + 1) * tf)
            up = slice(d_ff + i * tf, d_ff + (i + 1) * tf)
            hg = jnp.dot(u, win_ref[:, cols].astype(BF16), preferred_element_type=F32)
            if anchor is not None:
                hg = hg + anchor
            hu = jnp.dot(u, win_ref[:, up].astype(BF16), preferred_element_type=F32)
            act = (hg * jax.nn.sigmoid(hg) * hu).astype(BF16)
            part = jnp.dot(act, wout_ref[cols, :].astype(BF16), preferred_element_type=F32)
            if i == 0:
                acc_ref[...] = part
            else:
                acc_ref[...] += part
            anchor = prepare(i) if i < NEXT_PIECES else None
        h = x_ref[0] + (0.5 * gt_ref[0]) * acc_ref[...]
        if final_norm:
            h = _rmsnorm(h, fn_ref[...])
        o_ref[0] = h

    first = (pl.program_id(0) == 0) & (pl.program_id(1) == 0)

    @pl.when(first)
    def _():
        for i in range(n_chunk):
            for copy in weight_copies(i):
                copy.start()
        tile(True)

    @pl.when(jnp.logical_not(first))
    def _():
        tile(False)


def _ffn(x, sh, sc, gt, nw, w_in, w_out, fn, *, final_norm):
    bsz, s, d = x.shape
    d_ff = w_out.shape[0]
    tm, tf = FFN_ROWS, FFN_HIDDEN_CHUNK
    nt = s // tm
    vec = pl.BlockSpec((1, 1, d), lambda b, i: (b, 0, 0))
    row = pl.BlockSpec((1, tm, d), lambda b, i: (b, i, 0))
    hbm = pl.BlockSpec(memory_space=pl.ANY)
    return pl.pallas_call(
        functools.partial(_ffn_kernel, d_ff=d_ff, tf=tf, final_norm=final_norm),
        out_shape=jax.ShapeDtypeStruct((bsz, s, d), F32),
        grid=(bsz, nt),
        in_specs=[row, _next_rows(bsz, nt, tm, d), _resident(sh.shape), _resident(sc.shape), vec,
                  _resident((1, d)), hbm, hbm, _resident((1, d))],
        out_specs=row,
        scratch_shapes=[pltpu.VMEM((tm, d), BF16), pltpu.VMEM((tm, d), BF16),
                        pltpu.VMEM((d, 2 * d_ff), F32), pltpu.VMEM((d_ff, d), F32),
                        pltpu.SemaphoreType.DMA((3, d_ff // tf))],
        compiler_params=_params(("arbitrary", "arbitrary")),
        name="ffn_final" if final_norm else "ffn",
    )(x, x, sh, sc, gt, nw.reshape(1, d), w_in, w_out, fn.reshape(1, d))


def _proj_kernel(x_ref, xn_ref, sh_ref, sc_ref, nw_ref, wt_ref, w2_ref, b2_ref,
                 gq_ref, gk_ref, gv_ref, gr_ref, g_ref, dq_ref, dk_ref, dv_ref, sg_ref,
                 u_ref, un_ref):
    kw = GLA_HEADS * GLA_DK
    vw = GLA_HEADS * GLA_DV
    qw = DIFF_HEADS * 2 * DIFF_DH
    dvw = DIFF_HEADS * DIFF_DV
    step = pl.program_id(0) * pl.num_programs(1) + pl.program_id(1)
    last_step = pl.num_programs(0) * pl.num_programs(1) - 1

    def normed(x, batch):
        return (_rmsnorm(x, nw_ref[...]) * (1.0 + sc_ref[batch]) + sh_ref[batch]).astype(BF16)

    @pl.when(step == 0)
    def _():
        un_ref[...] = normed(x_ref[0], 0)

    u_ref[...] = un_ref[...]
    u = u_ref[...]
    next_batch = jnp.minimum(step + 1, last_step) // pl.num_programs(1)
    prepare = _next_input_pieces(xn_ref, un_ref, lambda x: normed(x, next_batch), step)
    pending = {"count": 0, "anchor": None}

    def seg(start, size):
        out = lax.dot_general(u, wt_ref[start:start + size, :].astype(BF16),
                              (((1,), (1,)), ((), ())), preferred_element_type=F32)
        if pending["anchor"] is not None:
            out = out + pending["anchor"]
        pending["anchor"] = prepare(pending["count"]) if pending["count"] < NEXT_PIECES else None
        pending["count"] += 1
        return out

    off = 0
    gq_ref[0] = (seg(off, kw) * (GLA_DK ** -0.5)).astype(BF16)
    off += kw
    gk_ref[0] = seg(off, kw).astype(BF16)
    off += kw
    gv_ref[0] = seg(off, vw).astype(BF16)
    off += vw
    r = seg(off, vw)
    gr_ref[0] = (r * jax.nn.sigmoid(r)).astype(BF16)
    off += vw
    a_low = seg(off, GLA_RANK).astype(BF16)
    z = jnp.dot(a_low, w2_ref[...].astype(BF16), preferred_element_type=F32) + b2_ref[...]
    g_ref[0] = (jnp.minimum(z, 0.0) - jnp.log1p(jnp.exp(-jnp.abs(z)))) * (LOG2E / GLA_TAU)
    off += GLA_RANK
    dq_ref[0] = (seg(off, qw) * (DIFF_DH ** -0.5 * LOG2E)).astype(BF16)
    off += qw
    dk_ref[0] = seg(off, qw).astype(BF16)
    off += qw
    dv_ref[0] = seg(off, dvw).astype(BF16)
    off += dvw
    for j in range(sg_ref.shape[-1] // GATE_CHUNK):
        gate = jax.nn.sigmoid(seg(off + j * GATE_CHUNK, GATE_CHUNK))
        sg_ref[0, :, j * GATE_CHUNK:(j + 1) * GATE_CHUNK] = gate.astype(BF16)
    assert pending["count"] >= NEXT_PIECES


def _mixer_proj(h, sh, sc, nw, w_t, w2, b2):
    bsz, s, d = h.shape
    tm = PROJ_ROWS
    kw = GLA_HEADS * GLA_DK
    vw = GLA_HEADS * GLA_DV
    qw = DIFF_HEADS * 2 * DIFF_DH
    dvw = DIFF_HEADS * DIFF_DV
    n_gate = 2 * d
    vec = pl.BlockSpec((1, 1, d), lambda b, i: (b, 0, 0))

    def rows(w):
        return pl.BlockSpec((1, tm, w), lambda b, i: (b, i, 0))

    widths = [(kw, BF16), (kw, BF16), (vw, BF16), (vw, BF16), (kw, F32),
              (qw, BF16), (qw, BF16), (dvw, BF16), (n_gate, BF16)]
    return pl.pallas_call(
        _proj_kernel,
        out_shape=[jax.ShapeDtypeStruct((bsz, s, w), dt) for w, dt in widths],
        grid=(bsz, s // tm),
        in_specs=[rows(d), _next_rows(bsz, s // tm, tm, d), _resident(sh.shape),
                  _resident(sc.shape), _resident((1, d)), _resident(w_t.shape),
                  _resident(w2.shape), _resident((1, kw))],
        out_specs=[rows(w) for w, _ in widths],
        scratch_shapes=[pltpu.VMEM((tm, d), BF16), pltpu.VMEM((tm, d), BF16)],
        compiler_params=_params(("arbitrary", "arbitrary")),
        name="mixer_proj",
    )(h, h, sh, sc, nw.reshape(1, d), w_t, w2, b2.reshape(1, kw))


def _gla_kernel(q_ref, k_ref, v_ref, r_ref, g_ref, hn_ref, o_ref, st_ref, inter_ref,
                *, chunk, sub):
    @pl.when(pl.program_id(1) == 0)
    def _():
        st_ref[...] = jnp.zeros_like(st_ref)

    nsub = chunk // sub
    row = lax.broadcasted_iota(jnp.int32, (chunk, chunk), 0)
    col = lax.broadcasted_iota(jnp.int32, (chunk, chunk), 1)
    causal = col <= row
    cum_mat = jnp.where(causal, 1.0, 0.0).astype(BF16)
    row_blk = lax.broadcasted_iota(jnp.int32, (chunk, LANES), 0) // sub
    lane_head = lax.broadcasted_iota(jnp.int32, (chunk, LANES), 1) // GLA_DK

    def cumsum(ck, bi, pair):
        return prefix_sums(g_ref[bi, ck * chunk:(ck + 1) * chunk, pair * LANES:(pair + 1) * LANES])

    def prefix_sums(g):
        g1 = g.astype(BF16)
        e1 = g - g1.astype(F32)
        g2 = e1.astype(BF16)
        g3 = (e1 - g2.astype(F32)).astype(BF16)
        cs = jnp.dot(cum_mat, jnp.concatenate([g1, g2, g3], axis=1), preferred_element_type=F32)
        return cs[:, :LANES] + cs[:, LANES:2 * LANES] + cs[:, 2 * LANES:]

    def decay(ck, bi, pair, b):
        lanes = slice(pair * LANES, (pair + 1) * LANES)
        b_last = b[chunk - 1:chunk]
        rows = slice(ck * chunk, (ck + 1) * chunk)
        q = q_ref[bi, rows, lanes].astype(F32)
        k = k_ref[bi, rows, lanes].astype(F32)
        q_cat, w = [], []
        for j in range(nsub):
            ref_b = b[j * sub - 1:j * sub] if j else jnp.zeros_like(b_last)
            q_cat.append((q * jnp.exp2(jnp.minimum(b - ref_b, 0.0))).astype(BF16))
            w.append(b[j * sub:(j + 1) * sub] - ref_b)
        w = jnp.concatenate(w, axis=0)
        return dict(
            q_cat=jnp.concatenate(q_cat, axis=1),
            q_state=(q * jnp.exp2(b)).astype(BF16),
            k_state=k * jnp.exp2(b_last - b),
            k_hat=k * jnp.exp2(jnp.minimum(-w, EXP2_CLAMP)),
            chunk_decay=jnp.exp2(b_last),
            key_exponent=jnp.max(-w, axis=0, keepdims=True))

    def scores(d):
        out = []
        for hh in range(2):
            kh = jnp.where(lane_head == hh, d["k_hat"], 0.0).astype(BF16)
            k_cat = jnp.concatenate(
                [jnp.where(row_blk == j, kh, jnp.zeros_like(kh)) for j in range(nsub)], axis=1)
            out.append(lax.dot_general(d["q_cat"], k_cat, (((1,), (1,)), ((), ())),
                                       preferred_element_type=F32))
        return out

    def finish(ck, bi, pair, d, sc):
        rows = slice(ck * chunk, (ck + 1) * chunk)
        for hh in range(2):
            head = pair * 2 + hh
            cols = slice(head * GLA_DV, (head + 1) * GLA_DV)
            intra = jnp.where(causal, sc[hh], 0.0).astype(BF16)
            vh = v_ref[bi, rows, cols]
            state_t = st_ref[bi * GLA_HEADS + head]
            o_inter = lax.dot_general(d["q_state"], state_t.astype(BF16), (((1,), (1,)), ((), ())),
                                      preferred_element_type=F32)
            inter_ref[(ck * n_batch + bi) * GLA_HEADS + head] = o_inter
            o = jnp.dot(intra, vh, preferred_element_type=F32) + o_inter
            ks = jnp.where(lane_head == hh, d["k_state"], 0.0).astype(BF16)
            kv_t = lax.dot_general(vh, ks, (((0,), (0,)), ((), ())), preferred_element_type=F32)
            st_ref[bi * GLA_HEADS + head] = state_t * d["chunk_decay"] + kv_t
            y = _rmsnorm(o, hn_ref[...]) * r_ref[bi, rows, cols].astype(F32)
            o_ref[bi, rows, cols] = y.astype(BF16)

    n_batch = q_ref.shape[0]
    n_chunk = q_ref.shape[1] // chunk
    per_chunk = list(itertools.product(range(n_batch), range(GLA_HEADS // 2)))
    chains = [(ck, bi, pair) for ck in range(n_chunk) for bi, pair in per_chunk]
    cums = {t: cumsum(*chains[t]) for t in range(len(per_chunk))}
    decayed, scored = {}, {}
    key_exponent = jnp.zeros((1, LANES), F32)
    for t in range(len(chains) + 1):
        if t + len(per_chunk) < len(chains):
            cums[t + len(per_chunk)] = cumsum(*chains[t + len(per_chunk)])
        if t < len(chains):
            decayed[t] = decay(*chains[t], cums.pop(t))
            key_exponent = jnp.maximum(key_exponent, decayed[t]["key_exponent"])
        if t >= 1:
            finish(*chains[t - 1], decayed.pop(t - 1), scored.pop(t - 1))
        if t < len(chains):
            scored[t] = scores(decayed[t])

    @pl.when(jnp.max(key_exponent) > EXP2_CLAMP)
    def _():
        def redo(i, carry):
            ck, bi = i // n_batch, i % n_batch
            rows = pl.ds(pl.multiple_of(ck * chunk, chunk), chunk)
            for pair in range(GLA_HEADS // 2):
                lanes = slice(pair * LANES, (pair + 1) * LANES)
                q = q_ref[bi, rows, lanes].astype(F32)
                k = k_ref[bi, rows, lanes].astype(F32)
                b = prefix_sums(g_ref[bi, rows, lanes])
                q_cat, to_end = [], []
                for j in range(nsub):
                    end_b = b[(j + 1) * sub - 1:(j + 1) * sub]
                    q_cat.append((q * jnp.exp2(jnp.minimum(b - end_b, 0.0))).astype(BF16))
                    to_end.append(end_b - b[j * sub:(j + 1) * sub])
                q_cat = jnp.concatenate(q_cat, axis=1)
                k_end = k * jnp.exp2(jnp.concatenate(to_end, axis=0))
                for hh in range(2):
                    head = pair * 2 + hh
                    cols = slice(head * GLA_DV, (head + 1) * GLA_DV)
                    vh = v_ref[bi, rows, cols]
                    kh = jnp.where(lane_head == hh, k_end, 0.0).astype(BF16)
                    k_cat = jnp.concatenate(
                        [jnp.where(row_blk == j, kh, jnp.zeros_like(kh)) for j in range(nsub)],
                        axis=1)
                    cross = lax.dot_general(q_cat, k_cat, (((1,), (1,)), ((), ())),
                                            preferred_element_type=F32)
                    cross = jnp.where(row // sub > col // sub, cross, 0.0).astype(BF16)
                    o = jnp.dot(cross, vh, preferred_element_type=F32)
                    v32 = vh.astype(F32)
                    in_blk = lax.broadcasted_iota(jnp.int32, (chunk, LANES), 0) % sub
                    for off in range(sub):
                        k_o = pltpu.roll(k, off, 0) if off else k
                        b_o = pltpu.roll(b, off, 0) if off else b
                        v_o = pltpu.roll(v32, off, 0) if off else v32
                        ok = (in_blk >= off) & (lane_head == hh)
                        term = jnp.where(ok, q * k_o * jnp.exp2(jnp.minimum(b - b_o, 0.0)), 0.0)
                        o += jnp.sum(term, axis=1, keepdims=True) * v_o
                    o += inter_ref[(ck * n_batch + bi) * GLA_HEADS + head]
                    y = _rmsnorm(o, hn_ref[...]) * r_ref[bi, rows, cols].astype(F32)
                    o_ref[bi, rows, cols] = y.astype(BF16)
            return carry

        lax.fori_loop(0, n_chunk * n_batch, redo, 0)


def _gla(gq, gk, gv, gr, g, head_norm):
    bsz, s, kw = gq.shape
    vw = gv.shape[-1]
    nb, chunk = GLA_BATCHES, GLA_CHUNK
    step_rows = GLA_STEP_CHUNKS * chunk

    def rows(w):
        return pl.BlockSpec((nb, step_rows, w), lambda b, c: (b, c, 0))

    return pl.pallas_call(
        functools.partial(_gla_kernel, chunk=chunk, sub=GLA_SUB),
        out_shape=jax.ShapeDtypeStruct((bsz, s, vw), BF16),
        grid=(bsz // nb, s // step_rows),
        in_specs=[rows(kw), rows(kw), rows(vw), rows(vw), rows(kw), _resident((1, GLA_DV))],
        out_specs=rows(vw),
        scratch_shapes=[pltpu.VMEM((nb * GLA_HEADS, GLA_DV, LANES), F32),
                        pltpu.VMEM((GLA_STEP_CHUNKS * nb * GLA_HEADS, chunk, GLA_DV), F32)],
        compiler_params=_params(("parallel", "arbitrary")),
        name="gla",
    )(gq, gk, gv, gr, g, head_norm.reshape(1, GLA_DV))


def _attn_kernel(q_ref, k_ref, v_ref, pos_ref, slope_ref, lam_ref, hn_ref, o_ref,
                 kk_ref, qq_ref, vt_ref, s0_ref, s1_ref, s2_ref, m_ref, acc_ref, *, tq, lam_init):
    seq = k_ref.shape[1]
    nq = seq // tq
    hq = tq // 2
    c = slope_ref[0] * LOG2E
    c1 = c.astype(BF16).astype(F32)
    c2 = (c - c1).astype(BF16).astype(F32)
    c3 = c - c1 - c2

    kk_ref[:, :LANES] = k_ref[0]
    kk_ref[:, LANES:] = pos_ref[...]
    sub = lax.broadcasted_iota(jnp.int32, (LANES, 2 * tq), 0)
    cf = jnp.where((sub == 0) | (sub == 3), c1, jnp.where((sub == 1) | (sub == 4), c2, c3))
    slope_rows = jnp.where(sub < 6, cf, 0.0).astype(BF16)
    for i in range(qq_ref.shape[0]):
        qq_ref[i, LANES:, :] = slope_rows
    for jb in range(nq):
        vt_ref[jb, :DIFF_DV, :] = v_ref[0, jb * tq:(jb + 1) * tq, :].astype(F32).T.astype(BF16)
        vt_ref[jb, DIFF_DV:, :] = jnp.ones((SUM_ROWS, tq), BF16)
    lam = lam_ref[...]
    lam = (jnp.exp(jnp.sum(lam[0:1] * lam[1:2], axis=-1, keepdims=True))
           - jnp.exp(jnp.sum(lam[2:3] * lam[3:4], axis=-1, keepdims=True)) + lam_init)

    def logits(qq_slot, j, s_blk):
        start = pl.multiple_of(j * tq, tq)
        s = jnp.dot(kk_ref[pl.ds(start, tq), :], qq_ref[qq_slot], preferred_element_type=F32)
        s_blk[...] = s
        return jnp.max(s.reshape(tq // 8, 8, 2 * tq), axis=0)

    def softmax_accumulate(qs, qi, j, s_blk, mx):
        m_old = m_ref[qs]
        shift = c * jnp.asarray((qi - j) * tq, F32)
        m_new = jnp.maximum(m_old, jnp.max(mx, axis=0, keepdims=True) - shift)
        alpha = jnp.exp2(m_old - m_new)
        m_ref[qs] = m_new
        m_shift = m_new + shift
        probs = jnp.concatenate([jnp.exp2(s_blk[:hq, :] - m_shift).astype(BF16),
                                 jnp.exp2(s_blk[hq:, :] - m_shift).astype(BF16)], axis=0)
        acc_ref[qs] = alpha * acc_ref[qs] + jnp.dot(vt_ref[j], probs, preferred_element_type=F32)

    def late_half(x):
        return jnp.concatenate([x[..., hq:tq], x[..., tq + hq:]], axis=-1)

    def diag_logits(qq_slot, j, s_blk):
        start = pl.multiple_of(j * tq, tq)
        s_blk[:hq, :] = jnp.dot(kk_ref[pl.ds(start, hq), :], qq_ref[qq_slot],
                                preferred_element_type=F32)
        s_blk[hq:, :tq] = jnp.dot(kk_ref[pl.ds(start + hq, hq), :], late_half(qq_ref[qq_slot]),
                                  preferred_element_type=F32)

    def diag_softmax_accumulate(qs, j, s_blk):
        m_old = m_ref[qs]
        key = lax.broadcasted_iota(jnp.int32, (hq, 2 * tq), 0)
        qry = lax.broadcasted_iota(jnp.int32, (hq, 2 * tq), 1) & (tq - 1)
        early = jnp.where(key <= qry, s_blk[:hq, :], -jnp.inf)
        key_l = lax.broadcasted_iota(jnp.int32, (hq, tq), 0)
        qry_l = lax.broadcasted_iota(jnp.int32, (hq, tq), 1) & (hq - 1)
        late = jnp.where(key_l <= qry_l, s_blk[hq:, :tq], -jnp.inf)
        mx_e = jnp.max(jnp.max(early.reshape(hq // 8, 8, 2 * tq), axis=0), axis=0, keepdims=True)
        mx_l = jnp.max(jnp.max(late.reshape(hq // 8, 8, tq), axis=0), axis=0, keepdims=True)
        mx = jnp.concatenate(
            [mx_e[:, :hq], jnp.maximum(mx_e[:, hq:tq], mx_l[:, :hq]),
             mx_e[:, tq:tq + hq], jnp.maximum(mx_e[:, tq + hq:], mx_l[:, hq:])], axis=1)
        m_new = jnp.maximum(m_old, mx)
        alpha = jnp.exp2(m_old - m_new)
        m_ref[qs] = m_new
        p_early = jnp.exp2(early - m_new).astype(BF16)
        p_late = jnp.exp2(late - late_half(m_new)).astype(BF16)
        vt = vt_ref[j]
        acc_ref[qs] = alpha * acc_ref[qs] + jnp.dot(vt[:, :hq], p_early,
                                                    preferred_element_type=F32)
        upd = jnp.dot(vt[:, hq:], p_late, preferred_element_type=F32)
        acc_ref[qs, :, hq:tq] += upd[:, :hq]
        acc_ref[qs, :, tq + hq:] += upd[:, hq:]

    bufs = (s0_ref, s1_ref, s2_ref)
    ahead = len(bufs) - 1
    assert (nq + 1) % len(bufs) == 0

    def build_queries(p, slot):
        for qs, blk in enumerate((p, nq - 1 - p)):
            q_t = q_ref[0, pl.ds(pl.multiple_of(blk * tq, tq), tq), :].astype(F32).T
            half = lax.broadcasted_iota(jnp.int32, (LANES, tq), 0)
            qq_ref[2 * slot + qs, :LANES, :tq] = jnp.where(half < DIFF_DH, q_t, 0.0).astype(BF16)
            qq_ref[2 * slot + qs, :LANES, tq:] = jnp.where(half >= DIFF_DH, q_t, 0.0).astype(BF16)

    def diag_prefetch(p, slot, t):
        qs, blk = ((1, nq - 1 - p), (0, p))[t]
        diag_logits(2 * slot + qs, blk, bufs[t])

    def pair(p, carry):
        slot = p % 2
        q_blk = (p, nq - 1 - p)
        for qs in range(2):
            m_ref[qs] = jnp.full((1, 2 * tq), -jnp.inf, F32)
            acc_ref[qs] = jnp.zeros((DIFF_DV + SUM_ROWS, 2 * tq), F32)
        items = [(1, q_blk[1], q_blk[1], True), (0, q_blk[0], q_blk[0], True)]
        for u in range(nq - 1):
            late = u < q_blk[1]
            items.append((jnp.where(late, 1, 0), jnp.where(late, q_blk[1], q_blk[0]),
                          jnp.where(late, u, u - q_blk[1]), False))
        col_max = {}
        p_next = jnp.minimum(p + 1, nq // 2 - 1)
        for t, (qs, qi, j, masked) in enumerate(items):
            ta = t + ahead
            if ta < len(items):
                col_max[ta] = logits(2 * slot + items[ta][0], items[ta][2], bufs[ta % len(bufs)])
            elif ta == len(items):
                build_queries(p_next, 1 - slot)
                diag_prefetch(p_next, 1 - slot, 0)
            else:
                diag_prefetch(p_next, 1 - slot, 1)
            if masked:
                diag_softmax_accumulate(qs, j, bufs[t % len(bufs)])
            else:
                softmax_accumulate(qs, qi, j, bufs[t % len(bufs)], col_max.pop(t))
        for qs in range(2):
            out = acc_ref[qs, :DIFF_DV, :] * (1.0 / acc_ref[qs, DIFF_DV:DIFF_DV + 1, :])
            y_t = out[:, :tq] - lam * out[:, tq:]
            y_t = y_t * lax.rsqrt(jnp.mean(y_t * y_t, axis=0, keepdims=True) + EPS)
            rows = pl.ds(pl.multiple_of(q_blk[qs] * tq, tq), tq)
            o_ref[0, rows, :] = (y_t.T * hn_ref[...] * (1.0 - lam_init)).astype(BF16)
        return carry

    build_queries(0, 0)
    diag_prefetch(0, 0, 0)
    diag_prefetch(0, 0, 1)
    lax.fori_loop(0, nq // 2, pair, 0)


def _diff_attn(dq, dk, dv, lam_vecs, head_norm, lam_init):
    bsz, s, _ = dq.shape
    tq = ATTN_BLOCK
    slopes = jnp.asarray(
        [2.0 ** (-ALIBI_MAX_BIAS * (i + 1) / DIFF_HEADS) for i in range(DIFF_HEADS)], F32
    ).reshape(DIFF_HEADS, 1, 1)
    pos = np.arange(s) % tq
    feat = np.zeros((s, LANES), np.float32)
    feat[:, 0:3] = (pos - pos % 16)[:, None]
    feat[:, 3:6] = (pos % 16)[:, None]
    return pl.pallas_call(
        functools.partial(_attn_kernel, tq=tq, lam_init=lam_init),
        out_shape=jax.ShapeDtypeStruct((bsz, s, DIFF_HEADS * DIFF_DV), BF16),
        grid=(bsz, DIFF_HEADS),
        in_specs=[
            pl.BlockSpec((1, s, LANES), lambda b, h: (b, 0, h)),
            pl.BlockSpec((1, s, LANES), lambda b, h: (b, 0, h)),
            pl.BlockSpec((1, s, DIFF_DV), lambda b, h: (b, 0, h)),
            _resident((s, LANES)),
            pl.BlockSpec((1, 1, 1), lambda b, h: (h, 0, 0)),
            _resident(lam_vecs.shape),
            _resident((1, DIFF_DV)),
        ],
        out_specs=pl.BlockSpec((1, s, DIFF_DV), lambda b, h: (b, 0, h)),
        scratch_shapes=[
            pltpu.VMEM((s, 2 * LANES), BF16),
            pltpu.VMEM((4, 2 * LANES, 2 * tq), BF16),
            pltpu.VMEM((s // tq, DIFF_DV + SUM_ROWS, tq), BF16),
            pltpu.VMEM((tq, 2 * tq), F32),
            pltpu.VMEM((tq, 2 * tq), F32),
            pltpu.VMEM((tq, 2 * tq), F32),
            pltpu.VMEM((2, 1, 2 * tq), F32),
            pltpu.VMEM((2, DIFF_DV + SUM_ROWS, 2 * tq), F32),
        ],
        compiler_params=_params(("parallel", "parallel")),
        name="diff_attn",
    )(dq, dk, dv, jnp.asarray(feat, BF16), slopes, lam_vecs, head_norm.reshape(1, DIFF_DV))


def _merge_kernel(h_hbm, za_hbm, zb_hbm, sg_hbm, gt_ref, wa_ref, wb_ref, wo_ref, o_ref,
                  h_buf, za_buf, zb_buf, sg_buf, sem_ref, *, tm):
    d = o_ref.shape[-1]
    nt = pl.num_programs(1)
    step = pl.program_id(0) * nt + pl.program_id(1)
    n_steps = pl.num_programs(0) * nt
    streams = ((h_hbm, h_buf), (za_hbm, za_buf), (zb_hbm, zb_buf), (sg_hbm, sg_buf))

    def copies(s):
        rows = pl.ds(pl.multiple_of((s % nt) * tm, tm), tm)
        return [pltpu.make_async_copy(src.at[s // nt, rows, :], buf.at[s % MERGE_RING],
                                      sem_ref.at[k, s % MERGE_RING])
                for k, (src, buf) in enumerate(streams)]

    @pl.when(step == 0)
    def _():
        for s in range(MERGE_RING - 1):
            for copy in copies(jnp.int32(s)):
                copy.start()

    @pl.when(step + (MERGE_RING - 1) < n_steps)
    def _():
        for copy in copies(step + (MERGE_RING - 1)):
            copy.start()

    for copy in copies(step):
        copy.wait()
    slot = step % MERGE_RING
    ya = jnp.dot(za_buf[slot], wa_ref[...].astype(BF16), preferred_element_type=F32)
    yb = jnp.dot(zb_buf[slot], wb_ref[...].astype(BF16), preferred_element_type=F32)
    mix = sg_buf[slot, :, :d].astype(F32) * ya + sg_buf[slot, :, d:].astype(F32) * yb
    m = jnp.dot(mix.astype(BF16), wo_ref[...].astype(BF16), preferred_element_type=F32)
    o_ref[0] = h_buf[slot] + gt_ref[0] * m


def _merge(h, za, zb, sg, gt, w_a, w_b, w_o):
    bsz, s, d = h.shape
    tm = MERGE_ROWS
    assert bsz * (s // tm) >= MERGE_RING - 1
    vec = pl.BlockSpec((1, 1, d), lambda b, i: (b, 0, 0))
    hbm = pl.BlockSpec(memory_space=pl.ANY)
    return pl.pallas_call(
        functools.partial(_merge_kernel, tm=tm),
        out_shape=jax.ShapeDtypeStruct((bsz, s, d), F32),
        grid=(bsz, s // tm),
        in_specs=[hbm, hbm, hbm, hbm, vec,
                  _resident(w_a.shape), _resident(w_b.shape), _resident(w_o.shape)],
        out_specs=pl.BlockSpec((1, tm, d), lambda b, i: (b, i, 0)),
        scratch_shapes=[pltpu.VMEM((MERGE_RING, tm, x.shape[-1]), x.dtype) for x in (h, za, zb, sg)]
        + [pltpu.SemaphoreType.DMA((4, MERGE_RING))],
        compiler_params=_params(("arbitrary", "arbitrary")),
        name="merge",
    )(h, za, zb, sg, gt, w_a, w_b, w_o)


def kernel(x, c, w_ada, b_ada, ffn1_norm, ffn1_w_in, ffn1_w_out, mix_norm, w_in, gla_alpha_w2, gla_alpha_b, gla_head_norm, diff_lq1, diff_lk1, diff_lq2, diff_lk2, diff_head_norm, w_branch_a, w_branch_b, w_out, ffn2_norm, ffn2_w_in, ffn2_w_out, final_norm):
    depth = w_ada.shape[0]
    bsz, s, d = x.shape
    assert bsz % GLA_BATCHES == 0 and s % (GLA_STEP_CHUNKS * GLA_CHUNK) == 0
    assert s % MERGE_ROWS == 0 and s % FFN_ROWS == 0 and s % PROJ_ROWS == 0
    assert (s // ATTN_BLOCK) % 2 == 0 and (N_MOD * d) % (ADALN_STEPS * LANES) == 0
    h = x
    for l in range(depth):
        lam_init = 0.8 - 0.6 * math.exp(-0.3 * l)
        mod = _adaln(c, w_ada[l], b_ada[l])
        sh1, sc1, gt1, sh2, sc2, gt2, sh3, sc3, gt3 = [
            mod[:, i * d:(i + 1) * d].reshape(bsz, 1, d) for i in range(N_MOD)]
        last = l == depth - 1
        h = _ffn(h, sh1, sc1, gt1, ffn1_norm[l], ffn1_w_in[l], ffn1_w_out[l], final_norm,
                 final_norm=False)
        w_t = jnp.swapaxes(w_in, 1, 2)[l]
        gq, gk, gv, gr, g, dq, dk, dv, sg = _mixer_proj(
            h, sh2, sc2, mix_norm[l], w_t, gla_alpha_w2[l], gla_alpha_b[l])
        za = _gla(gq, gk, gv, gr, g, gla_head_norm[l])
        lam_vecs = jnp.stack([diff_lq1[l], diff_lk1[l], diff_lq2[l], diff_lk2[l]])
        zb = _diff_attn(dq, dk, dv, lam_vecs, diff_head_norm[l], lam_init)
        h = _merge(h, za, zb, sg, gt2, w_branch_a[l], w_branch_b[l], w_out[l])
        h = _ffn(h, sh3, sc3, gt3, ffn2_norm[l], ffn2_w_in[l], ffn2_w_out[l], final_norm,
                 final_norm=last)
    return h
```

```python
import functools
import itertools
import math

import jax
import jax.numpy as jnp
import numpy as np
from jax import lax
from jax.experimental import pallas as pl
from jax.experimental.pallas import tpu as pltpu

F32 = jnp.float32
BF16 = jnp.bfloat16

EPS = 1e-6
GLA_HEADS = 4
GLA_DK = 64
GLA_DV = 128
GLA_RANK = 16
GLA_TAU = 16.0
DIFF_HEADS = 4
DIFF_DH = 64
DIFF_DV = 128
ALIBI_MAX_BIAS = 8.0
N_MOD = 9

LOG2E = math.log2(math.e)
LANES = 128
MXU_COLS = 256
VMEM_LIMIT = 56 * 1024 * 1024

ADALN_STEPS = 8
FFN_ROWS = 512
FFN_HIDDEN_CHUNK = MXU_COLS
NEXT_PIECES = 8
PROJ_ROWS = 512
GATE_CHUNK = 512
MERGE_ROWS = 1024
MERGE_RING = 3
GLA_CHUNK = 128
GLA_SUB = 16
GLA_BATCHES = 4
GLA_STEP_CHUNKS = 4
ATTN_BLOCK = 512
EXP2_CLAMP = 115.0
SUM_ROWS = 16


def _params(sem):
    return pltpu.CompilerParams(dimension_semantics=sem, vmem_limit_bytes=VMEM_LIMIT)


def _resident(shape):
    nd = len(shape)
    return pl.BlockSpec(shape, lambda *_: (0,) * nd, pipeline_mode=pl.Buffered(1))


def _rmsnorm(x, g):
    return x * lax.rsqrt(jnp.mean(x * x, axis=-1, keepdims=True) + EPS) * g


def _next_input_pieces(xn_ref, un_ref, norm_fn, step):
    piece = un_ref.shape[0] // NEXT_PIECES
    zero = jnp.minimum(step, 0).astype(F32)

    def prepare(i):
        rows = slice(i * piece, (i + 1) * piece)
        un = norm_fn(xn_ref[0, rows, :])
        un_ref[rows, :] = un
        return jnp.sum(un.astype(F32), keepdims=True) * zero

    return prepare


def _next_rows(bsz, nt, tm, d):
    def index(b, i):
        nxt = jnp.minimum(b * nt + i + 1, bsz * nt - 1)
        return nxt // nt, nxt % nt, 0
    return pl.BlockSpec((1, tm, d), index)


def _adaln_kernel(c_ref, w_hbm, b_ref, o_ref, w_ref, sem_ref):
    tn = w_ref.shape[1] // ADALN_STEPS
    copies = [pltpu.make_async_copy(w_hbm.at[:, pl.ds(j * tn, tn)], w_ref.at[:, pl.ds(j * tn, tn)],
                                    sem_ref.at[j]) for j in range(ADALN_STEPS)]
    for copy in copies:
        copy.start()
    c = c_ref[...]
    ca = (c * jax.nn.sigmoid(c)).astype(BF16)
    for j, copy in enumerate(copies):
        copy.wait()
        cols = slice(j * tn, (j + 1) * tn)
        o_ref[:, cols] = (jnp.dot(ca, w_ref[:, cols].astype(BF16), preferred_element_type=F32)
                          + b_ref[:, cols])


def _adaln(c, w_ada, b_ada):
    bsz, d = c.shape
    n = w_ada.shape[1]
    vmem = pl.BlockSpec(memory_space=pltpu.VMEM)
    return pl.pallas_call(
        _adaln_kernel,
        out_shape=jax.ShapeDtypeStruct((bsz, n), F32),
        in_specs=[vmem, pl.BlockSpec(memory_space=pl.ANY), vmem],
        out_specs=vmem,
        scratch_shapes=[pltpu.VMEM((d, n), F32), pltpu.SemaphoreType.DMA((ADALN_STEPS,))],
        compiler_params=pltpu.CompilerParams(vmem_limit_bytes=VMEM_LIMIT),
        name="adaln",
    )(c, w_ada, b_ada.reshape(1, n))


def _ffn_kernel(x_ref, xn_ref, sh_ref, sc_ref, gt_ref, nw_ref, win_hbm, wout_hbm, fn_ref, o_ref,
                u_ref, un_ref, win_ref, wout_ref, sem_ref, *, d_ff, tf, final_norm):
    n_chunk = d_ff // tf
    assert n_chunk >= NEXT_PIECES
    acc_ref = o_ref.at[0]
    step = pl.program_id(0) * pl.num_programs(1) + pl.program_id(1)
    last_step = pl.num_programs(0) * pl.num_programs(1) - 1

    def normed(x, batch):
        return (_rmsnorm(x, nw_ref[...]) * (1.0 + sc_ref[batch]) + sh_ref[batch]).astype(BF16)

    def weight_copies(i):
        cols = pl.ds(i * tf, tf)
        up = pl.ds(d_ff + i * tf, tf)
        return (pltpu.make_async_copy(win_hbm.at[:, cols], win_ref.at[:, cols], sem_ref.at[0, i]),
                pltpu.make_async_copy(win_hbm.at[:, up], win_ref.at[:, up], sem_ref.at[1, i]),
                pltpu.make_async_copy(wout_hbm.at[cols, :], wout_ref.at[cols, :], sem_ref.at[2, i]))

    def tile(is_first):
        if is_first:
            un_ref[...] = normed(x_ref[0], pl.program_id(0))
        u_ref[...] = un_ref[...]
        u = u_ref[...]
        next_batch = jnp.minimum(step + 1, last_step) // pl.num_programs(1)
        prepare = _next_input_pieces(xn_ref, un_ref, lambda x: normed(x, next_batch), step)
        anchor = None
        for i in range(n_chunk):
            if is_first:
                for copy in weight_copies(i):
                    copy.wait()
            cols = slice(i * tf, (i + 1) * tf)
            up = slice(d_ff + i * tf, d_ff + (i + 1) * tf)
            hg = jnp.dot(u, win_ref[:, cols].astype(BF16), preferred_element_type=F32)
            if anchor is not None:
                hg = hg + anchor
            hu = jnp.dot(u, win_ref[:, up].astype(BF16), preferred_element_type=F32)
            act = (hg * jax.nn.sigmoid(hg) * hu).astype(BF16)
            part = jnp.dot(act, wout_ref[cols, :].astype(BF16), preferred_element_type=F32)
            if i == 0:
                acc_ref[...] = part
            else:
                acc_ref[...] += part
            anchor = prepare(i) if i < NEXT_PIECES else None
        h = x_ref[0] + (0.5 * gt_ref[0]) * acc_ref[...]
        if final_norm:
            h = _rmsnorm(h, fn_ref[...])
        o_ref[0] = h

    first = (pl.program_id(0) == 0) & (pl.program_id(1) == 0)

    @pl.when(first)
    def _():
        for i in range(n_chunk):
            for copy in weight_copies(i):
                copy.start()
        tile(True)

    @pl.when(jnp.logical_not(first))
    def _():
        tile(False)


def _ffn(x, sh, sc, gt, nw, w_in, w_out, fn, *, final_norm):
    bsz, s, d = x.shape
    d_ff = w_out.shape[0]
    tm, tf = FFN_ROWS, FFN_HIDDEN_CHUNK
    nt = s // tm
    vec = pl.BlockSpec((1, 1, d), lambda b, i: (b, 0, 0))
    row = pl.BlockSpec((1, tm, d), lambda b, i: (b, i, 0))
    hbm = pl.BlockSpec(memory_space=pl.ANY)
    return pl.pallas_call(
        functools.partial(_ffn_kernel, d_ff=d_ff, tf=tf, final_norm=final_norm),
        out_shape=jax.ShapeDtypeStruct((bsz, s, d), F32),
        grid=(bsz, nt),
        in_specs=[row, _next_rows(bsz, nt, tm, d), _resident(sh.shape), _resident(sc.shape), vec,
                  _resident((1, d)), hbm, hbm, _resident((1, d))],
        out_specs=row,
        scratch_shapes=[pltpu.VMEM((tm, d), BF16), pltpu.VMEM((tm, d), BF16),
                        pltpu.VMEM((d, 2 * d_ff), F32), pltpu.VMEM((d_ff, d), F32),
                        pltpu.SemaphoreType.DMA((3, d_ff // tf))],
        compiler_params=_params(("arbitrary", "arbitrary")),
        name="ffn_final" if final_norm else "ffn",
    )(x, x, sh, sc, gt, nw.reshape(1, d), w_in, w_out, fn.reshape(1, d))


def _proj_kernel(x_ref, xn_ref, sh_ref, sc_ref, nw_ref, wt_ref, w2_ref, b2_ref,
                 gq_ref, gk_ref, gv_ref, gr_ref, g_ref, dq_ref, dk_ref, dv_ref, sg_ref,
                 u_ref, un_ref):
    kw = GLA_HEADS * GLA_DK
    vw = GLA_HEADS * GLA_DV
    qw = DIFF_HEADS * 2 * DIFF_DH
    dvw = DIFF_HEADS * DIFF_DV
    step = pl.program_id(0) * pl.num_programs(1) + pl.program_id(1)
    last_step = pl.num_programs(0) * pl.num_programs(1) - 1

    def normed(x, batch):
        return (_rmsnorm(x, nw_ref[...]) * (1.0 + sc_ref[batch]) + sh_ref[batch]).astype(BF16)

    @pl.when(step == 0)
    def _():
        un_ref[...] = normed(x_ref[0], 0)

    u_ref[...] = un_ref[...]
    u = u_ref[...]
    next_batch = jnp.minimum(step + 1, last_step) // pl.num_programs(1)
    prepare = _next_input_pieces(xn_ref, un_ref, lambda x: normed(x, next_batch), step)
    pending = {"count": 0, "anchor": None}

    def seg(start, size):
        out = lax.dot_general(u, wt_ref[start:start + size, :].astype(BF16),
                              (((1,), (1,)), ((), ())), preferred_element_type=F32)
        if pending["anchor"] is not None:
            out = out + pending["anchor"]
        pending["anchor"] = prepare(pending["count"]) if pending["count"] < NEXT_PIECES else None
        pending["count"] += 1
        return out

    off = 0
    gq_ref[0] = (seg(off, kw) * (GLA_DK ** -0.5)).astype(BF16)
    off += kw
    gk_ref[0] = seg(off, kw).astype(BF16)
    off += kw
    gv_ref[0] = seg(off, vw).astype(BF16)
    off += vw
    r = seg(off, vw)
    gr_ref[0] = (r * jax.nn.sigmoid(r)).astype(BF16)
    off += vw
    a_low = seg(off, GLA_RANK).astype(BF16)
    z = jnp.dot(a_low, w2_ref[...].astype(BF16), preferred_element_type=F32) + b2_ref[...]
    g_ref[0] = (jnp.minimum(z, 0.0) - jnp.log1p(jnp.exp(-jnp.abs(z)))) * (LOG2E / GLA_TAU)
    off += GLA_RANK
    dq_ref[0] = (seg(off, qw) * (DIFF_DH ** -0.5 * LOG2E)).astype(BF16)
    off += qw
    dk_ref[0] = seg(off, qw).astype(BF16)
    off += qw
    dv_ref[0] = seg(off, dvw).astype(BF16)
    off += dvw
    for j in range(sg_ref.shape[-1] // GATE_CHUNK):
        gate = jax.nn.sigmoid(seg(off + j * GATE_CHUNK, GATE_CHUNK))
        sg_ref[0, :, j * GATE_CHUNK:(j + 1) * GATE_CHUNK] = gate.astype(BF16)
    assert pending["count"] >= NEXT_PIECES


def _mixer_proj(h, sh, sc, nw, w_t, w2, b2):
    bsz, s, d = h.shape
    tm = PROJ_ROWS
    kw = GLA_HEADS * GLA_DK
    vw = GLA_HEADS * GLA_DV
    qw = DIFF_HEADS * 2 * DIFF_DH
    dvw = DIFF_HEADS * DIFF_DV
    n_gate = 2 * d
    vec = pl.BlockSpec((1, 1, d), lambda b, i: (b, 0, 0))

    def rows(w):
        return pl.BlockSpec((1, tm, w), lambda b, i: (b, i, 0))

    widths = [(kw, BF16), (kw, BF16), (vw, BF16), (vw, BF16), (kw, F32),
              (qw, BF16), (qw, BF16), (dvw, BF16), (n_gate, BF16)]
    return pl.pallas_call(
        _proj_kernel,
        out_shape=[jax.ShapeDtypeStruct((bsz, s, w), dt) for w, dt in widths],
        grid=(bsz, s // tm),
        in_specs=[pl.BlockSpec((1, tm, d), lambda b, i: (0, 0, 0)),
                  _next_rows(bsz, s // tm, tm, d), _resident(sh.shape),
                  _resident(sc.shape), _resident((1, d)), _resident(w_t.shape),
                  _resident(w2.shape), _resident((1, kw))],
        out_specs=[rows(w) for w, _ in widths],
        scratch_shapes=[pltpu.VMEM((tm, d), BF16), pltpu.VMEM((tm, d), BF16)],
        compiler_params=_params(("arbitrary", "arbitrary")),
        name="mixer_proj",
    )(h, h, sh, sc, nw.reshape(1, d), w_t, w2, b2.reshape(1, kw))


def _gla_kernel(q_ref, k_ref, v_ref, r_ref, g_ref, hn_ref, o_ref, st_ref, inter_ref,
                *, chunk, sub):
    @pl.when(pl.program_id(1) == 0)
    def _():
        st_ref[...] = jnp.zeros_like(st_ref)

    nsub = chunk // sub
    row = lax.broadcasted_iota(jnp.int32, (chunk, chunk), 0)
    col = lax.broadcasted_iota(jnp.int32, (chunk, chunk), 1)
    causal = col <= row
    cum_mat = jnp.where(causal, 1.0, 0.0).astype(BF16)
    row_blk = lax.broadcasted_iota(jnp.int32, (chunk, LANES), 0) // sub
    lane_head = lax.broadcasted_iota(jnp.int32, (chunk, LANES), 1) // GLA_DK

    def cumsum(ck, bi, pair):
        return prefix_sums(g_ref[bi, ck * chunk:(ck + 1) * chunk, pair * LANES:(pair + 1) * LANES])

    def prefix_sums(g):
        g1 = g.astype(BF16)
        e1 = g - g1.astype(F32)
        g2 = e1.astype(BF16)
        g3 = (e1 - g2.astype(F32)).astype(BF16)
        cs = jnp.dot(cum_mat, jnp.concatenate([g1, g2, g3], axis=1), preferred_element_type=F32)
        return cs[:, :LANES] + cs[:, LANES:2 * LANES] + cs[:, 2 * LANES:]

    def decay(ck, bi, pair, b):
        lanes = slice(pair * LANES, (pair + 1) * LANES)
        b_last = b[chunk - 1:chunk]
        rows = slice(ck * chunk, (ck + 1) * chunk)
        q = q_ref[bi, rows, lanes].astype(F32)
        k = k_ref[bi, rows, lanes].astype(F32)
        q_cat, w = [], []
        for j in range(nsub):
            ref_b = b[j * sub - 1:j * sub] if j else jnp.zeros_like(b_last)
            q_cat.append((q * jnp.exp2(jnp.minimum(b - ref_b, 0.0))).astype(BF16))
            w.append(b[j * sub:(j + 1) * sub] - ref_b)
        w = jnp.concatenate(w, axis=0)
        return dict(
            q_cat=jnp.concatenate(q_cat, axis=1),
            q_state=(q * jnp.exp2(b)).astype(BF16),
            k_state=k * jnp.exp2(b_last - b),
            k_hat=k * jnp.exp2(jnp.minimum(-w, EXP2_CLAMP)),
            chunk_decay=jnp.exp2(b_last),
            key_exponent=jnp.max(-w, axis=0, keepdims=True))

    def scores(d):
        out = []
        for hh in range(2):
            kh = jnp.where(lane_head == hh, d["k_hat"], 0.0).astype(BF16)
            k_cat = jnp.concatenate(
                [jnp.where(row_blk == j, kh, jnp.zeros_like(kh)) for j in range(nsub)], axis=1)
            out.append(lax.dot_general(d["q_cat"], k_cat, (((1,), (1,)), ((), ())),
                                       preferred_element_type=F32))
        return out

    def finish(ck, bi, pair, d, sc):
        rows = slice(ck * chunk, (ck + 1) * chunk)
        for hh in range(2):
            head = pair * 2 + hh
            cols = slice(head * GLA_DV, (head + 1) * GLA_DV)
            intra = jnp.where(causal, sc[hh], 0.0).astype(BF16)
            vh = v_ref[bi, rows, cols]
            state_t = st_ref[bi * GLA_HEADS + head]
            o_inter = lax.dot_general(d["q_state"], state_t.astype(BF16), (((1,), (1,)), ((), ())),
                                      preferred_element_type=F32)
            inter_ref[(ck * n_batch + bi) * GLA_HEADS + head] = o_inter
            o = jnp.dot(intra, vh, preferred_element_type=F32) + o_inter
            ks = jnp.where(lane_head == hh, d["k_state"], 0.0).astype(BF16)
            kv_t = lax.dot_general(vh, ks, (((0,), (0,)), ((), ())), preferred_element_type=F32)
            st_ref[bi * GLA_HEADS + head] = state_t * d["chunk_decay"] + kv_t
            y = _rmsnorm(o, hn_ref[...]) * r_ref[bi, rows, cols].astype(F32)
            o_ref[bi, rows, cols] = y.astype(BF16)

    n_batch = q_ref.shape[0]
    n_chunk = q_ref.shape[1] // chunk
    per_chunk = list(itertools.product(range(n_batch), range(GLA_HEADS // 2)))
    chains = [(ck, bi, pair) for ck in range(n_chunk) for bi, pair in per_chunk]
    cums = {t: cumsum(*chains[t]) for t in range(len(per_chunk))}
    decayed, scored = {}, {}
    key_exponent = jnp.zeros((1, LANES), F32)
    for t in range(len(chains) + 1):
        if t + len(per_chunk) < len(chains):
            cums[t + len(per_chunk)] = cumsum(*chains[t + len(per_chunk)])
        if t < len(chains):
            decayed[t] = decay(*chains[t], cums.pop(t))
            key_exponent = jnp.maximum(key_exponent, decayed[t]["key_exponent"])
        if t >= 1:
            finish(*chains[t - 1], decayed.pop(t - 1), scored.pop(t - 1))
        if t < len(chains):
            scored[t] = scores(decayed[t])

    @pl.when(jnp.max(key_exponent) > EXP2_CLAMP)
    def _():
        def redo(i, carry):
            ck, bi = i // n_batch, i % n_batch
            rows = pl.ds(pl.multiple_of(ck * chunk, chunk), chunk)
            for pair in range(GLA_HEADS // 2):
                lanes = slice(pair * LANES, (pair + 1) * LANES)
                q = q_ref[bi, rows, lanes].astype(F32)
                k = k_ref[bi, rows, lanes].astype(F32)
                b = prefix_sums(g_ref[bi, rows, lanes])
                q_cat, to_end = [], []
                for j in range(nsub):
                    end_b = b[(j + 1) * sub - 1:(j + 1) * sub]
                    q_cat.append((q * jnp.exp2(jnp.minimum(b - end_b, 0.0))).astype(BF16))
                    to_end.append(end_b - b[j * sub:(j + 1) * sub])
                q_cat = jnp.concatenate(q_cat, axis=1)
                k_end = k * jnp.exp2(jnp.concatenate(to_end, axis=0))
                for hh in range(2):
                    head = pair * 2 + hh
                    cols = slice(head * GLA_DV, (head + 1) * GLA_DV)
                    vh = v_ref[bi, rows, cols]
                    kh = jnp.where(lane_head == hh, k_end, 0.0).astype(BF16)
                    k_cat = jnp.concatenate(
                        [jnp.where(row_blk == j, kh, jnp.zeros_like(kh)) for j in range(nsub)],
                        axis=1)
                    cross = lax.dot_general(q_cat, k_cat, (((1,), (1,)), ((), ())),
                                            preferred_element_type=F32)
                    cross = jnp.where(row // sub > col // sub, cross, 0.0).astype(BF16)
                    o = jnp.dot(cross, vh, preferred_element_type=F32)
                    v32 = vh.astype(F32)
                    in_blk = lax.broadcasted_iota(jnp.int32, (chunk, LANES), 0) % sub
                    for off in range(sub):
                        k_o = pltpu.roll(k, off, 0) if off else k
                        b_o = pltpu.roll(b, off, 0) if off else b
                        v_o = pltpu.roll(v32, off, 0) if off else v32
                        ok = (in_blk >= off) & (lane_head == hh)
                        term = jnp.where(ok, q * k_o * jnp.exp2(jnp.minimum(b - b_o, 0.0)), 0.0)
                        o += jnp.sum(term, axis=1, keepdims=True) * v_o
                    o += inter_ref[(ck * n_batch + bi) * GLA_HEADS + head]
                    y = _rmsnorm(o, hn_ref[...]) * r_ref[bi, rows, cols].astype(F32)
                    o_ref[bi, rows, cols] = y.astype(BF16)
            return carry

        lax.fori_loop(0, n_chunk * n_batch, redo, 0)


def _gla(gq, gk, gv, gr, g, head_norm):
    bsz, s, kw = gq.shape
    vw = gv.shape[-1]
    nb, chunk = GLA_BATCHES, GLA_CHUNK
    step_rows = GLA_STEP_CHUNKS * chunk

    def rows(w):
        return pl.BlockSpec((nb, step_rows, w), lambda b, c: (b, c, 0))

    return pl.pallas_call(
        functools.partial(_gla_kernel, chunk=chunk, sub=GLA_SUB),
        out_shape=jax.ShapeDtypeStruct((bsz, s, vw), BF16),
        grid=(bsz // nb, s // step_rows),
        in_specs=[rows(kw), rows(kw), rows(vw), rows(vw), rows(kw), _resident((1, GLA_DV))],
        out_specs=rows(vw),
        scratch_shapes=[pltpu.VMEM((nb * GLA_HEADS, GLA_DV, LANES), F32),
                        pltpu.VMEM((GLA_STEP_CHUNKS * nb * GLA_HEADS, chunk, GLA_DV), F32)],
        compiler_params=_params(("parallel", "arbitrary")),
        name="gla",
    )(gq, gk, gv, gr, g, head_norm.reshape(1, GLA_DV))


def _attn_kernel(q_ref, k_ref, v_ref, pos_ref, slope_ref, lam_ref, hn_ref, o_ref,
                 kk_ref, qq_ref, vt_ref, s0_ref, s1_ref, s2_ref, m_ref, acc_ref, *, tq, lam_init):
    seq = k_ref.shape[1]
    nq = seq // tq
    hq = tq // 2
    c = slope_ref[0] * LOG2E
    c1 = c.astype(BF16).astype(F32)
    c2 = (c - c1).astype(BF16).astype(F32)
    c3 = c - c1 - c2

    kk_ref[:, :LANES] = k_ref[0]
    kk_ref[:, LANES:] = pos_ref[...]
    sub = lax.broadcasted_iota(jnp.int32, (LANES, 2 * tq), 0)
    cf = jnp.where((sub == 0) | (sub == 3), c1, jnp.where((sub == 1) | (sub == 4), c2, c3))
    slope_rows = jnp.where(sub < 6, cf, 0.0).astype(BF16)
    for i in range(qq_ref.shape[0]):
        qq_ref[i, LANES:, :] = slope_rows
    for jb in range(nq):
        vt_ref[jb, :DIFF_DV, :] = v_ref[0, jb * tq:(jb + 1) * tq, :].astype(F32).T.astype(BF16)
        vt_ref[jb, DIFF_DV:, :] = jnp.ones((SUM_ROWS, tq), BF16)
    lam = lam_ref[...]
    lam = (jnp.exp(jnp.sum(lam[0:1] * lam[1:2], axis=-1, keepdims=True))
           - jnp.exp(jnp.sum(lam[2:3] * lam[3:4], axis=-1, keepdims=True)) + lam_init)

    def logits(qq_slot, j, s_blk):
        start = pl.multiple_of(j * tq, tq)
        s = jnp.dot(kk_ref[pl.ds(start, tq), :], qq_ref[qq_slot], preferred_element_type=F32)
        s_blk[...] = s
        return jnp.max(s.reshape(tq // 8, 8, 2 * tq), axis=0)

    def softmax_accumulate(qs, qi, j, s_blk, mx):
        m_old = m_ref[qs]
        shift = c * jnp.asarray((qi - j) * tq, F32)
        m_new = jnp.maximum(m_old, jnp.max(mx, axis=0, keepdims=True) - shift)
        alpha = jnp.exp2(m_old - m_new)
        m_ref[qs] = m_new
        m_shift = m_new + shift
        probs = jnp.concatenate([jnp.exp2(s_blk[:hq, :] - m_shift).astype(BF16),
                                 jnp.exp2(s_blk[hq:, :] - m_shift).astype(BF16)], axis=0)
        acc_ref[qs] = alpha * acc_ref[qs] + jnp.dot(vt_ref[j], probs, preferred_element_type=F32)

    def late_half(x):
        return jnp.concatenate([x[..., hq:tq], x[..., tq + hq:]], axis=-1)

    def diag_logits(qq_slot, j, s_blk):
        start = pl.multiple_of(j * tq, tq)
        s_blk[:hq, :] = jnp.dot(kk_ref[pl.ds(start, hq), :], qq_ref[qq_slot],
                                preferred_element_type=F32)
        s_blk[hq:, :tq] = jnp.dot(kk_ref[pl.ds(start + hq, hq), :], late_half(qq_ref[qq_slot]),
                                  preferred_element_type=F32)

    def diag_softmax_accumulate(qs, j, s_blk):
        m_old = m_ref[qs]
        key = lax.broadcasted_iota(jnp.int32, (hq, 2 * tq), 0)
        qry = lax.broadcasted_iota(jnp.int32, (hq, 2 * tq), 1) & (tq - 1)
        early = jnp.where(key <= qry, s_blk[:hq, :], -jnp.inf)
        key_l = lax.broadcasted_iota(jnp.int32, (hq, tq), 0)
        qry_l = lax.broadcasted_iota(jnp.int32, (hq, tq), 1) & (hq - 1)
        late = jnp.where(key_l <= qry_l, s_blk[hq:, :tq], -jnp.inf)
        mx_e = jnp.max(jnp.max(early.reshape(hq // 8, 8, 2 * tq), axis=0), axis=0, keepdims=True)
        mx_l = jnp.max(jnp.max(late.reshape(hq // 8, 8, tq), axis=0), axis=0, keepdims=True)
        mx = jnp.concatenate(
            [mx_e[:, :hq], jnp.maximum(mx_e[:, hq:tq], mx_l[:, :hq]),
             mx_e[:, tq:tq + hq], jnp.maximum(mx_e[:, tq + hq:], mx_l[:, hq:])], axis=1)
        m_new = jnp.maximum(m_old, mx)
        alpha = jnp.exp2(m_old - m_new)
        m_ref[qs] = m_new
        p_early = jnp.exp2(early - m_new).astype(BF16)
        p_late = jnp.exp2(late - late_half(m_new)).astype(BF16)
        vt = vt_ref[j]
        acc_ref[qs] = alpha * acc_ref[qs] + jnp.dot(vt[:, :hq], p_early,
                                                    preferred_element_type=F32)
        upd = jnp.dot(vt[:, hq:], p_late, preferred_element_type=F32)
        acc_ref[qs, :, hq:tq] += upd[:, :hq]
        acc_ref[qs, :, tq + hq:] += upd[:, hq:]

    bufs = (s0_ref, s1_ref, s2_ref)
    ahead = len(bufs) - 1
    assert (nq + 1) % len(bufs) == 0

    def build_queries(p, slot):
        for qs, blk in enumerate((p, nq - 1 - p)):
            q_t = q_ref[0, pl.ds(pl.multiple_of(blk * tq, tq), tq), :].astype(F32).T
            half = lax.broadcasted_iota(jnp.int32, (LANES, tq), 0)
            qq_ref[2 * slot + qs, :LANES, :tq] = jnp.where(half < DIFF_DH, q_t, 0.0).astype(BF16)
            qq_ref[2 * slot + qs, :LANES, tq:] = jnp.where(half >= DIFF_DH, q_t, 0.0).astype(BF16)

    def diag_prefetch(p, slot, t):
        qs, blk = ((1, nq - 1 - p), (0, p))[t]
        diag_logits(2 * slot + qs, blk, bufs[t])

    def pair(p, carry):
        slot = p % 2
        q_blk = (p, nq - 1 - p)
        for qs in range(2):
            m_ref[qs] = jnp.full((1, 2 * tq), -jnp.inf, F32)
            acc_ref[qs] = jnp.zeros((DIFF_DV + SUM_ROWS, 2 * tq), F32)
        items = [(1, q_blk[1], q_blk[1], True), (0, q_blk[0], q_blk[0], True)]
        for u in range(nq - 1):
            late = u < q_blk[1]
            items.append((jnp.where(late, 1, 0), jnp.where(late, q_blk[1], q_blk[0]),
                          jnp.where(late, u, u - q_blk[1]), False))
        col_max = {}
        p_next = jnp.minimum(p + 1, nq // 2 - 1)
        for t, (qs, qi, j, masked) in enumerate(items):
            ta = t + ahead
            if ta < len(items):
                col_max[ta] = logits(2 * slot + items[ta][0], items[ta][2], bufs[ta % len(bufs)])
            elif ta == len(items):
                build_queries(p_next, 1 - slot)
                diag_prefetch(p_next, 1 - slot, 0)
            else:
                diag_prefetch(p_next, 1 - slot, 1)
            if masked:
                diag_softmax_accumulate(qs, j, bufs[t % len(bufs)])
            else:
                softmax_accumulate(qs, qi, j, bufs[t % len(bufs)], col_max.pop(t))
        for qs in range(2):
            out = acc_ref[qs, :DIFF_DV, :] * (1.0 / acc_ref[qs, DIFF_DV:DIFF_DV + 1, :])
            y_t = out[:, :tq] - lam * out[:, tq:]
            y_t = y_t * lax.rsqrt(jnp.mean(y_t * y_t, axis=0, keepdims=True) + EPS)
            rows = pl.ds(pl.multiple_of(q_blk[qs] * tq, tq), tq)
            o_ref[0, rows, :] = (y_t.T * hn_ref[...] * (1.0 - lam_init)).astype(BF16)
        return carry

    build_queries(0, 0)
    diag_prefetch(0, 0, 0)
    diag_prefetch(0, 0, 1)
    lax.fori_loop(0, nq // 2, pair, 0)


def _diff_attn(dq, dk, dv, lam_vecs, head_norm, lam_init):
    bsz, s, _ = dq.shape
    tq = ATTN_BLOCK
    slopes = jnp.asarray(
        [2.0 ** (-ALIBI_MAX_BIAS * (i + 1) / DIFF_HEADS) for i in range(DIFF_HEADS)], F32
    ).reshape(DIFF_HEADS, 1, 1)
    pos = np.arange(s) % tq
    feat = np.zeros((s, LANES), np.float32)
    feat[:, 0:3] = (pos - pos % 16)[:, None]
    feat[:, 3:6] = (pos % 16)[:, None]
    return pl.pallas_call(
        functools.partial(_attn_kernel, tq=tq, lam_init=lam_init),
        out_shape=jax.ShapeDtypeStruct((bsz, s, DIFF_HEADS * DIFF_DV), BF16),
        grid=(bsz, DIFF_HEADS),
        in_specs=[
            pl.BlockSpec((1, s, LANES), lambda b, h: (b, 0, h)),
            pl.BlockSpec((1, s, LANES), lambda b, h: (b, 0, h)),
            pl.BlockSpec((1, s, DIFF_DV), lambda b, h: (b, 0, h)),
            _resident((s, LANES)),
            pl.BlockSpec((1, 1, 1), lambda b, h: (h, 0, 0)),
            _resident(lam_vecs.shape),
            _resident((1, DIFF_DV)),
        ],
        out_specs=pl.BlockSpec((1, s, DIFF_DV), lambda b, h: (b, 0, h)),
        scratch_shapes=[
            pltpu.VMEM((s, 2 * LANES), BF16),
            pltpu.VMEM((4, 2 * LANES, 2 * tq), BF16),
            pltpu.VMEM((s // tq, DIFF_DV + SUM_ROWS, tq), BF16),
            pltpu.VMEM((tq, 2 * tq), F32),
            pltpu.VMEM((tq, 2 * tq), F32),
            pltpu.VMEM((tq, 2 * tq), F32),
            pltpu.VMEM((2, 1, 2 * tq), F32),
            pltpu.VMEM((2, DIFF_DV + SUM_ROWS, 2 * tq), F32),
        ],
        compiler_params=_params(("parallel", "parallel")),
        name="diff_attn",
    )(dq, dk, dv, jnp.asarray(feat, BF16), slopes, lam_vecs, head_norm.reshape(1, DIFF_DV))


def _merge_kernel(h_hbm, za_hbm, zb_hbm, sg_hbm, gt_ref, wa_ref, wb_ref, wo_ref, o_ref,
                  h_buf, za_buf, zb_buf, sg_buf, sem_ref, *, tm):
    d = o_ref.shape[-1]
    nt = pl.num_programs(1)
    step = pl.program_id(0) * nt + pl.program_id(1)
    n_steps = pl.num_programs(0) * nt
    streams = ((h_hbm, h_buf), (za_hbm, za_buf), (zb_hbm, zb_buf), (sg_hbm, sg_buf))

    def copies(s):
        rows = pl.ds(pl.multiple_of((s % nt) * tm, tm), tm)
        return [pltpu.make_async_copy(src.at[s // nt, rows, :], buf.at[s % MERGE_RING],
                                      sem_ref.at[k, s % MERGE_RING])
                for k, (src, buf) in enumerate(streams)]

    @pl.when(step == 0)
    def _():
        for s in range(MERGE_RING - 1):
            for copy in copies(jnp.int32(s)):
                copy.start()

    @pl.when(step + (MERGE_RING - 1) < n_steps)
    def _():
        for copy in copies(step + (MERGE_RING - 1)):
            copy.start()

    for copy in copies(step):
        copy.wait()
    slot = step % MERGE_RING
    ya = jnp.dot(za_buf[slot], wa_ref[...].astype(BF16), preferred_element_type=F32)
    yb = jnp.dot(zb_buf[slot], wb_ref[...].astype(BF16), preferred_element_type=F32)
    mix = sg_buf[slot, :, :d].astype(F32) * ya + sg_buf[slot, :, d:].astype(F32) * yb
    m = jnp.dot(mix.astype(BF16), wo_ref[...].astype(BF16), preferred_element_type=F32)
    o_ref[0] = h_buf[slot] + gt_ref[0] * m


def _merge(h, za, zb, sg, gt, w_a, w_b, w_o):
    bsz, s, d = h.shape
    tm = MERGE_ROWS
    assert bsz * (s // tm) >= MERGE_RING - 1
    vec = pl.BlockSpec((1, 1, d), lambda b, i: (b, 0, 0))
    hbm = pl.BlockSpec(memory_space=pl.ANY)
    return pl.pallas_call(
        functools.partial(_merge_kernel, tm=tm),
        out_shape=jax.ShapeDtypeStruct((bsz, s, d), F32),
        grid=(bsz, s // tm),
        in_specs=[hbm, hbm, hbm, hbm, vec,
                  _resident(w_a.shape), _resident(w_b.shape), _resident(w_o.shape)],
        out_specs=pl.BlockSpec((1, tm, d), lambda b, i: (b, i, 0)),
        scratch_shapes=[pltpu.VMEM((MERGE_RING, tm, x.shape[-1]), x.dtype) for x in (h, za, zb, sg)]
        + [pltpu.SemaphoreType.DMA((4, MERGE_RING))],
        compiler_params=_params(("arbitrary", "arbitrary")),
        name="merge",
    )(h, za, zb, sg, gt, w_a, w_b, w_o)


def kernel(x, c, w_ada, b_ada, ffn1_norm, ffn1_w_in, ffn1_w_out, mix_norm, w_in, gla_alpha_w2, gla_alpha_b, gla_head_norm, diff_lq1, diff_lk1, diff_lq2, diff_lk2, diff_head_norm, w_branch_a, w_branch_b, w_out, ffn2_norm, ffn2_w_in, ffn2_w_out, final_norm):
    depth = w_ada.shape[0]
    bsz, s, d = x.shape
    assert bsz % GLA_BATCHES == 0 and s % (GLA_STEP_CHUNKS * GLA_CHUNK) == 0
    assert s % MERGE_ROWS == 0 and s % FFN_ROWS == 0 and s % PROJ_ROWS == 0
    assert (s // ATTN_BLOCK) % 2 == 0 and (N_MOD * d) % (ADALN_STEPS * LANES) == 0
    h = x
    for l in range(depth):
        lam_init = 0.8 - 0.6 * math.exp(-0.3 * l)
        mod = _adaln(c, w_ada[l], b_ada[l])
        sh1, sc1, gt1, sh2, sc2, gt2, sh3, sc3, gt3 = [
            mod[:, i * d:(i + 1) * d].reshape(bsz, 1, d) for i in range(N_MOD)]
        last = l == depth - 1
        h = _ffn(h, sh1, sc1, gt1, ffn1_norm[l], ffn1_w_in[l], ffn1_w_out[l], final_norm,
                 final_norm=False)
        w_t = jnp.swapaxes(w_in, 1, 2)[l]
        gq, gk, gv, gr, g, dq, dk, dv, sg = _mixer_proj(
            h, sh2, sc2, mix_norm[l], w_t, gla_alpha_w2[l], gla_alpha_b[l])
        za = _gla(gq, gk, gv, gr, g, gla_head_norm[l])
        lam_vecs = jnp.stack([diff_lq1[l], diff_lk1[l], diff_lq2[l], diff_lk2[l]])
        zb = _diff_attn(dq, dk, dv, lam_vecs, diff_head_norm[l], lam_init)
        h = _merge(h, za, zb, sg, gt2, w_branch_a[l], w_branch_b[l], w_out[l])
        h = _ffn(h, sh3, sc3, gt3, ffn2_norm[l], ffn2_w_in[l], ffn2_w_out[l], final_norm,
                 final_norm=last)
    return h
```
